```python
import math
import jax
import jax.numpy as jnp
from jax import lax
import numpy as np

D_MODEL = 1024
BATCH = 2
SEQ = 8192
DEPTH = 4
DEC_BATCH = 128
DEC_SEQ = 4
PAST_LEN = 8192
PAGE_SIZE = 128

N_EVEN = (DEPTH + 1) // 2
N_ODD = DEPTH // 2
S5_WIDTH = D_MODEL // 2
S5_GROUP = 16
S5_GROUPS = S5_WIDTH // S5_GROUP
S5_STATE = 64
SWA_WIDTH = D_MODEL // 2
SWA_HEAD_DIM = 64
SWA_HEADS = SWA_WIDTH // SWA_HEAD_DIM
SWA_KV_HEADS = 2
SWA_GROUP = SWA_HEADS // SWA_KV_HEADS
WINDOW = 128
T5_BUCKETS = 32
T5_MAX_DIST = 128
HGRN_HEADS = 8
HGRN_DK = 128
HGRN_DV = D_MODEL // HGRN_HEADS
HGRN_CHUNK = 64
D_FF = 4 * D_MODEL
EVEN_IN = S5_WIDTH + SWA_WIDTH + 2 * SWA_KV_HEADS * SWA_HEAD_DIM
ODD_IN = 2 * HGRN_HEADS * HGRN_DK + 2 * HGRN_HEADS * HGRN_DV
RMS_EPS = 1e-6
NEG_INF = -1e30
LOG_FLOOR = 1e-30

kernel_name = 'hybrid_s5_swa_hgrn2_decode_step'


def rms_norm(x, w):
    xf = x.astype(jnp.float32)
    y = xf * lax.rsqrt(jnp.mean(xf * xf, axis=-1, keepdims=True) + RMS_EPS) * w.astype(jnp.float32)
    return y.astype(x.dtype)


def t5_bias(table, dist):
    n = jnp.maximum(dist, 0)
    max_exact = T5_BUCKETS // 2
    large = max_exact + (jnp.log(jnp.maximum(n, max_exact).astype(jnp.float32) / max_exact)
                         / math.log(T5_MAX_DIST / max_exact) * (T5_BUCKETS - max_exact)).astype(jnp.int32)
    large = jnp.minimum(large, T5_BUCKETS - 1)
    bucket = jnp.where(n < max_exact, n, large)
    bias = table.astype(jnp.float32)[bucket]
    return jnp.moveaxis(bias, -1, 0).reshape(SWA_KV_HEADS, SWA_GROUP, *dist.shape)


def sink_attention(q, k, v, bias, mask, sinks):
    f32 = jnp.float32
    s = jnp.einsum('bnqkgd,bnskd->bnkgqs', q.astype(f32), k.astype(f32)) * (SWA_HEAD_DIM ** -0.5)
    s = jnp.where(mask[None, :, None, None], s + bias[None, None], NEG_INF)
    sink = jnp.broadcast_to(sinks.astype(f32).reshape(SWA_KV_HEADS, SWA_GROUP)[None, None, :, :, None, None],
                            s.shape[:-1] + (1,))
    p = jax.nn.softmax(jnp.concatenate([s, sink], axis=-1), axis=-1)[..., :-1]
    return jnp.einsum('bnkgqs,bnskd->bnqkgd', p, v.astype(f32))


def swa_prompt(q, k, v, sinks, table):
    bt, L, _ = q.shape
    nb = L // WINDOW
    qb = q.reshape(bt, nb, WINDOW, SWA_KV_HEADS, SWA_GROUP, SWA_HEAD_DIM)
    kb = k.reshape(bt, nb, WINDOW, SWA_KV_HEADS, SWA_HEAD_DIM)
    vb = v.reshape(bt, nb, WINDOW, SWA_KV_HEADS, SWA_HEAD_DIM)

    def band(t):
        prev = jnp.concatenate([jnp.zeros_like(t[:, :1]), t[:, :-1]], axis=1)
        return jnp.concatenate([prev, t], axis=2)

    t_idx = jnp.arange(WINDOW)[:, None]
    m_idx = jnp.arange(2 * WINDOW)[None, :]
    dist = WINDOW + t_idx - m_idx
    in_win = (dist >= 0) & (dist < WINDOW)
    blk = jnp.arange(nb)[:, None, None]
    mask = in_win[None] & ((blk > 0) | (m_idx[None] >= WINDOW))
    o = sink_attention(qb, band(kb), band(vb), t5_bias(table, dist), mask, sinks)
    w = min(WINDOW, L)
    return o.reshape(bt, L, SWA_HEADS * SWA_HEAD_DIM), k[:, L - w:], v[:, L - w:]


def swa_sample(q, k, v, cache_k, cache_v, sinks, table):
    bt, L, _ = q.shape
    wc = cache_k.shape[1]
    kk = jnp.concatenate([cache_k, k.astype(cache_k.dtype)], axis=1)
    vv = jnp.concatenate([cache_v, v.astype(cache_v.dtype)], axis=1)
    dist = wc + jnp.arange(L)[:, None] - jnp.arange(wc + L)[None, :]
    mask = ((dist >= 0) & (dist < WINDOW))[None]
    qb = q.reshape(bt, 1, L, SWA_KV_HEADS, SWA_GROUP, SWA_HEAD_DIM)
    o = sink_attention(qb, kk[:, None], vv[:, None], t5_bias(table, dist), mask, sinks)
    return o.reshape(bt, L, SWA_HEADS * SWA_HEAD_DIM), kk[:, L:], vv[:, L:]


def s5_mixer(u, h0_re, h0_im, lam_re, lam_im, log_step, b_re, b_im, c_re, c_im, d_skip, w_glu):
    f32 = jnp.float32
    bt, L, _ = u.shape
    uf = u.astype(f32)
    ug = uf.reshape(bt, L, S5_GROUPS, S5_GROUP)
    lam_re = lam_re.astype(f32)
    lam_im = lam_im.astype(f32)
    dt = jnp.exp(log_step.astype(f32))[:, None]
    mag = jnp.exp(lam_re * dt)
    ang = lam_im * dt
    abar_re = mag * jnp.cos(ang)
    abar_im = mag * jnp.sin(ang)
    den = lam_re * lam_re + lam_im * lam_im
    nr = abar_re - 1.0
    coef_re = ((nr * lam_re + abar_im * lam_im) / den)[..., None]
    coef_im = ((abar_im * lam_re - nr * lam_im) / den)[..., None]
    b_re = b_re.astype(f32)
    b_im = b_im.astype(f32)
    bb_re = coef_re * b_re - coef_im * b_im
    bb_im = coef_re * b_im + coef_im * b_re
    bu_re = jnp.einsum('blgc,gpc->blgp', ug, bb_re)
    bu_im = jnp.einsum('blgc,gpc->blgp', ug, bb_im)
    h0_re = h0_re.astype(f32)
    h0_im = h0_im.astype(f32)
    bu_re = bu_re.at[:, 0].add(abar_re * h0_re - abar_im * h0_im)
    bu_im = bu_im.at[:, 0].add(abar_re * h0_im + abar_im * h0_re)
    a_re = jnp.broadcast_to(abar_re, bu_re.shape)
    a_im = jnp.broadcast_to(abar_im, bu_im.shape)

    def combine(e1, e2):
        a1r, a1i, b1r, b1i = e1
        a2r, a2i, b2r, b2i = e2
        return (a1r * a2r - a1i * a2i, a1r * a2i + a1i * a2r,
                a2r * b1r - a2i * b1i + b2r, a2r * b1i + a2i * b1r + b2i)

    _, _, h_re, h_im = lax.associative_scan(combine, (a_re, a_im, bu_re, bu_im), axis=1)
    y = (jnp.einsum('gcp,blgp->blgc', c_re.astype(f32), h_re)
         - jnp.einsum('gcp,blgp->blgc', c_im.astype(f32), h_im)).reshape(bt, L, S5_WIDTH)
    y = y + d_skip.astype(f32) * uf
    g = jax.nn.gelu(y)
    out = g * jax.nn.sigmoid(g @ w_glu.astype(f32))
    return out.astype(u.dtype), h_re[:, -1], h_im[:, -1]


def hgrn2_mixer(q, fz, iv, gz, lb, s0, gnorm_w):
    f32 = jnp.float32
    bt, L, _ = q.shape
    lb = lb.astype(f32)
    fzf = fz.astype(f32)
    qf = jax.nn.silu(q.astype(f32))
    log_f = jnp.logaddexp(jnp.log(jnp.maximum(lb, LOG_FLOOR)), jnp.log1p(-lb) + jax.nn.log_sigmoid(fzf))
    kf = (1.0 - lb) * jax.nn.sigmoid(-fzf)
    vf = iv.astype(f32)
    c = min(HGRN_CHUNK, L)
    nc = -(-L // c)
    pad = nc * c - L

    def chunked(t, dim):
        t = t.reshape(bt, L, HGRN_HEADS, dim)
        t = jnp.pad(t, ((0, 0), (0, pad), (0, 0), (0, 0)))
        return jnp.moveaxis(t.reshape(bt, nc, c, HGRN_HEADS, dim), 1, 0)

    xs = (chunked(qf, HGRN_DK), chunked(kf, HGRN_DK), chunked(log_f, HGRN_DK), chunked(vf, HGRN_DV))
    causal = jnp.tril(jnp.ones((c, c), dtype=bool))[None, :, :, None, None]

    def step(S, inp):
        qc, kc, gc, vc = inp
        b = jnp.cumsum(gc, axis=1)
        diff = b[:, :, None] - b[:, None, :]
        decay = jnp.exp(jnp.where(causal, diff, NEG_INF))
        scores = jnp.einsum('bthd,bshd,btshd->bhts', qc, kc, decay)
        o = (jnp.einsum('bhts,bshv->bthv', scores, vc)
             + jnp.einsum('bthd,bhdv->bthv', qc * jnp.exp(b), S))
        b_last = b[:, -1]
        S = (jnp.exp(b_last)[..., None] * S
             + jnp.einsum('bshd,bshv->bhdv', kc * jnp.exp(b_last[:, None] - b), vc))
        return S, o

    s_fin, o = lax.scan(step, s0.astype(f32), xs)
    o = jnp.moveaxis(o, 0, 1).reshape(bt, nc * c, HGRN_HEADS, HGRN_DV)[:, :L]
    o = o * lax.rsqrt(jnp.mean(o * o, axis=-1, keepdims=True) + RMS_EPS) * gnorm_w.astype(f32)
    o = o * jax.nn.silu(gz.astype(f32).reshape(bt, L, HGRN_HEADS, HGRN_DV))
    return o.reshape(bt, L, HGRN_HEADS * HGRN_DV).astype(q.dtype), s_fin


def trunk(x, cache_k, cache_v, st_re, st_im, st_h, p):
    prompt = cache_k is None
    bt, L, _ = x.shape
    sm = jax.nn.softmax(p['hgrn_lb_param'].astype(jnp.float32), axis=0)
    lower_bounds = jnp.cumsum(sm, axis=0) - sm[0]
    k_out, v_out, re_out, im_out, h_out = [], [], [], [], []
    split_even = [S5_WIDTH, S5_WIDTH + SWA_WIDTH, S5_WIDTH + SWA_WIDTH + SWA_KV_HEADS * SWA_HEAD_DIM]
    qk_w = HGRN_HEADS * HGRN_DK
    split_odd = [qk_w, 2 * qk_w, 2 * qk_w + HGRN_HEADS * HGRN_DV]
    for layer in range(DEPTH):
        idx = layer // 2
        h = rms_norm(x, p['norm_mix_pre'][layer])
        if layer % 2 == 0:
            proj = h @ p['w_in_even'][idx]
            u, q, k, v = jnp.split(proj, split_even, axis=-1)
            if prompt:
                h0_re = jnp.zeros((bt, S5_GROUPS, S5_STATE), jnp.float32)
                h0_im = jnp.zeros((bt, S5_GROUPS, S5_STATE), jnp.float32)
            else:
                h0_re, h0_im = st_re[idx], st_im[idx]
            a_out, h_re, h_im = s5_mixer(u, h0_re, h0_im, p['s5_lambda_re'][idx], p['s5_lambda_im'][idx],
                                         p['s5_log_step'][idx], p['s5_b_re'][idx], p['s5_b_im'][idx],
                                         p['s5_c_re'][idx], p['s5_c_im'][idx], p['s5_d'][idx],
                                         p['s5_w_glu'][idx])
            k = k.reshape(bt, L, SWA_KV_HEADS, SWA_HEAD_DIM)
            v = v.reshape(bt, L, SWA_KV_HEADS, SWA_HEAD_DIM)
            if prompt:
                b_out, kw, vw = swa_prompt(q, k, v, p['swa_sinks'][idx], p['t5_bias_table'])
            else:
                b_out, kw, vw = swa_sample(q, k, v, cache_k[idx], cache_v[idx],
                                           p['swa_sinks'][idx], p['t5_bias_table'])
            mix = jnp.concatenate([a_out, b_out.astype(a_out.dtype)], axis=-1) @ p['w_out_even'][idx]
            k_out.append(kw)
            v_out.append(vw)
            re_out.append(h_re.astype(x.dtype))
            im_out.append(h_im.astype(x.dtype))
        else:
            proj = h @ p['w_in_odd'][idx]
            q, fz, iv, gz = jnp.split(proj, split_odd, axis=-1)
            if prompt:
                s0 = jnp.zeros((bt, HGRN_HEADS, HGRN_DK, HGRN_DV), jnp.float32)
            else:
                s0 = st_h[idx]
            c_out, s_new = hgrn2_mixer(q, fz, iv, gz, lower_bounds[idx], s0, p['hgrn_gnorm'][idx])
            mix = c_out @ p['w_out_odd'][idx]
            h_out.append(s_new.astype(x.dtype))
        x = x + rms_norm(mix, p['norm_mix_post'][layer])
        h = rms_norm(x, p['norm_mlp_pre'][layer])
        h = jnp.square(jax.nn.relu(h @ p['w_up'][layer])) @ p['w_down'][layer]
        x = x + rms_norm(h, p['norm_mlp_post'][layer])
    return x, jnp.stack(k_out), jnp.stack(v_out), jnp.stack(re_out), jnp.stack(im_out), jnp.stack(h_out)


def setup_inputs(seed: int = 0) -> dict:
    key = jax.random.key(seed)
    keys = iter(jax.random.split(key, 40))
    f32 = jnp.float32

    def normal(shape, scale):
        return jax.random.normal(next(keys), shape, f32) * scale

    def gain(shape):
        return 1.0 + normal(shape, 0.02)

    sw = min(WINDOW, PAST_LEN)
    n_idx = jnp.arange(S5_STATE, dtype=f32)
    mix_w = S5_WIDTH + SWA_WIDTH
    return {
        'x_prompt': normal((BATCH, SEQ, D_MODEL), 1.0),
        'x_sample': normal((DEC_BATCH, DEC_SEQ, D_MODEL), 1.0),
        'cache_swa_k': normal((N_EVEN, DEC_BATCH, sw, SWA_KV_HEADS, SWA_HEAD_DIM), 1.0),
        'cache_swa_v': normal((N_EVEN, DEC_BATCH, sw, SWA_KV_HEADS, SWA_HEAD_DIM), 1.0),
        'state_s5_re': normal((N_EVEN, DEC_BATCH, S5_GROUPS, S5_STATE), 0.5),
        'state_s5_im': normal((N_EVEN, DEC_BATCH, S5_GROUPS, S5_STATE), 0.5),
        'state_hgrn': normal((N_ODD, DEC_BATCH, HGRN_HEADS, HGRN_DK, HGRN_DV), 0.5),
        't5_bias_table': normal((T5_BUCKETS, SWA_HEADS), 0.5),
        'norm_mix_pre': gain((DEPTH, D_MODEL)),
        'norm_mix_post': gain((DEPTH, D_MODEL)),
        'norm_mlp_pre': gain((DEPTH, D_MODEL)),
        'norm_mlp_post': gain((DEPTH, D_MODEL)),
        'w_in_even': normal((N_EVEN, D_MODEL, EVEN_IN), D_MODEL ** -0.5),
        'w_out_even': normal((N_EVEN, mix_w, D_MODEL), mix_w ** -0.5),
        's5_lambda_re': -0.5 + normal((N_EVEN, S5_GROUPS, S5_STATE), 0.01),
        's5_lambda_im': math.pi * n_idx + normal((N_EVEN, S5_GROUPS, S5_STATE), 0.01),
        's5_log_step': jax.random.uniform(next(keys), (N_EVEN, S5_GROUPS), f32,
                                          math.log(1e-3), math.log(1e-1)),
        's5_b_re': normal((N_EVEN, S5_GROUPS, S5_STATE, S5_GROUP), (2 * S5_GROUP) ** -0.5),
        's5_b_im': normal((N_EVEN, S5_GROUPS, S5_STATE, S5_GROUP), (2 * S5_GROUP) ** -0.5),
        's5_c_re': normal((N_EVEN, S5_GROUPS, S5_GROUP, S5_STATE), (2 * S5_STATE) ** -0.5),
        's5_c_im': normal((N_EVEN, S5_GROUPS, S5_GROUP, S5_STATE), (2 * S5_STATE) ** -0.5),
        's5_d': normal((N_EVEN, S5_WIDTH), 1.0),
        's5_w_glu': normal((N_EVEN, S5_WIDTH, S5_WIDTH), S5_WIDTH ** -0.5),
        'swa_sinks': normal((N_EVEN, SWA_HEADS), 0.5),
        'w_in_odd': normal((N_ODD, D_MODEL, ODD_IN), D_MODEL ** -0.5),
        'w_out_odd': normal((N_ODD, HGRN_HEADS * HGRN_DV, D_MODEL), (HGRN_HEADS * HGRN_DV) ** -0.5),
        'hgrn_lb_param': normal((N_ODD, HGRN_HEADS * HGRN_DK), 1.0),
        'hgrn_gnorm': gain((N_ODD, HGRN_DV)),
        'w_up': normal((DEPTH, D_MODEL, D_FF), D_MODEL ** -0.5),
        'w_down': normal((DEPTH, D_FF, D_MODEL), D_FF ** -0.5),
    }


def reference(x_prompt, x_sample, cache_swa_k, cache_swa_v, state_s5_re, state_s5_im, state_hgrn,
              t5_bias_table, norm_mix_pre, norm_mix_post, norm_mlp_pre, norm_mlp_post,
              w_in_even, w_out_even, s5_lambda_re, s5_lambda_im, s5_log_step, s5_b_re, s5_b_im,
              s5_c_re, s5_c_im, s5_d, s5_w_glu, swa_sinks, w_in_odd, w_out_odd, hgrn_lb_param,
              hgrn_gnorm, w_up, w_down):
    params = dict(t5_bias_table=t5_bias_table, norm_mix_pre=norm_mix_pre, norm_mix_post=norm_mix_post,
                  norm_mlp_pre=norm_mlp_pre, norm_mlp_post=norm_mlp_post, w_in_even=w_in_even,
                  w_out_even=w_out_even, s5_lambda_re=s5_lambda_re, s5_lambda_im=s5_lambda_im,
                  s5_log_step=s5_log_step, s5_b_re=s5_b_re, s5_b_im=s5_b_im, s5_c_re=s5_c_re,
                  s5_c_im=s5_c_im, s5_d=s5_d, s5_w_glu=s5_w_glu, swa_sinks=swa_sinks,
                  w_in_odd=w_in_odd, w_out_odd=w_out_odd, hgrn_lb_param=hgrn_lb_param,
                  hgrn_gnorm=hgrn_gnorm, w_up=w_up, w_down=w_down)
    y_prompt, k_p, v_p, re_p, im_p, hg_p = trunk(x_prompt, None, None, None, None, None, params)
    y_sample, k_s, v_s, re_s, im_s, hg_s = trunk(x_sample, cache_swa_k, cache_swa_v, state_s5_re,
                                                 state_s5_im, state_hgrn, params)
    return (y_prompt, y_sample, k_p, v_p, k_s, v_s, re_p, im_p, re_s, im_s, hg_p, hg_s)
```

```python
import functools
import math

import numpy as np
import jax
import jax.numpy as jnp
from jax import lax
from jax.experimental import pallas as pl
from jax.experimental.pallas import tpu as pltpu

F32 = jnp.float32
BF16 = jnp.bfloat16

D_MODEL = 1024
S5_WIDTH = 512
S5_GROUPS = 32
S5_GROUP = 16
S5_STATE = 64
S5_HALF_STATES = 1024
S5_HALF_CH = 256
SWA_WIDTH = 512
HEAD_DIM = 64
N_HEADS = 8
N_KV = 2
KV_GROUP = N_HEADS // N_KV
WINDOW = 128
T5_BUCKETS = 32
T5_MAX_DIST = 128
HG_HEADS = 8
HG_DK = 128
HG_DV = 128
HG_W = HG_HEADS * HG_DK
D_FF = 4096
RMS_EPS = 1e-6
NEG_INF = -1e30
LOG_FLOOR = 1e-30

S5_CHUNK = 64
HG_CHUNK = 64
HG_BASE = 16
TOKEN_BLOCK = 512
MLP_FF_BLOCK = 1024
SAMPLE_SEQ_BLOCK = 8
VMEM_LIMIT = 56 * 1024 * 1024


def _cparams(*sem):
    return pltpu.CompilerParams(dimension_semantics=sem, vmem_limit_bytes=VMEM_LIMIT)


def _resident(shape):
    nd = len(shape)
    return pl.BlockSpec(shape, lambda *_: (0,) * nd, pipeline_mode=pl.Buffered(1))


def _rms(x, w):
    return x * lax.rsqrt(jnp.mean(x * x, axis=-1, keepdims=True) + RMS_EPS) * w


def _sigmoid(x):
    return 1.0 / (1.0 + jnp.exp(-x))


def _silu(x):
    return x * _sigmoid(x)


def _log_sigmoid(x):
    return jnp.minimum(x, 0.0) - jnp.log1p(jnp.exp(-jnp.abs(x)))


def _gelu_tanh(x):
    return 0.5 * x * (1.0 + jnp.tanh(math.sqrt(2.0 / math.pi) * (x + 0.044715 * (x * x * x))))


def _dot(a, b):
    return jnp.dot(a, b, preferred_element_type=F32)


def _dot_nt(a, b):
    return lax.dot_general(a, b, (((1,), (1,)), ((), ())), preferred_element_type=F32)


def _dot_tn(a, b):
    return lax.dot_general(a, b, (((0,), (0,)), ((), ())), preferred_element_type=F32)


def _dot_exact(a, b):
    return jnp.dot(a, b, preferred_element_type=F32, precision=lax.Precision.HIGHEST)


def _norm_proj_body(x_ref, g_ref, w_ref, *o_refs, splits):
    xn = _rms(x_ref[...], g_ref[...]).astype(BF16)
    off = 0
    for o_ref, wd in zip(o_refs, splits):
        o_ref[...] = _dot(xn, w_ref[:, off:off + wd])
        off += wd


def _norm_proj(x2d, g, w_bf, splits, tm):
    t, n = x2d.shape[0], w_bf.shape[1]
    return pl.pallas_call(
        functools.partial(_norm_proj_body, splits=splits),
        grid=(t // tm,),
        in_specs=[pl.BlockSpec((tm, D_MODEL), lambda i: (i, 0)),
                  _resident((1, D_MODEL)), _resident((D_MODEL, n))],
        out_specs=[pl.BlockSpec((tm, wd), lambda i: (i, 0)) for wd in splits],
        out_shape=[jax.ShapeDtypeStruct((t, wd), F32) for wd in splits],
        compiler_params=_cparams("parallel"),
        name="norm_proj",
    )(x2d, g, w_bf)


def _post_body(*refs, widths):
    x_ref = refs[0]
    in_refs = refs[1:1 + len(widths)]
    wout_ref, gpost_ref, gpre_ref, gmpost_ref, wup_ref, wdn_ref, o_ref = refs[1 + len(widths):]
    mix = None
    off = 0
    for r, wd in zip(in_refs, widths):
        t = _dot(r[...].astype(BF16), wout_ref[off:off + wd, :])
        mix = t if mix is None else mix + t
        off += wd
    x1 = x_ref[...] + _rms(mix, gpost_ref[...])
    hn = _rms(x1, gpre_ref[...]).astype(BF16)
    acc = None
    for c in range(D_FF // MLP_FF_BLOCK):
        cs = slice(c * MLP_FF_BLOCK, (c + 1) * MLP_FF_BLOCK)
        hk = jnp.square(jnp.maximum(_dot(hn, wup_ref[:, cs]), 0.0)).astype(BF16)
        t = _dot(hk, wdn_ref[cs, :])
        acc = t if acc is None else acc + t
    o_ref[...] = x1 + _rms(acc, gmpost_ref[...])


def _post(x2d, mixer_outs, wout_bf, gpost, gpre, gmpost, wup_bf, wdn_bf, tm):
    t = x2d.shape[0]
    widths = tuple(a.shape[1] for a in mixer_outs)
    row = lambda wd: pl.BlockSpec((tm, wd), lambda i: (i, 0))
    return pl.pallas_call(
        functools.partial(_post_body, widths=widths),
        grid=(t // tm,),
        in_specs=[row(D_MODEL)] + [row(wd) for wd in widths] + [
            _resident((D_MODEL, D_MODEL)), _resident((1, D_MODEL)), _resident((1, D_MODEL)),
            _resident((1, D_MODEL)), _resident((D_MODEL, D_FF)), _resident((D_FF, D_MODEL))],
        out_specs=row(D_MODEL),
        out_shape=jax.ShapeDtypeStruct((t, D_MODEL), F32),
        compiler_params=_cparams("parallel"),
        name="post_mlp",
    )(x2d, *mixer_outs, wout_bf, gpost, gpre, gmpost, wup_bf, wdn_bf)


def _bucket_map():
    t = np.arange(WINDOW)[:, None]
    m = np.arange(2 * WINDOW)[None, :]
    n = np.maximum(WINDOW + t - m, 0)
    max_exact = T5_BUCKETS // 2
    large = max_exact + np.floor(np.log(np.maximum(n, max_exact) / max_exact)
                                 / math.log(T5_MAX_DIST / max_exact) * (T5_BUCKETS - max_exact)).astype(np.int64)
    large = np.minimum(large, T5_BUCKETS - 1)
    return np.where(n < max_exact, n, large).astype(np.int32)


def _bias_body(tab_ref, bm_ref, o_ref):
    bm = bm_ref[...]
    for h in range(N_HEADS):
        acc = jnp.zeros(bm.shape, F32)
        for b in range(T5_BUCKETS):
            acc = jnp.where(bm == b, tab_ref[b, h], acc)
        o_ref[h] = acc


def _t5_band(table):
    return pl.pallas_call(
        _bias_body,
        in_specs=[pl.BlockSpec(memory_space=pltpu.SMEM),
                  pl.BlockSpec((WINDOW, 2 * WINDOW), lambda: (0, 0))],
        out_specs=pl.BlockSpec((N_HEADS, WINDOW, 2 * WINDOW), lambda: (0, 0, 0)),
        out_shape=jax.ShapeDtypeStruct((N_HEADS, WINDOW, 2 * WINDOW), F32),
        name="t5_band",
    )(table, jnp.asarray(_bucket_map()))


def _sink_softmax(parts, sink):
    m = sink
    for s in parts:
        m = jnp.maximum(jnp.max(s, axis=-1, keepdims=True), m)
    es = [jnp.exp(s - m) for s in parts]
    den = jnp.exp(sink - m)
    for e in es:
        den = den + jnp.sum(e, axis=-1, keepdims=True)
    return [e / den for e in es]


def _swa_prompt_body(sink_ref, q_ref, kc_ref, kp_ref, vc_ref, vp_ref, bias_ref, o_ref):
    n = pl.program_id(1)
    row = lax.broadcasted_iota(jnp.int32, (WINDOW, 2 * WINDOW), 0)
    col = lax.broadcasted_iota(jnp.int32, (WINDOW, 2 * WINDOW), 1)
    dist = WINDOW + row - col
    valid = (dist >= 0) & (dist < WINDOW) & jnp.logical_or(col >= WINDOW, n > 0)
    kb = jnp.concatenate([kp_ref[...], kc_ref[...]], axis=0).astype(BF16)
    vb = jnp.concatenate([vp_ref[...], vc_ref[...]], axis=0).astype(BF16)
    q = q_ref[...]
    for h in range(N_HEADS):
        ks = slice((h // KV_GROUP) * HEAD_DIM, (h // KV_GROUP + 1) * HEAD_DIM)
        hs = slice(h * HEAD_DIM, (h + 1) * HEAD_DIM)
        s = _dot_nt(q[:, hs].astype(BF16), kb[:, ks]) * (HEAD_DIM ** -0.5)
        s = jnp.where(valid, s + bias_ref[h], NEG_INF)
        (p,) = _sink_softmax([s], sink_ref[h])
        o_ref[:, hs] = _dot(p.astype(BF16), vb[:, ks])


def _swa_prompt(q, k, v, sinks, band):
    b, l, _ = q.shape
    kvw = N_KV * HEAD_DIM
    cur = lambda w: pl.BlockSpec((None, WINDOW, w), lambda i, n: (i, n, 0))
    prev = lambda w: pl.BlockSpec((None, WINDOW, w), lambda i, n: (i, jnp.maximum(n - 1, 0), 0))
    return pl.pallas_call(
        _swa_prompt_body,
        grid=(b, l // WINDOW),
        in_specs=[pl.BlockSpec(memory_space=pltpu.SMEM), cur(SWA_WIDTH), cur(kvw), prev(kvw), cur(kvw), prev(kvw),
                  _resident((N_HEADS, WINDOW, 2 * WINDOW))],
        out_specs=cur(SWA_WIDTH),
        out_shape=jax.ShapeDtypeStruct((b, l, SWA_WIDTH), F32),
        compiler_params=_cparams("parallel", "arbitrary"),
        name="swa_prompt",
    )(sinks, q, k, k, v, v, band)


def _swa_sample_body(sink_ref, q_ref, kn_ref, vn_ref, ck_ref, cv_ref, bias_ref, o_ref, ko_ref, vo_ref, *, steps):
    ck, cv, kn, vn = ck_ref[...], cv_ref[...], kn_ref[...], vn_ref[...]
    ko_ref[:, 0:WINDOW - steps, :] = ck[:, steps:, :]
    ko_ref[:, WINDOW - steps:, :] = kn
    vo_ref[:, 0:WINDOW - steps, :] = cv[:, steps:, :]
    vo_ref[:, WINDOW - steps:, :] = vn
    valid_c = (lax.broadcasted_iota(jnp.int32, (steps, WINDOW), 1)
               > lax.broadcasted_iota(jnp.int32, (steps, WINDOW), 0))[None]
    valid_n = (lax.broadcasted_iota(jnp.int32, (steps, steps), 1)
               <= lax.broadcasted_iota(jnp.int32, (steps, steps), 0))[None]
    ckb, cvb, knb, vnb = (a.astype(BF16) for a in (ck, cv, kn, vn))
    q = q_ref[...]
    scale = HEAD_DIM ** -0.5
    for h in range(N_HEADS):
        ks = slice((h // KV_GROUP) * HEAD_DIM, (h // KV_GROUP + 1) * HEAD_DIM)
        hs = slice(h * HEAD_DIM, (h + 1) * HEAD_DIM)
        qh = q[:, :, hs].astype(BF16)
        sc = jnp.einsum("bqd,bkd->bqk", qh, ckb[:, :, ks], preferred_element_type=F32) * scale
        sn = jnp.einsum("bqd,bkd->bqk", qh, knb[:, :, ks], preferred_element_type=F32) * scale
        sc = jnp.where(valid_c, sc + bias_ref[h, 0:steps, 0:WINDOW][None], NEG_INF)
        sn = jnp.where(valid_n, sn + bias_ref[h, 0:steps, WINDOW:WINDOW + steps][None], NEG_INF)
        pc, pn = _sink_softmax([sc, sn], sink_ref[h])
        o_ref[:, :, hs] = (jnp.einsum("bqk,bkd->bqd", pc.astype(BF16), cvb[:, :, ks], preferred_element_type=F32)
                           + jnp.einsum("bqk,bkd->bqd", pn.astype(BF16), vnb[:, :, ks], preferred_element_type=F32))


def _swa_sample(q, kn, vn, ck, cv, sinks, band):
    b, steps, _ = q.shape
    kvw = N_KV * HEAD_DIM
    bb = SAMPLE_SEQ_BLOCK
    blk = lambda r, w: pl.BlockSpec((bb, r, w), lambda i: (i, 0, 0))
    return pl.pallas_call(
        functools.partial(_swa_sample_body, steps=steps),
        grid=(b // bb,),
        in_specs=[pl.BlockSpec(memory_space=pltpu.SMEM), blk(steps, SWA_WIDTH), blk(steps, kvw), blk(steps, kvw),
                  blk(WINDOW, kvw), blk(WINDOW, kvw), _resident((N_HEADS, WINDOW, 2 * WINDOW))],
        out_specs=[blk(steps, SWA_WIDTH), blk(WINDOW, kvw), blk(WINDOW, kvw)],
        out_shape=[jax.ShapeDtypeStruct((b, steps, SWA_WIDTH), F32),
                   jax.ShapeDtypeStruct((b, WINDOW, kvw), F32),
                   jax.ShapeDtypeStruct((b, WINDOW, kvw), F32)],
        compiler_params=_cparams("parallel"),
        name="swa_sample",
    )(sinks, q, kn, vn, ck, cv, band)


def _s5_prep_body(lr_ref, li_ref, ls_ref, br_ref, bi_ref, cr_ref, ci_ref, bm_ref, cm_ref, pw_ref, pinv_ref):
    hs = S5_HALF_STATES
    lr, li = lr_ref[...], li_ref[...]
    dt = jnp.exp(ls_ref[...])
    mag = jnp.exp(lr * dt)
    ang = li * dt
    ar = mag * jnp.cos(ang)
    ai = mag * jnp.sin(ang)
    den = lr * lr + li * li
    nr = ar - 1.0
    kr = (nr * lr + ai * li) / den
    ki = (ai * lr - nr * li) / den
    br, bi = br_ref[...], bi_ref[...]
    bm_ref[:, 0:hs] = (kr * br - ki * bi).astype(BF16)
    bm_ref[:, hs:2 * hs] = (kr * bi + ki * br).astype(BF16)
    cm_ref[0:hs, :] = cr_ref[...].astype(BF16)
    cm_ref[hs:2 * hs, :] = (-ci_ref[...]).astype(BF16)
    n2 = ar * ar + ai * ai
    ir, ii = ar / n2, -ai / n2
    pr, pi = jnp.ones_like(ar), jnp.zeros_like(ar)
    qr, qi = pr, pi
    for j in range(S5_CHUNK):
        pw_ref[0, j:j + 1, :] = pr
        pw_ref[1, j:j + 1, :] = pi
        pinv_ref[0, j:j + 1, :] = qr
        pinv_ref[1, j:j + 1, :] = qi
        pr, pi = pr * ar - pi * ai, pr * ai + pi * ar
        qr, qi = qr * ir - qi * ii, qr * ii + qi * ir


def _s5_prep(lam_re, lam_im, log_step, b_re, b_im, c_re, c_im):
    hs, hc = S5_HALF_STATES, S5_HALF_CH
    gh = S5_GROUPS // 2
    eye = jnp.eye(gh, dtype=bool)
    vec = lambda a: a.reshape(2, 1, hs)
    ls = jnp.broadcast_to(log_step[:, None], (S5_GROUPS, S5_STATE))

    def b_blockdiag(b):
        bt = b.reshape(2, gh, S5_STATE, S5_GROUP).transpose(0, 3, 1, 2)[:, None]
        full = jnp.where(eye[None, :, None, :, None], bt, 0.0)
        return full.reshape(2, hc, hs)

    def c_blockdiag(c):
        ct = c.reshape(2, gh, S5_GROUP, S5_STATE).transpose(0, 1, 3, 2)[:, :, :, None]
        full = jnp.where(eye[None, :, None, :, None], ct, 0.0)
        return full.reshape(2, hs, hc)

    vspec = pl.BlockSpec((None, 1, hs), lambda h: (h, 0, 0))
    bspec = pl.BlockSpec((None, hc, hs), lambda h: (h, 0, 0))
    cspec = pl.BlockSpec((None, hs, hc), lambda h: (h, 0, 0))
    pspec = pl.BlockSpec((None, 2, S5_CHUNK, hs), lambda h: (h, 0, 0, 0))
    return pl.pallas_call(
        _s5_prep_body,
        grid=(2,),
        in_specs=[vspec, vspec, vspec, bspec, bspec, cspec, cspec],
        out_specs=[pl.BlockSpec((None, hc, 2 * hs), lambda h: (h, 0, 0)),
                   pl.BlockSpec((None, 2 * hs, hc), lambda h: (h, 0, 0)), pspec, pspec],
        out_shape=[jax.ShapeDtypeStruct((2, hc, 2 * hs), BF16), jax.ShapeDtypeStruct((2, 2 * hs, hc), BF16),
                   jax.ShapeDtypeStruct((2, 2, S5_CHUNK, hs), F32), jax.ShapeDtypeStruct((2, 2, S5_CHUNK, hs), F32)],
        compiler_params=_cparams("parallel"),
        name="s5_prep",
    )(vec(lam_re), vec(lam_im), vec(ls), b_blockdiag(b_re), b_blockdiag(b_im), c_blockdiag(c_re), c_blockdiag(c_im))


def _s5_glu(y, u, d_ref, wg_ref):
    g = _gelu_tanh(y + d_ref[...] * u)
    return g * _sigmoid(_dot(g.astype(BF16), wg_ref[...]))


def _s5_prompt_body(u_ref, bm_ref, cm_ref, pw_ref, pinv_ref, d_ref, wg_ref, o_ref, hfin_ref, bu_scr, h_scr, *, tb):
    hs, hc, c0 = S5_HALF_STATES, S5_HALF_CH, S5_CHUNK

    @pl.when(pl.program_id(1) == 0)
    def _():
        h_scr[...] = jnp.zeros_like(h_scr)

    u = u_ref[...]
    ub = u.astype(BF16)
    for hf in range(2):
        bu_scr[:, 2 * hf * hs:2 * (hf + 1) * hs] = _dot(ub[:, hf * hc:(hf + 1) * hc], bm_ref[hf])
    tril = (lax.broadcasted_iota(jnp.int32, (c0, c0), 0) >= lax.broadcasted_iota(jnp.int32, (c0, c0), 1)).astype(BF16)

    def chunk(ci, carry):
        rows = pl.ds(pl.multiple_of(ci * c0, c0), c0)
        for hf in range(2):
            re = slice(2 * hf * hs, (2 * hf + 1) * hs)
            im = slice((2 * hf + 1) * hs, (2 * hf + 2) * hs)
            bur, bui = bu_scr[rows, re], bu_scr[rows, im]
            qr, qi = pinv_ref[hf, 0], pinv_ref[hf, 1]
            cr = _dot(tril, (qr * bur - qi * bui).astype(BF16))
            ci_ = _dot(tril, (qr * bui + qi * bur).astype(BF16))
            hr0, hi0 = h_scr[:, re], h_scr[:, im]
            ar, ai = pw_ref[hf, 0, 1:2, :], pw_ref[hf, 1, 1:2, :]
            sr = cr + (ar * hr0 - ai * hi0)
            si = ci_ + (ar * hi0 + ai * hr0)
            pr, pi = pw_ref[hf, 0], pw_ref[hf, 1]
            hr = pr * sr - pi * si
            hi = pr * si + pi * sr
            bu_scr[rows, re] = hr
            bu_scr[rows, im] = hi
            h_scr[:, re] = hr[c0 - 1:c0, :]
            h_scr[:, im] = hi[c0 - 1:c0, :]
        return carry

    lax.fori_loop(0, tb // c0, chunk, 0)
    y = jnp.concatenate([_dot(bu_scr[:, 2 * hf * hs:2 * (hf + 1) * hs].astype(BF16), cm_ref[hf]) for hf in range(2)],
                        axis=1)
    o_ref[...] = _s5_glu(y, u, d_ref, wg_ref)
    hfin_ref[...] = h_scr[...]


def _s5_prompt(u, prep, d_skip, wglu_bf, tb):
    b, l, _ = u.shape
    bm, cm, pw, pinv = prep
    ns = 4 * S5_HALF_STATES
    return pl.pallas_call(
        functools.partial(_s5_prompt_body, tb=tb),
        grid=(b, l // tb),
        in_specs=[pl.BlockSpec((None, tb, S5_WIDTH), lambda i, j: (i, j, 0)),
                  _resident(bm.shape), _resident(cm.shape), _resident(pw.shape), _resident(pinv.shape),
                  _resident((1, S5_WIDTH)), _resident((S5_WIDTH, S5_WIDTH))],
        out_specs=[pl.BlockSpec((None, tb, S5_WIDTH), lambda i, j: (i, j, 0)),
                   pl.BlockSpec((None, 1, ns), lambda i, j: (i, 0, 0))],
        out_shape=[jax.ShapeDtypeStruct((b, l, S5_WIDTH), F32), jax.ShapeDtypeStruct((b, 1, ns), F32)],
        scratch_shapes=[pltpu.VMEM((tb, ns), F32), pltpu.VMEM((1, ns), F32)],
        compiler_params=_cparams("parallel", "arbitrary"),
        name="s5_prompt",
    )(u, bm, cm, pw, pinv, d_skip, wglu_bf)


def _s5_sample_body(u_ref, h0_ref, bm_ref, cm_ref, pw_ref, d_ref, wg_ref, o_ref, hfin_ref, *, steps):
    hs, hc = S5_HALF_STATES, S5_HALF_CH
    h = [h0_ref[:, k * hs:(k + 1) * hs] for k in range(4)]
    for t in range(steps):
        u = u_ref[t]
        ub = u.astype(BF16)
        ys = []
        for hf in range(2):
            bu = _dot(ub[:, hf * hc:(hf + 1) * hc], bm_ref[hf])
            ar, ai = pw_ref[hf, 0, 1:2, :], pw_ref[hf, 1, 1:2, :]
            hr, hi = h[2 * hf], h[2 * hf + 1]
            h[2 * hf] = ar * hr - ai * hi + bu[:, 0:hs]
            h[2 * hf + 1] = ar * hi + ai * hr + bu[:, hs:2 * hs]
            ys.append(_dot(jnp.concatenate([h[2 * hf], h[2 * hf + 1]], axis=1).astype(BF16), cm_ref[hf]))
        o_ref[t] = _s5_glu(jnp.concatenate(ys, axis=1), u, d_ref, wg_ref)
    for k in range(4):
        hfin_ref[:, k * hs:(k + 1) * hs] = h[k]


def _s5_sample(u_t, h0, prep, d_skip, wglu_bf):
    steps, b, _ = u_t.shape
    bm, cm, pw, _ = prep
    ns = 4 * S5_HALF_STATES
    full = lambda shape: pl.BlockSpec(shape, lambda: (0,) * len(shape))
    return pl.pallas_call(
        functools.partial(_s5_sample_body, steps=steps),
        in_specs=[full(u_t.shape), full(h0.shape), full(bm.shape), full(cm.shape), full(pw.shape),
                  full((1, S5_WIDTH)), full((S5_WIDTH, S5_WIDTH))],
        out_specs=[full((steps, b, S5_WIDTH)), full((b, ns))],
        out_shape=[jax.ShapeDtypeStruct((steps, b, S5_WIDTH), F32), jax.ShapeDtypeStruct((b, ns), F32)],
        compiler_params=pltpu.CompilerParams(vmem_limit_bytes=VMEM_LIMIT),
        name="s5_sample",
    )(u_t, h0, bm, cm, pw, d_skip, wglu_bf)


def _s5_state_in(re, im):
    b = re.shape[0]
    return jnp.stack([re.reshape(b, 2, S5_HALF_STATES), im.reshape(b, 2, S5_HALF_STATES)], axis=2).reshape(b, -1)


def _s5_state_out(h):
    b = h.shape[0]
    h4 = h.reshape(b, 2, 2, S5_HALF_STATES)
    return (h4[:, :, 0].reshape(b, S5_GROUPS, S5_STATE), h4[:, :, 1].reshape(b, S5_GROUPS, S5_STATE))


def _lb_body(p_ref, o_ref):
    p = p_ref[...]
    e = jnp.exp(p - jnp.max(p, axis=0, keepdims=True))
    sm = e / jnp.sum(e, axis=0, keepdims=True)
    acc = jnp.zeros_like(sm[0:1])
    for i in range(p.shape[0]):
        acc = acc + sm[i:i + 1]
        o_ref[i:i + 1, :] = acc - sm[0:1]


def _hgrn_lower_bounds(lb_param):
    n = lb_param.shape[0]
    return pl.pallas_call(
        _lb_body,
        in_specs=[pl.BlockSpec((n, HG_W), lambda: (0, 0))],
        out_specs=pl.BlockSpec((n, HG_W), lambda: (0, 0)),
        out_shape=jax.ShapeDtypeStruct((n, HG_W), F32),
        name="hgrn_lower_bounds",
    )(lb_param)


def _hgrn_gates(q, fz, lb):
    a = jnp.log(jnp.maximum(lb, LOG_FLOOR))
    b = jnp.log1p(-lb) + _log_sigmoid(fz)
    log_f = jnp.maximum(a, b) + jnp.log1p(jnp.exp(-jnp.abs(a - b)))
    return _silu(q), log_f, (1.0 - lb) * _sigmoid(-fz)


def _group_norm_gate(o, gn, gate):
    return o * lax.rsqrt(jnp.mean(o * o, axis=-1, keepdims=True) + RMS_EPS) * gn * gate


def _row_of_block(b, rowi, stride, offset):
    n = b.shape[0] // stride
    out = b[(n - 1) * stride + offset:(n - 1) * stride + offset + 1, :]
    for m in range(n - 2, -1, -1):
        out = jnp.where(rowi < (m + 1) * stride, b[m * stride + offset:m * stride + offset + 1, :], out)
    return out


def _hgrn_prompt_body(x_ref, g_ref, w_ref, lb_ref, gn_ref, o_ref, st_ref, q_scr, k_scr, lf_scr, v_scr, s_scr, *, tb):
    c, w = HG_CHUNK, HG_W

    @pl.when(pl.program_id(1) == 0)
    def _():
        s_scr[...] = jnp.zeros_like(s_scr)

    xn = _rms(x_ref[...], g_ref[...]).astype(BF16)
    qf, log_f, kf = _hgrn_gates(_dot(xn, w_ref[:, 0:w]), _dot(xn, w_ref[:, w:2 * w]), lb_ref[...])
    q_scr[...] = qf
    lf_scr[...] = log_f
    k_scr[...] = kf
    v_scr[...] = _dot(xn, w_ref[:, 2 * w:3 * w])
    o_ref[...] = _silu(_dot(xn, w_ref[:, 3 * w:4 * w]))

    row = lax.broadcasted_iota(jnp.int32, (c, c), 0)
    col = lax.broadcasted_iota(jnp.int32, (c, c), 1)
    tril = (row >= col).astype(F32)
    rowi = lax.broadcasted_iota(jnp.int32, (c, 1), 0)
    base_mask = (row // HG_BASE == col // HG_BASE) & (col <= row)
    strides = []
    s = 2 * HG_BASE
    while s <= c:
        strides.append(s)
        s *= 2

    def chunk(ci, carry):
        rows = pl.ds(pl.multiple_of(ci * c, c), c)
        q, k, v = q_scr[rows, :], k_scr[rows, :], v_scr[rows, :]
        b = _dot_exact(tril, lf_scr[rows, :])
        e0 = b - _row_of_block(b, rowi, HG_BASE, 0)
        qs = [(q * jnp.exp(e0)).astype(BF16)]
        ks = [(k * jnp.exp(-e0)).astype(BF16)]
        masks = [base_mask]
        for st in strides:
            mid = _row_of_block(b, rowi, st, st // 2)
            upper = (rowi % st) >= st // 2
            qs.append((q * jnp.exp(jnp.where(upper, b - mid, NEG_INF))).astype(BF16))
            ks.append((k * jnp.exp(jnp.where(upper, NEG_INF, mid - b))).astype(BF16))
            masks.append(row // st == col // st)
        b_last = b[c - 1:c, :]
        q_in = (q * jnp.exp(b)).astype(BF16)
        k_st = (k * jnp.exp(b_last - b)).astype(BF16)
        e_last = jnp.exp(b_last)
        vb = v.astype(BF16)
        for h in range(HG_HEADS):
            hs = slice(h * HG_DK, (h + 1) * HG_DK)
            a = None
            for qh, kh, mk in zip(qs, ks, masks):
                t = jnp.where(mk, _dot_nt(qh[:, hs], kh[:, hs]), 0.0)
                a = t if a is None else a + t
            st_t = s_scr[h]
            o = _dot(a.astype(BF16), vb[:, hs]) + _dot_nt(q_in[:, hs], st_t.astype(BF16))
            s_scr[h] = st_t * e_last[:, hs] + _dot_tn(vb[:, hs], k_st[:, hs])
            o_ref[rows, hs] = _group_norm_gate(o, gn_ref[...], o_ref[rows, hs])
        return carry

    lax.fori_loop(0, tb // c, chunk, 0)
    st_ref[...] = s_scr[...]


def _hgrn_prompt(x, g, w_bf, lb, gn, tb):
    b, l, _ = x.shape
    row = pl.BlockSpec((None, tb, D_MODEL), lambda i, j: (i, j, 0))
    return pl.pallas_call(
        functools.partial(_hgrn_prompt_body, tb=tb),
        grid=(b, l // tb),
        in_specs=[row, _resident((1, D_MODEL)), _resident((D_MODEL, 4 * HG_W)), _resident((1, HG_W)),
                  _resident((1, HG_DV))],
        out_specs=[row, pl.BlockSpec((None, HG_HEADS, HG_DV, HG_DK), lambda i, j: (i, 0, 0, 0))],
        out_shape=[jax.ShapeDtypeStruct((b, l, HG_W), F32),
                   jax.ShapeDtypeStruct((b, HG_HEADS, HG_DV, HG_DK), F32)],
        scratch_shapes=[pltpu.VMEM((tb, HG_W), F32)] * 4 + [pltpu.VMEM((HG_HEADS, HG_DV, HG_DK), F32)],
        compiler_params=_cparams("parallel", "arbitrary"),
        name="hgrn_prompt",
    )(x, g, w_bf, lb, gn)


def _hgrn_sample_body(q_ref, fz_ref, iv_ref, gz_ref, s_ref, lb_ref, gn_ref, o_ref, so_ref, *, steps, bb):
    n = bb * steps
    qf, log_f, kf = _hgrn_gates(q_ref[...], fz_ref[...], lb_ref[...])
    v = iv_ref[...]
    gate = _silu(gz_ref[...])
    row = lax.broadcasted_iota(jnp.int32, (n, n), 0)
    col = lax.broadcasted_iota(jnp.int32, (n, n), 1)
    same = row // steps == col // steps
    causal = same & (col <= row)
    b = _dot_exact(causal.astype(F32), log_f)
    total = _dot_exact(same.astype(F32), log_f)
    first = (same & (col % steps == 0)).astype(F32)
    e0 = b - _dot_exact(first, log_f)
    q0 = (qf * jnp.exp(e0)).astype(BF16)
    k0 = (kf * jnp.exp(-e0)).astype(BF16)
    q_in = (qf * jnp.exp(b)).astype(BF16)
    k_st = (kf * jnp.exp(total - b)).astype(BF16)
    vb = v.astype(BF16)
    pick = (lax.broadcasted_iota(jnp.int32, (bb, n), 1) == steps * lax.broadcasted_iota(jnp.int32, (bb, n), 0))
    e_seq = jnp.exp(_dot_exact(pick.astype(F32), total))
    e_rows = jnp.concatenate([e_seq[:, h * HG_DK:(h + 1) * HG_DK] for h in range(HG_HEADS)], axis=0)
    pad = HG_DK - e_rows.shape[0]
    if pad:
        e_rows = jnp.concatenate([e_rows, jnp.zeros((pad, HG_DK), F32)], axis=0)
    e_cols = e_rows.T
    for h in range(HG_HEADS):
        hs = slice(h * HG_DK, (h + 1) * HG_DK)
        a = jnp.where(causal, _dot_nt(q0[:, hs], k0[:, hs]), 0.0)
        o_intra = _dot(a.astype(BF16), vb[:, hs])
        outs = []
        for i in range(bb):
            rs = slice(i * steps, (i + 1) * steps)
            s_old = s_ref[i, h]
            outs.append(_dot(q_in[rs, hs], s_old.astype(BF16)))
            so_ref[i, h] = (s_old * e_cols[:, h * bb + i:h * bb + i + 1]
                            + _dot_tn(k_st[rs, hs], vb[rs, hs]))
        o = o_intra + jnp.concatenate(outs, axis=0)
        o_ref[:, hs] = _group_norm_gate(o, gn_ref[...], gate[:, hs])


def _hgrn_sample(proj4, state, lb, gn, steps):
    b = state.shape[0]
    bb = SAMPLE_SEQ_BLOCK
    assert HG_HEADS * bb <= HG_DK
    rows = pl.BlockSpec((bb * steps, HG_W), lambda i: (i, 0))
    sspec = pl.BlockSpec((bb, HG_HEADS, HG_DK, HG_DV), lambda i: (i, 0, 0, 0))
    return pl.pallas_call(
        functools.partial(_hgrn_sample_body, steps=steps, bb=bb),
        grid=(b // bb,),
        in_specs=[rows, rows, rows, rows, sspec, _resident((1, HG_W)), _resident((1, HG_DV))],
        out_specs=[rows, sspec],
        out_shape=[jax.ShapeDtypeStruct((b * steps, HG_W), F32),
                   jax.ShapeDtypeStruct((b, HG_HEADS, HG_DK, HG_DV), F32)],
        compiler_params=_cparams("parallel"),
        name="hgrn_sample",
    )(*proj4, state, lb, gn)


def _trunk(x, cache_k, cache_v, st_re, st_im, st_h, p):
    prompt = cache_k is None
    bt, l, _ = x.shape
    tm = min(TOKEN_BLOCK, bt * l)
    depth = p["norm_mix_pre"].shape[0]
    kvw = N_KV * HEAD_DIM
    row1 = lambda a: a.reshape(1, -1)
    x2 = x.reshape(bt * l, D_MODEL)
    k_out, v_out, re_out, im_out, h_out = [], [], [], [], []
    for layer in range(depth):
        idx = layer // 2
        g_pre = row1(p["norm_mix_pre"][layer])
        if layer % 2 == 0:
            u, q, k, v = _norm_proj(x2, g_pre, p["w_in_even"][idx], (S5_WIDTH, SWA_WIDTH, kvw, kvw), tm)
            prep = p["s5_prep"][idx]
            d_skip, wglu = row1(p["s5_d"][idx]), p["s5_w_glu"][idx]
            q3, k3, v3 = q.reshape(bt, l, SWA_WIDTH), k.reshape(bt, l, kvw), v.reshape(bt, l, kvw)
            if prompt:
                a_out, h_fin = _s5_prompt(u.reshape(bt, l, S5_WIDTH), prep, d_skip, wglu, min(TOKEN_BLOCK, l))
                h_fin = h_fin.reshape(bt, -1)
                b_out = _swa_prompt(q3, k3, v3, p["swa_sinks"][idx], p["t5_band"])
                kw, vw = k3[:, l - WINDOW:], v3[:, l - WINDOW:]
            else:
                u_t = u.reshape(bt, l, S5_WIDTH).transpose(1, 0, 2)
                a_t, h_fin = _s5_sample(u_t, _s5_state_in(st_re[idx], st_im[idx]), prep, d_skip, wglu)
                a_out = a_t.transpose(1, 0, 2)
                b_out, kw, vw = _swa_sample(q3, k3, v3, cache_k[idx].reshape(bt, WINDOW, kvw),
                                            cache_v[idx].reshape(bt, WINDOW, kvw), p["swa_sinks"][idx], p["t5_band"])
            h_re, h_im = _s5_state_out(h_fin)
            k_out.append(kw.reshape(bt, WINDOW, N_KV, HEAD_DIM))
            v_out.append(vw.reshape(bt, WINDOW, N_KV, HEAD_DIM))
            re_out.append(h_re)
            im_out.append(h_im)
            mixer_outs = (a_out.reshape(bt * l, S5_WIDTH), b_out.reshape(bt * l, SWA_WIDTH))
            w_out = p["w_out_even"][idx]
        else:
            lb, gn = p["hgrn_lb"][idx:idx + 1], row1(p["hgrn_gnorm"][idx])
            if prompt:
                c_out, s_t = _hgrn_prompt(x2.reshape(bt, l, D_MODEL), g_pre, p["w_in_odd"][idx], lb, gn,
                                          min(TOKEN_BLOCK, l))
                s_new = s_t.transpose(0, 1, 3, 2)
            else:
                proj4 = _norm_proj(x2, g_pre, p["w_in_odd"][idx], (HG_W,) * 4, tm)
                c_out, s_new = _hgrn_sample(proj4, st_h[idx], lb, gn, l)
            h_out.append(s_new)
            mixer_outs = (c_out.reshape(bt * l, HG_W),)
            w_out = p["w_out_odd"][idx]
        x2 = _post(x2, mixer_outs, w_out, row1(p["norm_mix_post"][layer]), row1(p["norm_mlp_pre"][layer]),
                   row1(p["norm_mlp_post"][layer]), p["w_up"][layer], p["w_down"][layer], tm)
    return (x2.reshape(bt, l, D_MODEL), jnp.stack(k_out), jnp.stack(v_out), jnp.stack(re_out), jnp.stack(im_out),
            jnp.stack(h_out))


def kernel(x_prompt, x_sample, cache_swa_k, cache_swa_v, state_s5_re, state_s5_im, state_hgrn, t5_bias_table,
           norm_mix_pre, norm_mix_post, norm_mlp_pre, norm_mlp_post, w_in_even, w_out_even, s5_lambda_re,
           s5_lambda_im, s5_log_step, s5_b_re, s5_b_im, s5_c_re, s5_c_im, s5_d, s5_w_glu, swa_sinks, w_in_odd,
           w_out_odd, hgrn_lb_param, hgrn_gnorm, w_up, w_down):
    bf = lambda a: a.astype(BF16)
    n_even = w_in_even.shape[0]
    params = dict(
        norm_mix_pre=norm_mix_pre, norm_mix_post=norm_mix_post, norm_mlp_pre=norm_mlp_pre,
        norm_mlp_post=norm_mlp_post, w_in_even=bf(w_in_even), w_out_even=bf(w_out_even), s5_d=s5_d,
        s5_w_glu=bf(s5_w_glu), swa_sinks=swa_sinks, w_in_odd=bf(w_in_odd), w_out_odd=bf(w_out_odd),
        hgrn_gnorm=hgrn_gnorm, w_up=bf(w_up), w_down=bf(w_down),
        t5_band=_t5_band(t5_bias_table),
        hgrn_lb=_hgrn_lower_bounds(hgrn_lb_param),
        s5_prep=[_s5_prep(s5_lambda_re[i], s5_lambda_im[i], s5_log_step[i], s5_b_re[i], s5_b_im[i], s5_c_re[i],
                          s5_c_im[i]) for i in range(n_even)],
    )
    y_prompt, k_p, v_p, re_p, im_p, hg_p = _trunk(x_prompt, None, None, None, None, None, params)
    y_sample, k_s, v_s, re_s, im_s, hg_s = _trunk(x_sample, cache_swa_k, cache_swa_v, state_s5_re, state_s5_im,
                                                  state_hgrn, params)
    return (y_prompt, y_sample, k_p, v_p, k_s, v_s, re_p, im_p, re_s, im_s, hg_p, hg_s)
```

```python
import functools
import math

import numpy as np
import jax
import jax.numpy as jnp
from jax import lax
from jax.experimental import pallas as pl
from jax.experimental.pallas import tpu as pltpu

F32 = jnp.float32
BF16 = jnp.bfloat16

D_MODEL = 1024
S5_WIDTH = 512
S5_GROUPS = 32
S5_GROUP = 16
S5_STATE = 64
S5_HALF_STATES = 1024
S5_HALF_CH = 256
SWA_WIDTH = 512
HEAD_DIM = 64
N_HEADS = 8
N_KV = 2
KV_GROUP = N_HEADS // N_KV
WINDOW = 128
T5_BUCKETS = 32
T5_MAX_DIST = 128
HG_HEADS = 8
HG_DK = 128
HG_DV = 128
HG_W = HG_HEADS * HG_DK
D_FF = 4096
RMS_EPS = 1e-6
NEG_INF = -1e30
LOG_FLOOR = 1e-30

S5_CHUNK = 64
HG_CHUNK = 128
HG_BASE = 32
TOKEN_BLOCK = 512
MLP_FF_BLOCK = 1024
SAMPLE_SEQ_BLOCK = 8
VMEM_LIMIT = 56 * 1024 * 1024


def _cparams(*sem):
    return pltpu.CompilerParams(dimension_semantics=sem, vmem_limit_bytes=VMEM_LIMIT)


def _resident(shape):
    nd = len(shape)
    return pl.BlockSpec(shape, lambda *_: (0,) * nd, pipeline_mode=pl.Buffered(1))


def _rms(x, w):
    return x * lax.rsqrt(jnp.mean(x * x, axis=-1, keepdims=True) + RMS_EPS) * w


def _sigmoid(x):
    return 1.0 / (1.0 + jnp.exp(-x))


def _silu(x):
    return x * _sigmoid(x)


def _log_sigmoid(x):
    return jnp.minimum(x, 0.0) - jnp.log1p(jnp.exp(-jnp.abs(x)))


def _gelu_tanh(x):
    return 0.5 * x * (1.0 + jnp.tanh(math.sqrt(2.0 / math.pi) * (x + 0.044715 * (x * x * x))))


def _dot(a, b):
    return jnp.dot(a, b, preferred_element_type=F32)


def _dot_nt(a, b):
    return lax.dot_general(a, b, (((1,), (1,)), ((), ())), preferred_element_type=F32)


def _dot_tn(a, b):
    return lax.dot_general(a, b, (((0,), (0,)), ((), ())), preferred_element_type=F32)


def _dot_exact(a, b):
    return jnp.dot(a, b, preferred_element_type=F32, precision=lax.Precision.HIGHEST)


def _norm_proj_body(x_ref, g_ref, w_ref, *o_refs, splits):
    xn = _rms(x_ref[...], g_ref[...]).astype(BF16)
    off = 0
    for o_ref, wd in zip(o_refs, splits):
        o_ref[...] = _dot(xn, w_ref[:, off:off + wd])
        off += wd


def _norm_proj(x2d, g, w_bf, splits, tm):
    t, n = x2d.shape[0], w_bf.shape[1]
    return pl.pallas_call(
        functools.partial(_norm_proj_body, splits=splits),
        grid=(t // tm,),
        in_specs=[pl.BlockSpec((tm, D_MODEL), lambda i: (i, 0)),
                  _resident((1, D_MODEL)), _resident((D_MODEL, n))],
        out_specs=[pl.BlockSpec((tm, wd), lambda i: (i, 0)) for wd in splits],
        out_shape=[jax.ShapeDtypeStruct((t, wd), F32) for wd in splits],
        compiler_params=_cparams("parallel"),
        name="norm_proj",
    )(x2d, g, w_bf)


def _post_body(*refs, widths):
    x_ref = refs[0]
    in_refs = refs[1:1 + len(widths)]
    wout_ref, gpost_ref, gpre_ref, gmpost_ref, wup_ref, wdn_ref, o_ref = refs[1 + len(widths):]
    mix = None
    off = 0
    for r, wd in zip(in_refs, widths):
        t = _dot(r[...].astype(BF16), wout_ref[off:off + wd, :])
        mix = t if mix is None else mix + t
        off += wd
    x1 = x_ref[...] + _rms(mix, gpost_ref[...])
    hn = _rms(x1, gpre_ref[...]).astype(BF16)
    acc = None
    for c in range(D_FF // MLP_FF_BLOCK):
        cs = slice(c * MLP_FF_BLOCK, (c + 1) * MLP_FF_BLOCK)
        hk = jnp.square(jnp.maximum(_dot(hn, wup_ref[:, cs]), 0.0)).astype(BF16)
        t = _dot(hk, wdn_ref[cs, :])
        acc = t if acc is None else acc + t
    o_ref[...] = x1 + _rms(acc, gmpost_ref[...])


def _post(x2d, mixer_outs, wout_bf, gpost, gpre, gmpost, wup_bf, wdn_bf, tm):
    t = x2d.shape[0]
    widths = tuple(a.shape[1] for a in mixer_outs)
    row = lambda wd: pl.BlockSpec((tm, wd), lambda i: (i, 0))
    return pl.pallas_call(
        functools.partial(_post_body, widths=widths),
        grid=(t // tm,),
        in_specs=[row(D_MODEL)] + [row(wd) for wd in widths] + [
            _resident((D_MODEL, D_MODEL)), _resident((1, D_MODEL)), _resident((1, D_MODEL)),
            _resident((1, D_MODEL)), _resident((D_MODEL, D_FF)), _resident((D_FF, D_MODEL))],
        out_specs=row(D_MODEL),
        out_shape=jax.ShapeDtypeStruct((t, D_MODEL), F32),
        compiler_params=_cparams("parallel"),
        name="post_mlp",
    )(x2d, *mixer_outs, wout_bf, gpost, gpre, gmpost, wup_bf, wdn_bf)


def _bucket_map():
    t = np.arange(WINDOW)[:, None]
    m = np.arange(2 * WINDOW)[None, :]
    n = np.maximum(WINDOW + t - m, 0)
    max_exact = T5_BUCKETS // 2
    large = max_exact + np.floor(np.log(np.maximum(n, max_exact) / max_exact)
                                 / math.log(T5_MAX_DIST / max_exact) * (T5_BUCKETS - max_exact)).astype(np.int64)
    large = np.minimum(large, T5_BUCKETS - 1)
    return np.where(n < max_exact, n, large).astype(np.int32)


def _bias_body(tab_ref, bm_ref, o_ref):
    bm = bm_ref[...]
    for h in range(N_HEADS):
        acc = jnp.zeros(bm.shape, F32)
        for b in range(T5_BUCKETS):
            acc = jnp.where(bm == b, tab_ref[b, h], acc)
        o_ref[h] = acc


def _t5_band(table):
    return pl.pallas_call(
        _bias_body,
        in_specs=[pl.BlockSpec(memory_space=pltpu.SMEM),
                  pl.BlockSpec((WINDOW, 2 * WINDOW), lambda: (0, 0))],
        out_specs=pl.BlockSpec((N_HEADS, WINDOW, 2 * WINDOW), lambda: (0, 0, 0)),
        out_shape=jax.ShapeDtypeStruct((N_HEADS, WINDOW, 2 * WINDOW), F32),
        name="t5_band",
    )(table, jnp.asarray(_bucket_map()))


def _sink_softmax(parts, sink):
    m = sink
    for s in parts:
        m = jnp.maximum(jnp.max(s, axis=-1, keepdims=True), m)
    es = [jnp.exp(s - m) for s in parts]
    den = jnp.exp(sink - m)
    for e in es:
        den = den + jnp.sum(e, axis=-1, keepdims=True)
    return [e / den for e in es]


def _swa_prompt_body(sink_ref, q_ref, kc_ref, kp_ref, vc_ref, vp_ref, bias_ref, o_ref):
    n = pl.program_id(1)
    row = lax.broadcasted_iota(jnp.int32, (WINDOW, 2 * WINDOW), 0)
    col = lax.broadcasted_iota(jnp.int32, (WINDOW, 2 * WINDOW), 1)
    dist = WINDOW + row - col
    valid = (dist >= 0) & (dist < WINDOW) & jnp.logical_or(col >= WINDOW, n > 0)
    kb = jnp.concatenate([kp_ref[...], kc_ref[...]], axis=0).astype(BF16)
    vb = jnp.concatenate([vp_ref[...], vc_ref[...]], axis=0).astype(BF16)
    q = q_ref[...]
    for h in range(N_HEADS):
        ks = slice((h // KV_GROUP) * HEAD_DIM, (h // KV_GROUP + 1) * HEAD_DIM)
        hs = slice(h * HEAD_DIM, (h + 1) * HEAD_DIM)
        s = _dot_nt(q[:, hs].astype(BF16), kb[:, ks]) * (HEAD_DIM ** -0.5)
        s = jnp.where(valid, s + bias_ref[h], NEG_INF)
        (p,) = _sink_softmax([s], sink_ref[h])
        o_ref[:, hs] = _dot(p.astype(BF16), vb[:, ks])


def _swa_prompt(q, k, v, sinks, band):
    b, l, _ = q.shape
    kvw = N_KV * HEAD_DIM
    cur = lambda w: pl.BlockSpec((None, WINDOW, w), lambda i, n: (i, n, 0))
    prev = lambda w: pl.BlockSpec((None, WINDOW, w), lambda i, n: (i, jnp.maximum(n - 1, 0), 0))
    return pl.pallas_call(
        _swa_prompt_body,
        grid=(b, l // WINDOW),
        in_specs=[pl.BlockSpec(memory_space=pltpu.SMEM), cur(SWA_WIDTH), cur(kvw), prev(kvw), cur(kvw), prev(kvw),
                  _resident((N_HEADS, WINDOW, 2 * WINDOW))],
        out_specs=cur(SWA_WIDTH),
        out_shape=jax.ShapeDtypeStruct((b, l, SWA_WIDTH), F32),
        compiler_params=_cparams("parallel", "arbitrary"),
        name="swa_prompt",
    )(sinks, q, k, k, v, v, band)


def _swa_sample_body(sink_ref, q_ref, kn_ref, vn_ref, ck_ref, cv_ref, bias_ref, o_ref, ko_ref, vo_ref, *, steps):
    ck, cv, kn, vn = ck_ref[...], cv_ref[...], kn_ref[...], vn_ref[...]
    ko_ref[:, 0:WINDOW - steps, :] = ck[:, steps:, :]
    ko_ref[:, WINDOW - steps:, :] = kn
    vo_ref[:, 0:WINDOW - steps, :] = cv[:, steps:, :]
    vo_ref[:, WINDOW - steps:, :] = vn
    valid_c = (lax.broadcasted_iota(jnp.int32, (steps, WINDOW), 1)
               > lax.broadcasted_iota(jnp.int32, (steps, WINDOW), 0))[None]
    valid_n = (lax.broadcasted_iota(jnp.int32, (steps, steps), 1)
               <= lax.broadcasted_iota(jnp.int32, (steps, steps), 0))[None]
    ckb, cvb, knb, vnb = (a.astype(BF16) for a in (ck, cv, kn, vn))
    q = q_ref[...]
    scale = HEAD_DIM ** -0.5
    for h in range(N_HEADS):
        ks = slice((h // KV_GROUP) * HEAD_DIM, (h // KV_GROUP + 1) * HEAD_DIM)
        hs = slice(h * HEAD_DIM, (h + 1) * HEAD_DIM)
        qh = q[:, :, hs].astype(BF16)
        sc = jnp.einsum("bqd,bkd->bqk", qh, ckb[:, :, ks], preferred_element_type=F32) * scale
        sn = jnp.einsum("bqd,bkd->bqk", qh, knb[:, :, ks], preferred_element_type=F32) * scale
        sc = jnp.where(valid_c, sc + bias_ref[h, 0:steps, 0:WINDOW][None], NEG_INF)
        sn = jnp.where(valid_n, sn + bias_ref[h, 0:steps, WINDOW:WINDOW + steps][None], NEG_INF)
        pc, pn = _sink_softmax([sc, sn], sink_ref[h])
        o_ref[:, :, hs] = (jnp.einsum("bqk,bkd->bqd", pc.astype(BF16), cvb[:, :, ks], preferred_element_type=F32)
                           + jnp.einsum("bqk,bkd->bqd", pn.astype(BF16), vnb[:, :, ks], preferred_element_type=F32))


def _swa_sample(q, kn, vn, ck, cv, sinks, band):
    b, steps, _ = q.shape
    kvw = N_KV * HEAD_DIM
    bb = SAMPLE_SEQ_BLOCK
    blk = lambda r, w: pl.BlockSpec((bb, r, w), lambda i: (i, 0, 0))
    return pl.pallas_call(
        functools.partial(_swa_sample_body, steps=steps),
        grid=(b // bb,),
        in_specs=[pl.BlockSpec(memory_space=pltpu.SMEM), blk(steps, SWA_WIDTH), blk(steps, kvw), blk(steps, kvw),
                  blk(WINDOW, kvw), blk(WINDOW, kvw), _resident((N_HEADS, WINDOW, 2 * WINDOW))],
        out_specs=[blk(steps, SWA_WIDTH), blk(WINDOW, kvw), blk(WINDOW, kvw)],
        out_shape=[jax.ShapeDtypeStruct((b, steps, SWA_WIDTH), F32),
                   jax.ShapeDtypeStruct((b, WINDOW, kvw), F32),
                   jax.ShapeDtypeStruct((b, WINDOW, kvw), F32)],
        compiler_params=_cparams("parallel"),
        name="swa_sample",
    )(sinks, q, kn, vn, ck, cv, band)


def _s5_prep_body(lr_ref, li_ref, ls_ref, br_ref, bi_ref, cr_ref, ci_ref, bm_ref, cm_ref, pw_ref, pinv_ref):
    hs = S5_HALF_STATES
    lr, li = lr_ref[...], li_ref[...]
    dt = jnp.exp(ls_ref[...])
    mag = jnp.exp(lr * dt)
    ang = li * dt
    ar = mag * jnp.cos(ang)
    ai = mag * jnp.sin(ang)
    den = lr * lr + li * li
    nr = ar - 1.0
    kr = (nr * lr + ai * li) / den
    ki = (ai * lr - nr * li) / den
    br, bi = br_ref[...], bi_ref[...]
    bm_ref[:, 0:hs] = (kr * br - ki * bi).astype(BF16)
    bm_ref[:, hs:2 * hs] = (kr * bi + ki * br).astype(BF16)
    cm_ref[0:hs, :] = cr_ref[...].astype(BF16)
    cm_ref[hs:2 * hs, :] = (-ci_ref[...]).astype(BF16)
    n2 = ar * ar + ai * ai
    ir, ii = ar / n2, -ai / n2
    pr, pi = jnp.ones_like(ar), jnp.zeros_like(ar)
    qr, qi = pr, pi
    for j in range(S5_CHUNK):
        pw_ref[0, j:j + 1, :] = pr
        pw_ref[1, j:j + 1, :] = pi
        pinv_ref[0, j:j + 1, :] = qr
        pinv_ref[1, j:j + 1, :] = qi
        pr, pi = pr * ar - pi * ai, pr * ai + pi * ar
        qr, qi = qr * ir - qi * ii, qr * ii + qi * ir


def _s5_prep(lam_re, lam_im, log_step, b_re, b_im, c_re, c_im):
    hs, hc = S5_HALF_STATES, S5_HALF_CH
    gh = S5_GROUPS // 2
    eye = jnp.eye(gh, dtype=bool)
    vec = lambda a: a.reshape(2, 1, hs)
    ls = jnp.broadcast_to(log_step[:, None], (S5_GROUPS, S5_STATE))

    def b_blockdiag(b):
        bt = b.reshape(2, gh, S5_STATE, S5_GROUP).transpose(0, 3, 1, 2)[:, None]
        full = jnp.where(eye[None, :, None, :, None], bt, 0.0)
        return full.reshape(2, hc, hs)

    def c_blockdiag(c):
        ct = c.reshape(2, gh, S5_GROUP, S5_STATE).transpose(0, 1, 3, 2)[:, :, :, None]
        full = jnp.where(eye[None, :, None, :, None], ct, 0.0)
        return full.reshape(2, hs, hc)

    vspec = pl.BlockSpec((None, 1, hs), lambda h: (h, 0, 0))
    bspec = pl.BlockSpec((None, hc, hs), lambda h: (h, 0, 0))
    cspec = pl.BlockSpec((None, hs, hc), lambda h: (h, 0, 0))
    pspec = pl.BlockSpec((None, 2, S5_CHUNK, hs), lambda h: (h, 0, 0, 0))
    return pl.pallas_call(
        _s5_prep_body,
        grid=(2,),
        in_specs=[vspec, vspec, vspec, bspec, bspec, cspec, cspec],
        out_specs=[pl.BlockSpec((None, hc, 2 * hs), lambda h: (h, 0, 0)),
                   pl.BlockSpec((None, 2 * hs, hc), lambda h: (h, 0, 0)), pspec, pspec],
        out_shape=[jax.ShapeDtypeStruct((2, hc, 2 * hs), BF16), jax.ShapeDtypeStruct((2, 2 * hs, hc), BF16),
                   jax.ShapeDtypeStruct((2, 2, S5_CHUNK, hs), F32), jax.ShapeDtypeStruct((2, 2, S5_CHUNK, hs), F32)],
        compiler_params=_cparams("parallel"),
        name="s5_prep",
    )(vec(lam_re), vec(lam_im), vec(ls), b_blockdiag(b_re), b_blockdiag(b_im), c_blockdiag(c_re), c_blockdiag(c_im))


def _s5_glu(y, u, d_ref, wg_ref):
    g = _gelu_tanh(y + d_ref[...] * u)
    return g * _sigmoid(_dot(g.astype(BF16), wg_ref[...]))


def _s5_prompt_body(u_ref, bm_ref, cm_ref, pw_ref, pinv_ref, d_ref, wg_ref, o_ref, hfin_ref, bu_scr, h_scr, *, tb):
    hs, hc, c0 = S5_HALF_STATES, S5_HALF_CH, S5_CHUNK

    @pl.when(pl.program_id(1) == 0)
    def _():
        h_scr[...] = jnp.zeros_like(h_scr)

    u = u_ref[...]
    ub = u.astype(BF16)
    for hf in range(2):
        bu_scr[:, 2 * hf * hs:2 * (hf + 1) * hs] = _dot(ub[:, hf * hc:(hf + 1) * hc], bm_ref[hf])
    tril = (lax.broadcasted_iota(jnp.int32, (c0, c0), 0) >= lax.broadcasted_iota(jnp.int32, (c0, c0), 1)).astype(BF16)

    def chunk(ci, carry):
        rows = pl.ds(pl.multiple_of(ci * c0, c0), c0)
        for hf in range(2):
            re = slice(2 * hf * hs, (2 * hf + 1) * hs)
            im = slice((2 * hf + 1) * hs, (2 * hf + 2) * hs)
            bur, bui = bu_scr[rows, re], bu_scr[rows, im]
            qr, qi = pinv_ref[hf, 0], pinv_ref[hf, 1]
            cr = _dot(tril, (qr * bur - qi * bui).astype(BF16))
            ci_ = _dot(tril, (qr * bui + qi * bur).astype(BF16))
            hr0, hi0 = h_scr[:, re], h_scr[:, im]
            ar, ai = pw_ref[hf, 0, 1:2, :], pw_ref[hf, 1, 1:2, :]
            sr = cr + (ar * hr0 - ai * hi0)
            si = ci_ + (ar * hi0 + ai * hr0)
            pr, pi = pw_ref[hf, 0], pw_ref[hf, 1]
            hr = pr * sr - pi * si
            hi = pr * si + pi * sr
            bu_scr[rows, re] = hr
            bu_scr[rows, im] = hi
            h_scr[:, re] = hr[c0 - 1:c0, :]
            h_scr[:, im] = hi[c0 - 1:c0, :]
        return carry

    lax.fori_loop(0, tb // c0, chunk, 0)
    y = jnp.concatenate([_dot(bu_scr[:, 2 * hf * hs:2 * (hf + 1) * hs].astype(BF16), cm_ref[hf]) for hf in range(2)],
                        axis=1)
    o_ref[...] = _s5_glu(y, u, d_ref, wg_ref)
    hfin_ref[...] = h_scr[...]


def _s5_prompt(u, prep, d_skip, wglu_bf, tb):
    b, l, _ = u.shape
    bm, cm, pw, pinv = prep
    ns = 4 * S5_HALF_STATES
    return pl.pallas_call(
        functools.partial(_s5_prompt_body, tb=tb),
        grid=(b, l // tb),
        in_specs=[pl.BlockSpec((None, tb, S5_WIDTH), lambda i, j: (i, j, 0)),
                  _resident(bm.shape), _resident(cm.shape), _resident(pw.shape), _resident(pinv.shape),
                  _resident((1, S5_WIDTH)), _resident((S5_WIDTH, S5_WIDTH))],
        out_specs=[pl.BlockSpec((None, tb, S5_WIDTH), lambda i, j: (i, j, 0)),
                   pl.BlockSpec((None, 1, ns), lambda i, j: (i, 0, 0))],
        out_shape=[jax.ShapeDtypeStruct((b, l, S5_WIDTH), F32), jax.ShapeDtypeStruct((b, 1, ns), F32)],
        scratch_shapes=[pltpu.VMEM((tb, ns), F32), pltpu.VMEM((1, ns), F32)],
        compiler_params=_cparams("parallel", "arbitrary"),
        name="s5_prompt",
    )(u, bm, cm, pw, pinv, d_skip, wglu_bf)


def _s5_sample_body(u_ref, h0_ref, bm_ref, cm_ref, pw_ref, d_ref, wg_ref, o_ref, hfin_ref, *, steps):
    hs, hc = S5_HALF_STATES, S5_HALF_CH
    h = [h0_ref[:, k * hs:(k + 1) * hs] for k in range(4)]
    for t in range(steps):
        u = u_ref[t]
        ub = u.astype(BF16)
        ys = []
        for hf in range(2):
            bu = _dot(ub[:, hf * hc:(hf + 1) * hc], bm_ref[hf])
            ar, ai = pw_ref[hf, 0, 1:2, :], pw_ref[hf, 1, 1:2, :]
            hr, hi = h[2 * hf], h[2 * hf + 1]
            h[2 * hf] = ar * hr - ai * hi + bu[:, 0:hs]
            h[2 * hf + 1] = ar * hi + ai * hr + bu[:, hs:2 * hs]
            ys.append(_dot(jnp.concatenate([h[2 * hf], h[2 * hf + 1]], axis=1).astype(BF16), cm_ref[hf]))
        o_ref[t] = _s5_glu(jnp.concatenate(ys, axis=1), u, d_ref, wg_ref)
    for k in range(4):
        hfin_ref[:, k * hs:(k + 1) * hs] = h[k]


def _s5_sample(u_t, h0, prep, d_skip, wglu_bf):
    steps, b, _ = u_t.shape
    bm, cm, pw, _ = prep
    ns = 4 * S5_HALF_STATES
    full = lambda shape: pl.BlockSpec(shape, lambda: (0,) * len(shape))
    return pl.pallas_call(
        functools.partial(_s5_sample_body, steps=steps),
        in_specs=[full(u_t.shape), full(h0.shape), full(bm.shape), full(cm.shape), full(pw.shape),
                  full((1, S5_WIDTH)), full((S5_WIDTH, S5_WIDTH))],
        out_specs=[full((steps, b, S5_WIDTH)), full((b, ns))],
        out_shape=[jax.ShapeDtypeStruct((steps, b, S5_WIDTH), F32), jax.ShapeDtypeStruct((b, ns), F32)],
        compiler_params=pltpu.CompilerParams(vmem_limit_bytes=VMEM_LIMIT),
        name="s5_sample",
    )(u_t, h0, bm, cm, pw, d_skip, wglu_bf)


def _s5_state_in(re, im):
    b = re.shape[0]
    return jnp.stack([re.reshape(b, 2, S5_HALF_STATES), im.reshape(b, 2, S5_HALF_STATES)], axis=2).reshape(b, -1)


def _s5_state_out(h):
    b = h.shape[0]
    h4 = h.reshape(b, 2, 2, S5_HALF_STATES)
    return (h4[:, :, 0].reshape(b, S5_GROUPS, S5_STATE), h4[:, :, 1].reshape(b, S5_GROUPS, S5_STATE))


def _lb_body(p_ref, o_ref):
    p = p_ref[...]
    e = jnp.exp(p - jnp.max(p, axis=0, keepdims=True))
    sm = e / jnp.sum(e, axis=0, keepdims=True)
    acc = jnp.zeros_like(sm[0:1])
    for i in range(p.shape[0]):
        acc = acc + sm[i:i + 1]
        o_ref[i:i + 1, :] = acc - sm[0:1]


def _hgrn_lower_bounds(lb_param):
    n = lb_param.shape[0]
    return pl.pallas_call(
        _lb_body,
        in_specs=[pl.BlockSpec((n, HG_W), lambda: (0, 0))],
        out_specs=pl.BlockSpec((n, HG_W), lambda: (0, 0)),
        out_shape=jax.ShapeDtypeStruct((n, HG_W), F32),
        name="hgrn_lower_bounds",
    )(lb_param)


def _hgrn_gates(q, fz, lb):
    a = jnp.log(jnp.maximum(lb, LOG_FLOOR))
    b = jnp.log1p(-lb) + _log_sigmoid(fz)
    log_f = jnp.maximum(a, b) + jnp.log1p(jnp.exp(-jnp.abs(a - b)))
    return _silu(q), log_f, (1.0 - lb) * _sigmoid(-fz)


def _group_norm_gate(o, gn, gate):
    return o * lax.rsqrt(jnp.mean(o * o, axis=-1, keepdims=True) + RMS_EPS) * gn * gate


def _block_rows(b, stride, offset):
    return [b[m * stride + offset:m * stride + offset + 1, :] for m in range(b.shape[0] // stride)]


def _spread_rows(rows, rowi, stride):
    out = rows[-1]
    for m in range(len(rows) - 2, -1, -1):
        out = jnp.where(rowi < (m + 1) * stride, rows[m], out)
    return out


def _cumsum_rows(tril_bf, x):
    hi = x.astype(BF16)
    r1 = x - hi.astype(F32)
    mid = r1.astype(BF16)
    lo = (r1 - mid.astype(F32)).astype(BF16)
    return _dot(tril_bf, hi) + _dot(tril_bf, mid) + _dot(tril_bf, lo)


def _hgrn_prompt_body(x_ref, g_ref, w_ref, lb_ref, gn_ref, o_ref, st_ref, q_scr, k_scr, lf_scr, v_scr, s_scr, *, tb):
    c, w = HG_CHUNK, HG_W

    @pl.when(pl.program_id(1) == 0)
    def _():
        s_scr[...] = jnp.zeros_like(s_scr)

    xn = _rms(x_ref[...], g_ref[...]).astype(BF16)
    qf, log_f, kf = _hgrn_gates(_dot(xn, w_ref[:, 0:w]), _dot(xn, w_ref[:, w:2 * w]), lb_ref[...])
    q_scr[...] = qf
    lf_scr[...] = log_f
    k_scr[...] = kf
    v_scr[...] = _dot(xn, w_ref[:, 2 * w:3 * w]).astype(BF16)
    o_ref[...] = _silu(_dot(xn, w_ref[:, 3 * w:4 * w]))

    row = lax.broadcasted_iota(jnp.int32, (c, c), 0)
    col = lax.broadcasted_iota(jnp.int32, (c, c), 1)
    tril = (row >= col).astype(BF16)
    rowi = lax.broadcasted_iota(jnp.int32, (c, 1), 0)
    masks = [(row // HG_BASE == col // HG_BASE) & (col <= row)]
    strides = []
    s = 2 * HG_BASE
    while s <= c:
        strides.append(s)
        masks.append(row // s == col // s)
        s *= 2

    def chunk(ci, carry):
        rows = pl.ds(pl.multiple_of(ci * c, c), c)
        q, k, vb = q_scr[rows, :], k_scr[rows, :], v_scr[rows, :]
        b = _cumsum_rows(tril, lf_scr[rows, :])
        mids = _block_rows(b, HG_BASE, HG_BASE // 2)
        d = b - _spread_rows(mids, rowi, HG_BASE)
        q0 = q * jnp.exp(d)
        k0 = k * jnp.exp(-d)
        qs, ks = [q0.astype(BF16)], [k0.astype(BF16)]
        for st in strides:
            x = jnp.exp(-jnp.abs(b - _spread_rows(_block_rows(b, st, st // 2), rowi, st)))
            upper = (rowi % st) >= st // 2
            qs.append(jnp.where(upper, q * x, 0.0).astype(BF16))
            ks.append(jnp.where(upper, 0.0, k * x).astype(BF16))
        b_last = b[c - 1:c, :]
        q_in = (q0 * _spread_rows([jnp.exp(m) for m in mids], rowi, HG_BASE)).astype(BF16)
        k_st = (k0 * _spread_rows([jnp.exp(b_last - m) for m in mids], rowi, HG_BASE)).astype(BF16)
        e_last = jnp.exp(b_last)
        for h in range(HG_HEADS):
            hs = slice(h * HG_DK, (h + 1) * HG_DK)
            a = None
            for qh, kh, mk in zip(qs, ks, masks):
                t = jnp.where(mk, _dot_nt(qh[:, hs], kh[:, hs]), 0.0)
                a = t if a is None else a + t
            st_t = s_scr[h]
            o = _dot(a.astype(BF16), vb[:, hs]) + _dot_nt(q_in[:, hs], st_t.astype(BF16))
            s_scr[h] = st_t * e_last[:, hs] + _dot_tn(vb[:, hs], k_st[:, hs])
            o_ref[rows, hs] = _group_norm_gate(o, gn_ref[...], o_ref[rows, hs])
        return carry

    lax.fori_loop(0, tb // c, chunk, 0, unroll=True)
    st_ref[...] = s_scr[...]


def _hgrn_prompt(x, g, w_bf, lb, gn, tb):
    b, l, _ = x.shape
    row = pl.BlockSpec((None, tb, D_MODEL), lambda i, j: (i, j, 0))
    return pl.pallas_call(
        functools.partial(_hgrn_prompt_body, tb=tb),
        grid=(b, l // tb),
        in_specs=[row, _resident((1, D_MODEL)), _resident((D_MODEL, 4 * HG_W)), _resident((1, HG_W)),
                  _resident((1, HG_DV))],
        out_specs=[row, pl.BlockSpec((None, HG_HEADS, HG_DV, HG_DK), lambda i, j: (i, 0, 0, 0))],
        out_shape=[jax.ShapeDtypeStruct((b, l, HG_W), F32),
                   jax.ShapeDtypeStruct((b, HG_HEADS, HG_DV, HG_DK), F32)],
        scratch_shapes=[pltpu.VMEM((tb, HG_W), F32)] * 3 + [pltpu.VMEM((tb, HG_W), BF16),
                                                             pltpu.VMEM((HG_HEADS, HG_DV, HG_DK), F32)],
        compiler_params=_cparams("parallel", "arbitrary"),
        name="hgrn_prompt",
    )(x, g, w_bf, lb, gn)


def _hgrn_sample_body(q_ref, fz_ref, iv_ref, gz_ref, s_ref, lb_ref, gn_ref, o_ref, so_ref, *, steps, bb):
    n = bb * steps
    qf, log_f, kf = _hgrn_gates(q_ref[...], fz_ref[...], lb_ref[...])
    v = iv_ref[...]
    gate = _silu(gz_ref[...])
    row = lax.broadcasted_iota(jnp.int32, (n, n), 0)
    col = lax.broadcasted_iota(jnp.int32, (n, n), 1)
    same = row // steps == col // steps
    causal = same & (col <= row)
    b = _dot_exact(causal.astype(F32), log_f)
    total = _dot_exact(same.astype(F32), log_f)
    first = (same & (col % steps == 0)).astype(F32)
    e0 = b - _dot_exact(first, log_f)
    q0 = (qf * jnp.exp(e0)).astype(BF16)
    k0 = (kf * jnp.exp(-e0)).astype(BF16)
    q_in = (qf * jnp.exp(b)).astype(BF16)
    k_st = (kf * jnp.exp(total - b)).astype(BF16)
    vb = v.astype(BF16)
    pick = (lax.broadcasted_iota(jnp.int32, (bb, n), 1) == steps * lax.broadcasted_iota(jnp.int32, (bb, n), 0))
    e_seq = jnp.exp(_dot_exact(pick.astype(F32), total))
    e_rows = jnp.concatenate([e_seq[:, h * HG_DK:(h + 1) * HG_DK] for h in range(HG_HEADS)], axis=0)
    pad = HG_DK - e_rows.shape[0]
    if pad:
        e_rows = jnp.concatenate([e_rows, jnp.zeros((pad, HG_DK), F32)], axis=0)
    e_cols = e_rows.T
    for h in range(HG_HEADS):
        hs = slice(h * HG_DK, (h + 1) * HG_DK)
        a = jnp.where(causal, _dot_nt(q0[:, hs], k0[:, hs]), 0.0)
        o_intra = _dot(a.astype(BF16), vb[:, hs])
        outs = []
        for i in range(bb):
            rs = slice(i * steps, (i + 1) * steps)
            s_old = s_ref[i, h]
            outs.append(_dot(q_in[rs, hs], s_old.astype(BF16)))
            so_ref[i, h] = (s_old * e_cols[:, h * bb + i:h * bb + i + 1]
                            + _dot_tn(k_st[rs, hs], vb[rs, hs]))
        o = o_intra + jnp.concatenate(outs, axis=0)
        o_ref[:, hs] = _group_norm_gate(o, gn_ref[...], gate[:, hs])


def _hgrn_sample(proj4, state, lb, gn, steps):
    b = state.shape[0]
    bb = SAMPLE_SEQ_BLOCK
    assert HG_HEADS * bb <= HG_DK
    rows = pl.BlockSpec((bb * steps, HG_W), lambda i: (i, 0))
    sspec = pl.BlockSpec((bb, HG_HEADS, HG_DK, HG_DV), lambda i: (i, 0, 0, 0))
    return pl.pallas_call(
        functools.partial(_hgrn_sample_body, steps=steps, bb=bb),
        grid=(b // bb,),
        in_specs=[rows, rows, rows, rows, sspec, _resident((1, HG_W)), _resident((1, HG_DV))],
        out_specs=[rows, sspec],
        out_shape=[jax.ShapeDtypeStruct((b * steps, HG_W), F32),
                   jax.ShapeDtypeStruct((b, HG_HEADS, HG_DK, HG_DV), F32)],
        compiler_params=_cparams("parallel"),
        name="hgrn_sample",
    )(*proj4, state, lb, gn)


def _trunk(x, cache_k, cache_v, st_re, st_im, st_h, p):
    prompt = cache_k is None
    bt, l, _ = x.shape
    tm = min(TOKEN_BLOCK, bt * l)
    depth = p["norm_mix_pre"].shape[0]
    kvw = N_KV * HEAD_DIM
    row1 = lambda a: a.reshape(1, -1)
    x2 = x.reshape(bt * l, D_MODEL)
    k_out, v_out, re_out, im_out, h_out = [], [], [], [], []
    for layer in range(depth):
        idx = layer // 2
        g_pre = row1(p["norm_mix_pre"][layer])
        if layer % 2 == 0:
            u, q, k, v = _norm_proj(x2, g_pre, p["w_in_even"][idx], (S5_WIDTH, SWA_WIDTH, kvw, kvw), tm)
            prep = p["s5_prep"][idx]
            d_skip, wglu = row1(p["s5_d"][idx]), p["s5_w_glu"][idx]
            q3, k3, v3 = q.reshape(bt, l, SWA_WIDTH), k.reshape(bt, l, kvw), v.reshape(bt, l, kvw)
            if prompt:
                a_out, h_fin = _s5_prompt(u.reshape(bt, l, S5_WIDTH), prep, d_skip, wglu, min(TOKEN_BLOCK, l))
                h_fin = h_fin.reshape(bt, -1)
                b_out = _swa_prompt(q3, k3, v3, p["swa_sinks"][idx], p["t5_band"])
                kw, vw = k3[:, l - WINDOW:], v3[:, l - WINDOW:]
            else:
                u_t = u.reshape(bt, l, S5_WIDTH).transpose(1, 0, 2)
                a_t, h_fin = _s5_sample(u_t, _s5_state_in(st_re[idx], st_im[idx]), prep, d_skip, wglu)
                a_out = a_t.transpose(1, 0, 2)
                b_out, kw, vw = _swa_sample(q3, k3, v3, cache_k[idx].reshape(bt, WINDOW, kvw),
                                            cache_v[idx].reshape(bt, WINDOW, kvw), p["swa_sinks"][idx], p["t5_band"])
            h_re, h_im = _s5_state_out(h_fin)
            k_out.append(kw.reshape(bt, WINDOW, N_KV, HEAD_DIM))
            v_out.append(vw.reshape(bt, WINDOW, N_KV, HEAD_DIM))
            re_out.append(h_re)
            im_out.append(h_im)
            mixer_outs = (a_out.reshape(bt * l, S5_WIDTH), b_out.reshape(bt * l, SWA_WIDTH))
            w_out = p["w_out_even"][idx]
        else:
            lb, gn = p["hgrn_lb"][idx:idx + 1], row1(p["hgrn_gnorm"][idx])
            if prompt:
                c_out, s_t = _hgrn_prompt(x2.reshape(bt, l, D_MODEL), g_pre, p["w_in_odd"][idx], lb, gn,
                                          min(TOKEN_BLOCK, l))
                s_new = s_t.transpose(0, 1, 3, 2)
            else:
                proj4 = _norm_proj(x2, g_pre, p["w_in_odd"][idx], (HG_W,) * 4, tm)
                c_out, s_new = _hgrn_sample(proj4, st_h[idx], lb, gn, l)
            h_out.append(s_new)
            mixer_outs = (c_out.reshape(bt * l, HG_W),)
            w_out = p["w_out_odd"][idx]
        x2 = _post(x2, mixer_outs, w_out, row1(p["norm_mix_post"][layer]), row1(p["norm_mlp_pre"][layer]),
                   row1(p["norm_mlp_post"][layer]), p["w_up"][layer], p["w_down"][layer], tm)
    return (x2.reshape(bt, l, D_MODEL), jnp.stack(k_out), jnp.stack(v_out), jnp.stack(re_out), jnp.stack(im_out),
            jnp.stack(h_out))


def kernel(x_prompt, x_sample, cache_swa_k, cache_swa_v, state_s5_re, state_s5_im, state_hgrn, t5_bias_table,
           norm_mix_pre, norm_mix_post, norm_mlp_pre, norm_mlp_post, w_in_even, w_out_even, s5_lambda_re,
           s5_lambda_im, s5_log_step, s5_b_re, s5_b_im, s5_c_re, s5_c_im, s5_d, s5_w_glu, swa_sinks, w_in_odd,
           w_out_odd, hgrn_lb_param, hgrn_gnorm, w_up, w_down):
    bf = lambda a: a.astype(BF16)
    n_even = w_in_even.shape[0]
    params = dict(
        norm_mix_pre=norm_mix_pre, norm_mix_post=norm_mix_post, norm_mlp_pre=norm_mlp_pre,
        norm_mlp_post=norm_mlp_post, w_in_even=bf(w_in_even), w_out_even=bf(w_out_even), s5_d=s5_d,
        s5_w_glu=bf(s5_w_glu), swa_sinks=swa_sinks, w_in_odd=bf(w_in_odd), w_out_odd=bf(w_out_odd),
        hgrn_gnorm=hgrn_gnorm, w_up=bf(w_up), w_down=bf(w_down),
        t5_band=_t5_band(t5_bias_table),
        hgrn_lb=_hgrn_lower_bounds(hgrn_lb_param),
        s5_prep=[_s5_prep(s5_lambda_re[i], s5_lambda_im[i], s5_log_step[i], s5_b_re[i], s5_b_im[i], s5_c_re[i],
                          s5_c_im[i]) for i in range(n_even)],
    )
    y_prompt, k_p, v_p, re_p, im_p, hg_p = _trunk(x_prompt, None, None, None, None, None, params)
    y_sample, k_s, v_s, re_s, im_s, hg_s = _trunk(x_sample, cache_swa_k, cache_swa_v, state_s5_re, state_s5_im,
                                                  state_hgrn, params)
    return (y_prompt, y_sample, k_p, v_p, k_s, v_s, re_p, im_p, re_s, im_s, hg_p, hg_s)
```

```python
import functools
import math

import numpy as np
import jax
import jax.numpy as jnp
from jax import lax
from jax.experimental import pallas as pl
from jax.experimental.pallas import tpu as pltpu

F32 = jnp.float32
BF16 = jnp.bfloat16

D_MODEL = 1024
S5_WIDTH = 512
S5_GROUPS = 32
S5_GROUP = 16
S5_STATE = 64
S5_HALF_STATES = 1024
S5_HALF_CH = 256
SWA_WIDTH = 512
HEAD_DIM = 64
N_HEADS = 8
N_KV = 2
KV_GROUP = N_HEADS // N_KV
WINDOW = 128
T5_BUCKETS = 32
T5_MAX_DIST = 128
HG_HEADS = 8
HG_DK = 128
HG_DV = 128
HG_W = HG_HEADS * HG_DK
D_FF = 4096
RMS_EPS = 1e-6
NEG_INF = -1e30
LOG_FLOOR = 1e-30

S5_CHUNK = 64
HG_CHUNK = 128
HG_BASE = 32
TOKEN_BLOCK = 512
MLP_FF_BLOCK = 1024
SAMPLE_SEQ_BLOCK = 8
VMEM_LIMIT = 56 * 1024 * 1024


def _cparams(*sem):
    return pltpu.CompilerParams(dimension_semantics=sem, vmem_limit_bytes=VMEM_LIMIT)


def _resident(shape):
    nd = len(shape)
    return pl.BlockSpec(shape, lambda *_: (0,) * nd, pipeline_mode=pl.Buffered(1))


def _rms(x, w):
    return x * lax.rsqrt(jnp.mean(x * x, axis=-1, keepdims=True) + RMS_EPS) * w


def _sigmoid(x):
    return 1.0 / (1.0 + jnp.exp(-x))


def _silu(x):
    return x * _sigmoid(x)


def _log_sigmoid(x):
    return jnp.minimum(x, 0.0) - jnp.log1p(jnp.exp(-jnp.abs(x)))


def _gelu_tanh(x):
    return 0.5 * x * (1.0 + jnp.tanh(math.sqrt(2.0 / math.pi) * (x + 0.044715 * (x * x * x))))


def _dot(a, b):
    return jnp.dot(a, b, preferred_element_type=F32)


def _dot_nt(a, b):
    return lax.dot_general(a, b, (((1,), (1,)), ((), ())), preferred_element_type=F32)


def _dot_tn(a, b):
    return lax.dot_general(a, b, (((0,), (0,)), ((), ())), preferred_element_type=F32)


def _dot_exact(a, b):
    return jnp.dot(a, b, preferred_element_type=F32, precision=lax.Precision.HIGHEST)


def _norm_proj_body(x_ref, g_ref, w_ref, *o_refs, splits):
    xn = _rms(x_ref[...], g_ref[...]).astype(BF16)
    off = 0
    for o_ref, wd in zip(o_refs, splits):
        o_ref[...] = _dot(xn, w_ref[:, off:off + wd])
        off += wd


def _norm_proj(x2d, g, w_bf, splits, tm):
    t, n = x2d.shape[0], w_bf.shape[1]
    return pl.pallas_call(
        functools.partial(_norm_proj_body, splits=splits),
        grid=(t // tm,),
        in_specs=[pl.BlockSpec((tm, D_MODEL), lambda i: (i, 0)),
                  _resident((1, D_MODEL)), _resident((D_MODEL, n))],
        out_specs=[pl.BlockSpec((tm, wd), lambda i: (i, 0)) for wd in splits],
        out_shape=[jax.ShapeDtypeStruct((t, wd), F32) for wd in splits],
        compiler_params=_cparams("parallel"),
        name="norm_proj",
    )(x2d, g, w_bf)


def _post_body(*refs, widths):
    x_ref = refs[0]
    in_refs = refs[1:1 + len(widths)]
    wout_ref, gpost_ref, gpre_ref, gmpost_ref, wup_ref, wdn_ref, o_ref = refs[1 + len(widths):]
    mix = None
    off = 0
    for r, wd in zip(in_refs, widths):
        t = _dot(r[...].astype(BF16), wout_ref[off:off + wd, :])
        mix = t if mix is None else mix + t
        off += wd
    x1 = x_ref[...] + _rms(mix, gpost_ref[...])
    hn = _rms(x1, gpre_ref[...]).astype(BF16)
    acc = None
    for c in range(D_FF // MLP_FF_BLOCK):
        cs = slice(c * MLP_FF_BLOCK, (c + 1) * MLP_FF_BLOCK)
        hk = jnp.square(jnp.maximum(_dot(hn, wup_ref[:, cs]), 0.0)).astype(BF16)
        t = _dot(hk, wdn_ref[cs, :])
        acc = t if acc is None else acc + t
    o_ref[...] = x1 + _rms(acc, gmpost_ref[...])


def _post(x2d, mixer_outs, wout_bf, gpost, gpre, gmpost, wup_bf, wdn_bf, tm):
    t = x2d.shape[0]
    widths = tuple(a.shape[1] for a in mixer_outs)
    row = lambda wd: pl.BlockSpec((tm, wd), lambda i: (i, 0))
    return pl.pallas_call(
        functools.partial(_post_body, widths=widths),
        grid=(t // tm,),
        in_specs=[row(D_MODEL)] + [row(wd) for wd in widths] + [
            _resident((D_MODEL, D_MODEL)), _resident((1, D_MODEL)), _resident((1, D_MODEL)),
            _resident((1, D_MODEL)), _resident((D_MODEL, D_FF)), _resident((D_FF, D_MODEL))],
        out_specs=row(D_MODEL),
        out_shape=jax.ShapeDtypeStruct((t, D_MODEL), F32),
        compiler_params=_cparams("parallel"),
        name="post_mlp",
    )(x2d, *mixer_outs, wout_bf, gpost, gpre, gmpost, wup_bf, wdn_bf)


def _bucket_map():
    t = np.arange(WINDOW)[:, None]
    m = np.arange(2 * WINDOW)[None, :]
    n = np.maximum(WINDOW + t - m, 0)
    max_exact = T5_BUCKETS // 2
    large = max_exact + np.floor(np.log(np.maximum(n, max_exact) / max_exact)
                                 / math.log(T5_MAX_DIST / max_exact) * (T5_BUCKETS - max_exact)).astype(np.int64)
    large = np.minimum(large, T5_BUCKETS - 1)
    return np.where(n < max_exact, n, large).astype(np.int32)


def _bias_body(tab_ref, bm_ref, o_ref):
    bm = bm_ref[...]
    row = lax.broadcasted_iota(jnp.int32, bm.shape, 0)
    col = lax.broadcasted_iota(jnp.int32, bm.shape, 1)
    dist = WINDOW + row - col
    in_win = (dist >= 0) & (dist < WINDOW)
    for h in range(N_HEADS):
        acc = jnp.zeros(bm.shape, F32)
        for b in range(T5_BUCKETS):
            acc = jnp.where(bm == b, tab_ref[b, h], acc)
        o_ref[0, h] = jnp.where(in_win & (col >= WINDOW), acc, NEG_INF)
        o_ref[1, h] = jnp.where(in_win, acc, NEG_INF)


def _t5_band(table):
    return pl.pallas_call(
        _bias_body,
        in_specs=[pl.BlockSpec(memory_space=pltpu.SMEM),
                  pl.BlockSpec((WINDOW, 2 * WINDOW), lambda: (0, 0))],
        out_specs=pl.BlockSpec((2, N_HEADS, WINDOW, 2 * WINDOW), lambda: (0, 0, 0, 0)),
        out_shape=jax.ShapeDtypeStruct((2, N_HEADS, WINDOW, 2 * WINDOW), F32),
        name="t5_band",
    )(table, jnp.asarray(_bucket_map()))


def _pair_heads(a, axis):
    shp = a.shape
    a = a.reshape(shp[:axis] + (N_KV, KV_GROUP, HEAD_DIM) + shp[axis + 1:])
    a = jnp.swapaxes(a, axis, axis + 1)
    return a.reshape(shp)


def _swa_prompt_body(sink_ref, q_ref, kc_ref, kp_ref, vc_ref, vp_ref, band_ref, o_ref):
    tile = N_KV * HEAD_DIM
    kb = jnp.concatenate([kp_ref[...], kc_ref[...]], axis=0).astype(BF16)
    vb = jnp.concatenate([vp_ref[...], vc_ref[...]], axis=0).astype(BF16)
    v_ext = jnp.concatenate([vb, jnp.ones_like(vb)], axis=1)
    lo = lax.broadcasted_iota(jnp.int32, (WINDOW, tile), 1) < HEAD_DIM
    for t in range(KV_GROUP):
        qt = q_ref[:, t * tile:(t + 1) * tile] * (HEAD_DIM ** -0.5)
        halves = []
        for kv in range(N_KV):
            h = kv * KV_GROUP + t
            qh = jnp.where(lo if kv == 0 else jnp.logical_not(lo), qt, 0.0).astype(BF16)
            s = _dot_nt(qh, kb) + band_ref[h]
            sink = sink_ref[h]
            m = jnp.maximum(jnp.max(s, axis=-1, keepdims=True), sink)
            oe = _dot(jnp.exp(s - m).astype(BF16), v_ext)
            halves.append(oe[:, 0:tile] / (oe[:, tile:tile + 1] + jnp.exp(sink - m)))
        o_ref[:, t * tile:(t + 1) * tile] = jnp.where(lo, halves[0], halves[1])


def _swa_prompt(q, k, v, sinks, band):
    b, l, _ = q.shape
    kvw = N_KV * HEAD_DIM
    cur = lambda w: pl.BlockSpec((None, WINDOW, w), lambda i, n: (i, n, 0))
    prev = lambda w: pl.BlockSpec((None, WINDOW, w), lambda i, n: (i, jnp.maximum(n - 1, 0), 0))
    return pl.pallas_call(
        _swa_prompt_body,
        grid=(b, l // WINDOW),
        in_specs=[pl.BlockSpec(memory_space=pltpu.SMEM), cur(SWA_WIDTH), cur(kvw), prev(kvw), cur(kvw), prev(kvw),
                  pl.BlockSpec((None, N_HEADS, WINDOW, 2 * WINDOW), lambda i, n: (jnp.minimum(n, 1), 0, 0, 0))],
        out_specs=cur(SWA_WIDTH),
        out_shape=jax.ShapeDtypeStruct((b, l, SWA_WIDTH), F32),
        compiler_params=_cparams("parallel", "arbitrary"),
        name="swa_prompt",
    )(sinks, q, k, k, v, v, band)


def _swa_sample_body(q_ref, kn_ref, vn_ref, ck_ref, cv_ref, bias_ref, sink_ref, o_ref, ko_ref, vo_ref, *, steps):
    tile = N_KV * HEAD_DIM
    ck, cv, kn, vn = ck_ref[...], cv_ref[...], kn_ref[...], vn_ref[...]
    ko_ref[:, 0:WINDOW - steps, :] = ck[:, steps:, :]
    ko_ref[:, WINDOW - steps:, :] = kn
    vo_ref[:, 0:WINDOW - steps, :] = cv[:, steps:, :]
    vo_ref[:, WINDOW - steps:, :] = vn
    lo = lax.broadcasted_iota(jnp.int32, (1, 1, tile), 2) < HEAD_DIM
    q = q_ref[...] * (HEAD_DIM ** -0.5)
    qs = []
    for kv in range(N_KV):
        for t in range(KV_GROUP):
            qs.append(jnp.where(lo if kv == 0 else jnp.logical_not(lo), q[:, :, t * tile:(t + 1) * tile], 0.0))
    qall = jnp.concatenate(qs, axis=1).astype(BF16)
    bias = bias_ref[...]
    sink = sink_ref[...][None]
    bdot = lambda a, b_, spec: jnp.einsum(spec, a, b_, preferred_element_type=F32)
    sc = bdot(qall, ck.astype(BF16), "bqd,bkd->bqk") + bias[None, :, 0:WINDOW]
    sn = bdot(qall, kn.astype(BF16), "bqd,bkd->bqk") + bias[None, :, WINDOW:WINDOW + steps]
    m = jnp.maximum(jnp.maximum(jnp.max(sc, axis=-1, keepdims=True), jnp.max(sn, axis=-1, keepdims=True)), sink)
    ec, en = jnp.exp(sc - m), jnp.exp(sn - m)
    den = jnp.sum(ec, axis=-1, keepdims=True) + jnp.sum(en, axis=-1, keepdims=True) + jnp.exp(sink - m)
    o = (bdot(ec.astype(BF16), cv.astype(BF16), "bqk,bkd->bqd")
         + bdot(en.astype(BF16), vn.astype(BF16), "bqk,bkd->bqd")) / den
    for t in range(KV_GROUP):
        o_ref[:, :, t * tile:(t + 1) * tile] = jnp.where(
            lo, o[:, t * steps:(t + 1) * steps, :], o[:, (KV_GROUP + t) * steps:(KV_GROUP + t + 1) * steps, :])


def _swa_sample(q, kn, vn, cache_k, cache_v, idx, sinks, band):
    b, steps, _ = q.shape
    kvw = N_KV * HEAD_DIM
    bb = SAMPLE_SEQ_BLOCK
    nrow = N_HEADS * steps
    bias = band[1, :, 0:steps, :].reshape(nrow, 2 * WINDOW)
    sink_rows = jnp.repeat(sinks, steps).reshape(nrow, 1)
    blk = lambda r, w: pl.BlockSpec((bb, r, w), lambda i: (i, 0, 0))
    cache = pl.BlockSpec((None, bb, WINDOW, kvw), lambda i: (idx, i, 0, 0))
    return pl.pallas_call(
        functools.partial(_swa_sample_body, steps=steps),
        grid=(b // bb,),
        in_specs=[blk(steps, SWA_WIDTH), blk(steps, kvw), blk(steps, kvw), cache, cache,
                  _resident((nrow, 2 * WINDOW)), _resident((nrow, 1))],
        out_specs=[blk(steps, SWA_WIDTH), blk(WINDOW, kvw), blk(WINDOW, kvw)],
        out_shape=[jax.ShapeDtypeStruct((b, steps, SWA_WIDTH), F32),
                   jax.ShapeDtypeStruct((b, WINDOW, kvw), F32),
                   jax.ShapeDtypeStruct((b, WINDOW, kvw), F32)],
        compiler_params=_cparams("parallel"),
        name="swa_sample",
    )(q, kn, vn, cache_k, cache_v, bias, sink_rows)


def _s5_prep_body(lr_ref, li_ref, ls_ref, br_ref, bi_ref, cr_ref, ci_ref, bm_ref, cm_ref, pw_ref, pinv_ref):
    hs = S5_HALF_STATES
    lr, li = lr_ref[...], li_ref[...]
    dt = jnp.exp(ls_ref[...])
    mag = jnp.exp(lr * dt)
    ang = li * dt
    ar = mag * jnp.cos(ang)
    ai = mag * jnp.sin(ang)
    den = lr * lr + li * li
    nr = ar - 1.0
    kr = (nr * lr + ai * li) / den
    ki = (ai * lr - nr * li) / den
    br, bi = br_ref[...], bi_ref[...]
    bm_ref[:, 0:hs] = (kr * br - ki * bi).astype(BF16)
    bm_ref[:, hs:2 * hs] = (kr * bi + ki * br).astype(BF16)
    cm_ref[0:hs, :] = cr_ref[...].astype(BF16)
    cm_ref[hs:2 * hs, :] = (-ci_ref[...]).astype(BF16)
    n2 = ar * ar + ai * ai
    ir, ii = ar / n2, -ai / n2
    pr, pi = jnp.ones_like(ar), jnp.zeros_like(ar)
    qr, qi = pr, pi
    for j in range(S5_CHUNK):
        pw_ref[0, j:j + 1, :] = pr
        pw_ref[1, j:j + 1, :] = pi
        pinv_ref[0, j:j + 1, :] = qr
        pinv_ref[1, j:j + 1, :] = qi
        pr, pi = pr * ar - pi * ai, pr * ai + pi * ar
        qr, qi = qr * ir - qi * ii, qr * ii + qi * ir


def _s5_prep(lam_re, lam_im, log_step, b_re, b_im, c_re, c_im):
    hs, hc = S5_HALF_STATES, S5_HALF_CH
    gh = S5_GROUPS // 2
    eye = jnp.eye(gh, dtype=bool)
    vec = lambda a: a.reshape(2, 1, hs)
    ls = jnp.broadcast_to(log_step[:, None], (S5_GROUPS, S5_STATE))

    def b_blockdiag(b):
        bt = b.reshape(2, gh, S5_STATE, S5_GROUP).transpose(0, 3, 1, 2)[:, None]
        full = jnp.where(eye[None, :, None, :, None], bt, 0.0)
        return full.reshape(2, hc, hs)

    def c_blockdiag(c):
        ct = c.reshape(2, gh, S5_GROUP, S5_STATE).transpose(0, 1, 3, 2)[:, :, :, None]
        full = jnp.where(eye[None, :, None, :, None], ct, 0.0)
        return full.reshape(2, hs, hc)

    vspec = pl.BlockSpec((None, 1, hs), lambda h: (h, 0, 0))
    bspec = pl.BlockSpec((None, hc, hs), lambda h: (h, 0, 0))
    cspec = pl.BlockSpec((None, hs, hc), lambda h: (h, 0, 0))
    pspec = pl.BlockSpec((None, 2, S5_CHUNK, hs), lambda h: (h, 0, 0, 0))
    return pl.pallas_call(
        _s5_prep_body,
        grid=(2,),
        in_specs=[vspec, vspec, vspec, bspec, bspec, cspec, cspec],
        out_specs=[pl.BlockSpec((None, hc, 2 * hs), lambda h: (h, 0, 0)),
                   pl.BlockSpec((None, 2 * hs, hc), lambda h: (h, 0, 0)), pspec, pspec],
        out_shape=[jax.ShapeDtypeStruct((2, hc, 2 * hs), BF16), jax.ShapeDtypeStruct((2, 2 * hs, hc), BF16),
                   jax.ShapeDtypeStruct((2, 2, S5_CHUNK, hs), F32), jax.ShapeDtypeStruct((2, 2, S5_CHUNK, hs), F32)],
        compiler_params=_cparams("parallel"),
        name="s5_prep",
    )(vec(lam_re), vec(lam_im), vec(ls), b_blockdiag(b_re), b_blockdiag(b_im), c_blockdiag(c_re), c_blockdiag(c_im))


def _s5_glu(y, u, d_ref, wg_ref):
    g = _gelu_tanh(y + d_ref[...] * u)
    return g * _sigmoid(_dot(g.astype(BF16), wg_ref[...]))


def _s5_prompt_body(u_ref, bm_ref, cm_ref, pw_ref, pinv_ref, d_ref, wg_ref, o_ref, hfin_ref, bu_scr, h_scr, *, tb):
    hs, hc, c0 = S5_HALF_STATES, S5_HALF_CH, S5_CHUNK

    @pl.when(pl.program_id(1) == 0)
    def _():
        h_scr[...] = jnp.zeros_like(h_scr)

    u = u_ref[...]
    ub = u.astype(BF16)
    for hf in range(2):
        bu_scr[:, 2 * hf * hs:2 * (hf + 1) * hs] = _dot(ub[:, hf * hc:(hf + 1) * hc], bm_ref[hf])
    tril = (lax.broadcasted_iota(jnp.int32, (c0, c0), 0) >= lax.broadcasted_iota(jnp.int32, (c0, c0), 1)).astype(BF16)

    def chunk(ci, carry):
        rows = pl.ds(pl.multiple_of(ci * c0, c0), c0)
        for hf in range(2):
            re = slice(2 * hf * hs, (2 * hf + 1) * hs)
            im = slice((2 * hf + 1) * hs, (2 * hf + 2) * hs)
            bur, bui = bu_scr[rows, re], bu_scr[rows, im]
            qr, qi = pinv_ref[hf, 0], pinv_ref[hf, 1]
            cr = _dot(tril, (qr * bur - qi * bui).astype(BF16))
            ci_ = _dot(tril, (qr * bui + qi * bur).astype(BF16))
            hr0, hi0 = h_scr[:, re], h_scr[:, im]
            ar, ai = pw_ref[hf, 0, 1:2, :], pw_ref[hf, 1, 1:2, :]
            sr = cr + (ar * hr0 - ai * hi0)
            si = ci_ + (ar * hi0 + ai * hr0)
            pr, pi = pw_ref[hf, 0], pw_ref[hf, 1]
            hr = pr * sr - pi * si
            hi = pr * si + pi * sr
            bu_scr[rows, re] = hr
            bu_scr[rows, im] = hi
            h_scr[:, re] = hr[c0 - 1:c0, :]
            h_scr[:, im] = hi[c0 - 1:c0, :]
        return carry

    lax.fori_loop(0, tb // c0, chunk, 0)
    y = jnp.concatenate([_dot(bu_scr[:, 2 * hf * hs:2 * (hf + 1) * hs].astype(BF16), cm_ref[hf]) for hf in range(2)],
                        axis=1)
    o_ref[...] = _s5_glu(y, u, d_ref, wg_ref)
    hfin_ref[...] = h_scr[...]


def _s5_prompt(u, prep, d_skip, wglu_bf, tb):
    b, l, _ = u.shape
    bm, cm, pw, pinv = prep
    ns = 4 * S5_HALF_STATES
    return pl.pallas_call(
        functools.partial(_s5_prompt_body, tb=tb),
        grid=(b, l // tb),
        in_specs=[pl.BlockSpec((None, tb, S5_WIDTH), lambda i, j: (i, j, 0)),
                  _resident(bm.shape), _resident(cm.shape), _resident(pw.shape), _resident(pinv.shape),
                  _resident((1, S5_WIDTH)), _resident((S5_WIDTH, S5_WIDTH))],
        out_specs=[pl.BlockSpec((None, tb, S5_WIDTH), lambda i, j: (i, j, 0)),
                   pl.BlockSpec((None, 1, ns), lambda i, j: (i, 0, 0))],
        out_shape=[jax.ShapeDtypeStruct((b, l, S5_WIDTH), F32), jax.ShapeDtypeStruct((b, 1, ns), F32)],
        scratch_shapes=[pltpu.VMEM((tb, ns), F32), pltpu.VMEM((1, ns), F32)],
        compiler_params=_cparams("parallel", "arbitrary"),
        name="s5_prompt",
    )(u, bm, cm, pw, pinv, d_skip, wglu_bf)


def _s5_sample_body(u_ref, h0_ref, bm_ref, cm_ref, pw_ref, d_ref, wg_ref, o_ref, hfin_ref, *, steps):
    hs, hc = S5_HALF_STATES, S5_HALF_CH
    h = [h0_ref[:, k * hs:(k + 1) * hs] for k in range(4)]
    for t in range(steps):
        u = u_ref[t]
        ub = u.astype(BF16)
        ys = []
        for hf in range(2):
            bu = _dot(ub[:, hf * hc:(hf + 1) * hc], bm_ref[hf])
            ar, ai = pw_ref[hf, 0, 1:2, :], pw_ref[hf, 1, 1:2, :]
            hr, hi = h[2 * hf], h[2 * hf + 1]
            h[2 * hf] = ar * hr - ai * hi + bu[:, 0:hs]
            h[2 * hf + 1] = ar * hi + ai * hr + bu[:, hs:2 * hs]
            ys.append(_dot(jnp.concatenate([h[2 * hf], h[2 * hf + 1]], axis=1).astype(BF16), cm_ref[hf]))
        o_ref[t] = _s5_glu(jnp.concatenate(ys, axis=1), u, d_ref, wg_ref)
    for k in range(4):
        hfin_ref[:, k * hs:(k + 1) * hs] = h[k]


def _s5_sample(u_t, h0, prep, d_skip, wglu_bf):
    steps, b, _ = u_t.shape
    bm, cm, pw, _ = prep
    ns = 4 * S5_HALF_STATES
    full = lambda shape: pl.BlockSpec(shape, lambda: (0,) * len(shape))
    return pl.pallas_call(
        functools.partial(_s5_sample_body, steps=steps),
        in_specs=[full(u_t.shape), full(h0.shape), full(bm.shape), full(cm.shape), full(pw.shape),
                  full((1, S5_WIDTH)), full((S5_WIDTH, S5_WIDTH))],
        out_specs=[full((steps, b, S5_WIDTH)), full((b, ns))],
        out_shape=[jax.ShapeDtypeStruct((steps, b, S5_WIDTH), F32), jax.ShapeDtypeStruct((b, ns), F32)],
        compiler_params=pltpu.CompilerParams(vmem_limit_bytes=VMEM_LIMIT),
        name="s5_sample",
    )(u_t, h0, bm, cm, pw, d_skip, wglu_bf)


def _s5_state_in(re, im):
    b = re.shape[0]
    return jnp.stack([re.reshape(b, 2, S5_HALF_STATES), im.reshape(b, 2, S5_HALF_STATES)], axis=2).reshape(b, -1)


def _s5_state_out(h):
    b = h.shape[0]
    h4 = h.reshape(b, 2, 2, S5_HALF_STATES)
    return (h4[:, :, 0].reshape(b, S5_GROUPS, S5_STATE), h4[:, :, 1].reshape(b, S5_GROUPS, S5_STATE))


def _lb_body(p_ref, o_ref):
    p = p_ref[...]
    e = jnp.exp(p - jnp.max(p, axis=0, keepdims=True))
    sm = e / jnp.sum(e, axis=0, keepdims=True)
    acc = jnp.zeros_like(sm[0:1])
    for i in range(p.shape[0]):
        acc = acc + sm[i:i + 1]
        o_ref[i:i + 1, :] = acc - sm[0:1]


def _hgrn_lower_bounds(lb_param):
    n = lb_param.shape[0]
    return pl.pallas_call(
        _lb_body,
        in_specs=[pl.BlockSpec((n, HG_W), lambda: (0, 0))],
        out_specs=pl.BlockSpec((n, HG_W), lambda: (0, 0)),
        out_shape=jax.ShapeDtypeStruct((n, HG_W), F32),
        name="hgrn_lower_bounds",
    )(lb_param)


def _hgrn_gates(q, fz, lb):
    a = jnp.log(jnp.maximum(lb, LOG_FLOOR))
    b = jnp.log1p(-lb) + _log_sigmoid(fz)
    log_f = jnp.maximum(a, b) + jnp.log1p(jnp.exp(-jnp.abs(a - b)))
    return _silu(q), log_f, (1.0 - lb) * _sigmoid(-fz)


def _group_norm_gate(o, gn, gate):
    return o * lax.rsqrt(jnp.mean(o * o, axis=-1, keepdims=True) + RMS_EPS) * gn * gate


def _block_rows(b, stride, offset):
    return [b[m * stride + offset:m * stride + offset + 1, :] for m in range(b.shape[0] // stride)]


def _spread_rows(rows, rowi, stride):
    out = rows[-1]
    for m in range(len(rows) - 2, -1, -1):
        out = jnp.where(rowi < (m + 1) * stride, rows[m], out)
    return out


def _cumsum_rows(tril_bf, x):
    hi = x.astype(BF16)
    r1 = x - hi.astype(F32)
    mid = r1.astype(BF16)
    lo = (r1 - mid.astype(F32)).astype(BF16)
    return _dot(tril_bf, hi) + _dot(tril_bf, mid) + _dot(tril_bf, lo)


def _hgrn_prompt_body(x_ref, g_ref, w_ref, lb_ref, gn_ref, o_ref, st_ref, q_scr, k_scr, lf_scr, v_scr, s_scr, *, tb):
    c, w = HG_CHUNK, HG_W

    @pl.when(pl.program_id(1) == 0)
    def _():
        s_scr[...] = jnp.zeros_like(s_scr)

    xn = _rms(x_ref[...], g_ref[...]).astype(BF16)
    qf, log_f, kf = _hgrn_gates(_dot(xn, w_ref[:, 0:w]), _dot(xn, w_ref[:, w:2 * w]), lb_ref[...])
    q_scr[...] = qf
    lf_scr[...] = log_f
    k_scr[...] = kf
    v_scr[...] = _dot(xn, w_ref[:, 2 * w:3 * w]).astype(BF16)
    o_ref[...] = _silu(_dot(xn, w_ref[:, 3 * w:4 * w]))

    row = lax.broadcasted_iota(jnp.int32, (c, c), 0)
    col = lax.broadcasted_iota(jnp.int32, (c, c), 1)
    tril = (row >= col).astype(BF16)
    rowi = lax.broadcasted_iota(jnp.int32, (c, 1), 0)
    masks = [(row // HG_BASE == col // HG_BASE) & (col <= row)]
    strides = []
    s = 2 * HG_BASE
    while s <= c:
        strides.append(s)
        masks.append(row // s == col // s)
        s *= 2

    def chunk(ci, carry):
        rows = pl.ds(pl.multiple_of(ci * c, c), c)
        q, k, vb = q_scr[rows, :], k_scr[rows, :], v_scr[rows, :]
        b = _cumsum_rows(tril, lf_scr[rows, :])
        mids = _block_rows(b, HG_BASE, HG_BASE // 2)
        d = b - _spread_rows(mids, rowi, HG_BASE)
        q0 = q * jnp.exp(d)
        k0 = k * jnp.exp(-d)
        qs, ks = [q0.astype(BF16)], [k0.astype(BF16)]
        for st in strides:
            x = jnp.exp(-jnp.abs(b - _spread_rows(_block_rows(b, st, st // 2), rowi, st)))
            upper = (rowi % st) >= st // 2
            qs.append(jnp.where(upper, q * x, 0.0).astype(BF16))
            ks.append(jnp.where(upper, 0.0, k * x).astype(BF16))
        b_last = b[c - 1:c, :]
        q_in = (q0 * _spread_rows([jnp.exp(m) for m in mids], rowi, HG_BASE)).astype(BF16)
        k_st = (k0 * _spread_rows([jnp.exp(b_last - m) for m in mids], rowi, HG_BASE)).astype(BF16)
        e_last = jnp.exp(b_last)
        for h in range(HG_HEADS):
            hs = slice(h * HG_DK, (h + 1) * HG_DK)
            a = None
            for qh, kh, mk in zip(qs, ks, masks):
                t = jnp.where(mk, _dot_nt(qh[:, hs], kh[:, hs]), 0.0)
                a = t if a is None else a + t
            st_t = s_scr[h]
            o = _dot(a.astype(BF16), vb[:, hs]) + _dot_nt(q_in[:, hs], st_t.astype(BF16))
            s_scr[h] = st_t * e_last[:, hs] + _dot_tn(vb[:, hs], k_st[:, hs])
            o_ref[rows, hs] = _group_norm_gate(o, gn_ref[...], o_ref[rows, hs])
        return carry

    lax.fori_loop(0, tb // c, chunk, 0, unroll=True)
    st_ref[...] = s_scr[...]


def _hgrn_prompt(x, g, w_bf, lb, gn, tb):
    b, l, _ = x.shape
    row = pl.BlockSpec((None, tb, D_MODEL), lambda i, j: (i, j, 0))
    return pl.pallas_call(
        functools.partial(_hgrn_prompt_body, tb=tb),
        grid=(b, l // tb),
        in_specs=[row, _resident((1, D_MODEL)), _resident((D_MODEL, 4 * HG_W)), _resident((1, HG_W)),
                  _resident((1, HG_DV))],
        out_specs=[row, pl.BlockSpec((None, HG_HEADS, HG_DV, HG_DK), lambda i, j: (i, 0, 0, 0))],
        out_shape=[jax.ShapeDtypeStruct((b, l, HG_W), F32),
                   jax.ShapeDtypeStruct((b, HG_HEADS, HG_DV, HG_DK), F32)],
        scratch_shapes=[pltpu.VMEM((tb, HG_W), F32)] * 3 + [pltpu.VMEM((tb, HG_W), BF16),
                                                             pltpu.VMEM((HG_HEADS, HG_DV, HG_DK), F32)],
        compiler_params=_cparams("parallel", "arbitrary"),
        name="hgrn_prompt",
    )(x, g, w_bf, lb, gn)


def _hgrn_sample_body(q_ref, fz_ref, iv_ref, gz_ref, s_ref, lb_ref, gn_ref, *rest, steps, bb, layer, creates):
    o_ref, so_all = rest[-2:]
    so_ref = so_all.at[layer] if creates else so_all
    if creates:
        for other in range(so_all.shape[0]):
            if other != layer:
                so_all[other] = jnp.zeros(so_all.shape[1:], F32)
    n = bb * steps
    qf, log_f, kf = _hgrn_gates(q_ref[...], fz_ref[...], lb_ref[...])
    v = iv_ref[...]
    gate = _silu(gz_ref[...])
    row = lax.broadcasted_iota(jnp.int32, (n, n), 0)
    col = lax.broadcasted_iota(jnp.int32, (n, n), 1)
    same = row // steps == col // steps
    causal = same & (col <= row)
    b = _dot_exact(causal.astype(F32), log_f)
    total = _dot_exact(same.astype(F32), log_f)
    first = (same & (col % steps == 0)).astype(F32)
    e0 = b - _dot_exact(first, log_f)
    q0 = (qf * jnp.exp(e0)).astype(BF16)
    k0 = (kf * jnp.exp(-e0)).astype(BF16)
    q_in = (qf * jnp.exp(b)).astype(BF16)
    k_st = (kf * jnp.exp(total - b)).astype(BF16)
    vb = v.astype(BF16)
    pick = (lax.broadcasted_iota(jnp.int32, (bb, n), 1) == steps * lax.broadcasted_iota(jnp.int32, (bb, n), 0))
    e_seq = jnp.exp(_dot_exact(pick.astype(F32), total))
    e_rows = jnp.concatenate([e_seq[:, h * HG_DK:(h + 1) * HG_DK] for h in range(HG_HEADS)], axis=0)
    pad = HG_DK - e_rows.shape[0]
    if pad:
        e_rows = jnp.concatenate([e_rows, jnp.zeros((pad, HG_DK), F32)], axis=0)
    e_cols = e_rows.T
    for h in range(HG_HEADS):
        hs = slice(h * HG_DK, (h + 1) * HG_DK)
        a = jnp.where(causal, _dot_nt(q0[:, hs], k0[:, hs]), 0.0)
        o_intra = _dot(a.astype(BF16), vb[:, hs])
        outs = []
        for i in range(bb):
            rs = slice(i * steps, (i + 1) * steps)
            s_old = s_ref[i, h]
            outs.append(_dot(q_in[rs, hs], s_old.astype(BF16)))
            so_ref[i, h] = (s_old * e_cols[:, h * bb + i:h * bb + i + 1]
                            + _dot_tn(k_st[rs, hs], vb[rs, hs]))
        o = o_intra + jnp.concatenate(outs, axis=0)
        o_ref[:, hs] = _group_norm_gate(o, gn_ref[...], gate[:, hs])


def _hgrn_sample(proj4, states, layer, out_states, lb, gn, steps):
    n_layers, b = states.shape[0], states.shape[1]
    bb = SAMPLE_SEQ_BLOCK
    assert HG_HEADS * bb <= HG_DK
    creates = out_states is None
    rows = pl.BlockSpec((bb * steps, HG_W), lambda i: (i, 0))
    sshape = (bb, HG_HEADS, HG_DK, HG_DV)
    s_in = pl.BlockSpec((None,) + sshape, lambda i: (layer, i, 0, 0, 0))
    s_out = pl.BlockSpec((n_layers,) + sshape, lambda i: (0, i, 0, 0, 0)) if creates else s_in
    in_specs = [rows, rows, rows, rows, s_in, _resident((1, HG_W)), _resident((1, HG_DV))]
    args = [*proj4, states, lb, gn]
    aliases = {}
    if not creates:
        in_specs.append(pl.BlockSpec(memory_space=pl.ANY))
        args.append(out_states)
        aliases = {len(args) - 1: 1}
    return pl.pallas_call(
        functools.partial(_hgrn_sample_body, steps=steps, bb=bb, layer=layer, creates=creates),
        grid=(b // bb,),
        in_specs=in_specs,
        out_specs=[rows, s_out],
        out_shape=[jax.ShapeDtypeStruct((b * steps, HG_W), F32), jax.ShapeDtypeStruct(states.shape, F32)],
        input_output_aliases=aliases,
        compiler_params=_cparams("parallel"),
        name="hgrn_sample",
    )(*args)


def _trunk(x, cache_k, cache_v, st_re, st_im, st_h, p):
    prompt = cache_k is None
    bt, l, _ = x.shape
    tm = min(TOKEN_BLOCK, bt * l)
    depth = p["norm_mix_pre"].shape[0]
    kvw = N_KV * HEAD_DIM
    row1 = lambda a: a.reshape(1, -1)
    x2 = x.reshape(bt * l, D_MODEL)
    k_out, v_out, re_out, im_out, h_out = [], [], [], [], []
    h_all = None
    if not prompt:
        cache_k = cache_k.reshape(cache_k.shape[:3] + (kvw,))
        cache_v = cache_v.reshape(cache_v.shape[:3] + (kvw,))
    for layer in range(depth):
        idx = layer // 2
        g_pre = row1(p["norm_mix_pre"][layer])
        if layer % 2 == 0:
            u, q, k, v = _norm_proj(x2, g_pre, p["w_in_even"][idx], (S5_WIDTH, SWA_WIDTH, kvw, kvw), tm)
            prep = p["s5_prep"][idx]
            d_skip, wglu = row1(p["s5_d"][idx]), p["s5_w_glu"][idx]
            q3, k3, v3 = q.reshape(bt, l, SWA_WIDTH), k.reshape(bt, l, kvw), v.reshape(bt, l, kvw)
            if prompt:
                a_out, h_fin = _s5_prompt(u.reshape(bt, l, S5_WIDTH), prep, d_skip, wglu, min(TOKEN_BLOCK, l))
                h_fin = h_fin.reshape(bt, -1)
                b_out = _swa_prompt(q3, k3, v3, p["swa_sinks"][idx], p["t5_band"])
                kw, vw = k3[:, l - WINDOW:], v3[:, l - WINDOW:]
            else:
                u_t = u.reshape(bt, l, S5_WIDTH).transpose(1, 0, 2)
                a_t, h_fin = _s5_sample(u_t, _s5_state_in(st_re[idx], st_im[idx]), prep, d_skip, wglu)
                a_out = a_t.transpose(1, 0, 2)
                b_out, kw, vw = _swa_sample(q3, k3, v3, cache_k, cache_v, idx, p["swa_sinks"][idx], p["t5_band"])
            h_re, h_im = _s5_state_out(h_fin)
            k_out.append(kw.reshape(bt, WINDOW, N_KV, HEAD_DIM))
            v_out.append(vw.reshape(bt, WINDOW, N_KV, HEAD_DIM))
            re_out.append(h_re)
            im_out.append(h_im)
            mixer_outs = (a_out.reshape(bt * l, S5_WIDTH), b_out.reshape(bt * l, SWA_WIDTH))
            w_out = p["w_out_even"][idx]
        else:
            lb, gn = p["hgrn_lb"][idx:idx + 1], row1(p["hgrn_gnorm"][idx])
            if prompt:
                c_out, s_t = _hgrn_prompt(x2.reshape(bt, l, D_MODEL), g_pre, p["w_in_odd"][idx], lb, gn,
                                          min(TOKEN_BLOCK, l))
                h_out.append(s_t.transpose(0, 1, 3, 2))
            else:
                proj4 = _norm_proj(x2, g_pre, p["w_in_odd"][idx], (HG_W,) * 4, tm)
                c_out, h_all = _hgrn_sample(proj4, st_h, idx, h_all, lb, gn, l)
            mixer_outs = (c_out.reshape(bt * l, HG_W),)
            w_out = p["w_out_odd"][idx]
        x2 = _post(x2, mixer_outs, w_out, row1(p["norm_mix_post"][layer]), row1(p["norm_mlp_pre"][layer]),
                   row1(p["norm_mlp_post"][layer]), p["w_up"][layer], p["w_down"][layer], tm)
    return (x2.reshape(bt, l, D_MODEL), jnp.stack(k_out), jnp.stack(v_out), jnp.stack(re_out), jnp.stack(im_out),
            jnp.stack(h_out) if prompt else h_all)


def kernel(x_prompt, x_sample, cache_swa_k, cache_swa_v, state_s5_re, state_s5_im, state_hgrn, t5_bias_table,
           norm_mix_pre, norm_mix_post, norm_mlp_pre, norm_mlp_post, w_in_even, w_out_even, s5_lambda_re,
           s5_lambda_im, s5_log_step, s5_b_re, s5_b_im, s5_c_re, s5_c_im, s5_d, s5_w_glu, swa_sinks, w_in_odd,
           w_out_odd, hgrn_lb_param, hgrn_gnorm, w_up, w_down):
    bf = lambda a: a.astype(BF16)
    n_even = w_in_even.shape[0]
    q0, q1 = S5_WIDTH, S5_WIDTH + SWA_WIDTH
    w_in_even = jnp.concatenate([w_in_even[..., :q0], _pair_heads(w_in_even[..., q0:q1], 2), w_in_even[..., q1:]],
                                axis=-1)
    w_out_even = jnp.concatenate([w_out_even[:, :q0], _pair_heads(w_out_even[:, q0:], 1)], axis=1)
    params = dict(
        norm_mix_pre=norm_mix_pre, norm_mix_post=norm_mix_post, norm_mlp_pre=norm_mlp_pre,
        norm_mlp_post=norm_mlp_post, w_in_even=bf(w_in_even), w_out_even=bf(w_out_even), s5_d=s5_d,
        s5_w_glu=bf(s5_w_glu), swa_sinks=swa_sinks, w_in_odd=bf(w_in_odd), w_out_odd=bf(w_out_odd),
        hgrn_gnorm=hgrn_gnorm, w_up=bf(w_up), w_down=bf(w_down),
        t5_band=_t5_band(t5_bias_table),
        hgrn_lb=_hgrn_lower_bounds(hgrn_lb_param),
        s5_prep=[_s5_prep(s5_lambda_re[i], s5_lambda_im[i], s5_log_step[i], s5_b_re[i], s5_b_im[i], s5_c_re[i],
                          s5_c_im[i]) for i in range(n_even)],
    )
    y_prompt, k_p, v_p, re_p, im_p, hg_p = _trunk(x_prompt, None, None, None, None, None, params)
    y_sample, k_s, v_s, re_s, im_s, hg_s = _trunk(x_sample, cache_swa_k, cache_swa_v, state_s5_re, state_s5_im,
                                                  state_hgrn, params)
    return (y_prompt, y_sample, k_p, v_p, k_s, v_s, re_p, im_p, re_s, im_s, hg_p, hg_s)
```

```python
import functools
import math

import numpy as np
import jax
import jax.numpy as jnp
from jax import lax
from jax.experimental import pallas as pl
from jax.experimental.pallas import tpu as pltpu

F32 = jnp.float32
BF16 = jnp.bfloat16

D_MODEL = 1024
S5_WIDTH = 512
S5_GROUPS = 32
S5_GROUP = 16
S5_STATE = 64
S5_HALF_STATES = 1024
S5_HALF_CH = 256
SWA_WIDTH = 512
HEAD_DIM = 64
N_HEADS = 8
N_KV = 2
KV_GROUP = N_HEADS // N_KV
WINDOW = 128
T5_BUCKETS = 32
T5_MAX_DIST = 128
HG_HEADS = 8
HG_DK = 128
HG_DV = 128
HG_W = HG_HEADS * HG_DK
D_FF = 4096
RMS_EPS = 1e-6
NEG_INF = -1e30
LOG_FLOOR = 1e-30

S5_CHUNK = 64
HG_CHUNK = 128
HG_BASE = 32
TOKEN_BLOCK = 512
MLP_FF_BLOCK = 1024
SAMPLE_SEQ_BLOCK = 8
VMEM_LIMIT = 56 * 1024 * 1024
MXU_WIDTH = 256


def _cparams(*sem):
    return pltpu.CompilerParams(dimension_semantics=sem, vmem_limit_bytes=VMEM_LIMIT)


def _resident(shape):
    nd = len(shape)
    return pl.BlockSpec(shape, lambda *_: (0,) * nd, pipeline_mode=pl.Buffered(1))


def _rms(x, w):
    return x * lax.rsqrt(jnp.mean(x * x, axis=-1, keepdims=True) + RMS_EPS) * w


def _sigmoid(x):
    return 1.0 / (1.0 + jnp.exp(-x))


def _silu(x):
    return x * _sigmoid(x)


def _gelu_tanh(x):
    return 0.5 * x * (1.0 + jnp.tanh(math.sqrt(2.0 / math.pi) * (x + 0.044715 * (x * x * x))))


def _dot(a, b):
    return jnp.dot(a, b, preferred_element_type=F32)


def _dot_nt(a, b):
    return lax.dot_general(a, b, (((1,), (1,)), ((), ())), preferred_element_type=F32)


def _dot_tn(a, b):
    return lax.dot_general(a, b, (((0,), (0,)), ((), ())), preferred_element_type=F32)


def _dot_exact(a, b):
    return jnp.dot(a, b, preferred_element_type=F32, precision=lax.Precision.HIGHEST)


def _norm_proj_body(x_ref, g_ref, w_ref, *o_refs, splits):
    xn = _rms(x_ref[...], g_ref[...]).astype(BF16)
    off = 0
    i = 0
    while i < len(splits):
        j = i + 1
        while sum(splits[i:j]) < MXU_WIDTH and j < len(splits):
            j += 1
        y = _dot(xn, w_ref[:, off:off + sum(splits[i:j])])
        sub = 0
        for o_ref, wd in zip(o_refs[i:j], splits[i:j]):
            o_ref[...] = y[:, sub:sub + wd]
            sub += wd
        off += sub
        i = j


def _norm_proj(x2d, g, w_bf, splits, tm):
    t, n = x2d.shape[0], w_bf.shape[1]
    return pl.pallas_call(
        functools.partial(_norm_proj_body, splits=splits),
        grid=(t // tm,),
        in_specs=[pl.BlockSpec((tm, D_MODEL), lambda i: (i, 0)),
                  _resident((1, D_MODEL)), _resident((D_MODEL, n))],
        out_specs=[pl.BlockSpec((tm, wd), lambda i: (i, 0)) for wd in splits],
        out_shape=[jax.ShapeDtypeStruct((t, wd), F32) for wd in splits],
        compiler_params=_cparams("parallel"),
        name="norm_proj",
    )(x2d, g, w_bf)


def _post_body(*refs, widths):
    x_ref = refs[0]
    in_refs = refs[1:1 + len(widths)]
    wout_ref, gpost_ref, gpre_ref, gmpost_ref, wup_ref, wdn_ref, o_ref = refs[1 + len(widths):]
    mix = None
    off = 0
    for r, wd in zip(in_refs, widths):
        t = _dot(r[...].astype(BF16), wout_ref[off:off + wd, :])
        mix = t if mix is None else mix + t
        off += wd
    x1 = x_ref[...] + _rms(mix, gpost_ref[...])
    hn = _rms(x1, gpre_ref[...]).astype(BF16)
    acc = None
    for c in range(D_FF // MLP_FF_BLOCK):
        cs = slice(c * MLP_FF_BLOCK, (c + 1) * MLP_FF_BLOCK)
        hk = jnp.square(jnp.maximum(_dot(hn, wup_ref[:, cs]), 0.0)).astype(BF16)
        t = _dot(hk, wdn_ref[cs, :])
        acc = t if acc is None else acc + t
    o_ref[...] = x1 + _rms(acc, gmpost_ref[...])


def _post(x2d, mixer_outs, wout_bf, gpost, gpre, gmpost, wup_bf, wdn_bf, tm):
    t = x2d.shape[0]
    widths = tuple(a.shape[1] for a in mixer_outs)
    row = lambda wd: pl.BlockSpec((tm, wd), lambda i: (i, 0))
    return pl.pallas_call(
        functools.partial(_post_body, widths=widths),
        grid=(t // tm,),
        in_specs=[row(D_MODEL)] + [row(wd) for wd in widths] + [
            _resident((D_MODEL, D_MODEL)), _resident((1, D_MODEL)), _resident((1, D_MODEL)),
            _resident((1, D_MODEL)), _resident((D_MODEL, D_FF)), _resident((D_FF, D_MODEL))],
        out_specs=row(D_MODEL),
        out_shape=jax.ShapeDtypeStruct((t, D_MODEL), F32),
        compiler_params=_cparams("parallel"),
        name="post_mlp",
    )(x2d, *mixer_outs, wout_bf, gpost, gpre, gmpost, wup_bf, wdn_bf)


def _bucket_map():
    t = np.arange(WINDOW)[:, None]
    m = np.arange(2 * WINDOW)[None, :]
    n = np.maximum(WINDOW + t - m, 0)
    max_exact = T5_BUCKETS // 2
    large = max_exact + np.floor(np.log(np.maximum(n, max_exact) / max_exact)
                                 / math.log(T5_MAX_DIST / max_exact) * (T5_BUCKETS - max_exact)).astype(np.int64)
    large = np.minimum(large, T5_BUCKETS - 1)
    return np.where(n < max_exact, n, large).astype(np.int32)


def _bias_body(tab_ref, bm_ref, o_ref):
    bm = bm_ref[...]
    row = lax.broadcasted_iota(jnp.int32, bm.shape, 0)
    col = lax.broadcasted_iota(jnp.int32, bm.shape, 1)
    dist = WINDOW + row - col
    in_win = (dist >= 0) & (dist < WINDOW)
    for h in range(N_HEADS):
        acc = jnp.zeros(bm.shape, F32)
        for b in range(T5_BUCKETS):
            acc = jnp.where(bm == b, tab_ref[b, h], acc)
        o_ref[0, h] = jnp.where(in_win & (col >= WINDOW), acc, NEG_INF)
        o_ref[1, h] = jnp.where(in_win, acc, NEG_INF)


def _t5_band(table):
    return pl.pallas_call(
        _bias_body,
        in_specs=[pl.BlockSpec(memory_space=pltpu.SMEM),
                  pl.BlockSpec((WINDOW, 2 * WINDOW), lambda: (0, 0))],
        out_specs=pl.BlockSpec((2, N_HEADS, WINDOW, 2 * WINDOW), lambda: (0, 0, 0, 0)),
        out_shape=jax.ShapeDtypeStruct((2, N_HEADS, WINDOW, 2 * WINDOW), F32),
        name="t5_band",
    )(table, jnp.asarray(_bucket_map()))


def _pair_heads(a, axis):
    shp = a.shape
    a = a.reshape(shp[:axis] + (N_KV, KV_GROUP, HEAD_DIM) + shp[axis + 1:])
    a = jnp.swapaxes(a, axis, axis + 1)
    return a.reshape(shp)


def _swa_prompt_body(sink_ref, q_ref, kc_ref, kp_ref, vc_ref, vp_ref, band_ref, o_ref):
    tile = N_KV * HEAD_DIM
    kb = jnp.concatenate([kp_ref[...], kc_ref[...]], axis=0).astype(BF16)
    vb = jnp.concatenate([vp_ref[...], vc_ref[...]], axis=0).astype(BF16)
    v_ext = jnp.concatenate([vb, jnp.ones_like(vb)], axis=1)
    lo = lax.broadcasted_iota(jnp.int32, (WINDOW, tile), 1) < HEAD_DIM
    for t in range(KV_GROUP):
        qt = q_ref[:, t * tile:(t + 1) * tile] * (HEAD_DIM ** -0.5)
        halves = []
        for kv in range(N_KV):
            h = kv * KV_GROUP + t
            qh = jnp.where(lo if kv == 0 else jnp.logical_not(lo), qt, 0.0).astype(BF16)
            s = _dot_nt(qh, kb) + band_ref[h]
            sink = sink_ref[h]
            m = jnp.maximum(jnp.max(s, axis=-1, keepdims=True), sink)
            oe = _dot(jnp.exp(s - m).astype(BF16), v_ext)
            halves.append(oe[:, 0:tile] / (oe[:, tile:tile + 1] + jnp.exp(sink - m)))
        o_ref[:, t * tile:(t + 1) * tile] = jnp.where(lo, halves[0], halves[1])


def _swa_prompt(q, k, v, sinks, band):
    b, l, _ = q.shape
    kvw = N_KV * HEAD_DIM
    cur = lambda w: pl.BlockSpec((None, WINDOW, w), lambda i, n: (i, n, 0))
    prev = lambda w: pl.BlockSpec((None, WINDOW, w), lambda i, n: (i, jnp.maximum(n - 1, 0), 0))
    return pl.pallas_call(
        _swa_prompt_body,
        grid=(b, l // WINDOW),
        in_specs=[pl.BlockSpec(memory_space=pltpu.SMEM), cur(SWA_WIDTH), cur(kvw), prev(kvw), cur(kvw), prev(kvw),
                  pl.BlockSpec((None, N_HEADS, WINDOW, 2 * WINDOW), lambda i, n: (jnp.minimum(n, 1), 0, 0, 0))],
        out_specs=cur(SWA_WIDTH),
        out_shape=jax.ShapeDtypeStruct((b, l, SWA_WIDTH), F32),
        compiler_params=_cparams("parallel", "arbitrary"),
        name="swa_prompt",
    )(sinks, q, k, k, v, v, band)


def _swa_sample_body(q_ref, kn_ref, vn_ref, ck_ref, cv_ref, bias_ref, sink_ref, o_ref, ko_ref, vo_ref, *, steps):
    tile = N_KV * HEAD_DIM
    ck, cv, kn, vn = ck_ref[...], cv_ref[...], kn_ref[...], vn_ref[...]
    ko_ref[:, 0:WINDOW - steps, :] = ck[:, steps:, :]
    ko_ref[:, WINDOW - steps:, :] = kn
    vo_ref[:, 0:WINDOW - steps, :] = cv[:, steps:, :]
    vo_ref[:, WINDOW - steps:, :] = vn
    lo = lax.broadcasted_iota(jnp.int32, (1, 1, tile), 2) < HEAD_DIM
    q = q_ref[...] * (HEAD_DIM ** -0.5)
    qs = []
    for kv in range(N_KV):
        for t in range(KV_GROUP):
            qs.append(jnp.where(lo if kv == 0 else jnp.logical_not(lo), q[:, :, t * tile:(t + 1) * tile], 0.0))
    qall = jnp.concatenate(qs, axis=1).astype(BF16)
    bias = bias_ref[...]
    sink = sink_ref[...][None]
    bdot = lambda a, b_, spec: jnp.einsum(spec, a, b_, preferred_element_type=F32)
    sc = bdot(qall, ck.astype(BF16), "bqd,bkd->bqk") + bias[None, :, 0:WINDOW]
    sn = bdot(qall, kn.astype(BF16), "bqd,bkd->bqk") + bias[None, :, WINDOW:WINDOW + steps]
    m = jnp.maximum(jnp.maximum(jnp.max(sc, axis=-1, keepdims=True), jnp.max(sn, axis=-1, keepdims=True)), sink)
    ec, en = jnp.exp(sc - m), jnp.exp(sn - m)
    den = jnp.sum(ec, axis=-1, keepdims=True) + jnp.sum(en, axis=-1, keepdims=True) + jnp.exp(sink - m)
    o = (bdot(ec.astype(BF16), cv.astype(BF16), "bqk,bkd->bqd")
         + bdot(en.astype(BF16), vn.astype(BF16), "bqk,bkd->bqd")) / den
    for t in range(KV_GROUP):
        o_ref[:, :, t * tile:(t + 1) * tile] = jnp.where(
            lo, o[:, t * steps:(t + 1) * steps, :], o[:, (KV_GROUP + t) * steps:(KV_GROUP + t + 1) * steps, :])


def _swa_sample(q, kn, vn, cache_k, cache_v, idx, sinks, band):
    b, steps, _ = q.shape
    kvw = N_KV * HEAD_DIM
    bb = SAMPLE_SEQ_BLOCK
    nrow = N_HEADS * steps
    bias = band[1, :, 0:steps, :].reshape(nrow, 2 * WINDOW)
    sink_rows = jnp.repeat(sinks, steps).reshape(nrow, 1)
    blk = lambda r, w: pl.BlockSpec((bb, r, w), lambda i: (i, 0, 0))
    cache = pl.BlockSpec((None, bb, WINDOW, kvw), lambda i: (idx, i, 0, 0))
    return pl.pallas_call(
        functools.partial(_swa_sample_body, steps=steps),
        grid=(b // bb,),
        in_specs=[blk(steps, SWA_WIDTH), blk(steps, kvw), blk(steps, kvw), cache, cache,
                  _resident((nrow, 2 * WINDOW)), _resident((nrow, 1))],
        out_specs=[blk(steps, SWA_WIDTH), blk(WINDOW, kvw), blk(WINDOW, kvw)],
        out_shape=[jax.ShapeDtypeStruct((b, steps, SWA_WIDTH), F32),
                   jax.ShapeDtypeStruct((b, WINDOW, kvw), F32),
                   jax.ShapeDtypeStruct((b, WINDOW, kvw), F32)],
        compiler_params=_cparams("parallel"),
        name="swa_sample",
    )(q, kn, vn, cache_k, cache_v, bias, sink_rows)


def _s5_prep_body(lr_ref, li_ref, ls_ref, br_ref, bi_ref, cr_ref, ci_ref, bm_ref, cm_ref, pw_ref, pinv_ref):
    hs = S5_HALF_STATES
    lr, li = lr_ref[...], li_ref[...]
    dt = jnp.exp(ls_ref[...])
    mag = jnp.exp(lr * dt)
    ang = li * dt
    ar = mag * jnp.cos(ang)
    ai = mag * jnp.sin(ang)
    den = lr * lr + li * li
    nr = ar - 1.0
    kr = (nr * lr + ai * li) / den
    ki = (ai * lr - nr * li) / den
    br, bi = br_ref[...], bi_ref[...]
    bm_ref[:, 0:hs] = (kr * br - ki * bi).astype(BF16)
    bm_ref[:, hs:2 * hs] = (kr * bi + ki * br).astype(BF16)
    cm_ref[0:hs, :] = cr_ref[...].astype(BF16)
    cm_ref[hs:2 * hs, :] = (-ci_ref[...]).astype(BF16)
    n2 = ar * ar + ai * ai
    ir, ii = ar / n2, -ai / n2
    pr, pi = jnp.ones_like(ar), jnp.zeros_like(ar)
    qr, qi = pr, pi
    for j in range(S5_CHUNK):
        pw_ref[0, j:j + 1, :] = pr
        pw_ref[1, j:j + 1, :] = pi
        pinv_ref[0, j:j + 1, :] = qr
        pinv_ref[1, j:j + 1, :] = qi
        pr, pi = pr * ar - pi * ai, pr * ai + pi * ar
        qr, qi = qr * ir - qi * ii, qr * ii + qi * ir


def _s5_prep(lam_re, lam_im, log_step, b_re, b_im, c_re, c_im):
    hs, hc = S5_HALF_STATES, S5_HALF_CH
    gh = S5_GROUPS // 2
    eye = jnp.eye(gh, dtype=bool)
    vec = lambda a: a.reshape(2, 1, hs)
    ls = jnp.broadcast_to(log_step[:, None], (S5_GROUPS, S5_STATE))

    def b_blockdiag(b):
        bt = b.reshape(2, gh, S5_STATE, S5_GROUP).transpose(0, 3, 1, 2)[:, None]
        full = jnp.where(eye[None, :, None, :, None], bt, 0.0)
        return full.reshape(2, hc, hs)

    def c_blockdiag(c):
        ct = c.reshape(2, gh, S5_GROUP, S5_STATE).transpose(0, 1, 3, 2)[:, :, :, None]
        full = jnp.where(eye[None, :, None, :, None], ct, 0.0)
        return full.reshape(2, hs, hc)

    vspec = pl.BlockSpec((None, 1, hs), lambda h: (h, 0, 0))
    bspec = pl.BlockSpec((None, hc, hs), lambda h: (h, 0, 0))
    cspec = pl.BlockSpec((None, hs, hc), lambda h: (h, 0, 0))
    pspec = pl.BlockSpec((None, 2, S5_CHUNK, hs), lambda h: (h, 0, 0, 0))
    return pl.pallas_call(
        _s5_prep_body,
        grid=(2,),
        in_specs=[vspec, vspec, vspec, bspec, bspec, cspec, cspec],
        out_specs=[pl.BlockSpec((None, hc, 2 * hs), lambda h: (h, 0, 0)),
                   pl.BlockSpec((None, 2 * hs, hc), lambda h: (h, 0, 0)), pspec, pspec],
        out_shape=[jax.ShapeDtypeStruct((2, hc, 2 * hs), BF16), jax.ShapeDtypeStruct((2, 2 * hs, hc), BF16),
                   jax.ShapeDtypeStruct((2, 2, S5_CHUNK, hs), F32), jax.ShapeDtypeStruct((2, 2, S5_CHUNK, hs), F32)],
        compiler_params=_cparams("parallel"),
        name="s5_prep",
    )(vec(lam_re), vec(lam_im), vec(ls), b_blockdiag(b_re), b_blockdiag(b_im), c_blockdiag(c_re), c_blockdiag(c_im))


def _s5_glu(y, u, d_ref, wg_ref):
    g = _gelu_tanh(y + d_ref[...] * u)
    return g * _sigmoid(_dot(g.astype(BF16), wg_ref[...]))


def _s5_prompt_body(u_ref, bm_ref, cm_ref, pw_ref, pinv_ref, d_ref, wg_ref, o_ref, hfin_ref, bu_scr, hb_scr,
                    h_scr, *, tb):
    hs, hc, c0 = S5_HALF_STATES, S5_HALF_CH, S5_CHUNK

    @pl.when(pl.program_id(1) == 0)
    def _():
        h_scr[...] = jnp.zeros_like(h_scr)

    u = u_ref[...]
    ub = u.astype(BF16)
    tril = (lax.broadcasted_iota(jnp.int32, (c0, c0), 0) >= lax.broadcasted_iota(jnp.int32, (c0, c0), 1)).astype(BF16)
    ys = []
    for hf in range(2):
        re = slice(2 * hf * hs, (2 * hf + 1) * hs)
        im = slice((2 * hf + 1) * hs, (2 * hf + 2) * hs)
        both = slice(2 * hf * hs, (2 * hf + 2) * hs)
        bu_scr[:, both] = _dot(ub[:, hf * hc:(hf + 1) * hc], bm_ref[hf])
        ar, ai = pw_ref[hf, 0, 1:2, :], pw_ref[hf, 1, 1:2, :]
        hr0, hi0 = h_scr[:, re], h_scr[:, im]
        for ci in range(tb // c0):
            rows = slice(ci * c0, (ci + 1) * c0)
            bur, bui = bu_scr[rows, re], bu_scr[rows, im]
            qr, qi = pinv_ref[hf, 0], pinv_ref[hf, 1]
            sr = _dot(tril, (qr * bur - qi * bui).astype(BF16)) + (ar * hr0 - ai * hi0)
            si = _dot(tril, (qr * bui + qi * bur).astype(BF16)) + (ar * hi0 + ai * hr0)
            pr, pi = pw_ref[hf, 0], pw_ref[hf, 1]
            hr = pr * sr - pi * si
            hi = pr * si + pi * sr
            hb_scr[rows, re] = hr.astype(BF16)
            hb_scr[rows, im] = hi.astype(BF16)
            hr0, hi0 = hr[c0 - 1:c0, :], hi[c0 - 1:c0, :]
        h_scr[:, re] = hr0
        h_scr[:, im] = hi0
        ys.append(_dot(hb_scr[:, both], cm_ref[hf]))
    o_ref[...] = _s5_glu(jnp.concatenate(ys, axis=1), u, d_ref, wg_ref)
    hfin_ref[...] = h_scr[...]


def _s5_prompt(u, prep, d_skip, wglu_bf, tb):
    b, l, _ = u.shape
    bm, cm, pw, pinv = prep
    ns = 4 * S5_HALF_STATES
    return pl.pallas_call(
        functools.partial(_s5_prompt_body, tb=tb),
        grid=(b, l // tb),
        in_specs=[pl.BlockSpec((None, tb, S5_WIDTH), lambda i, j: (i, j, 0)),
                  _resident(bm.shape), _resident(cm.shape), _resident(pw.shape), _resident(pinv.shape),
                  _resident((1, S5_WIDTH)), _resident((S5_WIDTH, S5_WIDTH))],
        out_specs=[pl.BlockSpec((None, tb, S5_WIDTH), lambda i, j: (i, j, 0)),
                   pl.BlockSpec((None, 1, ns), lambda i, j: (i, 0, 0))],
        out_shape=[jax.ShapeDtypeStruct((b, l, S5_WIDTH), F32), jax.ShapeDtypeStruct((b, 1, ns), F32)],
        scratch_shapes=[pltpu.VMEM((tb, ns), F32), pltpu.VMEM((tb, ns), BF16), pltpu.VMEM((1, ns), F32)],
        compiler_params=_cparams("parallel", "arbitrary"),
        name="s5_prompt",
    )(u, bm, cm, pw, pinv, d_skip, wglu_bf)


def _s5_sample_body(u_ref, h0_ref, bm_ref, cm_ref, pw_ref, d_ref, wg_ref, o_ref, hfin_ref, *, steps):
    hs, hc = S5_HALF_STATES, S5_HALF_CH
    h = [h0_ref[:, k * hs:(k + 1) * hs] for k in range(4)]
    for t in range(steps):
        u = u_ref[t]
        ub = u.astype(BF16)
        ys = []
        for hf in range(2):
            bu = _dot(ub[:, hf * hc:(hf + 1) * hc], bm_ref[hf])
            ar, ai = pw_ref[hf, 0, 1:2, :], pw_ref[hf, 1, 1:2, :]
            hr, hi = h[2 * hf], h[2 * hf + 1]
            h[2 * hf] = ar * hr - ai * hi + bu[:, 0:hs]
            h[2 * hf + 1] = ar * hi + ai * hr + bu[:, hs:2 * hs]
            ys.append(_dot(jnp.concatenate([h[2 * hf], h[2 * hf + 1]], axis=1).astype(BF16), cm_ref[hf]))
        o_ref[t] = _s5_glu(jnp.concatenate(ys, axis=1), u, d_ref, wg_ref)
    for k in range(4):
        hfin_ref[:, k * hs:(k + 1) * hs] = h[k]


def _s5_sample(u_t, h0, prep, d_skip, wglu_bf):
    steps, b, _ = u_t.shape
    bm, cm, pw, _ = prep
    ns = 4 * S5_HALF_STATES
    full = lambda shape: pl.BlockSpec(shape, lambda: (0,) * len(shape))
    return pl.pallas_call(
        functools.partial(_s5_sample_body, steps=steps),
        in_specs=[full(u_t.shape), full(h0.shape), full(bm.shape), full(cm.shape), full(pw.shape),
                  full((1, S5_WIDTH)), full((S5_WIDTH, S5_WIDTH))],
        out_specs=[full((steps, b, S5_WIDTH)), full((b, ns))],
        out_shape=[jax.ShapeDtypeStruct((steps, b, S5_WIDTH), F32), jax.ShapeDtypeStruct((b, ns), F32)],
        compiler_params=pltpu.CompilerParams(vmem_limit_bytes=VMEM_LIMIT),
        name="s5_sample",
    )(u_t, h0, bm, cm, pw, d_skip, wglu_bf)


def _s5_state_in(re, im):
    b = re.shape[0]
    return jnp.stack([re.reshape(b, 2, S5_HALF_STATES), im.reshape(b, 2, S5_HALF_STATES)], axis=2).reshape(b, -1)


def _s5_state_out(h):
    b = h.shape[0]
    h4 = h.reshape(b, 2, 2, S5_HALF_STATES)
    return (h4[:, :, 0].reshape(b, S5_GROUPS, S5_STATE), h4[:, :, 1].reshape(b, S5_GROUPS, S5_STATE))


def _lb_body(p_ref, o_ref):
    p = p_ref[...]
    e = jnp.exp(p - jnp.max(p, axis=0, keepdims=True))
    sm = e / jnp.sum(e, axis=0, keepdims=True)
    acc = jnp.zeros_like(sm[0:1])
    for i in range(p.shape[0]):
        acc = acc + sm[i:i + 1]
        o_ref[i:i + 1, :] = acc - sm[0:1]


def _hgrn_lower_bounds(lb_param):
    n = lb_param.shape[0]
    return pl.pallas_call(
        _lb_body,
        in_specs=[pl.BlockSpec((n, HG_W), lambda: (0, 0))],
        out_specs=pl.BlockSpec((n, HG_W), lambda: (0, 0)),
        out_shape=jax.ShapeDtypeStruct((n, HG_W), F32),
        name="hgrn_lower_bounds",
    )(lb_param)


def _hgrn_gates(q, fz, lb):
    log_f = jnp.log(jnp.maximum(lb, LOG_FLOOR) + (1.0 - lb) * _sigmoid(fz))
    return _silu(q), log_f, (1.0 - lb) * _sigmoid(-fz)


def _group_norm_gate(o, gn, gate):
    return o * lax.rsqrt(jnp.mean(o * o, axis=-1, keepdims=True) + RMS_EPS) * gn * gate


def _block_rows(b, stride, offset):
    return [b[m * stride + offset:m * stride + offset + 1, :] for m in range(b.shape[0] // stride)]


def _spread_rows(rows, rowi, stride):
    out = rows[-1]
    for m in range(len(rows) - 2, -1, -1):
        out = jnp.where(rowi < (m + 1) * stride, rows[m], out)
    return out


def _cumsum_rows(tril_bf, x):
    hi = x.astype(BF16)
    r1 = x - hi.astype(F32)
    mid = r1.astype(BF16)
    lo = (r1 - mid.astype(F32)).astype(BF16)
    return _dot(tril_bf, hi) + _dot(tril_bf, mid) + _dot(tril_bf, lo)


def _hgrn_prompt_body(x_ref, g_ref, w_ref, lb_ref, gn_ref, o_ref, st_ref, q_scr, k_scr, lf_scr, v_scr, s_scr, *, tb):
    c, w = HG_CHUNK, HG_W

    @pl.when(pl.program_id(1) == 0)
    def _():
        s_scr[...] = jnp.zeros_like(s_scr)

    xn = _rms(x_ref[...], g_ref[...]).astype(BF16)
    qf, log_f, kf = _hgrn_gates(_dot(xn, w_ref[:, 0:w]), _dot(xn, w_ref[:, w:2 * w]), lb_ref[...])
    q_scr[...] = qf
    lf_scr[...] = log_f
    k_scr[...] = kf
    v_scr[...] = _dot(xn, w_ref[:, 2 * w:3 * w]).astype(BF16)
    o_ref[...] = _silu(_dot(xn, w_ref[:, 3 * w:4 * w]))

    row = lax.broadcasted_iota(jnp.int32, (c, c), 0)
    col = lax.broadcasted_iota(jnp.int32, (c, c), 1)
    tril = (row >= col).astype(BF16)
    rowi = lax.broadcasted_iota(jnp.int32, (c, 1), 0)
    masks = [(row // HG_BASE == col // HG_BASE) & (col <= row)]
    strides = []
    s = 2 * HG_BASE
    while s <= c:
        strides.append(s)
        masks.append(row // s == col // s)
        s *= 2

    def chunk(ci, carry):
        rows = pl.ds(pl.multiple_of(ci * c, c), c)
        q, k, vb = q_scr[rows, :], k_scr[rows, :], v_scr[rows, :]
        b = _cumsum_rows(tril, lf_scr[rows, :])
        mids = _block_rows(b, HG_BASE, HG_BASE // 2)
        d = b - _spread_rows(mids, rowi, HG_BASE)
        q0 = q * jnp.exp(d)
        k0 = k * jnp.exp(-d)
        qs, ks = [q0.astype(BF16)], [k0.astype(BF16)]
        for st in strides:
            x = jnp.exp(-jnp.abs(b - _spread_rows(_block_rows(b, st, st // 2), rowi, st)))
            upper = (rowi % st) >= st // 2
            qs.append(jnp.where(upper, q * x, 0.0).astype(BF16))
            ks.append(jnp.where(upper, 0.0, k * x).astype(BF16))
        b_last = b[c - 1:c, :]
        q_in = (q0 * _spread_rows([jnp.exp(m) for m in mids], rowi, HG_BASE)).astype(BF16)
        k_st = (k0 * _spread_rows([jnp.exp(b_last - m) for m in mids], rowi, HG_BASE)).astype(BF16)
        e_last = jnp.exp(b_last)
        for h in range(HG_HEADS):
            hs = slice(h * HG_DK, (h + 1) * HG_DK)
            a = None
            for qh, kh, mk in zip(qs, ks, masks):
                t = jnp.where(mk, _dot_nt(qh[:, hs], kh[:, hs]), 0.0)
                a = t if a is None else a + t
            st_t = s_scr[h]
            o = _dot(a.astype(BF16), vb[:, hs]) + _dot_nt(q_in[:, hs], st_t.astype(BF16))
            s_scr[h] = st_t * e_last[:, hs] + _dot_tn(vb[:, hs], k_st[:, hs])
            o_ref[rows, hs] = _group_norm_gate(o, gn_ref[...], o_ref[rows, hs])
        return carry

    lax.fori_loop(0, tb // c, chunk, 0, unroll=True)
    st_ref[...] = s_scr[...]


def _hgrn_prompt(x, g, w_bf, lb, gn, tb):
    b, l, _ = x.shape
    row = pl.BlockSpec((None, tb, D_MODEL), lambda i, j: (i, j, 0))
    return pl.pallas_call(
        functools.partial(_hgrn_prompt_body, tb=tb),
        grid=(b, l // tb),
        in_specs=[row, _resident((1, D_MODEL)), _resident((D_MODEL, 4 * HG_W)), _resident((1, HG_W)),
                  _resident((1, HG_DV))],
        out_specs=[row, pl.BlockSpec((None, HG_HEADS, HG_DV, HG_DK), lambda i, j: (i, 0, 0, 0))],
        out_shape=[jax.ShapeDtypeStruct((b, l, HG_W), F32),
                   jax.ShapeDtypeStruct((b, HG_HEADS, HG_DV, HG_DK), F32)],
        scratch_shapes=[pltpu.VMEM((tb, HG_W), F32)] * 3 + [pltpu.VMEM((tb, HG_W), BF16),
                                                             pltpu.VMEM((HG_HEADS, HG_DV, HG_DK), F32)],
        compiler_params=_cparams("parallel", "arbitrary"),
        name="hgrn_prompt",
    )(x, g, w_bf, lb, gn)


def _hgrn_sample_body(q_ref, fz_ref, iv_ref, gz_ref, s_ref, lb_ref, gn_ref, *rest, steps, bb, layer, creates):
    o_ref, so_all = rest[-2:]
    so_ref = so_all.at[layer] if creates else so_all
    if creates:
        for other in range(so_all.shape[0]):
            if other != layer:
                so_all[other] = jnp.zeros(so_all.shape[1:], F32)
    n = bb * steps
    qf, log_f, kf = _hgrn_gates(q_ref[...], fz_ref[...], lb_ref[...])
    v = iv_ref[...]
    gate = _silu(gz_ref[...])
    row = lax.broadcasted_iota(jnp.int32, (n, n), 0)
    col = lax.broadcasted_iota(jnp.int32, (n, n), 1)
    same = row // steps == col // steps
    causal = same & (col <= row)
    b = _dot_exact(causal.astype(F32), log_f)
    total = _dot_exact(same.astype(F32), log_f)
    first = (same & (col % steps == 0)).astype(F32)
    e0 = b - _dot_exact(first, log_f)
    q0 = (qf * jnp.exp(e0)).astype(BF16)
    k0 = (kf * jnp.exp(-e0)).astype(BF16)
    q_in = (qf * jnp.exp(b)).astype(BF16)
    k_st = (kf * jnp.exp(total - b)).astype(BF16)
    vb = v.astype(BF16)
    pick = (lax.broadcasted_iota(jnp.int32, (bb, n), 1) == steps * lax.broadcasted_iota(jnp.int32, (bb, n), 0))
    e_seq = jnp.exp(_dot_exact(pick.astype(F32), total))
    e_rows = jnp.concatenate([e_seq[:, h * HG_DK:(h + 1) * HG_DK] for h in range(HG_HEADS)], axis=0)
    pad = HG_DK - e_rows.shape[0]
    if pad:
        e_rows = jnp.concatenate([e_rows, jnp.zeros((pad, HG_DK), F32)], axis=0)
    e_cols = e_rows.T
    for h in range(HG_HEADS):
        hs = slice(h * HG_DK, (h + 1) * HG_DK)
        a = jnp.where(causal, _dot_nt(q0[:, hs], k0[:, hs]), 0.0)
        o_intra = _dot(a.astype(BF16), vb[:, hs])
        outs = []
        for i in range(bb):
            rs = slice(i * steps, (i + 1) * steps)
            s_old = s_ref[i, h]
            outs.append(_dot(q_in[rs, hs], s_old.astype(BF16)))
            so_ref[i, h] = (s_old * e_cols[:, h * bb + i:h * bb + i + 1]
                            + _dot_tn(k_st[rs, hs], vb[rs, hs]))
        o = o_intra + jnp.concatenate(outs, axis=0)
        o_ref[:, hs] = _group_norm_gate(o, gn_ref[...], gate[:, hs])


def _hgrn_sample(proj4, states, layer, out_states, lb, gn, steps):
    n_layers, b = states.shape[0], states.shape[1]
    bb = SAMPLE_SEQ_BLOCK
    assert HG_HEADS * bb <= HG_DK
    creates = out_states is None
    rows = pl.BlockSpec((bb * steps, HG_W), lambda i: (i, 0))
    sshape = (bb, HG_HEADS, HG_DK, HG_DV)
    s_in = pl.BlockSpec((None,) + sshape, lambda i: (layer, i, 0, 0, 0))
    s_out = pl.BlockSpec((n_layers,) + sshape, lambda i: (0, i, 0, 0, 0)) if creates else s_in
    in_specs = [rows, rows, rows, rows, s_in, _resident((1, HG_W)), _resident((1, HG_DV))]
    args = [*proj4, states, lb, gn]
    aliases = {}
    if not creates:
        in_specs.append(pl.BlockSpec(memory_space=pl.ANY))
        args.append(out_states)
        aliases = {len(args) - 1: 1}
    return pl.pallas_call(
        functools.partial(_hgrn_sample_body, steps=steps, bb=bb, layer=layer, creates=creates),
        grid=(b // bb,),
        in_specs=in_specs,
        out_specs=[rows, s_out],
        out_shape=[jax.ShapeDtypeStruct((b * steps, HG_W), F32), jax.ShapeDtypeStruct(states.shape, F32)],
        input_output_aliases=aliases,
        compiler_params=_cparams("parallel"),
        name="hgrn_sample",
    )(*args)


def _trunk(x, cache_k, cache_v, st_re, st_im, st_h, p):
    prompt = cache_k is None
    bt, l, _ = x.shape
    tm = min(TOKEN_BLOCK, bt * l)
    depth = p["norm_mix_pre"].shape[0]
    kvw = N_KV * HEAD_DIM
    row1 = lambda a: a.reshape(1, -1)
    x2 = x.reshape(bt * l, D_MODEL)
    k_out, v_out, re_out, im_out, h_out = [], [], [], [], []
    h_all = None
    if not prompt:
        cache_k = cache_k.reshape(cache_k.shape[:3] + (kvw,))
        cache_v = cache_v.reshape(cache_v.shape[:3] + (kvw,))
    for layer in range(depth):
        idx = layer // 2
        g_pre = row1(p["norm_mix_pre"][layer])
        if layer % 2 == 0:
            u, q, k, v = _norm_proj(x2, g_pre, p["w_in_even"][idx], (S5_WIDTH, SWA_WIDTH, kvw, kvw), tm)
            prep = p["s5_prep"][idx]
            d_skip, wglu = row1(p["s5_d"][idx]), p["s5_w_glu"][idx]
            q3, k3, v3 = q.reshape(bt, l, SWA_WIDTH), k.reshape(bt, l, kvw), v.reshape(bt, l, kvw)
            if prompt:
                a_out, h_fin = _s5_prompt(u.reshape(bt, l, S5_WIDTH), prep, d_skip, wglu, min(TOKEN_BLOCK, l))
                h_fin = h_fin.reshape(bt, -1)
                b_out = _swa_prompt(q3, k3, v3, p["swa_sinks"][idx], p["t5_band"])
                kw, vw = k3[:, l - WINDOW:], v3[:, l - WINDOW:]
            else:
                u_t = u.reshape(bt, l, S5_WIDTH).transpose(1, 0, 2)
                a_t, h_fin = _s5_sample(u_t, _s5_state_in(st_re[idx], st_im[idx]), prep, d_skip, wglu)
                a_out = a_t.transpose(1, 0, 2)
                b_out, kw, vw = _swa_sample(q3, k3, v3, cache_k, cache_v, idx, p["swa_sinks"][idx], p["t5_band"])
            h_re, h_im = _s5_state_out(h_fin)
            k_out.append(kw.reshape(bt, WINDOW, N_KV, HEAD_DIM))
            v_out.append(vw.reshape(bt, WINDOW, N_KV, HEAD_DIM))
            re_out.append(h_re)
            im_out.append(h_im)
            mixer_outs = (a_out.reshape(bt * l, S5_WIDTH), b_out.reshape(bt * l, SWA_WIDTH))
            w_out = p["w_out_even"][idx]
        else:
            lb, gn = p["hgrn_lb"][idx:idx + 1], row1(p["hgrn_gnorm"][idx])
            if prompt:
                c_out, s_t = _hgrn_prompt(x2.reshape(bt, l, D_MODEL), g_pre, p["w_in_odd"][idx], lb, gn,
                                          min(TOKEN_BLOCK, l))
                h_out.append(s_t.transpose(0, 1, 3, 2))
            else:
                proj4 = _norm_proj(x2, g_pre, p["w_in_odd"][idx], (HG_W,) * 4, tm)
                c_out, h_all = _hgrn_sample(proj4, st_h, idx, h_all, lb, gn, l)
            mixer_outs = (c_out.reshape(bt * l, HG_W),)
            w_out = p["w_out_odd"][idx]
        x2 = _post(x2, mixer_outs, w_out, row1(p["norm_mix_post"][layer]), row1(p["norm_mlp_pre"][layer]),
                   row1(p["norm_mlp_post"][layer]), p["w_up"][layer], p["w_down"][layer], tm)
    return (x2.reshape(bt, l, D_MODEL), jnp.stack(k_out), jnp.stack(v_out), jnp.stack(re_out), jnp.stack(im_out),
            jnp.stack(h_out) if prompt else h_all)


def kernel(x_prompt, x_sample, cache_swa_k, cache_swa_v, state_s5_re, state_s5_im, state_hgrn, t5_bias_table,
           norm_mix_pre, norm_mix_post, norm_mlp_pre, norm_mlp_post, w_in_even, w_out_even, s5_lambda_re,
           s5_lambda_im, s5_log_step, s5_b_re, s5_b_im, s5_c_re, s5_c_im, s5_d, s5_w_glu, swa_sinks, w_in_odd,
           w_out_odd, hgrn_lb_param, hgrn_gnorm, w_up, w_down):
    bf = lambda a: a.astype(BF16)
    n_even = w_in_even.shape[0]
    q0, q1 = S5_WIDTH, S5_WIDTH + SWA_WIDTH
    w_in_even = jnp.concatenate([w_in_even[..., :q0], _pair_heads(w_in_even[..., q0:q1], 2), w_in_even[..., q1:]],
                                axis=-1)
    w_out_even = jnp.concatenate([w_out_even[:, :q0], _pair_heads(w_out_even[:, q0:], 1)], axis=1)
    params = dict(
        norm_mix_pre=norm_mix_pre, norm_mix_post=norm_mix_post, norm_mlp_pre=norm_mlp_pre,
        norm_mlp_post=norm_mlp_post, w_in_even=bf(w_in_even), w_out_even=bf(w_out_even), s5_d=s5_d,
        s5_w_glu=bf(s5_w_glu), swa_sinks=swa_sinks, w_in_odd=bf(w_in_odd), w_out_odd=bf(w_out_odd),
        hgrn_gnorm=hgrn_gnorm, w_up=bf(w_up), w_down=bf(w_down),
        t5_band=_t5_band(t5_bias_table),
        hgrn_lb=_hgrn_lower_bounds(hgrn_lb_param),
        s5_prep=[_s5_prep(s5_lambda_re[i], s5_lambda_im[i], s5_log_step[i], s5_b_re[i], s5_b_im[i], s5_c_re[i],
                          s5_c_im[i]) for i in range(n_even)],
    )
    y_prompt, k_p, v_p, re_p, im_p, hg_p = _trunk(x_prompt, None, None, None, None, None, params)
    y_sample, k_s, v_s, re_s, im_s, hg_s = _trunk(x_sample, cache_swa_k, cache_swa_v, state_s5_re, state_s5_im,
                                                  state_hgrn, params)
    return (y_prompt, y_sample, k_p, v_p, k_s, v_s, re_p, im_p, re_s, im_s, hg_p, hg_s)
```

```python
import functools
import math

import numpy as np
import jax
import jax.numpy as jnp
from jax import lax
from jax.experimental import pallas as pl
from jax.experimental.pallas import tpu as pltpu

F32 = jnp.float32
BF16 = jnp.bfloat16

D_MODEL = 1024
S5_WIDTH = 512
S5_GROUPS = 32
S5_GROUP = 16
S5_STATE = 64
S5_HALF_STATES = 1024
S5_HALF_CH = 256
SWA_WIDTH = 512
HEAD_DIM = 64
N_HEADS = 8
N_KV = 2
KV_GROUP = N_HEADS // N_KV
WINDOW = 128
T5_BUCKETS = 32
T5_MAX_DIST = 128
HG_HEADS = 8
HG_DK = 128
HG_DV = 128
HG_W = HG_HEADS * HG_DK
D_FF = 4096
RMS_EPS = 1e-6
NEG_INF = -1e30
LOG_FLOOR = 1e-30

S5_CHUNK = 64
HG_CHUNK = 128
HG_BASE = 32
TOKEN_BLOCK = 512
MLP_FF_BLOCK = 1024
SAMPLE_SEQ_BLOCK = 8
VMEM_LIMIT = 56 * 1024 * 1024
MXU_WIDTH = 256


def _cparams(*sem):
    return pltpu.CompilerParams(dimension_semantics=sem, vmem_limit_bytes=VMEM_LIMIT)


def _resident(shape):
    nd = len(shape)
    return pl.BlockSpec(shape, lambda *_: (0,) * nd, pipeline_mode=pl.Buffered(1))


def _rms(x, w):
    return x * lax.rsqrt(jnp.mean(x * x, axis=-1, keepdims=True) + RMS_EPS) * w


def _sigmoid(x):
    return 1.0 / (1.0 + jnp.exp(-x))


def _silu(x):
    return x * _sigmoid(x)


def _gelu_tanh(x):
    return 0.5 * x * (1.0 + jnp.tanh(math.sqrt(2.0 / math.pi) * (x + 0.044715 * (x * x * x))))


def _dot(a, b):
    return jnp.dot(a, b, preferred_element_type=F32)


def _dot_nt(a, b):
    return lax.dot_general(a, b, (((1,), (1,)), ((), ())), preferred_element_type=F32)


def _dot_tn(a, b):
    return lax.dot_general(a, b, (((0,), (0,)), ((), ())), preferred_element_type=F32)


def _dot_exact(a, b):
    return jnp.dot(a, b, preferred_element_type=F32, precision=lax.Precision.HIGHEST)


def _norm_proj_body(x_ref, g_ref, w_ref, *o_refs, splits):
    xn = _rms(x_ref[...], g_ref[...]).astype(BF16)
    off = 0
    i = 0
    while i < len(splits):
        j = i + 1
        while sum(splits[i:j]) < MXU_WIDTH and j < len(splits):
            j += 1
        y = _dot(xn, w_ref[:, off:off + sum(splits[i:j])])
        sub = 0
        for o_ref, wd in zip(o_refs[i:j], splits[i:j]):
            o_ref[...] = y[:, sub:sub + wd]
            sub += wd
        off += sub
        i = j


def _norm_proj(x2d, g, w_bf, splits, tm):
    t, n = x2d.shape[0], w_bf.shape[1]
    return pl.pallas_call(
        functools.partial(_norm_proj_body, splits=splits),
        grid=(t // tm,),
        in_specs=[pl.BlockSpec((tm, D_MODEL), lambda i: (i, 0)),
                  _resident((1, D_MODEL)), _resident((D_MODEL, n))],
        out_specs=[pl.BlockSpec((tm, wd), lambda i: (i, 0)) for wd in splits],
        out_shape=[jax.ShapeDtypeStruct((t, wd), F32) for wd in splits],
        compiler_params=_cparams("parallel"),
        name="norm_proj",
    )(x2d, g, w_bf)


def _post_body(*refs, widths):
    x_ref = refs[0]
    in_refs = refs[1:1 + len(widths)]
    wout_ref, gpost_ref, gpre_ref, gmpost_ref, wup_ref, wdn_ref, o_ref = refs[1 + len(widths):]
    mix = None
    off = 0
    for r, wd in zip(in_refs, widths):
        t = _dot(r[...].astype(BF16), wout_ref[off:off + wd, :])
        mix = t if mix is None else mix + t
        off += wd
    x1 = x_ref[...] + _rms(mix, gpost_ref[...])
    hn = _rms(x1, gpre_ref[...]).astype(BF16)
    acc = None
    for c in range(D_FF // MLP_FF_BLOCK):
        cs = slice(c * MLP_FF_BLOCK, (c + 1) * MLP_FF_BLOCK)
        hk = jnp.square(jnp.maximum(_dot(hn, wup_ref[:, cs]), 0.0)).astype(BF16)
        t = _dot(hk, wdn_ref[cs, :])
        acc = t if acc is None else acc + t
    o_ref[...] = x1 + _rms(acc, gmpost_ref[...])


def _post(x2d, mixer_outs, wout_bf, gpost, gpre, gmpost, wup_bf, wdn_bf, tm):
    t = x2d.shape[0]
    widths = tuple(a.shape[1] for a in mixer_outs)
    row = lambda wd: pl.BlockSpec((tm, wd), lambda i: (i, 0))
    return pl.pallas_call(
        functools.partial(_post_body, widths=widths),
        grid=(t // tm,),
        in_specs=[row(D_MODEL)] + [row(wd) for wd in widths] + [
            _resident((D_MODEL, D_MODEL)), _resident((1, D_MODEL)), _resident((1, D_MODEL)),
            _resident((1, D_MODEL)), _resident((D_MODEL, D_FF)), _resident((D_FF, D_MODEL))],
        out_specs=row(D_MODEL),
        out_shape=jax.ShapeDtypeStruct((t, D_MODEL), F32),
        compiler_params=_cparams("parallel"),
        name="post_mlp",
    )(x2d, *mixer_outs, wout_bf, gpost, gpre, gmpost, wup_bf, wdn_bf)


def _bucket_map():
    t = np.arange(WINDOW)[:, None]
    m = np.arange(2 * WINDOW)[None, :]
    n = np.maximum(WINDOW + t - m, 0)
    max_exact = T5_BUCKETS // 2
    large = max_exact + np.floor(np.log(np.maximum(n, max_exact) / max_exact)
                                 / math.log(T5_MAX_DIST / max_exact) * (T5_BUCKETS - max_exact)).astype(np.int64)
    large = np.minimum(large, T5_BUCKETS - 1)
    return np.where(n < max_exact, n, large).astype(np.int32)


def _bias_body(tab_ref, bm_ref, o_ref):
    bm = bm_ref[...]
    row = lax.broadcasted_iota(jnp.int32, bm.shape, 0)
    col = lax.broadcasted_iota(jnp.int32, bm.shape, 1)
    dist = WINDOW + row - col
    in_win = (dist >= 0) & (dist < WINDOW)
    for h in range(N_HEADS):
        acc = jnp.zeros(bm.shape, F32)
        for b in range(T5_BUCKETS):
            acc = jnp.where(bm == b, tab_ref[b, h], acc)
        o_ref[0, h] = jnp.where(in_win & (col >= WINDOW), acc, NEG_INF)
        o_ref[1, h] = jnp.where(in_win, acc, NEG_INF)


def _t5_band(table):
    return pl.pallas_call(
        _bias_body,
        in_specs=[pl.BlockSpec(memory_space=pltpu.SMEM),
                  pl.BlockSpec((WINDOW, 2 * WINDOW), lambda: (0, 0))],
        out_specs=pl.BlockSpec((2, N_HEADS, WINDOW, 2 * WINDOW), lambda: (0, 0, 0, 0)),
        out_shape=jax.ShapeDtypeStruct((2, N_HEADS, WINDOW, 2 * WINDOW), F32),
        name="t5_band",
    )(table, jnp.asarray(_bucket_map()))


def _pair_heads(a, axis):
    shp = a.shape
    a = a.reshape(shp[:axis] + (N_KV, KV_GROUP, HEAD_DIM) + shp[axis + 1:])
    a = jnp.swapaxes(a, axis, axis + 1)
    return a.reshape(shp)


def _swa_prompt_body(sink_ref, q_ref, kc_ref, kp_ref, vc_ref, vp_ref, band_ref, o_ref):
    tile = N_KV * HEAD_DIM
    kb = jnp.concatenate([kp_ref[...], kc_ref[...]], axis=0).astype(BF16)
    vb = jnp.concatenate([vp_ref[...], vc_ref[...]], axis=0).astype(BF16)
    v_ext = jnp.concatenate([vb, jnp.ones_like(vb)], axis=1)
    lo = lax.broadcasted_iota(jnp.int32, (WINDOW, tile), 1) < HEAD_DIM
    for t in range(KV_GROUP):
        qt = q_ref[:, t * tile:(t + 1) * tile] * (HEAD_DIM ** -0.5)
        halves = []
        for kv in range(N_KV):
            h = kv * KV_GROUP + t
            qh = jnp.where(lo if kv == 0 else jnp.logical_not(lo), qt, 0.0).astype(BF16)
            s = _dot_nt(qh, kb) + band_ref[h]
            sink = sink_ref[h]
            m = jnp.maximum(jnp.max(s, axis=-1, keepdims=True), sink)
            oe = _dot(jnp.exp(s - m).astype(BF16), v_ext)
            halves.append(oe[:, 0:tile] / (oe[:, tile:tile + 1] + jnp.exp(sink - m)))
        o_ref[:, t * tile:(t + 1) * tile] = jnp.where(lo, halves[0], halves[1])


def _swa_prompt(q, k, v, sinks, band):
    b, l, _ = q.shape
    kvw = N_KV * HEAD_DIM
    cur = lambda w: pl.BlockSpec((None, WINDOW, w), lambda i, n: (i, n, 0))
    prev = lambda w: pl.BlockSpec((None, WINDOW, w), lambda i, n: (i, jnp.maximum(n - 1, 0), 0))
    return pl.pallas_call(
        _swa_prompt_body,
        grid=(b, l // WINDOW),
        in_specs=[pl.BlockSpec(memory_space=pltpu.SMEM), cur(SWA_WIDTH), cur(kvw), prev(kvw), cur(kvw), prev(kvw),
                  pl.BlockSpec((None, N_HEADS, WINDOW, 2 * WINDOW), lambda i, n: (jnp.minimum(n, 1), 0, 0, 0))],
        out_specs=cur(SWA_WIDTH),
        out_shape=jax.ShapeDtypeStruct((b, l, SWA_WIDTH), F32),
        compiler_params=_cparams("parallel", "arbitrary"),
        name="swa_prompt",
    )(sinks, q, k, k, v, v, band)


def _swa_sample_body(q_ref, kn_ref, vn_ref, ck_ref, cv_ref, bias_ref, sink_ref, o_ref, ko_ref, vo_ref, *, steps):
    tile = N_KV * HEAD_DIM
    ck, cv, kn, vn = ck_ref[...], cv_ref[...], kn_ref[...], vn_ref[...]
    ko_ref[:, 0:WINDOW - steps, :] = ck[:, steps:, :]
    ko_ref[:, WINDOW - steps:, :] = kn
    vo_ref[:, 0:WINDOW - steps, :] = cv[:, steps:, :]
    vo_ref[:, WINDOW - steps:, :] = vn
    lo = lax.broadcasted_iota(jnp.int32, (1, 1, tile), 2) < HEAD_DIM
    q = q_ref[...] * (HEAD_DIM ** -0.5)
    qs = []
    for kv in range(N_KV):
        for t in range(KV_GROUP):
            qs.append(jnp.where(lo if kv == 0 else jnp.logical_not(lo), q[:, :, t * tile:(t + 1) * tile], 0.0))
    qall = jnp.concatenate(qs, axis=1).astype(BF16)
    bias = bias_ref[...]
    sink = sink_ref[...][None]
    bdot = lambda a, b_, spec: jnp.einsum(spec, a, b_, preferred_element_type=F32)
    sc = bdot(qall, ck.astype(BF16), "bqd,bkd->bqk") + bias[None, :, 0:WINDOW]
    sn = bdot(qall, kn.astype(BF16), "bqd,bkd->bqk") + bias[None, :, WINDOW:WINDOW + steps]
    m = jnp.maximum(jnp.maximum(jnp.max(sc, axis=-1, keepdims=True), jnp.max(sn, axis=-1, keepdims=True)), sink)
    ec, en = jnp.exp(sc - m), jnp.exp(sn - m)
    den = jnp.sum(ec, axis=-1, keepdims=True) + jnp.sum(en, axis=-1, keepdims=True) + jnp.exp(sink - m)
    o = (bdot(ec.astype(BF16), cv.astype(BF16), "bqk,bkd->bqd")
         + bdot(en.astype(BF16), vn.astype(BF16), "bqk,bkd->bqd")) / den
    for t in range(KV_GROUP):
        o_ref[:, :, t * tile:(t + 1) * tile] = jnp.where(
            lo, o[:, t * steps:(t + 1) * steps, :], o[:, (KV_GROUP + t) * steps:(KV_GROUP + t + 1) * steps, :])


def _swa_sample(q, kn, vn, cache_k, cache_v, idx, sinks, band):
    b, steps, _ = q.shape
    kvw = N_KV * HEAD_DIM
    bb = SAMPLE_SEQ_BLOCK
    nrow = N_HEADS * steps
    bias = band[1, :, 0:steps, :].reshape(nrow, 2 * WINDOW)
    sink_rows = jnp.repeat(sinks, steps).reshape(nrow, 1)
    blk = lambda r, w: pl.BlockSpec((bb, r, w), lambda i: (i, 0, 0))
    cache = pl.BlockSpec((None, bb, WINDOW, kvw), lambda i: (idx, i, 0, 0))
    return pl.pallas_call(
        functools.partial(_swa_sample_body, steps=steps),
        grid=(b // bb,),
        in_specs=[blk(steps, SWA_WIDTH), blk(steps, kvw), blk(steps, kvw), cache, cache,
                  _resident((nrow, 2 * WINDOW)), _resident((nrow, 1))],
        out_specs=[blk(steps, SWA_WIDTH), blk(WINDOW, kvw), blk(WINDOW, kvw)],
        out_shape=[jax.ShapeDtypeStruct((b, steps, SWA_WIDTH), F32),
                   jax.ShapeDtypeStruct((b, WINDOW, kvw), F32),
                   jax.ShapeDtypeStruct((b, WINDOW, kvw), F32)],
        compiler_params=_cparams("parallel"),
        name="swa_sample",
    )(q, kn, vn, cache_k, cache_v, bias, sink_rows)


def _s5_prep_body(lr_ref, li_ref, ls_ref, br_ref, bi_ref, cr_ref, ci_ref, bm_ref, cm_ref, pw_ref, pinv_ref):
    hs = S5_HALF_STATES
    lr, li = lr_ref[...], li_ref[...]
    dt = jnp.exp(ls_ref[...])
    mag = jnp.exp(lr * dt)
    ang = li * dt
    ar = mag * jnp.cos(ang)
    ai = mag * jnp.sin(ang)
    den = lr * lr + li * li
    nr = ar - 1.0
    kr = (nr * lr + ai * li) / den
    ki = (ai * lr - nr * li) / den
    br, bi = br_ref[...], bi_ref[...]
    bm_ref[:, 0:hs] = (kr * br - ki * bi).astype(BF16)
    bm_ref[:, hs:2 * hs] = (kr * bi + ki * br).astype(BF16)
    cm_ref[0:hs, :] = cr_ref[...].astype(BF16)
    cm_ref[hs:2 * hs, :] = (-ci_ref[...]).astype(BF16)
    n2 = ar * ar + ai * ai
    ir, ii = ar / n2, -ai / n2
    pr, pi = jnp.ones_like(ar), jnp.zeros_like(ar)
    qr, qi = pr, pi
    for j in range(S5_CHUNK):
        pw_ref[0, j:j + 1, :] = pr
        pw_ref[1, j:j + 1, :] = pi
        pinv_ref[0, j:j + 1, :] = qr
        pinv_ref[1, j:j + 1, :] = qi
        pr, pi = pr * ar - pi * ai, pr * ai + pi * ar
        qr, qi = qr * ir - qi * ii, qr * ii + qi * ir


def _s5_prep(lam_re, lam_im, log_step, b_re, b_im, c_re, c_im):
    hs, hc = S5_HALF_STATES, S5_HALF_CH
    gh = S5_GROUPS // 2
    eye = jnp.eye(gh, dtype=bool)
    vec = lambda a: a.reshape(2, 1, hs)
    ls = jnp.broadcast_to(log_step[:, None], (S5_GROUPS, S5_STATE))

    def b_blockdiag(b):
        bt = b.reshape(2, gh, S5_STATE, S5_GROUP).transpose(0, 3, 1, 2)[:, None]
        full = jnp.where(eye[None, :, None, :, None], bt, 0.0)
        return full.reshape(2, hc, hs)

    def c_blockdiag(c):
        ct = c.reshape(2, gh, S5_GROUP, S5_STATE).transpose(0, 1, 3, 2)[:, :, :, None]
        full = jnp.where(eye[None, :, None, :, None], ct, 0.0)
        return full.reshape(2, hs, hc)

    vspec = pl.BlockSpec((None, 1, hs), lambda h: (h, 0, 0))
    bspec = pl.BlockSpec((None, hc, hs), lambda h: (h, 0, 0))
    cspec = pl.BlockSpec((None, hs, hc), lambda h: (h, 0, 0))
    pspec = pl.BlockSpec((None, 2, S5_CHUNK, hs), lambda h: (h, 0, 0, 0))
    return pl.pallas_call(
        _s5_prep_body,
        grid=(2,),
        in_specs=[vspec, vspec, vspec, bspec, bspec, cspec, cspec],
        out_specs=[pl.BlockSpec((None, hc, 2 * hs), lambda h: (h, 0, 0)),
                   pl.BlockSpec((None, 2 * hs, hc), lambda h: (h, 0, 0)), pspec, pspec],
        out_shape=[jax.ShapeDtypeStruct((2, hc, 2 * hs), BF16), jax.ShapeDtypeStruct((2, 2 * hs, hc), BF16),
                   jax.ShapeDtypeStruct((2, 2, S5_CHUNK, hs), F32), jax.ShapeDtypeStruct((2, 2, S5_CHUNK, hs), F32)],
        compiler_params=_cparams("parallel"),
        name="s5_prep",
    )(vec(lam_re), vec(lam_im), vec(ls), b_blockdiag(b_re), b_blockdiag(b_im), c_blockdiag(c_re), c_blockdiag(c_im))


def _s5_glu(y, u, d_ref, wg_ref):
    g = _gelu_tanh(y + d_ref[...] * u)
    return g * _sigmoid(_dot(g.astype(BF16), wg_ref[...]))


def _s5_prompt_body(u_ref, bm_ref, cm_ref, pw_ref, pinv_ref, d_ref, wg_ref, o_ref, hfin_ref, bu_scr, hb_scr,
                    h_scr, *, tb):
    hs, hc, c0 = S5_HALF_STATES, S5_HALF_CH, S5_CHUNK

    @pl.when(pl.program_id(1) == 0)
    def _():
        h_scr[...] = jnp.zeros_like(h_scr)

    u = u_ref[...]
    ub = u.astype(BF16)
    tril = (lax.broadcasted_iota(jnp.int32, (c0, c0), 0) >= lax.broadcasted_iota(jnp.int32, (c0, c0), 1)).astype(BF16)
    ys = []
    for hf in range(2):
        re = slice(2 * hf * hs, (2 * hf + 1) * hs)
        im = slice((2 * hf + 1) * hs, (2 * hf + 2) * hs)
        both = slice(2 * hf * hs, (2 * hf + 2) * hs)
        bu_scr[:, both] = _dot(ub[:, hf * hc:(hf + 1) * hc], bm_ref[hf])
        ar, ai = pw_ref[hf, 0, 1:2, :], pw_ref[hf, 1, 1:2, :]
        hr0, hi0 = h_scr[:, re], h_scr[:, im]
        for ci in range(tb // c0):
            rows = slice(ci * c0, (ci + 1) * c0)
            bur, bui = bu_scr[rows, re], bu_scr[rows, im]
            qr, qi = pinv_ref[hf, 0], pinv_ref[hf, 1]
            sr = _dot(tril, (qr * bur - qi * bui).astype(BF16)) + (ar * hr0 - ai * hi0)
            si = _dot(tril, (qr * bui + qi * bur).astype(BF16)) + (ar * hi0 + ai * hr0)
            pr, pi = pw_ref[hf, 0], pw_ref[hf, 1]
            hr = pr * sr - pi * si
            hi = pr * si + pi * sr
            hb_scr[rows, re] = hr.astype(BF16)
            hb_scr[rows, im] = hi.astype(BF16)
            hr0, hi0 = hr[c0 - 1:c0, :], hi[c0 - 1:c0, :]
        h_scr[:, re] = hr0
        h_scr[:, im] = hi0
        ys.append(_dot(hb_scr[:, both], cm_ref[hf]))
    o_ref[...] = _s5_glu(jnp.concatenate(ys, axis=1), u, d_ref, wg_ref)
    hfin_ref[...] = h_scr[...]


def _s5_prompt(u, prep, d_skip, wglu_bf, tb):
    b, l, _ = u.shape
    bm, cm, pw, pinv = prep
    ns = 4 * S5_HALF_STATES
    return pl.pallas_call(
        functools.partial(_s5_prompt_body, tb=tb),
        grid=(b, l // tb),
        in_specs=[pl.BlockSpec((None, tb, S5_WIDTH), lambda i, j: (i, j, 0)),
                  _resident(bm.shape), _resident(cm.shape), _resident(pw.shape), _resident(pinv.shape),
                  _resident((1, S5_WIDTH)), _resident((S5_WIDTH, S5_WIDTH))],
        out_specs=[pl.BlockSpec((None, tb, S5_WIDTH), lambda i, j: (i, j, 0)),
                   pl.BlockSpec((None, 1, ns), lambda i, j: (i, 0, 0))],
        out_shape=[jax.ShapeDtypeStruct((b, l, S5_WIDTH), F32), jax.ShapeDtypeStruct((b, 1, ns), F32)],
        scratch_shapes=[pltpu.VMEM((tb, ns), F32), pltpu.VMEM((tb, ns), BF16), pltpu.VMEM((1, ns), F32)],
        compiler_params=_cparams("parallel", "arbitrary"),
        name="s5_prompt",
    )(u, bm, cm, pw, pinv, d_skip, wglu_bf)


def _s5_sample_body(u_ref, h0_ref, bm_ref, cm_ref, pw_ref, d_ref, wg_ref, o_ref, hfin_ref, *, steps):
    hs, hc = S5_HALF_STATES, S5_HALF_CH
    h = [h0_ref[:, k * hs:(k + 1) * hs] for k in range(4)]
    for t in range(steps):
        u = u_ref[t]
        ub = u.astype(BF16)
        ys = []
        for hf in range(2):
            bu = _dot(ub[:, hf * hc:(hf + 1) * hc], bm_ref[hf])
            ar, ai = pw_ref[hf, 0, 1:2, :], pw_ref[hf, 1, 1:2, :]
            hr, hi = h[2 * hf], h[2 * hf + 1]
            h[2 * hf] = ar * hr - ai * hi + bu[:, 0:hs]
            h[2 * hf + 1] = ar * hi + ai * hr + bu[:, hs:2 * hs]
            ys.append(_dot(jnp.concatenate([h[2 * hf], h[2 * hf + 1]], axis=1).astype(BF16), cm_ref[hf]))
        o_ref[t] = _s5_glu(jnp.concatenate(ys, axis=1), u, d_ref, wg_ref)
    for k in range(4):
        hfin_ref[:, k * hs:(k + 1) * hs] = h[k]


def _s5_sample(u_t, h0, prep, d_skip, wglu_bf):
    steps, b, _ = u_t.shape
    bm, cm, pw, _ = prep
    ns = 4 * S5_HALF_STATES
    full = lambda shape: pl.BlockSpec(shape, lambda: (0,) * len(shape))
    return pl.pallas_call(
        functools.partial(_s5_sample_body, steps=steps),
        in_specs=[full(u_t.shape), full(h0.shape), full(bm.shape), full(cm.shape), full(pw.shape),
                  full((1, S5_WIDTH)), full((S5_WIDTH, S5_WIDTH))],
        out_specs=[full((steps, b, S5_WIDTH)), full((b, ns))],
        out_shape=[jax.ShapeDtypeStruct((steps, b, S5_WIDTH), F32), jax.ShapeDtypeStruct((b, ns), F32)],
        compiler_params=pltpu.CompilerParams(vmem_limit_bytes=VMEM_LIMIT),
        name="s5_sample",
    )(u_t, h0, bm, cm, pw, d_skip, wglu_bf)


def _s5_state_in(re, im):
    b = re.shape[0]
    return jnp.stack([re.reshape(b, 2, S5_HALF_STATES), im.reshape(b, 2, S5_HALF_STATES)], axis=2).reshape(b, -1)


def _s5_state_out(h):
    b = h.shape[0]
    h4 = h.reshape(b, 2, 2, S5_HALF_STATES)
    return (h4[:, :, 0].reshape(b, S5_GROUPS, S5_STATE), h4[:, :, 1].reshape(b, S5_GROUPS, S5_STATE))


def _lb_body(p_ref, o_ref):
    p = p_ref[...]
    e = jnp.exp(p - jnp.max(p, axis=0, keepdims=True))
    sm = e / jnp.sum(e, axis=0, keepdims=True)
    acc = jnp.zeros_like(sm[0:1])
    for i in range(p.shape[0]):
        acc = acc + sm[i:i + 1]
        o_ref[i:i + 1, :] = acc - sm[0:1]


def _hgrn_lower_bounds(lb_param):
    n = lb_param.shape[0]
    return pl.pallas_call(
        _lb_body,
        in_specs=[pl.BlockSpec((n, HG_W), lambda: (0, 0))],
        out_specs=pl.BlockSpec((n, HG_W), lambda: (0, 0)),
        out_shape=jax.ShapeDtypeStruct((n, HG_W), F32),
        name="hgrn_lower_bounds",
    )(lb_param)


def _hgrn_gates(q, fz, lb):
    log_f = jnp.log(jnp.maximum(lb, LOG_FLOOR) + (1.0 - lb) * _sigmoid(fz))
    return _silu(q), log_f, (1.0 - lb) * _sigmoid(-fz)


def _group_norm_gate(o, gn, gate):
    return o * lax.rsqrt(jnp.mean(o * o, axis=-1, keepdims=True) + RMS_EPS) * gn * gate


def _block_rows(b, stride, offset):
    return [b[m * stride + offset:m * stride + offset + 1, :] for m in range(b.shape[0] // stride)]


def _spread_rows(rows, rowi, stride):
    out = rows[-1]
    for m in range(len(rows) - 2, -1, -1):
        out = jnp.where(rowi < (m + 1) * stride, rows[m], out)
    return out


def _cumsum_rows(tril_bf, x):
    hi = x.astype(BF16)
    r1 = x - hi.astype(F32)
    mid = r1.astype(BF16)
    lo = (r1 - mid.astype(F32)).astype(BF16)
    return _dot(tril_bf, hi) + _dot(tril_bf, mid) + _dot(tril_bf, lo)


def _hgrn_prompt_body(x_ref, g_ref, w_ref, lb_ref, gn_ref, o_ref, st_ref, q_scr, k_scr, lf_scr, v_scr, s_scr, *, tb):
    c, w = HG_CHUNK, HG_W

    @pl.when(pl.program_id(1) == 0)
    def _():
        s_scr[...] = jnp.zeros_like(s_scr)

    xn = _rms(x_ref[...], g_ref[...]).astype(BF16)
    qf, log_f, kf = _hgrn_gates(_dot(xn, w_ref[:, 0:w]), _dot(xn, w_ref[:, w:2 * w]), lb_ref[...])
    q_scr[...] = qf
    lf_scr[...] = log_f
    k_scr[...] = kf
    v_scr[...] = _dot(xn, w_ref[:, 2 * w:3 * w]).astype(BF16)
    o_ref[...] = _silu(_dot(xn, w_ref[:, 3 * w:4 * w]))

    row = lax.broadcasted_iota(jnp.int32, (c, c), 0)
    col = lax.broadcasted_iota(jnp.int32, (c, c), 1)
    tril = (row >= col).astype(BF16)
    masks = [(row // HG_BASE == col // HG_BASE) & (col <= row)]
    strides = []
    s = 2 * HG_BASE
    while s <= c:
        strides.append(s)
        masks.append(row // s == col // s)
        s *= 2
    masks_all = [None if st == c else jnp.concatenate([mk] * HG_HEADS, axis=0)
                 for st, mk in zip([HG_BASE] + strides, masks)]
    cat = lambda parts: jnp.concatenate(parts, axis=0)

    def chunk(ci, carry):
        rows = pl.ds(pl.multiple_of(ci * c, c), c)
        q, k, vb = q_scr[rows, :], k_scr[rows, :], v_scr[rows, :]
        b = _cumsum_rows(tril, lf_scr[rows, :])
        b_last = b[c - 1:c, :]
        q0, k0, q_in, k_st = [], [], [], []
        for m in range(c // HG_BASE):
            blk = slice(m * HG_BASE, (m + 1) * HG_BASE)
            mid = b[m * HG_BASE + HG_BASE // 2:m * HG_BASE + HG_BASE // 2 + 1, :]
            d = b[blk, :] - mid
            q0.append(q[blk, :] * jnp.exp(d))
            k0.append(k[blk, :] * jnp.exp(-d))
            q_in.append(q0[-1] * jnp.exp(mid))
            k_st.append(k0[-1] * jnp.exp(b_last - mid))
        qs, ks = [cat(q0).astype(BF16)], [cat(k0).astype(BF16)]
        q_in, k_st = cat(q_in).astype(BF16), cat(k_st).astype(BF16)
        for st in strides:
            ql, kl = [], []
            zeros = jnp.zeros((st // 2, w), BF16)
            for m in range(c // st):
                lower = slice(m * st, m * st + st // 2)
                upper = slice(m * st + st // 2, (m + 1) * st)
                mid = b[m * st + st // 2:m * st + st // 2 + 1, :]
                ql += [zeros, (q[upper, :] * jnp.exp(b[upper, :] - mid)).astype(BF16)]
                kl += [(k[lower, :] * jnp.exp(mid - b[lower, :])).astype(BF16), zeros]
            qs.append(cat(ql))
            ks.append(cat(kl))
        e_last = jnp.exp(b_last)
        heads = [slice(h * HG_DK, (h + 1) * HG_DK) for h in range(HG_HEADS)]
        a = None
        for qh, kh, mk in zip(qs, ks, masks_all):
            t = cat([_dot_nt(qh[:, hs], kh[:, hs]) for hs in heads])
            if mk is not None:
                t = jnp.where(mk, t, 0.0)
            a = t if a is None else a + t
        a = a.astype(BF16)
        states = [s_scr[h] for h in range(HG_HEADS)]
        o = jnp.concatenate([_dot(a[h * c:(h + 1) * c, :], vb[:, hs]) + _dot_nt(q_in[:, hs], states[h].astype(BF16))
                             for h, hs in enumerate(heads)], axis=0)
        gate = jnp.concatenate([o_ref[rows, hs] for hs in heads], axis=0)
        out = _group_norm_gate(o, gn_ref[...], gate)
        for h, hs in enumerate(heads):
            s_scr[h] = states[h] * e_last[:, hs] + _dot_tn(vb[:, hs], k_st[:, hs])
            o_ref[rows, hs] = out[h * c:(h + 1) * c, :]
        return carry

    lax.fori_loop(0, tb // c, chunk, 0, unroll=True)
    st_ref[...] = s_scr[...]


def _hgrn_prompt(x, g, w_bf, lb, gn, tb):
    b, l, _ = x.shape
    row = pl.BlockSpec((None, tb, D_MODEL), lambda i, j: (i, j, 0))
    return pl.pallas_call(
        functools.partial(_hgrn_prompt_body, tb=tb),
        grid=(b, l // tb),
        in_specs=[row, _resident((1, D_MODEL)), _resident((D_MODEL, 4 * HG_W)), _resident((1, HG_W)),
                  _resident((1, HG_DV))],
        out_specs=[row, pl.BlockSpec((None, HG_HEADS, HG_DV, HG_DK), lambda i, j: (i, 0, 0, 0))],
        out_shape=[jax.ShapeDtypeStruct((b, l, HG_W), F32),
                   jax.ShapeDtypeStruct((b, HG_HEADS, HG_DV, HG_DK), F32)],
        scratch_shapes=[pltpu.VMEM((tb, HG_W), F32)] * 3 + [pltpu.VMEM((tb, HG_W), BF16),
                                                             pltpu.VMEM((HG_HEADS, HG_DV, HG_DK), F32)],
        compiler_params=_cparams("parallel", "arbitrary"),
        name="hgrn_prompt",
    )(x, g, w_bf, lb, gn)


def _hgrn_sample_body(q_ref, fz_ref, iv_ref, gz_ref, s_ref, lb_ref, gn_ref, *rest, steps, bb, layer, creates):
    o_ref, so_all = rest[-2:]
    so_ref = so_all.at[layer] if creates else so_all
    if creates:
        for other in range(so_all.shape[0]):
            if other != layer:
                so_all[other] = jnp.zeros(so_all.shape[1:], F32)
    n = bb * steps
    qf, log_f, kf = _hgrn_gates(q_ref[...], fz_ref[...], lb_ref[...])
    v = iv_ref[...]
    gate = _silu(gz_ref[...])
    row = lax.broadcasted_iota(jnp.int32, (n, n), 0)
    col = lax.broadcasted_iota(jnp.int32, (n, n), 1)
    same = row // steps == col // steps
    causal = same & (col <= row)
    b = _dot_exact(causal.astype(F32), log_f)
    total = _dot_exact(same.astype(F32), log_f)
    first = (same & (col % steps == 0)).astype(F32)
    e0 = b - _dot_exact(first, log_f)
    q0 = (qf * jnp.exp(e0)).astype(BF16)
    k0 = (kf * jnp.exp(-e0)).astype(BF16)
    q_in = (qf * jnp.exp(b)).astype(BF16)
    k_st = (kf * jnp.exp(total - b)).astype(BF16)
    vb = v.astype(BF16)
    pick = (lax.broadcasted_iota(jnp.int32, (bb, n), 1) == steps * lax.broadcasted_iota(jnp.int32, (bb, n), 0))
    e_seq = jnp.exp(_dot_exact(pick.astype(F32), total))
    e_rows = jnp.concatenate([e_seq[:, h * HG_DK:(h + 1) * HG_DK] for h in range(HG_HEADS)], axis=0)
    pad = HG_DK - e_rows.shape[0]
    if pad:
        e_rows = jnp.concatenate([e_rows, jnp.zeros((pad, HG_DK), F32)], axis=0)
    e_cols = e_rows.T
    for h in range(HG_HEADS):
        hs = slice(h * HG_DK, (h + 1) * HG_DK)
        a = jnp.where(causal, _dot_nt(q0[:, hs], k0[:, hs]), 0.0)
        o_intra = _dot(a.astype(BF16), vb[:, hs])
        outs = []
        for i in range(bb):
            rs = slice(i * steps, (i + 1) * steps)
            s_old = s_ref[i, h]
            outs.append(_dot(q_in[rs, hs], s_old.astype(BF16)))
            so_ref[i, h] = (s_old * e_cols[:, h * bb + i:h * bb + i + 1]
                            + _dot_tn(k_st[rs, hs], vb[rs, hs]))
        o = o_intra + jnp.concatenate(outs, axis=0)
        o_ref[:, hs] = _group_norm_gate(o, gn_ref[...], gate[:, hs])


def _hgrn_sample(proj4, states, layer, out_states, lb, gn, steps):
    n_layers, b = states.shape[0], states.shape[1]
    bb = SAMPLE_SEQ_BLOCK
    assert HG_HEADS * bb <= HG_DK
    creates = out_states is None
    rows = pl.BlockSpec((bb * steps, HG_W), lambda i: (i, 0))
    sshape = (bb, HG_HEADS, HG_DK, HG_DV)
    s_in = pl.BlockSpec((None,) + sshape, lambda i: (layer, i, 0, 0, 0))
    s_out = pl.BlockSpec((n_layers,) + sshape, lambda i: (0, i, 0, 0, 0)) if creates else s_in
    in_specs = [rows, rows, rows, rows, s_in, _resident((1, HG_W)), _resident((1, HG_DV))]
    args = [*proj4, states, lb, gn]
    aliases = {}
    if not creates:
        in_specs.append(pl.BlockSpec(memory_space=pl.ANY))
        args.append(out_states)
        aliases = {len(args) - 1: 1}
    return pl.pallas_call(
        functools.partial(_hgrn_sample_body, steps=steps, bb=bb, layer=layer, creates=creates),
        grid=(b // bb,),
        in_specs=in_specs,
        out_specs=[rows, s_out],
        out_shape=[jax.ShapeDtypeStruct((b * steps, HG_W), F32), jax.ShapeDtypeStruct(states.shape, F32)],
        input_output_aliases=aliases,
        compiler_params=_cparams("parallel"),
        name="hgrn_sample",
    )(*args)


def _trunk(x, cache_k, cache_v, st_re, st_im, st_h, p):
    prompt = cache_k is None
    bt, l, _ = x.shape
    tm = min(TOKEN_BLOCK, bt * l)
    depth = p["norm_mix_pre"].shape[0]
    kvw = N_KV * HEAD_DIM
    row1 = lambda a: a.reshape(1, -1)
    x2 = x.reshape(bt * l, D_MODEL)
    k_out, v_out, re_out, im_out, h_out = [], [], [], [], []
    h_all = None
    if not prompt:
        cache_k = cache_k.reshape(cache_k.shape[:3] + (kvw,))
        cache_v = cache_v.reshape(cache_v.shape[:3] + (kvw,))
    for layer in range(depth):
        idx = layer // 2
        g_pre = row1(p["norm_mix_pre"][layer])
        if layer % 2 == 0:
            u, q, k, v = _norm_proj(x2, g_pre, p["w_in_even"][idx], (S5_WIDTH, SWA_WIDTH, kvw, kvw), tm)
            prep = p["s5_prep"][idx]
            d_skip, wglu = row1(p["s5_d"][idx]), p["s5_w_glu"][idx]
            q3, k3, v3 = q.reshape(bt, l, SWA_WIDTH), k.reshape(bt, l, kvw), v.reshape(bt, l, kvw)
            if prompt:
                a_out, h_fin = _s5_prompt(u.reshape(bt, l, S5_WIDTH), prep, d_skip, wglu, min(TOKEN_BLOCK, l))
                h_fin = h_fin.reshape(bt, -1)
                b_out = _swa_prompt(q3, k3, v3, p["swa_sinks"][idx], p["t5_band"])
                kw, vw = k3[:, l - WINDOW:], v3[:, l - WINDOW:]
            else:
                u_t = u.reshape(bt, l, S5_WIDTH).transpose(1, 0, 2)
                a_t, h_fin = _s5_sample(u_t, _s5_state_in(st_re[idx], st_im[idx]), prep, d_skip, wglu)
                a_out = a_t.transpose(1, 0, 2)
                b_out, kw, vw = _swa_sample(q3, k3, v3, cache_k, cache_v, idx, p["swa_sinks"][idx], p["t5_band"])
            h_re, h_im = _s5_state_out(h_fin)
            k_out.append(kw.reshape(bt, WINDOW, N_KV, HEAD_DIM))
            v_out.append(vw.reshape(bt, WINDOW, N_KV, HEAD_DIM))
            re_out.append(h_re)
            im_out.append(h_im)
            mixer_outs = (a_out.reshape(bt * l, S5_WIDTH), b_out.reshape(bt * l, SWA_WIDTH))
            w_out = p["w_out_even"][idx]
        else:
            lb, gn = p["hgrn_lb"][idx:idx + 1], row1(p["hgrn_gnorm"][idx])
            if prompt:
                c_out, s_t = _hgrn_prompt(x2.reshape(bt, l, D_MODEL), g_pre, p["w_in_odd"][idx], lb, gn,
                                          min(TOKEN_BLOCK, l))
                h_out.append(s_t.transpose(0, 1, 3, 2))
            else:
                proj4 = _norm_proj(x2, g_pre, p["w_in_odd"][idx], (HG_W,) * 4, tm)
                c_out, h_all = _hgrn_sample(proj4, st_h, idx, h_all, lb, gn, l)
            mixer_outs = (c_out.reshape(bt * l, HG_W),)
            w_out = p["w_out_odd"][idx]
        x2 = _post(x2, mixer_outs, w_out, row1(p["norm_mix_post"][layer]), row1(p["norm_mlp_pre"][layer]),
                   row1(p["norm_mlp_post"][layer]), p["w_up"][layer], p["w_down"][layer], tm)
    return (x2.reshape(bt, l, D_MODEL), jnp.stack(k_out), jnp.stack(v_out), jnp.stack(re_out), jnp.stack(im_out),
            jnp.stack(h_out) if prompt else h_all)


def kernel(x_prompt, x_sample, cache_swa_k, cache_swa_v, state_s5_re, state_s5_im, state_hgrn, t5_bias_table,
           norm_mix_pre, norm_mix_post, norm_mlp_pre, norm_mlp_post, w_in_even, w_out_even, s5_lambda_re,
           s5_lambda_im, s5_log_step, s5_b_re, s5_b_im, s5_c_re, s5_c_im, s5_d, s5_w_glu, swa_sinks, w_in_odd,
           w_out_odd, hgrn_lb_param, hgrn_gnorm, w_up, w_down):
    bf = lambda a: a.astype(BF16)
    n_even = w_in_even.shape[0]
    q0, q1 = S5_WIDTH, S5_WIDTH + SWA_WIDTH
    w_in_even = jnp.concatenate([w_in_even[..., :q0], _pair_heads(w_in_even[..., q0:q1], 2), w_in_even[..., q1:]],
                                axis=-1)
    w_out_even = jnp.concatenate([w_out_even[:, :q0], _pair_heads(w_out_even[:, q0:], 1)], axis=1)
    params = dict(
        norm_mix_pre=norm_mix_pre, norm_mix_post=norm_mix_post, norm_mlp_pre=norm_mlp_pre,
        norm_mlp_post=norm_mlp_post, w_in_even=bf(w_in_even), w_out_even=bf(w_out_even), s5_d=s5_d,
        s5_w_glu=bf(s5_w_glu), swa_sinks=swa_sinks, w_in_odd=bf(w_in_odd), w_out_odd=bf(w_out_odd),
        hgrn_gnorm=hgrn_gnorm, w_up=bf(w_up), w_down=bf(w_down),
        t5_band=_t5_band(t5_bias_table),
        hgrn_lb=_hgrn_lower_bounds(hgrn_lb_param),
        s5_prep=[_s5_prep(s5_lambda_re[i], s5_lambda_im[i], s5_log_step[i], s5_b_re[i], s5_b_im[i], s5_c_re[i],
                          s5_c_im[i]) for i in range(n_even)],
    )
    y_prompt, k_p, v_p, re_p, im_p, hg_p = _trunk(x_prompt, None, None, None, None, None, params)
    y_sample, k_s, v_s, re_s, im_s, hg_s = _trunk(x_sample, cache_swa_k, cache_swa_v, state_s5_re, state_s5_im,
                                                  state_hgrn, params)
    return (y_prompt, y_sample, k_p, v_p, k_s, v_s, re_p, im_p, re_s, im_s, hg_p, hg_s)
```

```python
import functools
import math

import numpy as np
import jax
import jax.numpy as jnp
from jax import lax
from jax.experimental import pallas as pl
from jax.experimental.pallas import tpu as pltpu

F32 = jnp.float32
BF16 = jnp.bfloat16

D_MODEL = 1024
S5_WIDTH = 512
S5_GROUPS = 32
S5_GROUP = 16
S5_STATE = 64
S5_HALF_STATES = 1024
S5_HALF_CH = 256
SWA_WIDTH = 512
HEAD_DIM = 64
N_HEADS = 8
N_KV = 2
KV_GROUP = N_HEADS // N_KV
WINDOW = 128
T5_BUCKETS = 32
T5_MAX_DIST = 128
HG_HEADS = 8
HG_DK = 128
HG_DV = 128
HG_W = HG_HEADS * HG_DK
D_FF = 4096
RMS_EPS = 1e-6
NEG_INF = -1e30
LOG_FLOOR = 1e-30

S5_CHUNK = 64
HG_CHUNK = 128
HG_BASE = 32
TOKEN_BLOCK = 512
MLP_FF_BLOCK = 1024
SAMPLE_SEQ_BLOCK = 8
VMEM_LIMIT = 56 * 1024 * 1024
MXU_WIDTH = 256


def _cparams(*sem):
    return pltpu.CompilerParams(dimension_semantics=sem, vmem_limit_bytes=VMEM_LIMIT)


def _resident(shape):
    nd = len(shape)
    return pl.BlockSpec(shape, lambda *_: (0,) * nd, pipeline_mode=pl.Buffered(1))


def _rms(x, w):
    return x * lax.rsqrt(jnp.mean(x * x, axis=-1, keepdims=True) + RMS_EPS) * w


def _sigmoid(x):
    return 1.0 / (1.0 + jnp.exp(-x))


def _silu(x):
    return x * _sigmoid(x)


def _gelu_tanh(x):
    return 0.5 * x * (1.0 + jnp.tanh(math.sqrt(2.0 / math.pi) * (x + 0.044715 * (x * x * x))))


def _dot(a, b):
    return jnp.dot(a, b, preferred_element_type=F32)


def _dot_nt(a, b):
    return lax.dot_general(a, b, (((1,), (1,)), ((), ())), preferred_element_type=F32)


def _dot_tn(a, b):
    return lax.dot_general(a, b, (((0,), (0,)), ((), ())), preferred_element_type=F32)


def _dot_exact(a, b):
    return jnp.dot(a, b, preferred_element_type=F32, precision=lax.Precision.HIGHEST)


def _norm_proj_body(x_ref, g_ref, w_ref, *o_refs, splits):
    xn = _rms(x_ref[...], g_ref[...]).astype(BF16)
    off = 0
    i = 0
    while i < len(splits):
        j = i + 1
        while sum(splits[i:j]) < MXU_WIDTH and j < len(splits):
            j += 1
        y = _dot(xn, w_ref[:, off:off + sum(splits[i:j])])
        sub = 0
        for o_ref, wd in zip(o_refs[i:j], splits[i:j]):
            o_ref[...] = y[:, sub:sub + wd]
            sub += wd
        off += sub
        i = j


def _norm_proj(x2d, g, w_bf, splits, tm):
    t, n = x2d.shape[0], w_bf.shape[1]
    return pl.pallas_call(
        functools.partial(_norm_proj_body, splits=splits),
        grid=(t // tm,),
        in_specs=[pl.BlockSpec((tm, D_MODEL), lambda i: (i, 0)),
                  _resident((1, D_MODEL)), _resident((D_MODEL, n))],
        out_specs=[pl.BlockSpec((tm, wd), lambda i: (i, 0)) for wd in splits],
        out_shape=[jax.ShapeDtypeStruct((t, wd), F32) for wd in splits],
        compiler_params=_cparams("parallel"),
        name="norm_proj",
    )(x2d, g, w_bf)


def _post_body(*refs, widths):
    x_ref = refs[0]
    in_refs = refs[1:1 + len(widths)]
    wout_ref, gpost_ref, gpre_ref, gmpost_ref, wup_ref, wdn_ref, o_ref = refs[1 + len(widths):]
    mix = None
    off = 0
    for r, wd in zip(in_refs, widths):
        t = _dot(r[...].astype(BF16), wout_ref[off:off + wd, :])
        mix = t if mix is None else mix + t
        off += wd
    x1 = x_ref[...] + _rms(mix, gpost_ref[...])
    hn = _rms(x1, gpre_ref[...]).astype(BF16)
    acc = None
    for c in range(D_FF // MLP_FF_BLOCK):
        cs = slice(c * MLP_FF_BLOCK, (c + 1) * MLP_FF_BLOCK)
        hk = jnp.square(jnp.maximum(_dot(hn, wup_ref[:, cs]), 0.0)).astype(BF16)
        t = _dot(hk, wdn_ref[cs, :])
        acc = t if acc is None else acc + t
    o_ref[...] = x1 + _rms(acc, gmpost_ref[...])


def _post_stream_body(*refs, widths):
    x_ref = refs[0]
    in_refs = refs[1:1 + len(widths)]
    wout_ref, gpost_ref, gpre_ref, gmpost_ref, wup_ref, wdn_ref, o_ref, x1_scr, hn_scr, acc_scr = refs[1 + len(widths):]
    c = pl.program_id(0)

    @pl.when(c == 0)
    def _():
        mix = None
        off = 0
        for r, wd in zip(in_refs, widths):
            t = _dot(r[...].astype(BF16), wout_ref[off:off + wd, :])
            mix = t if mix is None else mix + t
            off += wd
        x1 = x_ref[...] + _rms(mix, gpost_ref[...])
        x1_scr[...] = x1
        hn_scr[...] = _rms(x1, gpre_ref[...]).astype(BF16)
        acc_scr[...] = jnp.zeros_like(acc_scr)

    hk = jnp.square(jnp.maximum(_dot(hn_scr[...], wup_ref[...]), 0.0)).astype(BF16)
    acc_scr[...] += _dot(hk, wdn_ref[...])

    @pl.when(c == pl.num_programs(0) - 1)
    def _():
        o_ref[...] = x1_scr[...] + _rms(acc_scr[...], gmpost_ref[...])


def _post(x2d, mixer_outs, wout_bf, gpost, gpre, gmpost, wup_bf, wdn_bf, tm):
    t = x2d.shape[0]
    widths = tuple(a.shape[1] for a in mixer_outs)
    if t == tm:
        whole = lambda wd: pl.BlockSpec((t, wd), lambda c: (0, 0))
        vec = pl.BlockSpec((1, D_MODEL), lambda c: (0, 0))
        return pl.pallas_call(
            functools.partial(_post_stream_body, widths=widths),
            grid=(D_FF // MLP_FF_BLOCK,),
            in_specs=[whole(D_MODEL)] + [whole(wd) for wd in widths] + [
                pl.BlockSpec((D_MODEL, D_MODEL), lambda c: (0, 0)), vec, vec, vec,
                pl.BlockSpec((D_MODEL, MLP_FF_BLOCK), lambda c: (0, c)),
                pl.BlockSpec((MLP_FF_BLOCK, D_MODEL), lambda c: (c, 0))],
            out_specs=whole(D_MODEL),
            out_shape=jax.ShapeDtypeStruct((t, D_MODEL), F32),
            scratch_shapes=[pltpu.VMEM((t, D_MODEL), F32), pltpu.VMEM((t, D_MODEL), BF16),
                            pltpu.VMEM((t, D_MODEL), F32)],
            compiler_params=_cparams("arbitrary"),
            name="post_mlp_stream",
        )(x2d, *mixer_outs, wout_bf, gpost, gpre, gmpost, wup_bf, wdn_bf)
    row = lambda wd: pl.BlockSpec((tm, wd), lambda i: (i, 0))
    return pl.pallas_call(
        functools.partial(_post_body, widths=widths),
        grid=(t // tm,),
        in_specs=[row(D_MODEL)] + [row(wd) for wd in widths] + [
            _resident((D_MODEL, D_MODEL)), _resident((1, D_MODEL)), _resident((1, D_MODEL)),
            _resident((1, D_MODEL)), _resident((D_MODEL, D_FF)), _resident((D_FF, D_MODEL))],
        out_specs=row(D_MODEL),
        out_shape=jax.ShapeDtypeStruct((t, D_MODEL), F32),
        compiler_params=_cparams("parallel"),
        name="post_mlp",
    )(x2d, *mixer_outs, wout_bf, gpost, gpre, gmpost, wup_bf, wdn_bf)


def _bucket_map():
    t = np.arange(WINDOW)[:, None]
    m = np.arange(2 * WINDOW)[None, :]
    n = np.maximum(WINDOW + t - m, 0)
    max_exact = T5_BUCKETS // 2
    large = max_exact + np.floor(np.log(np.maximum(n, max_exact) / max_exact)
                                 / math.log(T5_MAX_DIST / max_exact) * (T5_BUCKETS - max_exact)).astype(np.int64)
    large = np.minimum(large, T5_BUCKETS - 1)
    return np.where(n < max_exact, n, large).astype(np.int32)


def _bias_body(tab_ref, bm_ref, o_ref):
    bm = bm_ref[...]
    row = lax.broadcasted_iota(jnp.int32, bm.shape, 0)
    col = lax.broadcasted_iota(jnp.int32, bm.shape, 1)
    dist = WINDOW + row - col
    in_win = (dist >= 0) & (dist < WINDOW)
    for h in range(N_HEADS):
        acc = jnp.zeros(bm.shape, F32)
        for b in range(T5_BUCKETS):
            acc = jnp.where(bm == b, tab_ref[b, h], acc)
        o_ref[0, h] = jnp.where(in_win & (col >= WINDOW), acc, NEG_INF)
        o_ref[1, h] = jnp.where(in_win, acc, NEG_INF)


def _t5_band(table):
    return pl.pallas_call(
        _bias_body,
        in_specs=[pl.BlockSpec(memory_space=pltpu.SMEM),
                  pl.BlockSpec((WINDOW, 2 * WINDOW), lambda: (0, 0))],
        out_specs=pl.BlockSpec((2, N_HEADS, WINDOW, 2 * WINDOW), lambda: (0, 0, 0, 0)),
        out_shape=jax.ShapeDtypeStruct((2, N_HEADS, WINDOW, 2 * WINDOW), F32),
        name="t5_band",
    )(table, jnp.asarray(_bucket_map()))


def _pair_heads(a, axis):
    shp = a.shape
    a = a.reshape(shp[:axis] + (N_KV, KV_GROUP, HEAD_DIM) + shp[axis + 1:])
    a = jnp.swapaxes(a, axis, axis + 1)
    return a.reshape(shp)


def _swa_prompt_body(sink_ref, q_ref, kc_ref, kp_ref, vc_ref, vp_ref, band_ref, o_ref):
    tile = N_KV * HEAD_DIM
    kb = jnp.concatenate([kp_ref[...], kc_ref[...]], axis=0).astype(BF16)
    vb = jnp.concatenate([vp_ref[...], vc_ref[...]], axis=0).astype(BF16)
    v_ext = jnp.concatenate([vb, jnp.ones_like(vb)], axis=1)
    lo = lax.broadcasted_iota(jnp.int32, (WINDOW, tile), 1) < HEAD_DIM
    for t in range(KV_GROUP):
        qt = q_ref[:, t * tile:(t + 1) * tile] * (HEAD_DIM ** -0.5)
        halves = []
        for kv in range(N_KV):
            h = kv * KV_GROUP + t
            qh = jnp.where(lo if kv == 0 else jnp.logical_not(lo), qt, 0.0).astype(BF16)
            s = _dot_nt(qh, kb) + band_ref[h]
            sink = sink_ref[h]
            m = jnp.maximum(jnp.max(s, axis=-1, keepdims=True), sink)
            oe = _dot(jnp.exp(s - m).astype(BF16), v_ext)
            halves.append(oe[:, 0:tile] / (oe[:, tile:tile + 1] + jnp.exp(sink - m)))
        o_ref[:, t * tile:(t + 1) * tile] = jnp.where(lo, halves[0], halves[1])


def _swa_prompt(q, k, v, sinks, band):
    b, l, _ = q.shape
    kvw = N_KV * HEAD_DIM
    cur = lambda w: pl.BlockSpec((None, WINDOW, w), lambda i, n: (i, n, 0))
    prev = lambda w: pl.BlockSpec((None, WINDOW, w), lambda i, n: (i, jnp.maximum(n - 1, 0), 0))
    return pl.pallas_call(
        _swa_prompt_body,
        grid=(b, l // WINDOW),
        in_specs=[pl.BlockSpec(memory_space=pltpu.SMEM), cur(SWA_WIDTH), cur(kvw), prev(kvw), cur(kvw), prev(kvw),
                  pl.BlockSpec((None, N_HEADS, WINDOW, 2 * WINDOW), lambda i, n: (jnp.minimum(n, 1), 0, 0, 0))],
        out_specs=cur(SWA_WIDTH),
        out_shape=jax.ShapeDtypeStruct((b, l, SWA_WIDTH), F32),
        compiler_params=_cparams("parallel", "arbitrary"),
        name="swa_prompt",
    )(sinks, q, k, k, v, v, band)


def _swa_sample_body(q_ref, kn_ref, vn_ref, ck_ref, cv_ref, bias_ref, sink_ref, o_ref, ko_ref, vo_ref, *, steps):
    tile = N_KV * HEAD_DIM
    ck, cv, kn, vn = ck_ref[...], cv_ref[...], kn_ref[...], vn_ref[...]
    ko_ref[:, 0:WINDOW - steps, :] = ck[:, steps:, :]
    ko_ref[:, WINDOW - steps:, :] = kn
    vo_ref[:, 0:WINDOW - steps, :] = cv[:, steps:, :]
    vo_ref[:, WINDOW - steps:, :] = vn
    lo = lax.broadcasted_iota(jnp.int32, (1, 1, tile), 2) < HEAD_DIM
    q = q_ref[...] * (HEAD_DIM ** -0.5)
    qs = []
    for kv in range(N_KV):
        for t in range(KV_GROUP):
            qs.append(jnp.where(lo if kv == 0 else jnp.logical_not(lo), q[:, :, t * tile:(t + 1) * tile], 0.0))
    qall = jnp.concatenate(qs, axis=1).astype(BF16)
    bias = bias_ref[...]
    sink = sink_ref[...][None]
    bdot = lambda a, b_, spec: jnp.einsum(spec, a, b_, preferred_element_type=F32)
    sc = bdot(qall, ck.astype(BF16), "bqd,bkd->bqk") + bias[None, :, 0:WINDOW]
    sn = bdot(qall, kn.astype(BF16), "bqd,bkd->bqk") + bias[None, :, WINDOW:WINDOW + steps]
    m = jnp.maximum(jnp.maximum(jnp.max(sc, axis=-1, keepdims=True), jnp.max(sn, axis=-1, keepdims=True)), sink)
    ec, en = jnp.exp(sc - m), jnp.exp(sn - m)
    den = jnp.sum(ec, axis=-1, keepdims=True) + jnp.sum(en, axis=-1, keepdims=True) + jnp.exp(sink - m)
    o = (bdot(ec.astype(BF16), cv.astype(BF16), "bqk,bkd->bqd")
         + bdot(en.astype(BF16), vn.astype(BF16), "bqk,bkd->bqd")) / den
    for t in range(KV_GROUP):
        o_ref[:, :, t * tile:(t + 1) * tile] = jnp.where(
            lo, o[:, t * steps:(t + 1) * steps, :], o[:, (KV_GROUP + t) * steps:(KV_GROUP + t + 1) * steps, :])


def _swa_sample(q, kn, vn, cache_k, cache_v, idx, sinks, band):
    b, steps, _ = q.shape
    kvw = N_KV * HEAD_DIM
    bb = SAMPLE_SEQ_BLOCK
    nrow = N_HEADS * steps
    bias = band[1, :, 0:steps, :].reshape(nrow, 2 * WINDOW)
    sink_rows = jnp.repeat(sinks, steps).reshape(nrow, 1)
    blk = lambda r, w: pl.BlockSpec((bb, r, w), lambda i: (i, 0, 0))
    cache = pl.BlockSpec((None, bb, WINDOW, kvw), lambda i: (idx, i, 0, 0))
    return pl.pallas_call(
        functools.partial(_swa_sample_body, steps=steps),
        grid=(b // bb,),
        in_specs=[blk(steps, SWA_WIDTH), blk(steps, kvw), blk(steps, kvw), cache, cache,
                  _resident((nrow, 2 * WINDOW)), _resident((nrow, 1))],
        out_specs=[blk(steps, SWA_WIDTH), blk(WINDOW, kvw), blk(WINDOW, kvw)],
        out_shape=[jax.ShapeDtypeStruct((b, steps, SWA_WIDTH), F32),
                   jax.ShapeDtypeStruct((b, WINDOW, kvw), F32),
                   jax.ShapeDtypeStruct((b, WINDOW, kvw), F32)],
        compiler_params=_cparams("parallel"),
        name="swa_sample",
    )(q, kn, vn, cache_k, cache_v, bias, sink_rows)


def _s5_prep_body(lr_ref, li_ref, ls_ref, br_ref, bi_ref, cr_ref, ci_ref, bm_ref, cm_ref, pw_ref, pinv_ref):
    hs = S5_HALF_STATES
    lr, li = lr_ref[...], li_ref[...]
    dt = jnp.exp(ls_ref[...])
    mag = jnp.exp(lr * dt)
    ang = li * dt
    ar = mag * jnp.cos(ang)
    ai = mag * jnp.sin(ang)
    den = lr * lr + li * li
    nr = ar - 1.0
    kr = (nr * lr + ai * li) / den
    ki = (ai * lr - nr * li) / den
    br, bi = br_ref[...], bi_ref[...]
    bm_ref[:, 0:hs] = (kr * br - ki * bi).astype(BF16)
    bm_ref[:, hs:2 * hs] = (kr * bi + ki * br).astype(BF16)
    cm_ref[0:hs, :] = cr_ref[...].astype(BF16)
    cm_ref[hs:2 * hs, :] = (-ci_ref[...]).astype(BF16)
    n2 = ar * ar + ai * ai
    ir, ii = ar / n2, -ai / n2
    pr, pi = jnp.ones_like(ar), jnp.zeros_like(ar)
    qr, qi = pr, pi
    for j in range(S5_CHUNK):
        pw_ref[0, j:j + 1, :] = pr
        pw_ref[1, j:j + 1, :] = pi
        pinv_ref[0, j:j + 1, :] = qr
        pinv_ref[1, j:j + 1, :] = qi
        pr, pi = pr * ar - pi * ai, pr * ai + pi * ar
        qr, qi = qr * ir - qi * ii, qr * ii + qi * ir


def _s5_prep(lam_re, lam_im, log_step, b_re, b_im, c_re, c_im):
    hs, hc = S5_HALF_STATES, S5_HALF_CH
    gh = S5_GROUPS // 2
    eye = jnp.eye(gh, dtype=bool)
    vec = lambda a: a.reshape(2, 1, hs)
    ls = jnp.broadcast_to(log_step[:, None], (S5_GROUPS, S5_STATE))

    def b_blockdiag(b):
        bt = b.reshape(2, gh, S5_STATE, S5_GROUP).transpose(0, 3, 1, 2)[:, None]
        full = jnp.where(eye[None, :, None, :, None], bt, 0.0)
        return full.reshape(2, hc, hs)

    def c_blockdiag(c):
        ct = c.reshape(2, gh, S5_GROUP, S5_STATE).transpose(0, 1, 3, 2)[:, :, :, None]
        full = jnp.where(eye[None, :, None, :, None], ct, 0.0)
        return full.reshape(2, hs, hc)

    vspec = pl.BlockSpec((None, 1, hs), lambda h: (h, 0, 0))
    bspec = pl.BlockSpec((None, hc, hs), lambda h: (h, 0, 0))
    cspec = pl.BlockSpec((None, hs, hc), lambda h: (h, 0, 0))
    pspec = pl.BlockSpec((None, 2, S5_CHUNK, hs), lambda h: (h, 0, 0, 0))
    return pl.pallas_call(
        _s5_prep_body,
        grid=(2,),
        in_specs=[vspec, vspec, vspec, bspec, bspec, cspec, cspec],
        out_specs=[pl.BlockSpec((None, hc, 2 * hs), lambda h: (h, 0, 0)),
                   pl.BlockSpec((None, 2 * hs, hc), lambda h: (h, 0, 0)), pspec, pspec],
        out_shape=[jax.ShapeDtypeStruct((2, hc, 2 * hs), BF16), jax.ShapeDtypeStruct((2, 2 * hs, hc), BF16),
                   jax.ShapeDtypeStruct((2, 2, S5_CHUNK, hs), F32), jax.ShapeDtypeStruct((2, 2, S5_CHUNK, hs), F32)],
        compiler_params=_cparams("parallel"),
        name="s5_prep",
    )(vec(lam_re), vec(lam_im), vec(ls), b_blockdiag(b_re), b_blockdiag(b_im), c_blockdiag(c_re), c_blockdiag(c_im))


def _s5_glu(y, u, d_ref, wg_ref):
    g = _gelu_tanh(y + d_ref[...] * u)
    return g * _sigmoid(_dot(g.astype(BF16), wg_ref[...]))


def _s5_prompt_body(u_ref, bm_ref, cm_ref, pw_ref, pinv_ref, d_ref, wg_ref, o_ref, hfin_ref, bu_scr, hb_scr,
                    h_scr, *, tb):
    hs, hc, c0 = S5_HALF_STATES, S5_HALF_CH, S5_CHUNK

    @pl.when(pl.program_id(1) == 0)
    def _():
        h_scr[...] = jnp.zeros_like(h_scr)

    u = u_ref[...]
    ub = u.astype(BF16)
    tril = (lax.broadcasted_iota(jnp.int32, (c0, c0), 0) >= lax.broadcasted_iota(jnp.int32, (c0, c0), 1)).astype(BF16)
    ys = []
    nchunk = tb // c0
    last = c0 - 1
    for hf in range(2):
        re = slice(2 * hf * hs, (2 * hf + 1) * hs)
        im = slice((2 * hf + 1) * hs, (2 * hf + 2) * hs)
        both = slice(2 * hf * hs, (2 * hf + 2) * hs)
        bu_scr[:, both] = _dot(ub[:, hf * hc:(hf + 1) * hc], bm_ref[hf])
        for ci in range(nchunk):
            rows = slice(ci * c0, (ci + 1) * c0)
            bur, bui = bu_scr[rows, re], bu_scr[rows, im]
            qr, qi = pinv_ref[hf, 0], pinv_ref[hf, 1]
            bu_scr[rows, re] = _dot(tril, (qr * bur - qi * bui).astype(BF16))
            bu_scr[rows, im] = _dot(tril, (qr * bui + qi * bur).astype(BF16))
        ar, ai = pw_ref[hf, 0, 1:2, :], pw_ref[hf, 1, 1:2, :]
        er, ei = pw_ref[hf, 0, last:c0, :], pw_ref[hf, 1, last:c0, :]
        hr0, hi0 = h_scr[:, re], h_scr[:, im]
        carries = []
        for ci in range(nchunk):
            gr, gi = ar * hr0 - ai * hi0, ar * hi0 + ai * hr0
            carries.append((gr, gi))
            sr = bu_scr[ci * c0 + last:(ci + 1) * c0, re] + gr
            si = bu_scr[ci * c0 + last:(ci + 1) * c0, im] + gi
            hr0, hi0 = er * sr - ei * si, er * si + ei * sr
        h_scr[:, re] = hr0
        h_scr[:, im] = hi0
        for ci in range(nchunk):
            rows = slice(ci * c0, (ci + 1) * c0)
            sr = bu_scr[rows, re] + carries[ci][0]
            si = bu_scr[rows, im] + carries[ci][1]
            pr, pi = pw_ref[hf, 0], pw_ref[hf, 1]
            hb_scr[rows, re] = (pr * sr - pi * si).astype(BF16)
            hb_scr[rows, im] = (pr * si + pi * sr).astype(BF16)
        ys.append(_dot(hb_scr[:, both], cm_ref[hf]))
    o_ref[...] = _s5_glu(jnp.concatenate(ys, axis=1), u, d_ref, wg_ref)
    hfin_ref[...] = h_scr[...]


def _s5_prompt(u, prep, d_skip, wglu_bf, tb):
    b, l, _ = u.shape
    bm, cm, pw, pinv = prep
    ns = 4 * S5_HALF_STATES
    return pl.pallas_call(
        functools.partial(_s5_prompt_body, tb=tb),
        grid=(b, l // tb),
        in_specs=[pl.BlockSpec((None, tb, S5_WIDTH), lambda i, j: (i, j, 0)),
                  _resident(bm.shape), _resident(cm.shape), _resident(pw.shape), _resident(pinv.shape),
                  _resident((1, S5_WIDTH)), _resident((S5_WIDTH, S5_WIDTH))],
        out_specs=[pl.BlockSpec((None, tb, S5_WIDTH), lambda i, j: (i, j, 0)),
                   pl.BlockSpec((None, 1, ns), lambda i, j: (i, 0, 0))],
        out_shape=[jax.ShapeDtypeStruct((b, l, S5_WIDTH), F32), jax.ShapeDtypeStruct((b, 1, ns), F32)],
        scratch_shapes=[pltpu.VMEM((tb, ns), F32), pltpu.VMEM((tb, ns), BF16), pltpu.VMEM((1, ns), F32)],
        compiler_params=_cparams("parallel", "arbitrary"),
        name="s5_prompt",
    )(u, bm, cm, pw, pinv, d_skip, wglu_bf)


def _s5_sample_body(u_ref, h0_ref, bm_ref, cm_ref, pw_ref, d_ref, wg_ref, o_ref, hfin_ref, *, steps):
    hs, hc = S5_HALF_STATES, S5_HALF_CH
    h = [h0_ref[:, k * hs:(k + 1) * hs] for k in range(4)]
    for t in range(steps):
        u = u_ref[t]
        ub = u.astype(BF16)
        ys = []
        for hf in range(2):
            bu = _dot(ub[:, hf * hc:(hf + 1) * hc], bm_ref[hf])
            ar, ai = pw_ref[hf, 0, 1:2, :], pw_ref[hf, 1, 1:2, :]
            hr, hi = h[2 * hf], h[2 * hf + 1]
            h[2 * hf] = ar * hr - ai * hi + bu[:, 0:hs]
            h[2 * hf + 1] = ar * hi + ai * hr + bu[:, hs:2 * hs]
            ys.append(_dot(jnp.concatenate([h[2 * hf], h[2 * hf + 1]], axis=1).astype(BF16), cm_ref[hf]))
        o_ref[t] = _s5_glu(jnp.concatenate(ys, axis=1), u, d_ref, wg_ref)
    for k in range(4):
        hfin_ref[:, k * hs:(k + 1) * hs] = h[k]


def _s5_sample(u_t, h0, prep, d_skip, wglu_bf):
    steps, b, _ = u_t.shape
    bm, cm, pw, _ = prep
    ns = 4 * S5_HALF_STATES
    full = lambda shape: pl.BlockSpec(shape, lambda: (0,) * len(shape))
    return pl.pallas_call(
        functools.partial(_s5_sample_body, steps=steps),
        in_specs=[full(u_t.shape), full(h0.shape), full(bm.shape), full(cm.shape), full(pw.shape),
                  full((1, S5_WIDTH)), full((S5_WIDTH, S5_WIDTH))],
        out_specs=[full((steps, b, S5_WIDTH)), full((b, ns))],
        out_shape=[jax.ShapeDtypeStruct((steps, b, S5_WIDTH), F32), jax.ShapeDtypeStruct((b, ns), F32)],
        compiler_params=pltpu.CompilerParams(vmem_limit_bytes=VMEM_LIMIT),
        name="s5_sample",
    )(u_t, h0, bm, cm, pw, d_skip, wglu_bf)


def _s5_state_in(re, im):
    b = re.shape[0]
    return jnp.stack([re.reshape(b, 2, S5_HALF_STATES), im.reshape(b, 2, S5_HALF_STATES)], axis=2).reshape(b, -1)


def _s5_state_out(h):
    b = h.shape[0]
    h4 = h.reshape(b, 2, 2, S5_HALF_STATES)
    return (h4[:, :, 0].reshape(b, S5_GROUPS, S5_STATE), h4[:, :, 1].reshape(b, S5_GROUPS, S5_STATE))


def _lb_body(p_ref, o_ref):
    p = p_ref[...]
    e = jnp.exp(p - jnp.max(p, axis=0, keepdims=True))
    sm = e / jnp.sum(e, axis=0, keepdims=True)
    acc = jnp.zeros_like(sm[0:1])
    for i in range(p.shape[0]):
        acc = acc + sm[i:i + 1]
        o_ref[i:i + 1, :] = acc - sm[0:1]


def _hgrn_lower_bounds(lb_param):
    n = lb_param.shape[0]
    return pl.pallas_call(
        _lb_body,
        in_specs=[pl.BlockSpec((n, HG_W), lambda: (0, 0))],
        out_specs=pl.BlockSpec((n, HG_W), lambda: (0, 0)),
        out_shape=jax.ShapeDtypeStruct((n, HG_W), F32),
        name="hgrn_lower_bounds",
    )(lb_param)


def _hgrn_forget(fz, lb):
    log_f = jnp.log(jnp.maximum(lb, LOG_FLOOR) + (1.0 - lb) * _sigmoid(fz))
    return log_f, (1.0 - lb) * _sigmoid(-fz)


def _group_norm_gate(o, gn, gate):
    return o * lax.rsqrt(jnp.mean(o * o, axis=-1, keepdims=True) + RMS_EPS) * gn * gate


def _block_rows(b, stride, offset):
    return [b[m * stride + offset:m * stride + offset + 1, :] for m in range(b.shape[0] // stride)]


def _spread_rows(rows, rowi, stride):
    out = rows[-1]
    for m in range(len(rows) - 2, -1, -1):
        out = jnp.where(rowi < (m + 1) * stride, rows[m], out)
    return out


def _cumsum_rows(tril_bf, x):
    hi = x.astype(BF16)
    r1 = x - hi.astype(F32)
    mid = r1.astype(BF16)
    lo = (r1 - mid.astype(F32)).astype(BF16)
    return _dot(tril_bf, hi) + _dot(tril_bf, mid) + _dot(tril_bf, lo)


def _hgrn_prompt_body(x_ref, g_ref, w_ref, lb_ref, gn_ref, o_ref, st_ref, q_scr, k_scr, b_scr, v_scr, s_scr, *, tb):
    c, w = HG_CHUNK, HG_W

    @pl.when(pl.program_id(1) == 0)
    def _():
        s_scr[...] = jnp.zeros_like(s_scr)

    row = lax.broadcasted_iota(jnp.int32, (c, c), 0)
    col = lax.broadcasted_iota(jnp.int32, (c, c), 1)
    tril = (row >= col).astype(BF16)
    xn = _rms(x_ref[...], g_ref[...]).astype(BF16)
    log_f, kf = _hgrn_forget(_dot(xn, w_ref[:, w:2 * w]), lb_ref[...])
    k_scr[...] = kf
    for ci in range(tb // c):
        b_scr[ci * c:(ci + 1) * c, :] = _cumsum_rows(tril, log_f[ci * c:(ci + 1) * c, :])
    q_scr[...] = _silu(_dot(xn, w_ref[:, 0:w]))
    v_scr[...] = _dot(xn, w_ref[:, 2 * w:3 * w]).astype(BF16)
    o_ref[...] = _silu(_dot(xn, w_ref[:, 3 * w:4 * w]))

    masks = [(row // HG_BASE == col // HG_BASE) & (col <= row)]
    strides = []
    s = 2 * HG_BASE
    while s <= c:
        strides.append(s)
        masks.append(row // s == col // s)
        s *= 2
    masks_all = [None if st == c else jnp.concatenate([mk] * HG_HEADS, axis=0)
                 for st, mk in zip([HG_BASE] + strides, masks)]
    cat = lambda parts: jnp.concatenate(parts, axis=0)

    def chunk(ci, carry):
        rows = pl.ds(pl.multiple_of(ci * c, c), c)
        q, k, vb = q_scr[rows, :], k_scr[rows, :], v_scr[rows, :]
        b = b_scr[rows, :]
        b_last = b[c - 1:c, :]
        q0, k0, q_in, k_st = [], [], [], []
        for m in range(c // HG_BASE):
            blk = slice(m * HG_BASE, (m + 1) * HG_BASE)
            mid = b[m * HG_BASE + HG_BASE // 2:m * HG_BASE + HG_BASE // 2 + 1, :]
            d = b[blk, :] - mid
            q0.append(q[blk, :] * jnp.exp(d))
            k0.append(k[blk, :] * jnp.exp(-d))
            q_in.append(q0[-1] * jnp.exp(mid))
            k_st.append(k0[-1] * jnp.exp(b_last - mid))
        qs, ks = [cat(q0).astype(BF16)], [cat(k0).astype(BF16)]
        q_in, k_st = cat(q_in).astype(BF16), cat(k_st).astype(BF16)
        for st in strides:
            ql, kl = [], []
            zeros = jnp.zeros((st // 2, w), BF16)
            for m in range(c // st):
                lower = slice(m * st, m * st + st // 2)
                upper = slice(m * st + st // 2, (m + 1) * st)
                mid = b[m * st + st // 2:m * st + st // 2 + 1, :]
                ql += [zeros, (q[upper, :] * jnp.exp(b[upper, :] - mid)).astype(BF16)]
                kl += [(k[lower, :] * jnp.exp(mid - b[lower, :])).astype(BF16), zeros]
            qs.append(cat(ql))
            ks.append(cat(kl))
        e_last = jnp.exp(b_last)
        heads = [slice(h * HG_DK, (h + 1) * HG_DK) for h in range(HG_HEADS)]
        a = None
        for qh, kh, mk in zip(qs, ks, masks_all):
            t = cat([_dot_nt(qh[:, hs], kh[:, hs]) for hs in heads])
            if mk is not None:
                t = jnp.where(mk, t, 0.0)
            a = t if a is None else a + t
        a = a.astype(BF16)
        states = [s_scr[h] for h in range(HG_HEADS)]
        o = jnp.concatenate([_dot(a[h * c:(h + 1) * c, :], vb[:, hs]) + _dot_nt(q_in[:, hs], states[h].astype(BF16))
                             for h, hs in enumerate(heads)], axis=0)
        gate = jnp.concatenate([o_ref[rows, hs] for hs in heads], axis=0)
        out = _group_norm_gate(o, gn_ref[...], gate)
        for h, hs in enumerate(heads):
            s_scr[h] = states[h] * e_last[:, hs] + _dot_tn(vb[:, hs], k_st[:, hs])
            o_ref[rows, hs] = out[h * c:(h + 1) * c, :]
        return carry

    lax.fori_loop(0, tb // c, chunk, 0, unroll=True)
    st_ref[...] = s_scr[...]


def _hgrn_prompt(x, g, w_bf, lb, gn, tb):
    b, l, _ = x.shape
    row = pl.BlockSpec((None, tb, D_MODEL), lambda i, j: (i, j, 0))
    return pl.pallas_call(
        functools.partial(_hgrn_prompt_body, tb=tb),
        grid=(b, l // tb),
        in_specs=[row, _resident((1, D_MODEL)), _resident((D_MODEL, 4 * HG_W)), _resident((1, HG_W)),
                  _resident((1, HG_DV))],
        out_specs=[row, pl.BlockSpec((None, HG_HEADS, HG_DV, HG_DK), lambda i, j: (i, 0, 0, 0))],
        out_shape=[jax.ShapeDtypeStruct((b, l, HG_W), F32),
                   jax.ShapeDtypeStruct((b, HG_HEADS, HG_DV, HG_DK), F32)],
        scratch_shapes=[pltpu.VMEM((tb, HG_W), F32)] * 3 + [pltpu.VMEM((tb, HG_W), BF16),
                                                             pltpu.VMEM((HG_HEADS, HG_DV, HG_DK), F32)],
        compiler_params=_cparams("parallel", "arbitrary"),
        name="hgrn_prompt",
    )(x, g, w_bf, lb, gn)


def _hgrn_sample_body(q_ref, fz_ref, iv_ref, gz_ref, s_ref, lb_ref, gn_ref, *rest, steps, bb, layer, creates):
    o_ref, so_all = rest[-2:]
    so_ref = so_all.at[layer] if creates else so_all
    if creates:
        for other in range(so_all.shape[0]):
            if other != layer:
                so_all[other] = jnp.zeros(so_all.shape[1:], F32)
    n = bb * steps
    qf = _silu(q_ref[...])
    log_f, kf = _hgrn_forget(fz_ref[...], lb_ref[...])
    v = iv_ref[...]
    gate = _silu(gz_ref[...])
    row = lax.broadcasted_iota(jnp.int32, (n, n), 0)
    col = lax.broadcasted_iota(jnp.int32, (n, n), 1)
    same = row // steps == col // steps
    causal = same & (col <= row)
    b = _dot_exact(causal.astype(F32), log_f)
    total = _dot_exact(same.astype(F32), log_f)
    first = (same & (col % steps == 0)).astype(F32)
    e0 = b - _dot_exact(first, log_f)
    q0 = (qf * jnp.exp(e0)).astype(BF16)
    k0 = (kf * jnp.exp(-e0)).astype(BF16)
    q_in = (qf * jnp.exp(b)).astype(BF16)
    k_st = (kf * jnp.exp(total - b)).astype(BF16)
    vb = v.astype(BF16)
    pick = (lax.broadcasted_iota(jnp.int32, (bb, n), 1) == steps * lax.broadcasted_iota(jnp.int32, (bb, n), 0))
    e_seq = jnp.exp(_dot_exact(pick.astype(F32), total))
    e_rows = jnp.concatenate([e_seq[:, h * HG_DK:(h + 1) * HG_DK] for h in range(HG_HEADS)], axis=0)
    pad = HG_DK - e_rows.shape[0]
    if pad:
        e_rows = jnp.concatenate([e_rows, jnp.zeros((pad, HG_DK), F32)], axis=0)
    e_cols = e_rows.T
    for h in range(HG_HEADS):
        hs = slice(h * HG_DK, (h + 1) * HG_DK)
        a = jnp.where(causal, _dot_nt(q0[:, hs], k0[:, hs]), 0.0)
        o_intra = _dot(a.astype(BF16), vb[:, hs])
        outs = []
        for i in range(bb):
            rs = slice(i * steps, (i + 1) * steps)
            s_old = s_ref[i, h]
            outs.append(_dot(q_in[rs, hs], s_old.astype(BF16)))
            so_ref[i, h] = (s_old * e_cols[:, h * bb + i:h * bb + i + 1]
                            + _dot_tn(k_st[rs, hs], vb[rs, hs]))
        o = o_intra + jnp.concatenate(outs, axis=0)
        o_ref[:, hs] = _group_norm_gate(o, gn_ref[...], gate[:, hs])


def _hgrn_sample(proj4, states, layer, out_states, lb, gn, steps):
    n_layers, b = states.shape[0], states.shape[1]
    bb = SAMPLE_SEQ_BLOCK
    assert HG_HEADS * bb <= HG_DK
    creates = out_states is None
    rows = pl.BlockSpec((bb * steps, HG_W), lambda i: (i, 0))
    sshape = (bb, HG_HEADS, HG_DK, HG_DV)
    s_in = pl.BlockSpec((None,) + sshape, lambda i: (layer, i, 0, 0, 0))
    s_out = pl.BlockSpec((n_layers,) + sshape, lambda i: (0, i, 0, 0, 0)) if creates else s_in
    in_specs = [rows, rows, rows, rows, s_in, _resident((1, HG_W)), _resident((1, HG_DV))]
    args = [*proj4, states, lb, gn]
    aliases = {}
    if not creates:
        in_specs.append(pl.BlockSpec(memory_space=pl.ANY))
        args.append(out_states)
        aliases = {len(args) - 1: 1}
    return pl.pallas_call(
        functools.partial(_hgrn_sample_body, steps=steps, bb=bb, layer=layer, creates=creates),
        grid=(b // bb,),
        in_specs=in_specs,
        out_specs=[rows, s_out],
        out_shape=[jax.ShapeDtypeStruct((b * steps, HG_W), F32), jax.ShapeDtypeStruct(states.shape, F32)],
        input_output_aliases=aliases,
        compiler_params=_cparams("parallel"),
        name="hgrn_sample",
    )(*args)


def _trunk(x, cache_k, cache_v, st_re, st_im, st_h, p):
    prompt = cache_k is None
    bt, l, _ = x.shape
    tm = min(TOKEN_BLOCK, bt * l)
    depth = p["norm_mix_pre"].shape[0]
    kvw = N_KV * HEAD_DIM
    row1 = lambda a: a.reshape(1, -1)
    x2 = x.reshape(bt * l, D_MODEL)
    k_out, v_out, re_out, im_out, h_out = [], [], [], [], []
    h_all = None
    if not prompt:
        cache_k = cache_k.reshape(cache_k.shape[:3] + (kvw,))
        cache_v = cache_v.reshape(cache_v.shape[:3] + (kvw,))
    for layer in range(depth):
        idx = layer // 2
        g_pre = row1(p["norm_mix_pre"][layer])
        if layer % 2 == 0:
            u, q, k, v = _norm_proj(x2, g_pre, p["w_in_even"][idx], (S5_WIDTH, SWA_WIDTH, kvw, kvw), tm)
            prep = p["s5_prep"][idx]
            d_skip, wglu = row1(p["s5_d"][idx]), p["s5_w_glu"][idx]
            q3, k3, v3 = q.reshape(bt, l, SWA_WIDTH), k.reshape(bt, l, kvw), v.reshape(bt, l, kvw)
            if prompt:
                a_out, h_fin = _s5_prompt(u.reshape(bt, l, S5_WIDTH), prep, d_skip, wglu, min(TOKEN_BLOCK, l))
                h_fin = h_fin.reshape(bt, -1)
                b_out = _swa_prompt(q3, k3, v3, p["swa_sinks"][idx], p["t5_band"])
                kw, vw = k3[:, l - WINDOW:], v3[:, l - WINDOW:]
            else:
                u_t = u.reshape(bt, l, S5_WIDTH).transpose(1, 0, 2)
                a_t, h_fin = _s5_sample(u_t, _s5_state_in(st_re[idx], st_im[idx]), prep, d_skip, wglu)
                a_out = a_t.transpose(1, 0, 2)
                b_out, kw, vw = _swa_sample(q3, k3, v3, cache_k, cache_v, idx, p["swa_sinks"][idx], p["t5_band"])
            h_re, h_im = _s5_state_out(h_fin)
            k_out.append(kw.reshape(bt, WINDOW, N_KV, HEAD_DIM))
            v_out.append(vw.reshape(bt, WINDOW, N_KV, HEAD_DIM))
            re_out.append(h_re)
            im_out.append(h_im)
            mixer_outs = (a_out.reshape(bt * l, S5_WIDTH), b_out.reshape(bt * l, SWA_WIDTH))
            w_out = p["w_out_even"][idx]
        else:
            lb, gn = p["hgrn_lb"][idx:idx + 1], row1(p["hgrn_gnorm"][idx])
            if prompt:
                c_out, s_t = _hgrn_prompt(x2.reshape(bt, l, D_MODEL), g_pre, p["w_in_odd"][idx], lb, gn,
                                          min(TOKEN_BLOCK, l))
                h_out.append(s_t.transpose(0, 1, 3, 2))
            else:
                proj4 = _norm_proj(x2, g_pre, p["w_in_odd"][idx], (HG_W,) * 4, tm)
                c_out, h_all = _hgrn_sample(proj4, st_h, idx, h_all, lb, gn, l)
            mixer_outs = (c_out.reshape(bt * l, HG_W),)
            w_out = p["w_out_odd"][idx]
        x2 = _post(x2, mixer_outs, w_out, row1(p["norm_mix_post"][layer]), row1(p["norm_mlp_pre"][layer]),
                   row1(p["norm_mlp_post"][layer]), p["w_up"][layer], p["w_down"][layer], tm)
    return (x2.reshape(bt, l, D_MODEL), jnp.stack(k_out), jnp.stack(v_out), jnp.stack(re_out), jnp.stack(im_out),
            jnp.stack(h_out) if prompt else h_all)


def kernel(x_prompt, x_sample, cache_swa_k, cache_swa_v, state_s5_re, state_s5_im, state_hgrn, t5_bias_table,
           norm_mix_pre, norm_mix_post, norm_mlp_pre, norm_mlp_post, w_in_even, w_out_even, s5_lambda_re,
           s5_lambda_im, s5_log_step, s5_b_re, s5_b_im, s5_c_re, s5_c_im, s5_d, s5_w_glu, swa_sinks, w_in_odd,
           w_out_odd, hgrn_lb_param, hgrn_gnorm, w_up, w_down):
    bf = lambda a: a.astype(BF16)
    n_even = w_in_even.shape[0]
    q0, q1 = S5_WIDTH, S5_WIDTH + SWA_WIDTH
    w_in_even = jnp.concatenate([w_in_even[..., :q0], _pair_heads(w_in_even[..., q0:q1], 2), w_in_even[..., q1:]],
                                axis=-1)
    w_out_even = jnp.concatenate([w_out_even[:, :q0], _pair_heads(w_out_even[:, q0:], 1)], axis=1)
    params = dict(
        norm_mix_pre=norm_mix_pre, norm_mix_post=norm_mix_post, norm_mlp_pre=norm_mlp_pre,
        norm_mlp_post=norm_mlp_post, w_in_even=bf(w_in_even), w_out_even=bf(w_out_even), s5_d=s5_d,
        s5_w_glu=bf(s5_w_glu), swa_sinks=swa_sinks, w_in_odd=bf(w_in_odd), w_out_odd=bf(w_out_odd),
        hgrn_gnorm=hgrn_gnorm, w_up=bf(w_up), w_down=bf(w_down),
        t5_band=_t5_band(t5_bias_table),
        hgrn_lb=_hgrn_lower_bounds(hgrn_lb_param),
        s5_prep=[_s5_prep(s5_lambda_re[i], s5_lambda_im[i], s5_log_step[i], s5_b_re[i], s5_b_im[i], s5_c_re[i],
                          s5_c_im[i]) for i in range(n_even)],
    )
    y_prompt, k_p, v_p, re_p, im_p, hg_p = _trunk(x_prompt, None, None, None, None, None, params)
    y_sample, k_s, v_s, re_s, im_s, hg_s = _trunk(x_sample, cache_swa_k, cache_swa_v, state_s5_re, state_s5_im,
                                                  state_hgrn, params)
    return (y_prompt, y_sample, k_p, v_p, k_s, v_s, re_p, im_p, re_s, im_s, hg_p, hg_s)
```

```python
import functools
import math

import numpy as np
import jax
import jax.numpy as jnp
from jax import lax
from jax.experimental import pallas as pl
from jax.experimental.pallas import tpu as pltpu

F32 = jnp.float32
BF16 = jnp.bfloat16

D_MODEL = 1024
S5_WIDTH = 512
S5_GROUPS = 32
S5_GROUP = 16
S5_STATE = 64
S5_HALF_STATES = 1024
S5_HALF_CH = 256
SWA_WIDTH = 512
HEAD_DIM = 64
N_HEADS = 8
N_KV = 2
KV_GROUP = N_HEADS // N_KV
WINDOW = 128
T5_BUCKETS = 32
T5_MAX_DIST = 128
HG_HEADS = 8
HG_DK = 128
HG_DV = 128
HG_W = HG_HEADS * HG_DK
D_FF = 4096
RMS_EPS = 1e-6
NEG_INF = -1e30
LOG_FLOOR = 1e-30

S5_CHUNK = 64
HG_CHUNK = 128
HG_BASE = 32
TOKEN_BLOCK = 512
PROJ_TOKEN_BLOCK = 1024
SWA_BLOCKS_PER_STEP = 4
MLP_FF_BLOCK = 1024
SAMPLE_SEQ_BLOCK = 8
VMEM_LIMIT = 56 * 1024 * 1024
MXU_WIDTH = 256


def _cparams(*sem):
    return pltpu.CompilerParams(dimension_semantics=sem, vmem_limit_bytes=VMEM_LIMIT)


def _resident(shape):
    nd = len(shape)
    return pl.BlockSpec(shape, lambda *_: (0,) * nd, pipeline_mode=pl.Buffered(1))


def _rms(x, w):
    return x * lax.rsqrt(jnp.mean(x * x, axis=-1, keepdims=True) + RMS_EPS) * w


def _sigmoid(x):
    return 1.0 / (1.0 + jnp.exp(-x))


def _silu(x):
    return x * _sigmoid(x)


def _gelu_tanh(x):
    return 0.5 * x * (1.0 + jnp.tanh(math.sqrt(2.0 / math.pi) * (x + 0.044715 * (x * x * x))))


def _dot(a, b):
    return jnp.dot(a, b, preferred_element_type=F32)


def _dot_nt(a, b):
    return lax.dot_general(a, b, (((1,), (1,)), ((), ())), preferred_element_type=F32)


def _dot_tn(a, b):
    return lax.dot_general(a, b, (((0,), (0,)), ((), ())), preferred_element_type=F32)


def _dot_exact(a, b):
    return jnp.dot(a, b, preferred_element_type=F32, precision=lax.Precision.HIGHEST)


def _norm_proj_body(x_ref, g_ref, w_ref, *o_refs, splits):
    xn = _rms(x_ref[...], g_ref[...]).astype(BF16)
    off = 0
    i = 0
    while i < len(splits):
        j = i + 1
        while sum(splits[i:j]) < MXU_WIDTH and j < len(splits):
            j += 1
        y = _dot(xn, w_ref[:, off:off + sum(splits[i:j])])
        sub = 0
        for o_ref, wd in zip(o_refs[i:j], splits[i:j]):
            o_ref[...] = y[:, sub:sub + wd]
            sub += wd
        off += sub
        i = j


def _norm_proj(x2d, g, w_bf, splits, tm):
    t, n = x2d.shape[0], w_bf.shape[1]
    return pl.pallas_call(
        functools.partial(_norm_proj_body, splits=splits),
        grid=(t // tm,),
        in_specs=[pl.BlockSpec((tm, D_MODEL), lambda i: (i, 0)),
                  _resident((1, D_MODEL)), _resident((D_MODEL, n))],
        out_specs=[pl.BlockSpec((tm, wd), lambda i: (i, 0)) for wd in splits],
        out_shape=[jax.ShapeDtypeStruct((t, wd), F32) for wd in splits],
        compiler_params=_cparams("parallel"),
        name="norm_proj",
    )(x2d, g, w_bf)


def _post_body(*refs, widths):
    x_ref = refs[0]
    in_refs = refs[1:1 + len(widths)]
    wout_ref, gpost_ref, gpre_ref, gmpost_ref, wup_ref, wdn_ref, o_ref = refs[1 + len(widths):]
    mix = None
    off = 0
    for r, wd in zip(in_refs, widths):
        t = _dot(r[...].astype(BF16), wout_ref[off:off + wd, :])
        mix = t if mix is None else mix + t
        off += wd
    x1 = x_ref[...] + _rms(mix, gpost_ref[...])
    hn = _rms(x1, gpre_ref[...]).astype(BF16)
    acc = None
    for c in range(D_FF // MLP_FF_BLOCK):
        cs = slice(c * MLP_FF_BLOCK, (c + 1) * MLP_FF_BLOCK)
        hk = jnp.square(jnp.maximum(_dot(hn, wup_ref[:, cs]), 0.0)).astype(BF16)
        t = _dot(hk, wdn_ref[cs, :])
        acc = t if acc is None else acc + t
    o_ref[...] = x1 + _rms(acc, gmpost_ref[...])


def _post_stream_body(*refs, widths):
    x_ref = refs[0]
    in_refs = refs[1:1 + len(widths)]
    wout_ref, gpost_ref, gpre_ref, gmpost_ref, wup_ref, wdn_ref, o_ref, x1_scr, hn_scr, acc_scr = refs[1 + len(widths):]
    c = pl.program_id(0)

    @pl.when(c == 0)
    def _():
        mix = None
        off = 0
        for r, wd in zip(in_refs, widths):
            t = _dot(r[...].astype(BF16), wout_ref[off:off + wd, :])
            mix = t if mix is None else mix + t
            off += wd
        x1 = x_ref[...] + _rms(mix, gpost_ref[...])
        x1_scr[...] = x1
        hn_scr[...] = _rms(x1, gpre_ref[...]).astype(BF16)
        acc_scr[...] = jnp.zeros_like(acc_scr)

    hk = jnp.square(jnp.maximum(_dot(hn_scr[...], wup_ref[...]), 0.0)).astype(BF16)
    acc_scr[...] += _dot(hk, wdn_ref[...])

    @pl.when(c == pl.num_programs(0) - 1)
    def _():
        o_ref[...] = x1_scr[...] + _rms(acc_scr[...], gmpost_ref[...])


def _post(x2d, mixer_outs, wout_bf, gpost, gpre, gmpost, wup_bf, wdn_bf, tm):
    t = x2d.shape[0]
    widths = tuple(a.shape[1] for a in mixer_outs)
    if t == tm:
        whole = lambda wd: pl.BlockSpec((t, wd), lambda c: (0, 0))
        vec = pl.BlockSpec((1, D_MODEL), lambda c: (0, 0))
        return pl.pallas_call(
            functools.partial(_post_stream_body, widths=widths),
            grid=(D_FF // MLP_FF_BLOCK,),
            in_specs=[whole(D_MODEL)] + [whole(wd) for wd in widths] + [
                pl.BlockSpec((D_MODEL, D_MODEL), lambda c: (0, 0)), vec, vec, vec,
                pl.BlockSpec((D_MODEL, MLP_FF_BLOCK), lambda c: (0, c)),
                pl.BlockSpec((MLP_FF_BLOCK, D_MODEL), lambda c: (c, 0))],
            out_specs=whole(D_MODEL),
            out_shape=jax.ShapeDtypeStruct((t, D_MODEL), F32),
            scratch_shapes=[pltpu.VMEM((t, D_MODEL), F32), pltpu.VMEM((t, D_MODEL), BF16),
                            pltpu.VMEM((t, D_MODEL), F32)],
            compiler_params=_cparams("arbitrary"),
            name="post_mlp_stream",
        )(x2d, *mixer_outs, wout_bf, gpost, gpre, gmpost, wup_bf, wdn_bf)
    row = lambda wd: pl.BlockSpec((tm, wd), lambda i: (i, 0))
    return pl.pallas_call(
        functools.partial(_post_body, widths=widths),
        grid=(t // tm,),
        in_specs=[row(D_MODEL)] + [row(wd) for wd in widths] + [
            _resident((D_MODEL, D_MODEL)), _resident((1, D_MODEL)), _resident((1, D_MODEL)),
            _resident((1, D_MODEL)), _resident((D_MODEL, D_FF)), _resident((D_FF, D_MODEL))],
        out_specs=row(D_MODEL),
        out_shape=jax.ShapeDtypeStruct((t, D_MODEL), F32),
        compiler_params=_cparams("parallel"),
        name="post_mlp",
    )(x2d, *mixer_outs, wout_bf, gpost, gpre, gmpost, wup_bf, wdn_bf)


def _bucket_map():
    t = np.arange(WINDOW)[:, None]
    m = np.arange(2 * WINDOW)[None, :]
    n = np.maximum(WINDOW + t - m, 0)
    max_exact = T5_BUCKETS // 2
    large = max_exact + np.floor(np.log(np.maximum(n, max_exact) / max_exact)
                                 / math.log(T5_MAX_DIST / max_exact) * (T5_BUCKETS - max_exact)).astype(np.int64)
    large = np.minimum(large, T5_BUCKETS - 1)
    return np.where(n < max_exact, n, large).astype(np.int32)


def _bias_body(tab_ref, bm_ref, o_ref):
    bm = bm_ref[...]
    row = lax.broadcasted_iota(jnp.int32, bm.shape, 0)
    col = lax.broadcasted_iota(jnp.int32, bm.shape, 1)
    dist = WINDOW + row - col
    in_win = (dist >= 0) & (dist < WINDOW)
    for h in range(N_HEADS):
        acc = jnp.zeros(bm.shape, F32)
        for b in range(T5_BUCKETS):
            acc = jnp.where(bm == b, tab_ref[b, h], acc)
        o_ref[0, h] = jnp.where(in_win & (col >= WINDOW), acc, NEG_INF)
        o_ref[1, h] = jnp.where(in_win, acc, NEG_INF)


def _t5_band(table):
    return pl.pallas_call(
        _bias_body,
        in_specs=[pl.BlockSpec(memory_space=pltpu.SMEM),
                  pl.BlockSpec((WINDOW, 2 * WINDOW), lambda: (0, 0))],
        out_specs=pl.BlockSpec((2, N_HEADS, WINDOW, 2 * WINDOW), lambda: (0, 0, 0, 0)),
        out_shape=jax.ShapeDtypeStruct((2, N_HEADS, WINDOW, 2 * WINDOW), F32),
        name="t5_band",
    )(table, jnp.asarray(_bucket_map()))


def _pair_heads(a, axis):
    shp = a.shape
    a = a.reshape(shp[:axis] + (N_KV, KV_GROUP, HEAD_DIM) + shp[axis + 1:])
    a = jnp.swapaxes(a, axis, axis + 1)
    return a.reshape(shp)


def _swa_prompt_body(sink_ref, q_ref, kc_ref, kp_ref, vc_ref, vp_ref, band_ref, o_ref, *, nsub):
    tile = N_KV * HEAD_DIM
    lo = lax.broadcasted_iota(jnp.int32, (WINDOW, tile), 1) < HEAD_DIM
    first_band = jnp.minimum(pl.program_id(1), 1)
    for sb in range(nsub):
        rows = slice(sb * WINDOW, (sb + 1) * WINDOW)
        before = slice((sb - 1) * WINDOW, sb * WINDOW)
        k_prev = kp_ref[...] if sb == 0 else kc_ref[before, :]
        v_prev = vp_ref[...] if sb == 0 else vc_ref[before, :]
        kb = jnp.concatenate([k_prev, kc_ref[rows, :]], axis=0).astype(BF16)
        vb = jnp.concatenate([v_prev, vc_ref[rows, :]], axis=0).astype(BF16)
        v_ext = jnp.concatenate([vb, jnp.ones_like(vb)], axis=1)
        for t in range(KV_GROUP):
            qt = q_ref[rows, t * tile:(t + 1) * tile] * (HEAD_DIM ** -0.5)
            halves = []
            for kv in range(N_KV):
                h = kv * KV_GROUP + t
                qh = jnp.where(lo if kv == 0 else jnp.logical_not(lo), qt, 0.0).astype(BF16)
                s = _dot_nt(qh, kb) + (band_ref[first_band, h] if sb == 0 else band_ref[1, h])
                sink = sink_ref[h]
                m = jnp.maximum(jnp.max(s, axis=-1, keepdims=True), sink)
                oe = _dot(jnp.exp(s - m).astype(BF16), v_ext)
                halves.append(oe[:, 0:tile] / (oe[:, tile:tile + 1] + jnp.exp(sink - m)))
            o_ref[rows, t * tile:(t + 1) * tile] = jnp.where(lo, halves[0], halves[1])


def _swa_prompt(q, k, v, sinks, band):
    b, l, _ = q.shape
    kvw = N_KV * HEAD_DIM
    nsub = min(SWA_BLOCKS_PER_STEP, l // WINDOW)
    rows = nsub * WINDOW
    cur = lambda w: pl.BlockSpec((None, rows, w), lambda i, n: (i, n, 0))
    prev = lambda w: pl.BlockSpec((None, WINDOW, w), lambda i, n: (i, jnp.maximum(n * nsub - 1, 0), 0))
    return pl.pallas_call(
        functools.partial(_swa_prompt_body, nsub=nsub),
        grid=(b, l // rows),
        in_specs=[pl.BlockSpec(memory_space=pltpu.SMEM), cur(SWA_WIDTH), cur(kvw), prev(kvw), cur(kvw), prev(kvw),
                  _resident(band.shape)],
        out_specs=cur(SWA_WIDTH),
        out_shape=jax.ShapeDtypeStruct((b, l, SWA_WIDTH), F32),
        compiler_params=_cparams("parallel", "arbitrary"),
        name="swa_prompt",
    )(sinks, q, k, k, v, v, band)


def _swa_sample_body(q_ref, kn_ref, vn_ref, ck_ref, cv_ref, bias_ref, sink_ref, o_ref, ko_ref, vo_ref, *, steps):
    tile = N_KV * HEAD_DIM
    ck, cv, kn, vn = ck_ref[...], cv_ref[...], kn_ref[...], vn_ref[...]
    ko_ref[:, 0:WINDOW - steps, :] = ck[:, steps:, :]
    ko_ref[:, WINDOW - steps:, :] = kn
    vo_ref[:, 0:WINDOW - steps, :] = cv[:, steps:, :]
    vo_ref[:, WINDOW - steps:, :] = vn
    lo = lax.broadcasted_iota(jnp.int32, (1, 1, tile), 2) < HEAD_DIM
    q = q_ref[...] * (HEAD_DIM ** -0.5)
    qs = []
    for kv in range(N_KV):
        for t in range(KV_GROUP):
            qs.append(jnp.where(lo if kv == 0 else jnp.logical_not(lo), q[:, :, t * tile:(t + 1) * tile], 0.0))
    qall = jnp.concatenate(qs, axis=1).astype(BF16)
    bias = bias_ref[...]
    sink = sink_ref[...][None]
    bdot = lambda a, b_, spec: jnp.einsum(spec, a, b_, preferred_element_type=F32)
    sc = bdot(qall, ck.astype(BF16), "bqd,bkd->bqk") + bias[None, :, 0:WINDOW]
    sn = bdot(qall, kn.astype(BF16), "bqd,bkd->bqk") + bias[None, :, WINDOW:WINDOW + steps]
    m = jnp.maximum(jnp.maximum(jnp.max(sc, axis=-1, keepdims=True), jnp.max(sn, axis=-1, keepdims=True)), sink)
    ec, en = jnp.exp(sc - m), jnp.exp(sn - m)
    den = jnp.sum(ec, axis=-1, keepdims=True) + jnp.sum(en, axis=-1, keepdims=True) + jnp.exp(sink - m)
    o = (bdot(ec.astype(BF16), cv.astype(BF16), "bqk,bkd->bqd")
         + bdot(en.astype(BF16), vn.astype(BF16), "bqk,bkd->bqd")) / den
    for t in range(KV_GROUP):
        o_ref[:, :, t * tile:(t + 1) * tile] = jnp.where(
            lo, o[:, t * steps:(t + 1) * steps, :], o[:, (KV_GROUP + t) * steps:(KV_GROUP + t + 1) * steps, :])


def _swa_sample(q, kn, vn, cache_k, cache_v, idx, sinks, band):
    b, steps, _ = q.shape
    kvw = N_KV * HEAD_DIM
    bb = SAMPLE_SEQ_BLOCK
    nrow = N_HEADS * steps
    bias = band[1, :, 0:steps, :].reshape(nrow, 2 * WINDOW)
    sink_rows = jnp.repeat(sinks, steps).reshape(nrow, 1)
    blk = lambda r, w: pl.BlockSpec((bb, r, w), lambda i: (i, 0, 0))
    cache = pl.BlockSpec((None, bb, WINDOW, kvw), lambda i: (idx, i, 0, 0))
    return pl.pallas_call(
        functools.partial(_swa_sample_body, steps=steps),
        grid=(b // bb,),
        in_specs=[blk(steps, SWA_WIDTH), blk(steps, kvw), blk(steps, kvw), cache, cache,
                  _resident((nrow, 2 * WINDOW)), _resident((nrow, 1))],
        out_specs=[blk(steps, SWA_WIDTH), blk(WINDOW, kvw), blk(WINDOW, kvw)],
        out_shape=[jax.ShapeDtypeStruct((b, steps, SWA_WIDTH), F32),
                   jax.ShapeDtypeStruct((b, WINDOW, kvw), F32),
                   jax.ShapeDtypeStruct((b, WINDOW, kvw), F32)],
        compiler_params=_cparams("parallel"),
        name="swa_sample",
    )(q, kn, vn, cache_k, cache_v, bias, sink_rows)


def _s5_prep_body(lr_ref, li_ref, ls_ref, br_ref, bi_ref, cr_ref, ci_ref, bm_ref, cm_ref, pw_ref, pinv_ref):
    hs = S5_HALF_STATES
    lr, li = lr_ref[...], li_ref[...]
    dt = jnp.exp(ls_ref[...])
    mag = jnp.exp(lr * dt)
    ang = li * dt
    ar = mag * jnp.cos(ang)
    ai = mag * jnp.sin(ang)
    den = lr * lr + li * li
    nr = ar - 1.0
    kr = (nr * lr + ai * li) / den
    ki = (ai * lr - nr * li) / den
    br, bi = br_ref[...], bi_ref[...]
    bm_ref[:, 0:hs] = (kr * br - ki * bi).astype(BF16)
    bm_ref[:, hs:2 * hs] = (kr * bi + ki * br).astype(BF16)
    cm_ref[0:hs, :] = cr_ref[...].astype(BF16)
    cm_ref[hs:2 * hs, :] = (-ci_ref[...]).astype(BF16)
    n2 = ar * ar + ai * ai
    ir, ii = ar / n2, -ai / n2
    pr, pi = jnp.ones_like(ar), jnp.zeros_like(ar)
    qr, qi = pr, pi
    for j in range(S5_CHUNK):
        pw_ref[0, j:j + 1, :] = pr
        pw_ref[1, j:j + 1, :] = pi
        pinv_ref[0, j:j + 1, :] = qr
        pinv_ref[1, j:j + 1, :] = qi
        pr, pi = pr * ar - pi * ai, pr * ai + pi * ar
        qr, qi = qr * ir - qi * ii, qr * ii + qi * ir


def _s5_prep(lam_re, lam_im, log_step, b_re, b_im, c_re, c_im):
    hs, hc = S5_HALF_STATES, S5_HALF_CH
    gh = S5_GROUPS // 2
    eye = jnp.eye(gh, dtype=bool)
    vec = lambda a: a.reshape(2, 1, hs)
    ls = jnp.broadcast_to(log_step[:, None], (S5_GROUPS, S5_STATE))

    def b_blockdiag(b):
        bt = b.reshape(2, gh, S5_STATE, S5_GROUP).transpose(0, 3, 1, 2)[:, None]
        full = jnp.where(eye[None, :, None, :, None], bt, 0.0)
        return full.reshape(2, hc, hs)

    def c_blockdiag(c):
        ct = c.reshape(2, gh, S5_GROUP, S5_STATE).transpose(0, 1, 3, 2)[:, :, :, None]
        full = jnp.where(eye[None, :, None, :, None], ct, 0.0)
        return full.reshape(2, hs, hc)

    vspec = pl.BlockSpec((None, 1, hs), lambda h: (h, 0, 0))
    bspec = pl.BlockSpec((None, hc, hs), lambda h: (h, 0, 0))
    cspec = pl.BlockSpec((None, hs, hc), lambda h: (h, 0, 0))
    pspec = pl.BlockSpec((None, 2, S5_CHUNK, hs), lambda h: (h, 0, 0, 0))
    return pl.pallas_call(
        _s5_prep_body,
        grid=(2,),
        in_specs=[vspec, vspec, vspec, bspec, bspec, cspec, cspec],
        out_specs=[pl.BlockSpec((None, hc, 2 * hs), lambda h: (h, 0, 0)),
                   pl.BlockSpec((None, 2 * hs, hc), lambda h: (h, 0, 0)), pspec, pspec],
        out_shape=[jax.ShapeDtypeStruct((2, hc, 2 * hs), BF16), jax.ShapeDtypeStruct((2, 2 * hs, hc), BF16),
                   jax.ShapeDtypeStruct((2, 2, S5_CHUNK, hs), F32), jax.ShapeDtypeStruct((2, 2, S5_CHUNK, hs), F32)],
        compiler_params=_cparams("parallel"),
        name="s5_prep",
    )(vec(lam_re), vec(lam_im), vec(ls), b_blockdiag(b_re), b_blockdiag(b_im), c_blockdiag(c_re), c_blockdiag(c_im))


def _s5_glu(y, u, d_ref, wg_ref):
    g = _gelu_tanh(y + d_ref[...] * u)
    return g * _sigmoid(_dot(g.astype(BF16), wg_ref[...]))


def _s5_prompt_body(u_ref, bm_ref, cm_ref, pw_ref, pinv_ref, d_ref, wg_ref, o_ref, hfin_ref, bu_scr, hb_scr,
                    h_scr, *, tb):
    hs, hc, c0 = S5_HALF_STATES, S5_HALF_CH, S5_CHUNK

    @pl.when(pl.program_id(1) == 0)
    def _():
        h_scr[...] = jnp.zeros_like(h_scr)

    u = u_ref[...]
    ub = u.astype(BF16)
    tril = (lax.broadcasted_iota(jnp.int32, (c0, c0), 0) >= lax.broadcasted_iota(jnp.int32, (c0, c0), 1)).astype(BF16)
    ys = []
    nchunk = tb // c0
    last = c0 - 1
    for hf in range(2):
        re = slice(2 * hf * hs, (2 * hf + 1) * hs)
        im = slice((2 * hf + 1) * hs, (2 * hf + 2) * hs)
        both = slice(2 * hf * hs, (2 * hf + 2) * hs)
        bu_scr[:, both] = _dot(ub[:, hf * hc:(hf + 1) * hc], bm_ref[hf])
        for ci in range(nchunk):
            rows = slice(ci * c0, (ci + 1) * c0)
            bur, bui = bu_scr[rows, re], bu_scr[rows, im]
            qr, qi = pinv_ref[hf, 0], pinv_ref[hf, 1]
            bu_scr[rows, re] = _dot(tril, (qr * bur - qi * bui).astype(BF16))
            bu_scr[rows, im] = _dot(tril, (qr * bui + qi * bur).astype(BF16))
        ar, ai = pw_ref[hf, 0, 1:2, :], pw_ref[hf, 1, 1:2, :]
        er, ei = pw_ref[hf, 0, last:c0, :], pw_ref[hf, 1, last:c0, :]
        hr0, hi0 = h_scr[:, re], h_scr[:, im]
        carries = []
        for ci in range(nchunk):
            gr, gi = ar * hr0 - ai * hi0, ar * hi0 + ai * hr0
            carries.append((gr, gi))
            sr = bu_scr[ci * c0 + last:(ci + 1) * c0, re] + gr
            si = bu_scr[ci * c0 + last:(ci + 1) * c0, im] + gi
            hr0, hi0 = er * sr - ei * si, er * si + ei * sr
        h_scr[:, re] = hr0
        h_scr[:, im] = hi0
        for ci in range(nchunk):
            rows = slice(ci * c0, (ci + 1) * c0)
            sr = bu_scr[rows, re] + carries[ci][0]
            si = bu_scr[rows, im] + carries[ci][1]
            pr, pi = pw_ref[hf, 0], pw_ref[hf, 1]
            hb_scr[rows, re] = (pr * sr - pi * si).astype(BF16)
            hb_scr[rows, im] = (pr * si + pi * sr).astype(BF16)
        ys.append(_dot(hb_scr[:, both], cm_ref[hf]))
    o_ref[...] = _s5_glu(jnp.concatenate(ys, axis=1), u, d_ref, wg_ref)
    hfin_ref[...] = h_scr[...]


def _s5_prompt(u, prep, d_skip, wglu_bf, tb):
    b, l, _ = u.shape
    bm, cm, pw, pinv = prep
    ns = 4 * S5_HALF_STATES
    return pl.pallas_call(
        functools.partial(_s5_prompt_body, tb=tb),
        grid=(b, l // tb),
        in_specs=[pl.BlockSpec((None, tb, S5_WIDTH), lambda i, j: (i, j, 0)),
                  _resident(bm.shape), _resident(cm.shape), _resident(pw.shape), _resident(pinv.shape),
                  _resident((1, S5_WIDTH)), _resident((S5_WIDTH, S5_WIDTH))],
        out_specs=[pl.BlockSpec((None, tb, S5_WIDTH), lambda i, j: (i, j, 0)),
                   pl.BlockSpec((None, 1, ns), lambda i, j: (i, 0, 0))],
        out_shape=[jax.ShapeDtypeStruct((b, l, S5_WIDTH), F32), jax.ShapeDtypeStruct((b, 1, ns), F32)],
        scratch_shapes=[pltpu.VMEM((tb, ns), F32), pltpu.VMEM((tb, ns), BF16), pltpu.VMEM((1, ns), F32)],
        compiler_params=_cparams("parallel", "arbitrary"),
        name="s5_prompt",
    )(u, bm, cm, pw, pinv, d_skip, wglu_bf)


def _s5_sample_body(u_ref, h0_ref, bm_ref, cm_ref, pw_ref, d_ref, wg_ref, o_ref, hfin_ref, *, steps):
    hs, hc = S5_HALF_STATES, S5_HALF_CH
    h = [h0_ref[:, k * hs:(k + 1) * hs] for k in range(4)]
    for t in range(steps):
        u = u_ref[t]
        ub = u.astype(BF16)
        ys = []
        for hf in range(2):
            bu = _dot(ub[:, hf * hc:(hf + 1) * hc], bm_ref[hf])
            ar, ai = pw_ref[hf, 0, 1:2, :], pw_ref[hf, 1, 1:2, :]
            hr, hi = h[2 * hf], h[2 * hf + 1]
            h[2 * hf] = ar * hr - ai * hi + bu[:, 0:hs]
            h[2 * hf + 1] = ar * hi + ai * hr + bu[:, hs:2 * hs]
            ys.append(_dot(jnp.concatenate([h[2 * hf], h[2 * hf + 1]], axis=1).astype(BF16), cm_ref[hf]))
        o_ref[t] = _s5_glu(jnp.concatenate(ys, axis=1), u, d_ref, wg_ref)
    for k in range(4):
        hfin_ref[:, k * hs:(k + 1) * hs] = h[k]


def _s5_sample(u_t, h0, prep, d_skip, wglu_bf):
    steps, b, _ = u_t.shape
    bm, cm, pw, _ = prep
    ns = 4 * S5_HALF_STATES
    full = lambda shape: pl.BlockSpec(shape, lambda: (0,) * len(shape))
    return pl.pallas_call(
        functools.partial(_s5_sample_body, steps=steps),
        in_specs=[full(u_t.shape), full(h0.shape), full(bm.shape), full(cm.shape), full(pw.shape),
                  full((1, S5_WIDTH)), full((S5_WIDTH, S5_WIDTH))],
        out_specs=[full((steps, b, S5_WIDTH)), full((b, ns))],
        out_shape=[jax.ShapeDtypeStruct((steps, b, S5_WIDTH), F32), jax.ShapeDtypeStruct((b, ns), F32)],
        compiler_params=pltpu.CompilerParams(vmem_limit_bytes=VMEM_LIMIT),
        name="s5_sample",
    )(u_t, h0, bm, cm, pw, d_skip, wglu_bf)


def _s5_state_in(re, im):
    b = re.shape[0]
    return jnp.stack([re.reshape(b, 2, S5_HALF_STATES), im.reshape(b, 2, S5_HALF_STATES)], axis=2).reshape(b, -1)


def _s5_state_out(h):
    b = h.shape[0]
    h4 = h.reshape(b, 2, 2, S5_HALF_STATES)
    return (h4[:, :, 0].reshape(b, S5_GROUPS, S5_STATE), h4[:, :, 1].reshape(b, S5_GROUPS, S5_STATE))


def _lb_body(p_ref, o_ref):
    p = p_ref[...]
    e = jnp.exp(p - jnp.max(p, axis=0, keepdims=True))
    sm = e / jnp.sum(e, axis=0, keepdims=True)
    acc = jnp.zeros_like(sm[0:1])
    for i in range(p.shape[0]):
        acc = acc + sm[i:i + 1]
        o_ref[i:i + 1, :] = acc - sm[0:1]


def _hgrn_lower_bounds(lb_param):
    n = lb_param.shape[0]
    return pl.pallas_call(
        _lb_body,
        in_specs=[pl.BlockSpec((n, HG_W), lambda: (0, 0))],
        out_specs=pl.BlockSpec((n, HG_W), lambda: (0, 0)),
        out_shape=jax.ShapeDtypeStruct((n, HG_W), F32),
        name="hgrn_lower_bounds",
    )(lb_param)


def _hgrn_forget(fz, lb):
    log_f = jnp.log(jnp.maximum(lb, LOG_FLOOR) + (1.0 - lb) * _sigmoid(fz))
    return log_f, (1.0 - lb) * _sigmoid(-fz)


def _group_norm_gate(o, gn, gate):
    return o * lax.rsqrt(jnp.mean(o * o, axis=-1, keepdims=True) + RMS_EPS) * gn * gate


def _block_rows(b, stride, offset):
    return [b[m * stride + offset:m * stride + offset + 1, :] for m in range(b.shape[0] // stride)]


def _spread_rows(rows, rowi, stride):
    out = rows[-1]
    for m in range(len(rows) - 2, -1, -1):
        out = jnp.where(rowi < (m + 1) * stride, rows[m], out)
    return out


def _cumsum_rows(tril_bf, x):
    hi = x.astype(BF16)
    r1 = x - hi.astype(F32)
    mid = r1.astype(BF16)
    lo = (r1 - mid.astype(F32)).astype(BF16)
    return _dot(tril_bf, hi) + _dot(tril_bf, mid) + _dot(tril_bf, lo)


def _hgrn_prompt_body(x_ref, g_ref, w_ref, lb_ref, gn_ref, o_ref, st_ref, q_scr, k_scr, b_scr, v_scr, s_scr, *, tb):
    c, w = HG_CHUNK, HG_W

    @pl.when(pl.program_id(1) == 0)
    def _():
        s_scr[...] = jnp.zeros_like(s_scr)

    row = lax.broadcasted_iota(jnp.int32, (c, c), 0)
    col = lax.broadcasted_iota(jnp.int32, (c, c), 1)
    tril = (row >= col).astype(BF16)
    xn = _rms(x_ref[...], g_ref[...]).astype(BF16)
    log_f, kf = _hgrn_forget(_dot(xn, w_ref[:, w:2 * w]), lb_ref[...])
    k_scr[...] = kf
    for ci in range(tb // c):
        b_scr[ci * c:(ci + 1) * c, :] = _cumsum_rows(tril, log_f[ci * c:(ci + 1) * c, :])
    q_scr[...] = _silu(_dot(xn, w_ref[:, 0:w]))
    v_scr[...] = _dot(xn, w_ref[:, 2 * w:3 * w]).astype(BF16)
    o_ref[...] = _silu(_dot(xn, w_ref[:, 3 * w:4 * w]))

    masks = [(row // HG_BASE == col // HG_BASE) & (col <= row)]
    strides = []
    s = 2 * HG_BASE
    while s <= c:
        strides.append(s)
        masks.append(row // s == col // s)
        s *= 2
    masks_all = [None if st == c else jnp.concatenate([mk] * HG_HEADS, axis=0)
                 for st, mk in zip([HG_BASE] + strides, masks)]
    cat = lambda parts: jnp.concatenate(parts, axis=0)

    def chunk(ci, carry):
        rows = pl.ds(pl.multiple_of(ci * c, c), c)
        q, k, vb = q_scr[rows, :], k_scr[rows, :], v_scr[rows, :]
        b = b_scr[rows, :]
        b_last = b[c - 1:c, :]
        q0, k0, q_in, k_st = [], [], [], []
        for m in range(c // HG_BASE):
            blk = slice(m * HG_BASE, (m + 1) * HG_BASE)
            mid = b[m * HG_BASE + HG_BASE // 2:m * HG_BASE + HG_BASE // 2 + 1, :]
            d = b[blk, :] - mid
            q0.append(q[blk, :] * jnp.exp(d))
            k0.append(k[blk, :] * jnp.exp(-d))
            q_in.append(q0[-1] * jnp.exp(mid))
            k_st.append(k0[-1] * jnp.exp(b_last - mid))
        qs, ks = [cat(q0).astype(BF16)], [cat(k0).astype(BF16)]
        q_in, k_st = cat(q_in).astype(BF16), cat(k_st).astype(BF16)
        for st in strides:
            ql, kl = [], []
            zeros = jnp.zeros((st // 2, w), BF16)
            for m in range(c // st):
                lower = slice(m * st, m * st + st // 2)
                upper = slice(m * st + st // 2, (m + 1) * st)
                mid = b[m * st + st // 2:m * st + st // 2 + 1, :]
                ql += [zeros, (q[upper, :] * jnp.exp(b[upper, :] - mid)).astype(BF16)]
                kl += [(k[lower, :] * jnp.exp(mid - b[lower, :])).astype(BF16), zeros]
            qs.append(cat(ql))
            ks.append(cat(kl))
        e_last = jnp.exp(b_last)
        heads = [slice(h * HG_DK, (h + 1) * HG_DK) for h in range(HG_HEADS)]
        a = None
        for qh, kh, mk in zip(qs, ks, masks_all):
            t = cat([_dot_nt(qh[:, hs], kh[:, hs]) for hs in heads])
            if mk is not None:
                t = jnp.where(mk, t, 0.0)
            a = t if a is None else a + t
        a = a.astype(BF16)
        states = [s_scr[h] for h in range(HG_HEADS)]
        o = jnp.concatenate([_dot(a[h * c:(h + 1) * c, :], vb[:, hs]) + _dot_nt(q_in[:, hs], states[h].astype(BF16))
                             for h, hs in enumerate(heads)], axis=0)
        gate = jnp.concatenate([o_ref[rows, hs] for hs in heads], axis=0)
        out = _group_norm_gate(o, gn_ref[...], gate)
        for h, hs in enumerate(heads):
            s_scr[h] = states[h] * e_last[:, hs] + _dot_tn(vb[:, hs], k_st[:, hs])
            o_ref[rows, hs] = out[h * c:(h + 1) * c, :]
        return carry

    lax.fori_loop(0, tb // c, chunk, 0, unroll=True)
    st_ref[...] = s_scr[...]


def _hgrn_prompt(x, g, w_bf, lb, gn, tb):
    b, l, _ = x.shape
    row = pl.BlockSpec((None, tb, D_MODEL), lambda i, j: (i, j, 0))
    return pl.pallas_call(
        functools.partial(_hgrn_prompt_body, tb=tb),
        grid=(b, l // tb),
        in_specs=[row, _resident((1, D_MODEL)), _resident((D_MODEL, 4 * HG_W)), _resident((1, HG_W)),
                  _resident((1, HG_DV))],
        out_specs=[row, pl.BlockSpec((None, HG_HEADS, HG_DV, HG_DK), lambda i, j: (i, 0, 0, 0))],
        out_shape=[jax.ShapeDtypeStruct((b, l, HG_W), F32),
                   jax.ShapeDtypeStruct((b, HG_HEADS, HG_DV, HG_DK), F32)],
        scratch_shapes=[pltpu.VMEM((tb, HG_W), F32)] * 3 + [pltpu.VMEM((tb, HG_W), BF16),
                                                             pltpu.VMEM((HG_HEADS, HG_DV, HG_DK), F32)],
        compiler_params=_cparams("parallel", "arbitrary"),
        name="hgrn_prompt",
    )(x, g, w_bf, lb, gn)


def _hgrn_sample_body(q_ref, fz_ref, iv_ref, gz_ref, s_ref, lb_ref, gn_ref, *rest, steps, bb, layer, creates):
    o_ref, so_all = rest[-2:]
    so_ref = so_all.at[layer] if creates else so_all
    if creates:
        for other in range(so_all.shape[0]):
            if other != layer:
                so_all[other] = jnp.zeros(so_all.shape[1:], F32)
    n = bb * steps
    qf = _silu(q_ref[...])
    log_f, kf = _hgrn_forget(fz_ref[...], lb_ref[...])
    v = iv_ref[...]
    gate = _silu(gz_ref[...])
    row = lax.broadcasted_iota(jnp.int32, (n, n), 0)
    col = lax.broadcasted_iota(jnp.int32, (n, n), 1)
    same = row // steps == col // steps
    causal = same & (col <= row)
    b = _dot_exact(causal.astype(F32), log_f)
    total = _dot_exact(same.astype(F32), log_f)
    first = (same & (col % steps == 0)).astype(F32)
    e0 = b - _dot_exact(first, log_f)
    q0 = (qf * jnp.exp(e0)).astype(BF16)
    k0 = (kf * jnp.exp(-e0)).astype(BF16)
    q_in = (qf * jnp.exp(b)).astype(BF16)
    k_st = (kf * jnp.exp(total - b)).astype(BF16)
    vb = v.astype(BF16)
    pick = (lax.broadcasted_iota(jnp.int32, (bb, n), 1) == steps * lax.broadcasted_iota(jnp.int32, (bb, n), 0))
    e_seq = jnp.exp(_dot_exact(pick.astype(F32), total))
    e_rows = jnp.concatenate([e_seq[:, h * HG_DK:(h + 1) * HG_DK] for h in range(HG_HEADS)], axis=0)
    pad = HG_DK - e_rows.shape[0]
    if pad:
        e_rows = jnp.concatenate([e_rows, jnp.zeros((pad, HG_DK), F32)], axis=0)
    e_cols = e_rows.T
    for h in range(HG_HEADS):
        hs = slice(h * HG_DK, (h + 1) * HG_DK)
        a = jnp.where(causal, _dot_nt(q0[:, hs], k0[:, hs]), 0.0)
        o_intra = _dot(a.astype(BF16), vb[:, hs])
        outs = []
        for i in range(bb):
            rs = slice(i * steps, (i + 1) * steps)
            s_old = s_ref[i, h]
            outs.append(_dot(q_in[rs, hs], s_old.astype(BF16)))
            so_ref[i, h] = (s_old * e_cols[:, h * bb + i:h * bb + i + 1]
                            + _dot_tn(k_st[rs, hs], vb[rs, hs]))
        o = o_intra + jnp.concatenate(outs, axis=0)
        o_ref[:, hs] = _group_norm_gate(o, gn_ref[...], gate[:, hs])


def _hgrn_sample(proj4, states, layer, out_states, lb, gn, steps):
    n_layers, b = states.shape[0], states.shape[1]
    bb = SAMPLE_SEQ_BLOCK
    assert HG_HEADS * bb <= HG_DK
    creates = out_states is None
    rows = pl.BlockSpec((bb * steps, HG_W), lambda i: (i, 0))
    sshape = (bb, HG_HEADS, HG_DK, HG_DV)
    s_in = pl.BlockSpec((None,) + sshape, lambda i: (layer, i, 0, 0, 0))
    s_out = pl.BlockSpec((n_layers,) + sshape, lambda i: (0, i, 0, 0, 0)) if creates else s_in
    in_specs = [rows, rows, rows, rows, s_in, _resident((1, HG_W)), _resident((1, HG_DV))]
    args = [*proj4, states, lb, gn]
    aliases = {}
    if not creates:
        in_specs.append(pl.BlockSpec(memory_space=pl.ANY))
        args.append(out_states)
        aliases = {len(args) - 1: 1}
    return pl.pallas_call(
        functools.partial(_hgrn_sample_body, steps=steps, bb=bb, layer=layer, creates=creates),
        grid=(b // bb,),
        in_specs=in_specs,
        out_specs=[rows, s_out],
        out_shape=[jax.ShapeDtypeStruct((b * steps, HG_W), F32), jax.ShapeDtypeStruct(states.shape, F32)],
        input_output_aliases=aliases,
        compiler_params=_cparams("parallel"),
        name="hgrn_sample",
    )(*args)


def _trunk(x, cache_k, cache_v, st_re, st_im, st_h, p):
    prompt = cache_k is None
    bt, l, _ = x.shape
    tm = min(TOKEN_BLOCK, bt * l)
    depth = p["norm_mix_pre"].shape[0]
    kvw = N_KV * HEAD_DIM
    row1 = lambda a: a.reshape(1, -1)
    x2 = x.reshape(bt * l, D_MODEL)
    k_out, v_out, re_out, im_out, h_out = [], [], [], [], []
    h_all = None
    if not prompt:
        cache_k = cache_k.reshape(cache_k.shape[:3] + (kvw,))
        cache_v = cache_v.reshape(cache_v.shape[:3] + (kvw,))
    for layer in range(depth):
        idx = layer // 2
        g_pre = row1(p["norm_mix_pre"][layer])
        if layer % 2 == 0:
            u, q, k, v = _norm_proj(x2, g_pre, p["w_in_even"][idx], (S5_WIDTH, SWA_WIDTH, kvw, kvw),
                                    min(PROJ_TOKEN_BLOCK, bt * l))
            prep = p["s5_prep"][idx]
            d_skip, wglu = row1(p["s5_d"][idx]), p["s5_w_glu"][idx]
            q3, k3, v3 = q.reshape(bt, l, SWA_WIDTH), k.reshape(bt, l, kvw), v.reshape(bt, l, kvw)
            if prompt:
                a_out, h_fin = _s5_prompt(u.reshape(bt, l, S5_WIDTH), prep, d_skip, wglu, min(TOKEN_BLOCK, l))
                h_fin = h_fin.reshape(bt, -1)
                b_out = _swa_prompt(q3, k3, v3, p["swa_sinks"][idx], p["t5_band"])
                kw, vw = k3[:, l - WINDOW:], v3[:, l - WINDOW:]
            else:
                u_t = u.reshape(bt, l, S5_WIDTH).transpose(1, 0, 2)
                a_t, h_fin = _s5_sample(u_t, _s5_state_in(st_re[idx], st_im[idx]), prep, d_skip, wglu)
                a_out = a_t.transpose(1, 0, 2)
                b_out, kw, vw = _swa_sample(q3, k3, v3, cache_k, cache_v, idx, p["swa_sinks"][idx], p["t5_band"])
            h_re, h_im = _s5_state_out(h_fin)
            k_out.append(kw.reshape(bt, WINDOW, N_KV, HEAD_DIM))
            v_out.append(vw.reshape(bt, WINDOW, N_KV, HEAD_DIM))
            re_out.append(h_re)
            im_out.append(h_im)
            mixer_outs = (a_out.reshape(bt * l, S5_WIDTH), b_out.reshape(bt * l, SWA_WIDTH))
            w_out = p["w_out_even"][idx]
        else:
            lb, gn = p["hgrn_lb"][idx:idx + 1], row1(p["hgrn_gnorm"][idx])
            if prompt:
                c_out, s_t = _hgrn_prompt(x2.reshape(bt, l, D_MODEL), g_pre, p["w_in_odd"][idx], lb, gn,
                                          min(TOKEN_BLOCK, l))
                h_out.append(s_t.transpose(0, 1, 3, 2))
            else:
                proj4 = _norm_proj(x2, g_pre, p["w_in_odd"][idx], (HG_W,) * 4, tm)
                c_out, h_all = _hgrn_sample(proj4, st_h, idx, h_all, lb, gn, l)
            mixer_outs = (c_out.reshape(bt * l, HG_W),)
            w_out = p["w_out_odd"][idx]
        x2 = _post(x2, mixer_outs, w_out, row1(p["norm_mix_post"][layer]), row1(p["norm_mlp_pre"][layer]),
                   row1(p["norm_mlp_post"][layer]), p["w_up"][layer], p["w_down"][layer], tm)
    return (x2.reshape(bt, l, D_MODEL), jnp.stack(k_out), jnp.stack(v_out), jnp.stack(re_out), jnp.stack(im_out),
            jnp.stack(h_out) if prompt else h_all)


def kernel(x_prompt, x_sample, cache_swa_k, cache_swa_v, state_s5_re, state_s5_im, state_hgrn, t5_bias_table,
           norm_mix_pre, norm_mix_post, norm_mlp_pre, norm_mlp_post, w_in_even, w_out_even, s5_lambda_re,
           s5_lambda_im, s5_log_step, s5_b_re, s5_b_im, s5_c_re, s5_c_im, s5_d, s5_w_glu, swa_sinks, w_in_odd,
           w_out_odd, hgrn_lb_param, hgrn_gnorm, w_up, w_down):
    bf = lambda a: a.astype(BF16)
    n_even = w_in_even.shape[0]
    q0, q1 = S5_WIDTH, S5_WIDTH + SWA_WIDTH
    w_in_even = jnp.concatenate([w_in_even[..., :q0], _pair_heads(w_in_even[..., q0:q1], 2), w_in_even[..., q1:]],
                                axis=-1)
    w_out_even = jnp.concatenate([w_out_even[:, :q0], _pair_heads(w_out_even[:, q0:], 1)], axis=1)
    params = dict(
        norm_mix_pre=norm_mix_pre, norm_mix_post=norm_mix_post, norm_mlp_pre=norm_mlp_pre,
        norm_mlp_post=norm_mlp_post, w_in_even=bf(w_in_even), w_out_even=bf(w_out_even), s5_d=s5_d,
        s5_w_glu=bf(s5_w_glu), swa_sinks=swa_sinks, w_in_odd=bf(w_in_odd), w_out_odd=bf(w_out_odd),
        hgrn_gnorm=hgrn_gnorm, w_up=bf(w_up), w_down=bf(w_down),
        t5_band=_t5_band(t5_bias_table),
        hgrn_lb=_hgrn_lower_bounds(hgrn_lb_param),
        s5_prep=[_s5_prep(s5_lambda_re[i], s5_lambda_im[i], s5_log_step[i], s5_b_re[i], s5_b_im[i], s5_c_re[i],
                          s5_c_im[i]) for i in range(n_even)],
    )
    y_prompt, k_p, v_p, re_p, im_p, hg_p = _trunk(x_prompt, None, None, None, None, None, params)
    y_sample, k_s, v_s, re_s, im_s, hg_s = _trunk(x_sample, cache_swa_k, cache_swa_v, state_s5_re, state_s5_im,
                                                  state_hgrn, params)
    return (y_prompt, y_sample, k_p, v_p, k_s, v_s, re_p, im_p, re_s, im_s, hg_p, hg_s)
```

```python
import functools
import math

import numpy as np
import jax
import jax.numpy as jnp
from jax import lax
from jax.experimental import pallas as pl
from jax.experimental.pallas import tpu as pltpu

F32 = jnp.float32
BF16 = jnp.bfloat16

D_MODEL = 1024
S5_WIDTH = 512
S5_GROUPS = 32
S5_GROUP = 16
S5_STATE = 64
S5_HALF_STATES = 1024
S5_HALF_CH = 256
SWA_WIDTH = 512
HEAD_DIM = 64
N_HEADS = 8
N_KV = 2
KV_GROUP = N_HEADS // N_KV
WINDOW = 128
T5_BUCKETS = 32
T5_MAX_DIST = 128
HG_HEADS = 8
HG_DK = 128
HG_DV = 128
HG_W = HG_HEADS * HG_DK
D_FF = 4096
RMS_EPS = 1e-6
NEG_INF = -1e30
LOG_FLOOR = 1e-30

S5_CHUNK = 64
HG_CHUNK = 128
HG_BASE = 32
TOKEN_BLOCK = 512
PROJ_TOKEN_BLOCK = 1024
SWA_BLOCKS_PER_STEP = 8
MLP_FF_BLOCK = 1024
SAMPLE_SEQ_BLOCK = 8
VMEM_LIMIT = 56 * 1024 * 1024
MXU_WIDTH = 256


def _cparams(*sem):
    return pltpu.CompilerParams(dimension_semantics=sem, vmem_limit_bytes=VMEM_LIMIT)


def _resident(shape):
    nd = len(shape)
    return pl.BlockSpec(shape, lambda *_: (0,) * nd, pipeline_mode=pl.Buffered(1))


def _rms(x, w):
    return x * lax.rsqrt(jnp.mean(x * x, axis=-1, keepdims=True) + RMS_EPS) * w


def _sigmoid(x):
    return 1.0 / (1.0 + jnp.exp(-x))


def _silu(x):
    return x * _sigmoid(x)


def _gelu_tanh(x):
    return 0.5 * x * (1.0 + jnp.tanh(math.sqrt(2.0 / math.pi) * (x + 0.044715 * (x * x * x))))


def _dot(a, b):
    return jnp.dot(a, b, preferred_element_type=F32)


def _dot_nt(a, b):
    return lax.dot_general(a, b, (((1,), (1,)), ((), ())), preferred_element_type=F32)


def _dot_tn(a, b):
    return lax.dot_general(a, b, (((0,), (0,)), ((), ())), preferred_element_type=F32)


def _dot_exact(a, b):
    return jnp.dot(a, b, preferred_element_type=F32, precision=lax.Precision.HIGHEST)


def _norm_proj_body(x_ref, g_ref, w_ref, *o_refs, splits):
    xn = _rms(x_ref[...], g_ref[...]).astype(BF16)
    off = 0
    i = 0
    while i < len(splits):
        j = i + 1
        while sum(splits[i:j]) < MXU_WIDTH and j < len(splits):
            j += 1
        y = _dot(xn, w_ref[:, off:off + sum(splits[i:j])])
        sub = 0
        for o_ref, wd in zip(o_refs[i:j], splits[i:j]):
            o_ref[...] = y[:, sub:sub + wd]
            sub += wd
        off += sub
        i = j


def _norm_proj(x2d, g, w_bf, splits, tm):
    t, n = x2d.shape[0], w_bf.shape[1]
    return pl.pallas_call(
        functools.partial(_norm_proj_body, splits=splits),
        grid=(t // tm,),
        in_specs=[pl.BlockSpec((tm, D_MODEL), lambda i: (i, 0)),
                  _resident((1, D_MODEL)), _resident((D_MODEL, n))],
        out_specs=[pl.BlockSpec((tm, wd), lambda i: (i, 0)) for wd in splits],
        out_shape=[jax.ShapeDtypeStruct((t, wd), F32) for wd in splits],
        compiler_params=_cparams("parallel"),
        name="norm_proj",
    )(x2d, g, w_bf)


def _post_body(*refs, widths):
    x_ref = refs[0]
    in_refs = refs[1:1 + len(widths)]
    wout_ref, gpost_ref, gpre_ref, gmpost_ref, wup_ref, wdn_ref, o_ref = refs[1 + len(widths):]
    mix = None
    off = 0
    for r, wd in zip(in_refs, widths):
        t = _dot(r[...].astype(BF16), wout_ref[off:off + wd, :])
        mix = t if mix is None else mix + t
        off += wd
    x1 = x_ref[...] + _rms(mix, gpost_ref[...])
    hn = _rms(x1, gpre_ref[...]).astype(BF16)
    acc = None
    for c in range(D_FF // MLP_FF_BLOCK):
        cs = slice(c * MLP_FF_BLOCK, (c + 1) * MLP_FF_BLOCK)
        hk = jnp.square(jnp.maximum(_dot(hn, wup_ref[:, cs]), 0.0)).astype(BF16)
        t = _dot(hk, wdn_ref[cs, :])
        acc = t if acc is None else acc + t
    o_ref[...] = x1 + _rms(acc, gmpost_ref[...])


def _post_stream_body(*refs, widths):
    x_ref = refs[0]
    in_refs = refs[1:1 + len(widths)]
    wout_ref, gpost_ref, gpre_ref, gmpost_ref, wup_ref, wdn_ref, o_ref, x1_scr, hn_scr, acc_scr = refs[1 + len(widths):]
    c = pl.program_id(0)

    @pl.when(c == 0)
    def _():
        mix = None
        off = 0
        for r, wd in zip(in_refs, widths):
            t = _dot(r[...].astype(BF16), wout_ref[off:off + wd, :])
            mix = t if mix is None else mix + t
            off += wd
        x1 = x_ref[...] + _rms(mix, gpost_ref[...])
        x1_scr[...] = x1
        hn_scr[...] = _rms(x1, gpre_ref[...]).astype(BF16)
        acc_scr[...] = jnp.zeros_like(acc_scr)

    hk = jnp.square(jnp.maximum(_dot(hn_scr[...], wup_ref[...]), 0.0)).astype(BF16)
    acc_scr[...] += _dot(hk, wdn_ref[...])

    @pl.when(c == pl.num_programs(0) - 1)
    def _():
        o_ref[...] = x1_scr[...] + _rms(acc_scr[...], gmpost_ref[...])


def _post(x2d, mixer_outs, wout_bf, gpost, gpre, gmpost, wup_bf, wdn_bf, tm):
    t = x2d.shape[0]
    widths = tuple(a.shape[1] for a in mixer_outs)
    if t == tm:
        whole = lambda wd: pl.BlockSpec((t, wd), lambda c: (0, 0))
        vec = pl.BlockSpec((1, D_MODEL), lambda c: (0, 0))
        return pl.pallas_call(
            functools.partial(_post_stream_body, widths=widths),
            grid=(D_FF // MLP_FF_BLOCK,),
            in_specs=[whole(D_MODEL)] + [whole(wd) for wd in widths] + [
                pl.BlockSpec((D_MODEL, D_MODEL), lambda c: (0, 0)), vec, vec, vec,
                pl.BlockSpec((D_MODEL, MLP_FF_BLOCK), lambda c: (0, c)),
                pl.BlockSpec((MLP_FF_BLOCK, D_MODEL), lambda c: (c, 0))],
            out_specs=whole(D_MODEL),
            out_shape=jax.ShapeDtypeStruct((t, D_MODEL), F32),
            scratch_shapes=[pltpu.VMEM((t, D_MODEL), F32), pltpu.VMEM((t, D_MODEL), BF16),
                            pltpu.VMEM((t, D_MODEL), F32)],
            compiler_params=_cparams("arbitrary"),
            name="post_mlp_stream",
        )(x2d, *mixer_outs, wout_bf, gpost, gpre, gmpost, wup_bf, wdn_bf)
    row = lambda wd: pl.BlockSpec((tm, wd), lambda i: (i, 0))
    return pl.pallas_call(
        functools.partial(_post_body, widths=widths),
        grid=(t // tm,),
        in_specs=[row(D_MODEL)] + [row(wd) for wd in widths] + [
            _resident((D_MODEL, D_MODEL)), _resident((1, D_MODEL)), _resident((1, D_MODEL)),
            _resident((1, D_MODEL)), _resident((D_MODEL, D_FF)), _resident((D_FF, D_MODEL))],
        out_specs=row(D_MODEL),
        out_shape=jax.ShapeDtypeStruct((t, D_MODEL), F32),
        compiler_params=_cparams("parallel"),
        name="post_mlp",
    )(x2d, *mixer_outs, wout_bf, gpost, gpre, gmpost, wup_bf, wdn_bf)


def _bucket_map():
    t = np.arange(WINDOW)[:, None]
    m = np.arange(2 * WINDOW)[None, :]
    n = np.maximum(WINDOW + t - m, 0)
    max_exact = T5_BUCKETS // 2
    large = max_exact + np.floor(np.log(np.maximum(n, max_exact) / max_exact)
                                 / math.log(T5_MAX_DIST / max_exact) * (T5_BUCKETS - max_exact)).astype(np.int64)
    large = np.minimum(large, T5_BUCKETS - 1)
    return np.where(n < max_exact, n, large).astype(np.int32)


def _bias_body(tab_ref, bm_ref, o_ref):
    bm = bm_ref[...]
    row = lax.broadcasted_iota(jnp.int32, bm.shape, 0)
    col = lax.broadcasted_iota(jnp.int32, bm.shape, 1)
    dist = WINDOW + row - col
    in_win = (dist >= 0) & (dist < WINDOW)
    for h in range(N_HEADS):
        acc = jnp.zeros(bm.shape, F32)
        for b in range(T5_BUCKETS):
            acc = jnp.where(bm == b, tab_ref[b, h], acc)
        o_ref[0, h] = jnp.where(in_win & (col >= WINDOW), acc, NEG_INF)
        o_ref[1, h] = jnp.where(in_win, acc, NEG_INF)


def _t5_band(table):
    return pl.pallas_call(
        _bias_body,
        in_specs=[pl.BlockSpec(memory_space=pltpu.SMEM),
                  pl.BlockSpec((WINDOW, 2 * WINDOW), lambda: (0, 0))],
        out_specs=pl.BlockSpec((2, N_HEADS, WINDOW, 2 * WINDOW), lambda: (0, 0, 0, 0)),
        out_shape=jax.ShapeDtypeStruct((2, N_HEADS, WINDOW, 2 * WINDOW), F32),
        name="t5_band",
    )(table, jnp.asarray(_bucket_map()))


def _pair_heads(a, axis):
    shp = a.shape
    a = a.reshape(shp[:axis] + (N_KV, KV_GROUP, HEAD_DIM) + shp[axis + 1:])
    a = jnp.swapaxes(a, axis, axis + 1)
    return a.reshape(shp)


def _swa_prompt_body(sink_ref, q_ref, kc_ref, kp_ref, vc_ref, vp_ref, band_ref, o_ref, *, nsub):
    tile = N_KV * HEAD_DIM
    lo = lax.broadcasted_iota(jnp.int32, (WINDOW, tile), 1) < HEAD_DIM
    first_band = jnp.minimum(pl.program_id(1), 1)
    for sb in range(nsub):
        rows = slice(sb * WINDOW, (sb + 1) * WINDOW)
        before = slice((sb - 1) * WINDOW, sb * WINDOW)
        k_prev = kp_ref[...] if sb == 0 else kc_ref[before, :]
        v_prev = vp_ref[...] if sb == 0 else vc_ref[before, :]
        kb = jnp.concatenate([k_prev, kc_ref[rows, :]], axis=0).astype(BF16)
        vb = jnp.concatenate([v_prev, vc_ref[rows, :]], axis=0).astype(BF16)
        v_ext = jnp.concatenate([vb, jnp.ones_like(vb)], axis=1)
        for t in range(KV_GROUP):
            qt = q_ref[rows, t * tile:(t + 1) * tile] * (HEAD_DIM ** -0.5)
            halves = []
            for kv in range(N_KV):
                h = kv * KV_GROUP + t
                qh = jnp.where(lo if kv == 0 else jnp.logical_not(lo), qt, 0.0).astype(BF16)
                s = _dot_nt(qh, kb) + (band_ref[first_band, h] if sb == 0 else band_ref[1, h])
                sink = sink_ref[h]
                m = jnp.maximum(jnp.max(s, axis=-1, keepdims=True), sink)
                oe = _dot(jnp.exp(s - m).astype(BF16), v_ext)
                halves.append(oe[:, 0:tile] / (oe[:, tile:tile + 1] + jnp.exp(sink - m)))
            o_ref[rows, t * tile:(t + 1) * tile] = jnp.where(lo, halves[0], halves[1])


def _swa_prompt(q, k, v, sinks, band):
    b, l, _ = q.shape
    kvw = N_KV * HEAD_DIM
    nsub = min(SWA_BLOCKS_PER_STEP, l // WINDOW)
    rows = nsub * WINDOW
    cur = lambda w: pl.BlockSpec((None, rows, w), lambda i, n: (i, n, 0))
    prev = lambda w: pl.BlockSpec((None, WINDOW, w), lambda i, n: (i, jnp.maximum(n * nsub - 1, 0), 0))
    return pl.pallas_call(
        functools.partial(_swa_prompt_body, nsub=nsub),
        grid=(b, l // rows),
        in_specs=[pl.BlockSpec(memory_space=pltpu.SMEM), cur(SWA_WIDTH), cur(kvw), prev(kvw), cur(kvw), prev(kvw),
                  _resident(band.shape)],
        out_specs=cur(SWA_WIDTH),
        out_shape=jax.ShapeDtypeStruct((b, l, SWA_WIDTH), F32),
        compiler_params=_cparams("parallel", "arbitrary"),
        name="swa_prompt",
    )(sinks, q, k, k, v, v, band)


def _swa_sample_body(q_ref, kn_ref, vn_ref, ck_ref, cv_ref, bias_ref, sink_ref, *rest, steps, layer, creates):
    o_ref, ko_all, vo_all = rest[-3:]
    ko_ref = ko_all.at[layer] if creates else ko_all
    vo_ref = vo_all.at[layer] if creates else vo_all
    if creates:
        for other in range(ko_all.shape[0]):
            if other != layer:
                ko_all[other] = jnp.zeros(ko_all.shape[1:], F32)
                vo_all[other] = jnp.zeros(vo_all.shape[1:], F32)
    tile = N_KV * HEAD_DIM
    ck, cv, kn, vn = ck_ref[...], cv_ref[...], kn_ref[...], vn_ref[...]
    ko_ref[:, 0:WINDOW - steps, :] = ck[:, steps:, :]
    ko_ref[:, WINDOW - steps:, :] = kn
    vo_ref[:, 0:WINDOW - steps, :] = cv[:, steps:, :]
    vo_ref[:, WINDOW - steps:, :] = vn
    lo = lax.broadcasted_iota(jnp.int32, (1, 1, tile), 2) < HEAD_DIM
    q = q_ref[...] * (HEAD_DIM ** -0.5)
    qs = []
    for kv in range(N_KV):
        for t in range(KV_GROUP):
            qs.append(jnp.where(lo if kv == 0 else jnp.logical_not(lo), q[:, :, t * tile:(t + 1) * tile], 0.0))
    qall = jnp.concatenate(qs, axis=1).astype(BF16)
    bias = bias_ref[...]
    sink = sink_ref[...][None]
    bdot = lambda a, b_, spec: jnp.einsum(spec, a, b_, preferred_element_type=F32)
    sc = bdot(qall, ck.astype(BF16), "bqd,bkd->bqk") + bias[None, :, 0:WINDOW]
    sn = bdot(qall, kn.astype(BF16), "bqd,bkd->bqk") + bias[None, :, WINDOW:WINDOW + steps]
    m = jnp.maximum(jnp.maximum(jnp.max(sc, axis=-1, keepdims=True), jnp.max(sn, axis=-1, keepdims=True)), sink)
    ec, en = jnp.exp(sc - m), jnp.exp(sn - m)
    den = jnp.sum(ec, axis=-1, keepdims=True) + jnp.sum(en, axis=-1, keepdims=True) + jnp.exp(sink - m)
    o = (bdot(ec.astype(BF16), cv.astype(BF16), "bqk,bkd->bqd")
         + bdot(en.astype(BF16), vn.astype(BF16), "bqk,bkd->bqd")) / den
    for t in range(KV_GROUP):
        o_ref[:, :, t * tile:(t + 1) * tile] = jnp.where(
            lo, o[:, t * steps:(t + 1) * steps, :], o[:, (KV_GROUP + t) * steps:(KV_GROUP + t + 1) * steps, :])


def _swa_sample(q, kn, vn, cache_k, cache_v, idx, new_caches, sinks, band):
    b, steps, _ = q.shape
    kvw = N_KV * HEAD_DIM
    bb = SAMPLE_SEQ_BLOCK
    nrow = N_HEADS * steps
    creates = new_caches is None
    bias = band[1, :, 0:steps, :].reshape(nrow, 2 * WINDOW)
    sink_rows = jnp.repeat(sinks, steps).reshape(nrow, 1)
    blk = lambda r, w: pl.BlockSpec((bb, r, w), lambda i: (i, 0, 0))
    cache = pl.BlockSpec((None, bb, WINDOW, kvw), lambda i: (idx, i, 0, 0))
    cache_out = (pl.BlockSpec((cache_k.shape[0], bb, WINDOW, kvw), lambda i: (0, i, 0, 0)) if creates else cache)
    in_specs = [blk(steps, SWA_WIDTH), blk(steps, kvw), blk(steps, kvw), cache, cache,
                _resident((nrow, 2 * WINDOW)), _resident((nrow, 1))]
    args = [q, kn, vn, cache_k, cache_v, bias, sink_rows]
    aliases = {}
    if not creates:
        in_specs += [pl.BlockSpec(memory_space=pl.ANY)] * 2
        args += list(new_caches)
        aliases = {len(args) - 2: 1, len(args) - 1: 2}
    o, k_new, v_new = pl.pallas_call(
        functools.partial(_swa_sample_body, steps=steps, layer=idx, creates=creates),
        grid=(b // bb,),
        in_specs=in_specs,
        out_specs=[blk(steps, SWA_WIDTH), cache_out, cache_out],
        out_shape=[jax.ShapeDtypeStruct((b, steps, SWA_WIDTH), F32),
                   jax.ShapeDtypeStruct(cache_k.shape, F32), jax.ShapeDtypeStruct(cache_v.shape, F32)],
        input_output_aliases=aliases,
        compiler_params=_cparams("parallel"),
        name="swa_sample",
    )(*args)
    return o, (k_new, v_new)


def _s5_prep_body(lr_ref, li_ref, ls_ref, br_ref, bi_ref, cr_ref, ci_ref, bm_ref, cm_ref, pw_ref, pinv_ref):
    hs = S5_HALF_STATES
    lr, li = lr_ref[...], li_ref[...]
    dt = jnp.exp(ls_ref[...])
    mag = jnp.exp(lr * dt)
    ang = li * dt
    ar = mag * jnp.cos(ang)
    ai = mag * jnp.sin(ang)
    den = lr * lr + li * li
    nr = ar - 1.0
    kr = (nr * lr + ai * li) / den
    ki = (ai * lr - nr * li) / den
    br, bi = br_ref[...], bi_ref[...]
    bm_ref[:, 0:hs] = (kr * br - ki * bi).astype(BF16)
    bm_ref[:, hs:2 * hs] = (kr * bi + ki * br).astype(BF16)
    cm_ref[0:hs, :] = cr_ref[...].astype(BF16)
    cm_ref[hs:2 * hs, :] = (-ci_ref[...]).astype(BF16)
    n2 = ar * ar + ai * ai
    ir, ii = ar / n2, -ai / n2
    pr, pi = jnp.ones_like(ar), jnp.zeros_like(ar)
    qr, qi = pr, pi
    for j in range(S5_CHUNK):
        pw_ref[0, j:j + 1, :] = pr
        pw_ref[1, j:j + 1, :] = pi
        pinv_ref[0, j:j + 1, :] = qr
        pinv_ref[1, j:j + 1, :] = qi
        pr, pi = pr * ar - pi * ai, pr * ai + pi * ar
        qr, qi = qr * ir - qi * ii, qr * ii + qi * ir


def _s5_prep(lam_re, lam_im, log_step, b_re, b_im, c_re, c_im):
    hs, hc = S5_HALF_STATES, S5_HALF_CH
    gh = S5_GROUPS // 2
    nh = 2 * lam_re.shape[0]
    eye = jnp.eye(gh, dtype=bool)
    vec = lambda a: a.reshape(nh, 1, hs)
    ls = jnp.broadcast_to(log_step[..., None], log_step.shape + (S5_STATE,))

    def b_blockdiag(b):
        bt = b.reshape(nh, gh, S5_STATE, S5_GROUP).transpose(0, 3, 1, 2)[:, None]
        full = jnp.where(eye[None, :, None, :, None], bt, 0.0)
        return full.reshape(nh, hc, hs)

    def c_blockdiag(c):
        ct = c.reshape(nh, gh, S5_GROUP, S5_STATE).transpose(0, 1, 3, 2)[:, :, :, None]
        full = jnp.where(eye[None, :, None, :, None], ct, 0.0)
        return full.reshape(nh, hs, hc)

    vspec = pl.BlockSpec((None, 1, hs), lambda h: (h, 0, 0))
    bspec = pl.BlockSpec((None, hc, hs), lambda h: (h, 0, 0))
    cspec = pl.BlockSpec((None, hs, hc), lambda h: (h, 0, 0))
    pspec = pl.BlockSpec((None, 2, S5_CHUNK, hs), lambda h: (h, 0, 0, 0))
    return pl.pallas_call(
        _s5_prep_body,
        grid=(nh,),
        in_specs=[vspec, vspec, vspec, bspec, bspec, cspec, cspec],
        out_specs=[pl.BlockSpec((None, hc, 2 * hs), lambda h: (h, 0, 0)),
                   pl.BlockSpec((None, 2 * hs, hc), lambda h: (h, 0, 0)), pspec, pspec],
        out_shape=[jax.ShapeDtypeStruct((nh, hc, 2 * hs), BF16), jax.ShapeDtypeStruct((nh, 2 * hs, hc), BF16),
                   jax.ShapeDtypeStruct((nh, 2, S5_CHUNK, hs), F32), jax.ShapeDtypeStruct((nh, 2, S5_CHUNK, hs), F32)],
        compiler_params=_cparams("parallel"),
        name="s5_prep",
    )(vec(lam_re), vec(lam_im), vec(ls), b_blockdiag(b_re), b_blockdiag(b_im), c_blockdiag(c_re), c_blockdiag(c_im))


def _s5_glu(y, u, d_ref, wg_ref):
    g = _gelu_tanh(y + d_ref[...] * u)
    return g * _sigmoid(_dot(g.astype(BF16), wg_ref[...]))


def _s5_prompt_body(u_ref, bm_ref, cm_ref, pw_ref, pinv_ref, d_ref, wg_ref, o_ref, hfin_ref, bu_scr, hb_scr,
                    h_scr, *, tb):
    hs, hc, c0 = S5_HALF_STATES, S5_HALF_CH, S5_CHUNK

    @pl.when(pl.program_id(1) == 0)
    def _():
        h_scr[...] = jnp.zeros_like(h_scr)

    u = u_ref[...]
    ub = u.astype(BF16)
    tril = (lax.broadcasted_iota(jnp.int32, (c0, c0), 0) >= lax.broadcasted_iota(jnp.int32, (c0, c0), 1)).astype(BF16)
    ys = []
    nchunk = tb // c0
    last = c0 - 1
    for hf in range(2):
        re = slice(2 * hf * hs, (2 * hf + 1) * hs)
        im = slice((2 * hf + 1) * hs, (2 * hf + 2) * hs)
        both = slice(2 * hf * hs, (2 * hf + 2) * hs)
        bu_scr[:, both] = _dot(ub[:, hf * hc:(hf + 1) * hc], bm_ref[hf])
        for ci in range(nchunk):
            rows = slice(ci * c0, (ci + 1) * c0)
            bur, bui = bu_scr[rows, re], bu_scr[rows, im]
            qr, qi = pinv_ref[hf, 0], pinv_ref[hf, 1]
            bu_scr[rows, re] = _dot(tril, (qr * bur - qi * bui).astype(BF16))
            bu_scr[rows, im] = _dot(tril, (qr * bui + qi * bur).astype(BF16))
        ar, ai = pw_ref[hf, 0, 1:2, :], pw_ref[hf, 1, 1:2, :]
        er, ei = pw_ref[hf, 0, last:c0, :], pw_ref[hf, 1, last:c0, :]
        hr0, hi0 = h_scr[:, re], h_scr[:, im]
        carries = []
        for ci in range(nchunk):
            gr, gi = ar * hr0 - ai * hi0, ar * hi0 + ai * hr0
            carries.append((gr, gi))
            sr = bu_scr[ci * c0 + last:(ci + 1) * c0, re] + gr
            si = bu_scr[ci * c0 + last:(ci + 1) * c0, im] + gi
            hr0, hi0 = er * sr - ei * si, er * si + ei * sr
        h_scr[:, re] = hr0
        h_scr[:, im] = hi0
        for ci in range(nchunk):
            rows = slice(ci * c0, (ci + 1) * c0)
            sr = bu_scr[rows, re] + carries[ci][0]
            si = bu_scr[rows, im] + carries[ci][1]
            pr, pi = pw_ref[hf, 0], pw_ref[hf, 1]
            hb_scr[rows, re] = (pr * sr - pi * si).astype(BF16)
            hb_scr[rows, im] = (pr * si + pi * sr).astype(BF16)
        ys.append(_dot(hb_scr[:, both], cm_ref[hf]))
    o_ref[...] = _s5_glu(jnp.concatenate(ys, axis=1), u, d_ref, wg_ref)
    hfin_ref[...] = h_scr[...]


def _layer_halves(a, layer):
    return pl.BlockSpec((2,) + a.shape[1:], lambda *_: (layer,) + (0,) * (a.ndim - 1), pipeline_mode=pl.Buffered(1))


def _s5_prompt(u, prep, layer, d_skip, wglu_bf, tb):
    b, l, _ = u.shape
    bm, cm, pw, pinv = prep
    ns = 4 * S5_HALF_STATES
    return pl.pallas_call(
        functools.partial(_s5_prompt_body, tb=tb),
        grid=(b, l // tb),
        in_specs=[pl.BlockSpec((None, tb, S5_WIDTH), lambda i, j: (i, j, 0)),
                  *[_layer_halves(a, layer) for a in (bm, cm, pw, pinv)],
                  _resident((1, S5_WIDTH)), _resident((S5_WIDTH, S5_WIDTH))],
        out_specs=[pl.BlockSpec((None, tb, S5_WIDTH), lambda i, j: (i, j, 0)),
                   pl.BlockSpec((None, 1, ns), lambda i, j: (i, 0, 0))],
        out_shape=[jax.ShapeDtypeStruct((b, l, S5_WIDTH), F32), jax.ShapeDtypeStruct((b, 1, ns), F32)],
        scratch_shapes=[pltpu.VMEM((tb, ns), F32), pltpu.VMEM((tb, ns), BF16), pltpu.VMEM((1, ns), F32)],
        compiler_params=_cparams("parallel", "arbitrary"),
        name="s5_prompt",
    )(u, bm, cm, pw, pinv, d_skip, wglu_bf)


def _s5_sample_body(u_ref, h0_ref, bm_ref, cm_ref, pw_ref, d_ref, wg_ref, o_ref, hfin_ref, *, steps):
    hs, hc = S5_HALF_STATES, S5_HALF_CH
    h = [h0_ref[:, k * hs:(k + 1) * hs] for k in range(4)]
    for t in range(steps):
        u = u_ref[t]
        ub = u.astype(BF16)
        ys = []
        for hf in range(2):
            bu = _dot(ub[:, hf * hc:(hf + 1) * hc], bm_ref[hf])
            ar, ai = pw_ref[hf, 0, 1:2, :], pw_ref[hf, 1, 1:2, :]
            hr, hi = h[2 * hf], h[2 * hf + 1]
            h[2 * hf] = ar * hr - ai * hi + bu[:, 0:hs]
            h[2 * hf + 1] = ar * hi + ai * hr + bu[:, hs:2 * hs]
            ys.append(_dot(jnp.concatenate([h[2 * hf], h[2 * hf + 1]], axis=1).astype(BF16), cm_ref[hf]))
        o_ref[t] = _s5_glu(jnp.concatenate(ys, axis=1), u, d_ref, wg_ref)
    for k in range(4):
        hfin_ref[:, k * hs:(k + 1) * hs] = h[k]


def _s5_sample(u_t, h0, prep, layer, d_skip, wglu_bf):
    steps, b, _ = u_t.shape
    bm, cm, pw, _ = prep
    ns = 4 * S5_HALF_STATES
    full = lambda shape: pl.BlockSpec(shape, lambda i: (0,) * len(shape))
    halves = lambda a: pl.BlockSpec((2,) + a.shape[1:], lambda i: (layer,) + (0,) * (a.ndim - 1))
    return pl.pallas_call(
        functools.partial(_s5_sample_body, steps=steps),
        grid=(1,),
        in_specs=[full(u_t.shape), full(h0.shape), halves(bm), halves(cm), halves(pw),
                  full((1, S5_WIDTH)), full((S5_WIDTH, S5_WIDTH))],
        out_specs=[full((steps, b, S5_WIDTH)), full((b, ns))],
        out_shape=[jax.ShapeDtypeStruct((steps, b, S5_WIDTH), F32), jax.ShapeDtypeStruct((b, ns), F32)],
        compiler_params=_cparams("arbitrary"),
        name="s5_sample",
    )(u_t, h0, bm, cm, pw, d_skip, wglu_bf)


def _s5_state_in(re, im):
    b = re.shape[0]
    return jnp.stack([re.reshape(b, 2, S5_HALF_STATES), im.reshape(b, 2, S5_HALF_STATES)], axis=2).reshape(b, -1)


def _s5_state_out(h):
    b = h.shape[0]
    h4 = h.reshape(b, 2, 2, S5_HALF_STATES)
    return (h4[:, :, 0].reshape(b, S5_GROUPS, S5_STATE), h4[:, :, 1].reshape(b, S5_GROUPS, S5_STATE))


def _lb_body(p_ref, o_ref):
    p = p_ref[...]
    e = jnp.exp(p - jnp.max(p, axis=0, keepdims=True))
    sm = e / jnp.sum(e, axis=0, keepdims=True)
    acc = jnp.zeros_like(sm[0:1])
    for i in range(p.shape[0]):
        acc = acc + sm[i:i + 1]
        o_ref[i:i + 1, :] = acc - sm[0:1]


def _hgrn_lower_bounds(lb_param):
    n = lb_param.shape[0]
    return pl.pallas_call(
        _lb_body,
        in_specs=[pl.BlockSpec((n, HG_W), lambda: (0, 0))],
        out_specs=pl.BlockSpec((n, HG_W), lambda: (0, 0)),
        out_shape=jax.ShapeDtypeStruct((n, HG_W), F32),
        name="hgrn_lower_bounds",
    )(lb_param)


def _hgrn_forget(fz, lb):
    log_f = jnp.log(jnp.maximum(lb, LOG_FLOOR) + (1.0 - lb) * _sigmoid(fz))
    return log_f, (1.0 - lb) * _sigmoid(-fz)


def _group_norm_gate(o, gn, gate):
    return o * lax.rsqrt(jnp.mean(o * o, axis=-1, keepdims=True) + RMS_EPS) * gn * gate


def _block_rows(b, stride, offset):
    return [b[m * stride + offset:m * stride + offset + 1, :] for m in range(b.shape[0] // stride)]


def _spread_rows(rows, rowi, stride):
    out = rows[-1]
    for m in range(len(rows) - 2, -1, -1):
        out = jnp.where(rowi < (m + 1) * stride, rows[m], out)
    return out


def _cumsum_rows(tril_bf, x):
    hi = x.astype(BF16)
    r1 = x - hi.astype(F32)
    mid = r1.astype(BF16)
    lo = (r1 - mid.astype(F32)).astype(BF16)
    return _dot(tril_bf, hi) + _dot(tril_bf, mid) + _dot(tril_bf, lo)


def _hgrn_prompt_body(x_ref, g_ref, w_ref, lb_ref, gn_ref, o_ref, st_ref, q_scr, k_scr, b_scr, v_scr, s_scr, *, tb):
    c, w = HG_CHUNK, HG_W

    @pl.when(pl.program_id(1) == 0)
    def _():
        s_scr[...] = jnp.zeros_like(s_scr)

    row = lax.broadcasted_iota(jnp.int32, (c, c), 0)
    col = lax.broadcasted_iota(jnp.int32, (c, c), 1)
    tril = (row >= col).astype(BF16)
    xn = _rms(x_ref[...], g_ref[...]).astype(BF16)
    log_f, kf = _hgrn_forget(_dot(xn, w_ref[:, w:2 * w]), lb_ref[...])
    k_scr[...] = kf
    for ci in range(tb // c):
        b_scr[ci * c:(ci + 1) * c, :] = _cumsum_rows(tril, log_f[ci * c:(ci + 1) * c, :])
    q_scr[...] = _silu(_dot(xn, w_ref[:, 0:w]))
    v_scr[...] = _dot(xn, w_ref[:, 2 * w:3 * w]).astype(BF16)
    o_ref[...] = _silu(_dot(xn, w_ref[:, 3 * w:4 * w]))

    masks = [(row // HG_BASE == col // HG_BASE) & (col <= row)]
    strides = []
    s = 2 * HG_BASE
    while s <= c:
        strides.append(s)
        masks.append(row // s == col // s)
        s *= 2
    masks_all = [None if st == c else jnp.concatenate([mk] * HG_HEADS, axis=0)
                 for st, mk in zip([HG_BASE] + strides, masks)]
    cat = lambda parts: jnp.concatenate(parts, axis=0)

    def chunk(ci, carry):
        rows = pl.ds(pl.multiple_of(ci * c, c), c)
        q, k, vb = q_scr[rows, :], k_scr[rows, :], v_scr[rows, :]
        b = b_scr[rows, :]
        b_last = b[c - 1:c, :]
        q0, k0, q_in, k_st = [], [], [], []
        for m in range(c // HG_BASE):
            blk = slice(m * HG_BASE, (m + 1) * HG_BASE)
            mid = b[m * HG_BASE + HG_BASE // 2:m * HG_BASE + HG_BASE // 2 + 1, :]
            d = b[blk, :] - mid
            q0.append(q[blk, :] * jnp.exp(d))
            k0.append(k[blk, :] * jnp.exp(-d))
            q_in.append(q0[-1] * jnp.exp(mid))
            k_st.append(k0[-1] * jnp.exp(b_last - mid))
        qs, ks = [cat(q0).astype(BF16)], [cat(k0).astype(BF16)]
        q_in, k_st = cat(q_in).astype(BF16), cat(k_st).astype(BF16)
        for st in strides:
            ql, kl = [], []
            zeros = jnp.zeros((st // 2, w), BF16)
            for m in range(c // st):
                lower = slice(m * st, m * st + st // 2)
                upper = slice(m * st + st // 2, (m + 1) * st)
                mid = b[m * st + st // 2:m * st + st // 2 + 1, :]
                ql += [zeros, (q[upper, :] * jnp.exp(b[upper, :] - mid)).astype(BF16)]
                kl += [(k[lower, :] * jnp.exp(mid - b[lower, :])).astype(BF16), zeros]
            qs.append(cat(ql))
            ks.append(cat(kl))
        e_last = jnp.exp(b_last)
        heads = [slice(h * HG_DK, (h + 1) * HG_DK) for h in range(HG_HEADS)]
        a = None
        for qh, kh, mk in zip(qs, ks, masks_all):
            t = cat([_dot_nt(qh[:, hs], kh[:, hs]) for hs in heads])
            if mk is not None:
                t = jnp.where(mk, t, 0.0)
            a = t if a is None else a + t
        a = a.astype(BF16)
        states = [s_scr[h] for h in range(HG_HEADS)]
        o = jnp.concatenate([_dot(a[h * c:(h + 1) * c, :], vb[:, hs]) + _dot_nt(q_in[:, hs], states[h].astype(BF16))
                             for h, hs in enumerate(heads)], axis=0)
        gate = jnp.concatenate([o_ref[rows, hs] for hs in heads], axis=0)
        out = _group_norm_gate(o, gn_ref[...], gate)
        for h, hs in enumerate(heads):
            s_scr[h] = states[h] * e_last[:, hs] + _dot_tn(vb[:, hs], k_st[:, hs])
            o_ref[rows, hs] = out[h * c:(h + 1) * c, :]
        return carry

    lax.fori_loop(0, tb // c, chunk, 0, unroll=True)
    st_ref[...] = s_scr[...]


def _hgrn_prompt(x, g, w_bf, lb, gn, tb):
    b, l, _ = x.shape
    row = pl.BlockSpec((None, tb, D_MODEL), lambda i, j: (i, j, 0))
    return pl.pallas_call(
        functools.partial(_hgrn_prompt_body, tb=tb),
        grid=(b, l // tb),
        in_specs=[row, _resident((1, D_MODEL)), _resident((D_MODEL, 4 * HG_W)), _resident((1, HG_W)),
                  _resident((1, HG_DV))],
        out_specs=[row, pl.BlockSpec((None, HG_HEADS, HG_DV, HG_DK), lambda i, j: (i, 0, 0, 0))],
        out_shape=[jax.ShapeDtypeStruct((b, l, HG_W), F32),
                   jax.ShapeDtypeStruct((b, HG_HEADS, HG_DV, HG_DK), F32)],
        scratch_shapes=[pltpu.VMEM((tb, HG_W), F32)] * 3 + [pltpu.VMEM((tb, HG_W), BF16),
                                                             pltpu.VMEM((HG_HEADS, HG_DV, HG_DK), F32)],
        compiler_params=_cparams("parallel", "arbitrary"),
        name="hgrn_prompt",
    )(x, g, w_bf, lb, gn)


def _hgrn_sample_body(q_ref, fz_ref, iv_ref, gz_ref, s_ref, lb_ref, gn_ref, *rest, steps, bb, layer, creates):
    o_ref, so_all = rest[-2:]
    so_ref = so_all.at[layer] if creates else so_all
    if creates:
        for other in range(so_all.shape[0]):
            if other != layer:
                so_all[other] = jnp.zeros(so_all.shape[1:], F32)
    n = bb * steps
    qf = _silu(q_ref[...])
    log_f, kf = _hgrn_forget(fz_ref[...], lb_ref[...])
    v = iv_ref[...]
    gate = _silu(gz_ref[...])
    row = lax.broadcasted_iota(jnp.int32, (n, n), 0)
    col = lax.broadcasted_iota(jnp.int32, (n, n), 1)
    same = row // steps == col // steps
    causal = same & (col <= row)
    b = _dot_exact(causal.astype(F32), log_f)
    total = _dot_exact(same.astype(F32), log_f)
    first = (same & (col % steps == 0)).astype(F32)
    e0 = b - _dot_exact(first, log_f)
    q0 = (qf * jnp.exp(e0)).astype(BF16)
    k0 = (kf * jnp.exp(-e0)).astype(BF16)
    q_in = (qf * jnp.exp(b)).astype(BF16)
    k_st = (kf * jnp.exp(total - b)).astype(BF16)
    vb = v.astype(BF16)
    pick = (lax.broadcasted_iota(jnp.int32, (bb, n), 1) == steps * lax.broadcasted_iota(jnp.int32, (bb, n), 0))
    e_seq = jnp.exp(_dot_exact(pick.astype(F32), total))
    e_rows = jnp.concatenate([e_seq[:, h * HG_DK:(h + 1) * HG_DK] for h in range(HG_HEADS)], axis=0)
    pad = HG_DK - e_rows.shape[0]
    if pad:
        e_rows = jnp.concatenate([e_rows, jnp.zeros((pad, HG_DK), F32)], axis=0)
    e_cols = e_rows.T
    for h in range(HG_HEADS):
        hs = slice(h * HG_DK, (h + 1) * HG_DK)
        a = jnp.where(causal, _dot_nt(q0[:, hs], k0[:, hs]), 0.0)
        o_intra = _dot(a.astype(BF16), vb[:, hs])
        outs = []
        for i in range(bb):
            rs = slice(i * steps, (i + 1) * steps)
            s_old = s_ref[i, h]
            outs.append(_dot(q_in[rs, hs], s_old.astype(BF16)))
            so_ref[i, h] = (s_old * e_cols[:, h * bb + i:h * bb + i + 1]
                            + _dot_tn(k_st[rs, hs], vb[rs, hs]))
        o = o_intra + jnp.concatenate(outs, axis=0)
        o_ref[:, hs] = _group_norm_gate(o, gn_ref[...], gate[:, hs])


def _hgrn_sample(proj4, states, layer, out_states, lb, gn, steps):
    n_layers, b = states.shape[0], states.shape[1]
    bb = SAMPLE_SEQ_BLOCK
    assert HG_HEADS * bb <= HG_DK
    creates = out_states is None
    rows = pl.BlockSpec((bb * steps, HG_W), lambda i: (i, 0))
    sshape = (bb, HG_HEADS, HG_DK, HG_DV)
    s_in = pl.BlockSpec((None,) + sshape, lambda i: (layer, i, 0, 0, 0))
    s_out = pl.BlockSpec((n_layers,) + sshape, lambda i: (0, i, 0, 0, 0)) if creates else s_in
    in_specs = [rows, rows, rows, rows, s_in, _resident((1, HG_W)), _resident((1, HG_DV))]
    args = [*proj4, states, lb, gn]
    aliases = {}
    if not creates:
        in_specs.append(pl.BlockSpec(memory_space=pl.ANY))
        args.append(out_states)
        aliases = {len(args) - 1: 1}
    return pl.pallas_call(
        functools.partial(_hgrn_sample_body, steps=steps, bb=bb, layer=layer, creates=creates),
        grid=(b // bb,),
        in_specs=in_specs,
        out_specs=[rows, s_out],
        out_shape=[jax.ShapeDtypeStruct((b * steps, HG_W), F32), jax.ShapeDtypeStruct(states.shape, F32)],
        input_output_aliases=aliases,
        compiler_params=_cparams("parallel"),
        name="hgrn_sample",
    )(*args)


def _trunk(x, cache_k, cache_v, st_re, st_im, st_h, p):
    prompt = cache_k is None
    bt, l, _ = x.shape
    tm = min(TOKEN_BLOCK, bt * l)
    depth = p["norm_mix_pre"].shape[0]
    kvw = N_KV * HEAD_DIM
    row1 = lambda a: a.reshape(1, -1)
    x2 = x.reshape(bt * l, D_MODEL)
    k_out, v_out, re_out, im_out, h_out = [], [], [], [], []
    h_all = kv_all = None
    if not prompt:
        cache_k = cache_k.reshape(cache_k.shape[:3] + (kvw,))
        cache_v = cache_v.reshape(cache_v.shape[:3] + (kvw,))
    for layer in range(depth):
        idx = layer // 2
        g_pre = row1(p["norm_mix_pre"][layer])
        if layer % 2 == 0:
            u, q, k, v = _norm_proj(x2, g_pre, p["w_in_even"][idx], (S5_WIDTH, SWA_WIDTH, kvw, kvw),
                                    min(PROJ_TOKEN_BLOCK, bt * l))
            prep = p["s5_prep"]
            d_skip, wglu = row1(p["s5_d"][idx]), p["s5_w_glu"][idx]
            q3, k3, v3 = q.reshape(bt, l, SWA_WIDTH), k.reshape(bt, l, kvw), v.reshape(bt, l, kvw)
            if prompt:
                a_out, h_fin = _s5_prompt(u.reshape(bt, l, S5_WIDTH), prep, idx, d_skip, wglu, min(TOKEN_BLOCK, l))
                h_fin = h_fin.reshape(bt, -1)
                b_out = _swa_prompt(q3, k3, v3, p["swa_sinks"][idx], p["t5_band"])
                kw, vw = k3[:, l - WINDOW:], v3[:, l - WINDOW:]
            else:
                u_t = u.reshape(bt, l, S5_WIDTH).transpose(1, 0, 2)
                a_t, h_fin = _s5_sample(u_t, _s5_state_in(st_re[idx], st_im[idx]), prep, idx, d_skip, wglu)
                a_out = a_t.transpose(1, 0, 2)
                b_out, kv_all = _swa_sample(q3, k3, v3, cache_k, cache_v, idx, kv_all, p["swa_sinks"][idx],
                                            p["t5_band"])
            h_re, h_im = _s5_state_out(h_fin)
            if prompt:
                k_out.append(kw.reshape(bt, WINDOW, N_KV, HEAD_DIM))
                v_out.append(vw.reshape(bt, WINDOW, N_KV, HEAD_DIM))
            re_out.append(h_re)
            im_out.append(h_im)
            mixer_outs = (a_out.reshape(bt * l, S5_WIDTH), b_out.reshape(bt * l, SWA_WIDTH))
            w_out = p["w_out_even"][idx]
        else:
            lb, gn = p["hgrn_lb"][idx:idx + 1], row1(p["hgrn_gnorm"][idx])
            if prompt:
                c_out, s_t = _hgrn_prompt(x2.reshape(bt, l, D_MODEL), g_pre, p["w_in_odd"][idx], lb, gn,
                                          min(TOKEN_BLOCK, l))
                h_out.append(s_t.transpose(0, 1, 3, 2))
            else:
                proj4 = _norm_proj(x2, g_pre, p["w_in_odd"][idx], (HG_W,) * 4, tm)
                c_out, h_all = _hgrn_sample(proj4, st_h, idx, h_all, lb, gn, l)
            mixer_outs = (c_out.reshape(bt * l, HG_W),)
            w_out = p["w_out_odd"][idx]
        x2 = _post(x2, mixer_outs, w_out, row1(p["norm_mix_post"][layer]), row1(p["norm_mlp_pre"][layer]),
                   row1(p["norm_mlp_post"][layer]), p["w_up"][layer], p["w_down"][layer], tm)
    if prompt:
        k_new, v_new, h_new = jnp.stack(k_out), jnp.stack(v_out), jnp.stack(h_out)
    else:
        k_new, v_new = (a.reshape(a.shape[:3] + (N_KV, HEAD_DIM)) for a in kv_all)
        h_new = h_all
    return x2.reshape(bt, l, D_MODEL), k_new, v_new, jnp.stack(re_out), jnp.stack(im_out), h_new


def kernel(x_prompt, x_sample, cache_swa_k, cache_swa_v, state_s5_re, state_s5_im, state_hgrn, t5_bias_table,
           norm_mix_pre, norm_mix_post, norm_mlp_pre, norm_mlp_post, w_in_even, w_out_even, s5_lambda_re,
           s5_lambda_im, s5_log_step, s5_b_re, s5_b_im, s5_c_re, s5_c_im, s5_d, s5_w_glu, swa_sinks, w_in_odd,
           w_out_odd, hgrn_lb_param, hgrn_gnorm, w_up, w_down):
    bf = lambda a: a.astype(BF16)
    q0, q1 = S5_WIDTH, S5_WIDTH + SWA_WIDTH
    w_in_even = jnp.concatenate([w_in_even[..., :q0], _pair_heads(w_in_even[..., q0:q1], 2), w_in_even[..., q1:]],
                                axis=-1)
    w_out_even = jnp.concatenate([w_out_even[:, :q0], _pair_heads(w_out_even[:, q0:], 1)], axis=1)
    params = dict(
        norm_mix_pre=norm_mix_pre, norm_mix_post=norm_mix_post, norm_mlp_pre=norm_mlp_pre,
        norm_mlp_post=norm_mlp_post, w_in_even=bf(w_in_even), w_out_even=bf(w_out_even), s5_d=s5_d,
        s5_w_glu=bf(s5_w_glu), swa_sinks=swa_sinks, w_in_odd=bf(w_in_odd), w_out_odd=bf(w_out_odd),
        hgrn_gnorm=hgrn_gnorm, w_up=bf(w_up), w_down=bf(w_down),
        t5_band=_t5_band(t5_bias_table),
        hgrn_lb=_hgrn_lower_bounds(hgrn_lb_param),
        s5_prep=_s5_prep(s5_lambda_re, s5_lambda_im, s5_log_step, s5_b_re, s5_b_im, s5_c_re, s5_c_im),
    )
    y_prompt, k_p, v_p, re_p, im_p, hg_p = _trunk(x_prompt, None, None, None, None, None, params)
    y_sample, k_s, v_s, re_s, im_s, hg_s = _trunk(x_sample, cache_swa_k, cache_swa_v, state_s5_re, state_s5_im,
                                                  state_hgrn, params)
    return (y_prompt, y_sample, k_p, v_p, k_s, v_s, re_p, im_p, re_s, im_s, hg_p, hg_s)
```

```python
import functools
import math

import numpy as np
import jax
import jax.numpy as jnp
from jax import lax
from jax.experimental import pallas as pl
from jax.experimental.pallas import tpu as pltpu

F32 = jnp.float32
BF16 = jnp.bfloat16

D_MODEL = 1024
S5_WIDTH = 512
S5_GROUPS = 32
S5_GROUP = 16
S5_STATE = 64
S5_HALF_STATES = 1024
S5_HALF_CH = 256
SWA_WIDTH = 512
HEAD_DIM = 64
N_HEADS = 8
N_KV = 2
KV_GROUP = N_HEADS // N_KV
WINDOW = 128
T5_BUCKETS = 32
T5_MAX_DIST = 128
HG_HEADS = 8
HG_DK = 128
HG_DV = 128
HG_W = HG_HEADS * HG_DK
D_FF = 4096
RMS_EPS = 1e-6
NEG_INF = -1e30
LOG_FLOOR = 1e-30

S5_CHUNK = 64
HG_CHUNK = 128
HG_BASE = 32
TOKEN_BLOCK = 512
PROJ_TOKEN_BLOCK = 1024
SWA_BLOCKS_PER_STEP = 8
MLP_FF_BLOCK = 1024
SAMPLE_SEQ_BLOCK = 8
VMEM_LIMIT = 56 * 1024 * 1024
MXU_WIDTH = 256


def _cparams(*sem):
    return pltpu.CompilerParams(dimension_semantics=sem, vmem_limit_bytes=VMEM_LIMIT)


def _resident(shape):
    nd = len(shape)
    return pl.BlockSpec(shape, lambda *_: (0,) * nd, pipeline_mode=pl.Buffered(1))


def _layer_resident(stack, layer):
    nd = stack.ndim
    return pl.BlockSpec((None,) + stack.shape[1:], lambda *_: (layer,) + (0,) * (nd - 1),
                        pipeline_mode=pl.Buffered(1))


def _rms(x, w):
    return x * lax.rsqrt(jnp.mean(x * x, axis=-1, keepdims=True) + RMS_EPS) * w


def _sigmoid(x):
    return 1.0 / (1.0 + jnp.exp(-x))


def _silu(x):
    return x * _sigmoid(x)


def _gelu_tanh(x):
    return 0.5 * x * (1.0 + jnp.tanh(math.sqrt(2.0 / math.pi) * (x + 0.044715 * (x * x * x))))


def _dot(a, b):
    return jnp.dot(a, b, preferred_element_type=F32)


def _dot_nt(a, b):
    return lax.dot_general(a, b, (((1,), (1,)), ((), ())), preferred_element_type=F32)


def _dot_tn(a, b):
    return lax.dot_general(a, b, (((0,), (0,)), ((), ())), preferred_element_type=F32)


def _dot_exact(a, b):
    return jnp.dot(a, b, preferred_element_type=F32, precision=lax.Precision.HIGHEST)


def _norm_proj_body(x_ref, g_ref, w_ref, *o_refs, splits):
    xn = _rms(x_ref[...], g_ref[...]).astype(BF16)
    off = 0
    i = 0
    while i < len(splits):
        j = i + 1
        while sum(splits[i:j]) < MXU_WIDTH and j < len(splits):
            j += 1
        y = _dot(xn, w_ref[:, off:off + sum(splits[i:j])])
        sub = 0
        for o_ref, wd in zip(o_refs[i:j], splits[i:j]):
            o_ref[...] = y[:, sub:sub + wd]
            sub += wd
        off += sub
        i = j


def _norm_proj(x2d, g, w_bf, splits, tm):
    t, n = x2d.shape[0], w_bf[0].shape[-1]
    return pl.pallas_call(
        functools.partial(_norm_proj_body, splits=splits),
        grid=(t // tm,),
        in_specs=[pl.BlockSpec((tm, D_MODEL), lambda i: (i, 0)),
                  _layer_resident(*g), _layer_resident(*w_bf)],
        out_specs=[pl.BlockSpec((tm, wd), lambda i: (i, 0)) for wd in splits],
        out_shape=[jax.ShapeDtypeStruct((t, wd), F32) for wd in splits],
        compiler_params=_cparams("parallel"),
        name="norm_proj",
    )(x2d, g[0], w_bf[0])


def _post_body(*refs, widths):
    x_ref = refs[0]
    in_refs = refs[1:1 + len(widths)]
    wout_ref, gpost_ref, gpre_ref, gmpost_ref, wup_ref, wdn_ref, o_ref = refs[1 + len(widths):]
    mix = None
    off = 0
    for r, wd in zip(in_refs, widths):
        t = _dot(r[...].astype(BF16), wout_ref[off:off + wd, :])
        mix = t if mix is None else mix + t
        off += wd
    x1 = x_ref[...] + _rms(mix, gpost_ref[...])
    hn = _rms(x1, gpre_ref[...]).astype(BF16)
    acc = None
    for c in range(D_FF // MLP_FF_BLOCK):
        cs = slice(c * MLP_FF_BLOCK, (c + 1) * MLP_FF_BLOCK)
        hk = jnp.square(jnp.maximum(_dot(hn, wup_ref[:, cs]), 0.0)).astype(BF16)
        t = _dot(hk, wdn_ref[cs, :])
        acc = t if acc is None else acc + t
    o_ref[...] = x1 + _rms(acc, gmpost_ref[...])


def _post_stream_body(*refs, widths):
    x_ref = refs[0]
    in_refs = refs[1:1 + len(widths)]
    wout_ref, gpost_ref, gpre_ref, gmpost_ref, wup_ref, wdn_ref, o_ref, x1_scr, hn_scr, acc_scr = refs[1 + len(widths):]
    c = pl.program_id(0)

    @pl.when(c == 0)
    def _():
        mix = None
        off = 0
        for r, wd in zip(in_refs, widths):
            t = _dot(r[...].astype(BF16), wout_ref[off:off + wd, :])
            mix = t if mix is None else mix + t
            off += wd
        x1 = x_ref[...] + _rms(mix, gpost_ref[...])
        x1_scr[...] = x1
        hn_scr[...] = _rms(x1, gpre_ref[...]).astype(BF16)
        acc_scr[...] = jnp.zeros_like(acc_scr)

    hk = jnp.square(jnp.maximum(_dot(hn_scr[...], wup_ref[...]), 0.0)).astype(BF16)
    acc_scr[...] += _dot(hk, wdn_ref[...])

    @pl.when(c == pl.num_programs(0) - 1)
    def _():
        o_ref[...] = x1_scr[...] + _rms(acc_scr[...], gmpost_ref[...])


def _post(x2d, mixer_outs, wout_bf, gpost, gpre, gmpost, wup_bf, wdn_bf, tm):
    t = x2d.shape[0]
    widths = tuple(a.shape[1] for a in mixer_outs)
    params = [wout_bf, gpost, gpre, gmpost, wup_bf, wdn_bf]
    stacks = [a for a, _ in params]
    if t == tm:
        whole = lambda wd: pl.BlockSpec((t, wd), lambda c: (0, 0))
        slab = lambda pr, blk, idx: pl.BlockSpec((None,) + blk, lambda c: (pr[1],) + idx(c))
        vec = lambda pr: slab(pr, (1, D_MODEL), lambda c: (0, 0))
        return pl.pallas_call(
            functools.partial(_post_stream_body, widths=widths),
            grid=(D_FF // MLP_FF_BLOCK,),
            in_specs=[whole(D_MODEL)] + [whole(wd) for wd in widths] + [
                slab(wout_bf, (D_MODEL, D_MODEL), lambda c: (0, 0)), vec(gpost), vec(gpre), vec(gmpost),
                slab(wup_bf, (D_MODEL, MLP_FF_BLOCK), lambda c: (0, c)),
                slab(wdn_bf, (MLP_FF_BLOCK, D_MODEL), lambda c: (c, 0))],
            out_specs=whole(D_MODEL),
            out_shape=jax.ShapeDtypeStruct((t, D_MODEL), F32),
            scratch_shapes=[pltpu.VMEM((t, D_MODEL), F32), pltpu.VMEM((t, D_MODEL), BF16),
                            pltpu.VMEM((t, D_MODEL), F32)],
            compiler_params=_cparams("arbitrary"),
            name="post_mlp_stream",
        )(x2d, *mixer_outs, *stacks)
    row = lambda wd: pl.BlockSpec((tm, wd), lambda i: (i, 0))
    return pl.pallas_call(
        functools.partial(_post_body, widths=widths),
        grid=(t // tm,),
        in_specs=[row(D_MODEL)] + [row(wd) for wd in widths] + [_layer_resident(*pr) for pr in params],
        out_specs=row(D_MODEL),
        out_shape=jax.ShapeDtypeStruct((t, D_MODEL), F32),
        compiler_params=_cparams("parallel"),
        name="post_mlp",
    )(x2d, *mixer_outs, *stacks)


def _bucket_map():
    t = np.arange(WINDOW)[:, None]
    m = np.arange(2 * WINDOW)[None, :]
    n = np.maximum(WINDOW + t - m, 0)
    max_exact = T5_BUCKETS // 2
    large = max_exact + np.floor(np.log(np.maximum(n, max_exact) / max_exact)
                                 / math.log(T5_MAX_DIST / max_exact) * (T5_BUCKETS - max_exact)).astype(np.int64)
    large = np.minimum(large, T5_BUCKETS - 1)
    return np.where(n < max_exact, n, large).astype(np.int32)


def _bias_body(tab_ref, bm_ref, o_ref):
    bm = bm_ref[...]
    row = lax.broadcasted_iota(jnp.int32, bm.shape, 0)
    col = lax.broadcasted_iota(jnp.int32, bm.shape, 1)
    dist = WINDOW + row - col
    in_win = (dist >= 0) & (dist < WINDOW)
    for h in range(N_HEADS):
        acc = jnp.zeros(bm.shape, F32)
        for b in range(T5_BUCKETS):
            acc = jnp.where(bm == b, tab_ref[b, h], acc)
        o_ref[0, h] = jnp.where(in_win & (col >= WINDOW), acc, NEG_INF)
        o_ref[1, h] = jnp.where(in_win, acc, NEG_INF)


def _t5_band(table):
    return pl.pallas_call(
        _bias_body,
        in_specs=[pl.BlockSpec(memory_space=pltpu.SMEM),
                  pl.BlockSpec((WINDOW, 2 * WINDOW), lambda: (0, 0))],
        out_specs=pl.BlockSpec((2, N_HEADS, WINDOW, 2 * WINDOW), lambda: (0, 0, 0, 0)),
        out_shape=jax.ShapeDtypeStruct((2, N_HEADS, WINDOW, 2 * WINDOW), F32),
        name="t5_band",
    )(table, jnp.asarray(_bucket_map()))


def _pair_heads(a, axis):
    shp = a.shape
    a = a.reshape(shp[:axis] + (N_KV, KV_GROUP, HEAD_DIM) + shp[axis + 1:])
    a = jnp.swapaxes(a, axis, axis + 1)
    return a.reshape(shp)


def _swa_prompt_body(sink_ref, q_ref, kc_ref, kp_ref, vc_ref, vp_ref, band_ref, o_ref, *, nsub):
    tile = N_KV * HEAD_DIM
    lo = lax.broadcasted_iota(jnp.int32, (WINDOW, tile), 1) < HEAD_DIM
    first_band = jnp.minimum(pl.program_id(1), 1)
    for sb in range(nsub):
        rows = slice(sb * WINDOW, (sb + 1) * WINDOW)
        before = slice((sb - 1) * WINDOW, sb * WINDOW)
        k_prev = kp_ref[...] if sb == 0 else kc_ref[before, :]
        v_prev = vp_ref[...] if sb == 0 else vc_ref[before, :]
        kb = jnp.concatenate([k_prev, kc_ref[rows, :]], axis=0).astype(BF16)
        vb = jnp.concatenate([v_prev, vc_ref[rows, :]], axis=0).astype(BF16)
        v_ext = jnp.concatenate([vb, jnp.ones_like(vb)], axis=1)
        for t in range(KV_GROUP):
            qt = q_ref[rows, t * tile:(t + 1) * tile] * (HEAD_DIM ** -0.5)
            halves = []
            for kv in range(N_KV):
                h = kv * KV_GROUP + t
                qh = jnp.where(lo if kv == 0 else jnp.logical_not(lo), qt, 0.0).astype(BF16)
                s = _dot_nt(qh, kb) + (band_ref[first_band, h] if sb == 0 else band_ref[1, h])
                sink = sink_ref[h]
                m = jnp.maximum(jnp.max(s, axis=-1, keepdims=True), sink)
                oe = _dot(jnp.exp(s - m).astype(BF16), v_ext)
                halves.append(oe[:, 0:tile] / (oe[:, tile:tile + 1] + jnp.exp(sink - m)))
            o_ref[rows, t * tile:(t + 1) * tile] = jnp.where(lo, halves[0], halves[1])


def _swa_prompt(q, k, v, sinks, band):
    b, l, _ = q.shape
    kvw = N_KV * HEAD_DIM
    nsub = min(SWA_BLOCKS_PER_STEP, l // WINDOW)
    rows = nsub * WINDOW
    cur = lambda w: pl.BlockSpec((None, rows, w), lambda i, n: (i, n, 0))
    prev = lambda w: pl.BlockSpec((None, WINDOW, w), lambda i, n: (i, jnp.maximum(n * nsub - 1, 0), 0))
    return pl.pallas_call(
        functools.partial(_swa_prompt_body, nsub=nsub),
        grid=(b, l // rows),
        in_specs=[pl.BlockSpec(memory_space=pltpu.SMEM), cur(SWA_WIDTH), cur(kvw), prev(kvw), cur(kvw), prev(kvw),
                  _resident(band.shape)],
        out_specs=cur(SWA_WIDTH),
        out_shape=jax.ShapeDtypeStruct((b, l, SWA_WIDTH), F32),
        compiler_params=_cparams("parallel", "arbitrary"),
        name="swa_prompt",
    )(sinks, q, k, k, v, v, band)


def _swa_sample_body(q_ref, kn_ref, vn_ref, ck_ref, cv_ref, bias_ref, sink_ref, *rest, steps, layer, creates):
    o_ref, ko_all, vo_all = rest[-3:]
    ko_ref = ko_all.at[layer] if creates else ko_all
    vo_ref = vo_all.at[layer] if creates else vo_all
    if creates:
        for other in range(ko_all.shape[0]):
            if other != layer:
                ko_all[other] = jnp.zeros(ko_all.shape[1:], F32)
                vo_all[other] = jnp.zeros(vo_all.shape[1:], F32)
    tile = N_KV * HEAD_DIM
    ck, cv, kn, vn = ck_ref[...], cv_ref[...], kn_ref[...], vn_ref[...]
    ko_ref[:, 0:WINDOW - steps, :] = ck[:, steps:, :]
    ko_ref[:, WINDOW - steps:, :] = kn
    vo_ref[:, 0:WINDOW - steps, :] = cv[:, steps:, :]
    vo_ref[:, WINDOW - steps:, :] = vn
    lo = lax.broadcasted_iota(jnp.int32, (1, 1, tile), 2) < HEAD_DIM
    q = q_ref[...] * (HEAD_DIM ** -0.5)
    qs = []
    for kv in range(N_KV):
        for t in range(KV_GROUP):
            qs.append(jnp.where(lo if kv == 0 else jnp.logical_not(lo), q[:, :, t * tile:(t + 1) * tile], 0.0))
    qall = jnp.concatenate(qs, axis=1).astype(BF16)
    bias = bias_ref[...]
    sink = sink_ref[...][None]
    bdot = lambda a, b_, spec: jnp.einsum(spec, a, b_, preferred_element_type=F32)
    sc = bdot(qall, ck.astype(BF16), "bqd,bkd->bqk") + bias[None, :, 0:WINDOW]
    sn = bdot(qall, kn.astype(BF16), "bqd,bkd->bqk") + bias[None, :, WINDOW:WINDOW + steps]
    m = jnp.maximum(jnp.maximum(jnp.max(sc, axis=-1, keepdims=True), jnp.max(sn, axis=-1, keepdims=True)), sink)
    ec, en = jnp.exp(sc - m), jnp.exp(sn - m)
    den = jnp.sum(ec, axis=-1, keepdims=True) + jnp.sum(en, axis=-1, keepdims=True) + jnp.exp(sink - m)
    o = (bdot(ec.astype(BF16), cv.astype(BF16), "bqk,bkd->bqd")
         + bdot(en.astype(BF16), vn.astype(BF16), "bqk,bkd->bqd")) / den
    for t in range(KV_GROUP):
        o_ref[:, :, t * tile:(t + 1) * tile] = jnp.where(
            lo, o[:, t * steps:(t + 1) * steps, :], o[:, (KV_GROUP + t) * steps:(KV_GROUP + t + 1) * steps, :])


def _swa_sample(q, kn, vn, cache_k, cache_v, idx, new_caches, sinks, band):
    b, steps, _ = q.shape
    kvw = N_KV * HEAD_DIM
    bb = SAMPLE_SEQ_BLOCK
    nrow = N_HEADS * steps
    creates = new_caches is None
    bias = band[1, :, 0:steps, :].reshape(nrow, 2 * WINDOW)
    sink_rows = jnp.repeat(sinks, steps).reshape(nrow, 1)
    blk = lambda r, w: pl.BlockSpec((bb, r, w), lambda i: (i, 0, 0))
    cache = pl.BlockSpec((None, bb, WINDOW, kvw), lambda i: (idx, i, 0, 0))
    cache_out = (pl.BlockSpec((cache_k.shape[0], bb, WINDOW, kvw), lambda i: (0, i, 0, 0)) if creates else cache)
    in_specs = [blk(steps, SWA_WIDTH), blk(steps, kvw), blk(steps, kvw), cache, cache,
                _resident((nrow, 2 * WINDOW)), _resident((nrow, 1))]
    args = [q, kn, vn, cache_k, cache_v, bias, sink_rows]
    aliases = {}
    if not creates:
        in_specs += [pl.BlockSpec(memory_space=pl.ANY)] * 2
        args += list(new_caches)
        aliases = {len(args) - 2: 1, len(args) - 1: 2}
    o, k_new, v_new = pl.pallas_call(
        functools.partial(_swa_sample_body, steps=steps, layer=idx, creates=creates),
        grid=(b // bb,),
        in_specs=in_specs,
        out_specs=[blk(steps, SWA_WIDTH), cache_out, cache_out],
        out_shape=[jax.ShapeDtypeStruct((b, steps, SWA_WIDTH), F32),
                   jax.ShapeDtypeStruct(cache_k.shape, F32), jax.ShapeDtypeStruct(cache_v.shape, F32)],
        input_output_aliases=aliases,
        compiler_params=_cparams("parallel"),
        name="swa_sample",
    )(*args)
    return o, (k_new, v_new)


def _s5_prep_body(lr_ref, li_ref, ls_ref, br_ref, bi_ref, cr_ref, ci_ref, bm_ref, cm_ref, pw_ref, pinv_ref):
    hs = S5_HALF_STATES
    lr, li = lr_ref[...], li_ref[...]
    dt = jnp.exp(ls_ref[...])
    mag = jnp.exp(lr * dt)
    ang = li * dt
    ar = mag * jnp.cos(ang)
    ai = mag * jnp.sin(ang)
    den = lr * lr + li * li
    nr = ar - 1.0
    kr = (nr * lr + ai * li) / den
    ki = (ai * lr - nr * li) / den
    hc = S5_HALF_CH
    own = (lax.broadcasted_iota(jnp.int32, (hc, hs), 0) // S5_GROUP
           == lax.broadcasted_iota(jnp.int32, (hc, hs), 1) // S5_STATE)
    blockdiag = lambda a: jnp.where(own, jnp.concatenate([a] * (hc // S5_GROUP), axis=0), 0.0)
    br, bi = br_ref[...], bi_ref[...]
    bm_ref[:, 0:hs] = blockdiag(kr * br - ki * bi).astype(BF16)
    bm_ref[:, hs:2 * hs] = blockdiag(kr * bi + ki * br).astype(BF16)
    cm_ref[0:hs, :] = blockdiag(cr_ref[...]).T.astype(BF16)
    cm_ref[hs:2 * hs, :] = blockdiag(-ci_ref[...]).T.astype(BF16)
    n2 = ar * ar + ai * ai
    ir, ii = ar / n2, -ai / n2
    pr, pi = jnp.ones_like(ar), jnp.zeros_like(ar)
    qr, qi = pr, pi
    for j in range(S5_CHUNK):
        pw_ref[0, j:j + 1, :] = pr
        pw_ref[1, j:j + 1, :] = pi
        pinv_ref[0, j:j + 1, :] = qr
        pinv_ref[1, j:j + 1, :] = qi
        pr, pi = pr * ar - pi * ai, pr * ai + pi * ar
        qr, qi = qr * ir - qi * ii, qr * ii + qi * ir


def _s5_prep(lam_re, lam_im, log_step, b_re, b_im, c_re, c_im):
    hs, hc = S5_HALF_STATES, S5_HALF_CH
    gh = S5_GROUPS // 2
    nh = 2 * lam_re.shape[0]
    vec = lambda a: a.reshape(nh, 1, hs)
    ls = jnp.broadcast_to(log_step[..., None], log_step.shape + (S5_STATE,))
    b_rows = lambda b: b.reshape(nh, gh, S5_STATE, S5_GROUP).transpose(0, 3, 1, 2).reshape(nh, S5_GROUP, hs)
    c_rows = lambda c: c.reshape(nh, gh, S5_GROUP, S5_STATE).transpose(0, 2, 1, 3).reshape(nh, S5_GROUP, hs)

    vspec = pl.BlockSpec((None, 1, hs), lambda h: (h, 0, 0))
    rspec = pl.BlockSpec((None, S5_GROUP, hs), lambda h: (h, 0, 0))
    pspec = pl.BlockSpec((None, 2, S5_CHUNK, hs), lambda h: (h, 0, 0, 0))
    return pl.pallas_call(
        _s5_prep_body,
        grid=(nh,),
        in_specs=[vspec, vspec, vspec, rspec, rspec, rspec, rspec],
        out_specs=[pl.BlockSpec((None, hc, 2 * hs), lambda h: (h, 0, 0)),
                   pl.BlockSpec((None, 2 * hs, hc), lambda h: (h, 0, 0)), pspec, pspec],
        out_shape=[jax.ShapeDtypeStruct((nh, hc, 2 * hs), BF16), jax.ShapeDtypeStruct((nh, 2 * hs, hc), BF16),
                   jax.ShapeDtypeStruct((nh, 2, S5_CHUNK, hs), F32), jax.ShapeDtypeStruct((nh, 2, S5_CHUNK, hs), F32)],
        compiler_params=_cparams("parallel"),
        name="s5_prep",
    )(vec(lam_re), vec(lam_im), vec(ls), b_rows(b_re), b_rows(b_im), c_rows(c_re), c_rows(c_im))


def _s5_glu(y, u, d_ref, wg_ref):
    g = _gelu_tanh(y + d_ref[...] * u)
    return g * _sigmoid(_dot(g.astype(BF16), wg_ref[...]))


def _s5_prompt_body(u_ref, bm_ref, cm_ref, pw_ref, pinv_ref, d_ref, wg_ref, o_ref, hfin_ref, bu_scr, hb_scr,
                    h_scr, *, tb):
    hs, hc, c0 = S5_HALF_STATES, S5_HALF_CH, S5_CHUNK

    @pl.when(pl.program_id(1) == 0)
    def _():
        h_scr[...] = jnp.zeros_like(h_scr)

    u = u_ref[...]
    ub = u.astype(BF16)
    tril = (lax.broadcasted_iota(jnp.int32, (c0, c0), 0) >= lax.broadcasted_iota(jnp.int32, (c0, c0), 1)).astype(BF16)
    ys = []
    nchunk = tb // c0
    last = c0 - 1
    for hf in range(2):
        re = slice(2 * hf * hs, (2 * hf + 1) * hs)
        im = slice((2 * hf + 1) * hs, (2 * hf + 2) * hs)
        both = slice(2 * hf * hs, (2 * hf + 2) * hs)
        bu_scr[:, both] = _dot(ub[:, hf * hc:(hf + 1) * hc], bm_ref[hf])
        for ci in range(nchunk):
            rows = slice(ci * c0, (ci + 1) * c0)
            bur, bui = bu_scr[rows, re], bu_scr[rows, im]
            qr, qi = pinv_ref[hf, 0], pinv_ref[hf, 1]
            bu_scr[rows, re] = _dot(tril, (qr * bur - qi * bui).astype(BF16))
            bu_scr[rows, im] = _dot(tril, (qr * bui + qi * bur).astype(BF16))
        ar, ai = pw_ref[hf, 0, 1:2, :], pw_ref[hf, 1, 1:2, :]
        er, ei = pw_ref[hf, 0, last:c0, :], pw_ref[hf, 1, last:c0, :]
        hr0, hi0 = h_scr[:, re], h_scr[:, im]
        carries = []
        for ci in range(nchunk):
            gr, gi = ar * hr0 - ai * hi0, ar * hi0 + ai * hr0
            carries.append((gr, gi))
            sr = bu_scr[ci * c0 + last:(ci + 1) * c0, re] + gr
            si = bu_scr[ci * c0 + last:(ci + 1) * c0, im] + gi
            hr0, hi0 = er * sr - ei * si, er * si + ei * sr
        h_scr[:, re] = hr0
        h_scr[:, im] = hi0
        for ci in range(nchunk):
            rows = slice(ci * c0, (ci + 1) * c0)
            sr = bu_scr[rows, re] + carries[ci][0]
            si = bu_scr[rows, im] + carries[ci][1]
            pr, pi = pw_ref[hf, 0], pw_ref[hf, 1]
            hb_scr[rows, re] = (pr * sr - pi * si).astype(BF16)
            hb_scr[rows, im] = (pr * si + pi * sr).astype(BF16)
        ys.append(_dot(hb_scr[:, both], cm_ref[hf]))
    o_ref[...] = _s5_glu(jnp.concatenate(ys, axis=1), u, d_ref, wg_ref)
    hfin_ref[...] = h_scr[...]


def _layer_halves(a, layer):
    return pl.BlockSpec((2,) + a.shape[1:], lambda *_: (layer,) + (0,) * (a.ndim - 1), pipeline_mode=pl.Buffered(1))


def _s5_prompt(u, prep, layer, d_skip, wglu_bf, tb):
    b, l, _ = u.shape
    bm, cm, pw, pinv = prep
    ns = 4 * S5_HALF_STATES
    return pl.pallas_call(
        functools.partial(_s5_prompt_body, tb=tb),
        grid=(b, l // tb),
        in_specs=[pl.BlockSpec((None, tb, S5_WIDTH), lambda i, j: (i, j, 0)),
                  *[_layer_halves(a, layer) for a in (bm, cm, pw, pinv)],
                  _layer_resident(d_skip, layer), _layer_resident(wglu_bf, layer)],
        out_specs=[pl.BlockSpec((None, tb, S5_WIDTH), lambda i, j: (i, j, 0)),
                   pl.BlockSpec((None, 1, ns), lambda i, j: (i, 0, 0))],
        out_shape=[jax.ShapeDtypeStruct((b, l, S5_WIDTH), F32), jax.ShapeDtypeStruct((b, 1, ns), F32)],
        scratch_shapes=[pltpu.VMEM((tb, ns), F32), pltpu.VMEM((tb, ns), BF16), pltpu.VMEM((1, ns), F32)],
        compiler_params=_cparams("parallel", "arbitrary"),
        name="s5_prompt",
    )(u, bm, cm, pw, pinv, d_skip, wglu_bf)


def _s5_sample_body(u_ref, h0_ref, bm_ref, cm_ref, pw_ref, d_ref, wg_ref, o_ref, hfin_ref, *, steps):
    hs, hc = S5_HALF_STATES, S5_HALF_CH
    h = [h0_ref[:, k * hs:(k + 1) * hs] for k in range(4)]
    for t in range(steps):
        u = u_ref[t]
        ub = u.astype(BF16)
        ys = []
        for hf in range(2):
            bu = _dot(ub[:, hf * hc:(hf + 1) * hc], bm_ref[hf])
            ar, ai = pw_ref[hf, 0, 1:2, :], pw_ref[hf, 1, 1:2, :]
            hr, hi = h[2 * hf], h[2 * hf + 1]
            h[2 * hf] = ar * hr - ai * hi + bu[:, 0:hs]
            h[2 * hf + 1] = ar * hi + ai * hr + bu[:, hs:2 * hs]
            ys.append(_dot(jnp.concatenate([h[2 * hf], h[2 * hf + 1]], axis=1).astype(BF16), cm_ref[hf]))
        o_ref[t] = _s5_glu(jnp.concatenate(ys, axis=1), u, d_ref, wg_ref)
    for k in range(4):
        hfin_ref[:, k * hs:(k + 1) * hs] = h[k]


def _s5_sample(u_t, h0, prep, layer, d_skip, wglu_bf):
    steps, b, _ = u_t.shape
    bm, cm, pw, _ = prep
    ns = 4 * S5_HALF_STATES
    full = lambda shape: pl.BlockSpec(shape, lambda i: (0,) * len(shape))
    halves = lambda a: pl.BlockSpec((2,) + a.shape[1:], lambda i: (layer,) + (0,) * (a.ndim - 1))
    return pl.pallas_call(
        functools.partial(_s5_sample_body, steps=steps),
        grid=(1,),
        in_specs=[full(u_t.shape), full(h0.shape), halves(bm), halves(cm), halves(pw),
                  _layer_resident(d_skip, layer), _layer_resident(wglu_bf, layer)],
        out_specs=[full((steps, b, S5_WIDTH)), full((b, ns))],
        out_shape=[jax.ShapeDtypeStruct((steps, b, S5_WIDTH), F32), jax.ShapeDtypeStruct((b, ns), F32)],
        compiler_params=_cparams("arbitrary"),
        name="s5_sample",
    )(u_t, h0, bm, cm, pw, d_skip, wglu_bf)


def _s5_state_in(re, im):
    b = re.shape[0]
    return jnp.stack([re.reshape(b, 2, S5_HALF_STATES), im.reshape(b, 2, S5_HALF_STATES)], axis=2).reshape(b, -1)


def _s5_state_out(h):
    b = h.shape[0]
    h4 = h.reshape(b, 2, 2, S5_HALF_STATES)
    return (h4[:, :, 0].reshape(b, S5_GROUPS, S5_STATE), h4[:, :, 1].reshape(b, S5_GROUPS, S5_STATE))


def _lb_body(p_ref, o_ref):
    p = p_ref[...]
    e = jnp.exp(p - jnp.max(p, axis=0, keepdims=True))
    sm = e / jnp.sum(e, axis=0, keepdims=True)
    acc = jnp.zeros_like(sm[0:1])
    for i in range(p.shape[0]):
        acc = acc + sm[i:i + 1]
        o_ref[i:i + 1, :] = acc - sm[0:1]


def _hgrn_lower_bounds(lb_param):
    n = lb_param.shape[0]
    return pl.pallas_call(
        _lb_body,
        in_specs=[pl.BlockSpec((n, HG_W), lambda: (0, 0))],
        out_specs=pl.BlockSpec((n, HG_W), lambda: (0, 0)),
        out_shape=jax.ShapeDtypeStruct((n, HG_W), F32),
        name="hgrn_lower_bounds",
    )(lb_param)


def _hgrn_forget(fz, lb):
    log_f = jnp.log(jnp.maximum(lb, LOG_FLOOR) + (1.0 - lb) * _sigmoid(fz))
    return log_f, (1.0 - lb) * _sigmoid(-fz)


def _group_norm_gate(o, gn, gate):
    return o * lax.rsqrt(jnp.mean(o * o, axis=-1, keepdims=True) + RMS_EPS) * gn * gate


def _block_rows(b, stride, offset):
    return [b[m * stride + offset:m * stride + offset + 1, :] for m in range(b.shape[0] // stride)]


def _spread_rows(rows, rowi, stride):
    out = rows[-1]
    for m in range(len(rows) - 2, -1, -1):
        out = jnp.where(rowi < (m + 1) * stride, rows[m], out)
    return out


def _cumsum_rows(tril_bf, x):
    hi = x.astype(BF16)
    r1 = x - hi.astype(F32)
    mid = r1.astype(BF16)
    lo = (r1 - mid.astype(F32)).astype(BF16)
    return _dot(tril_bf, hi) + _dot(tril_bf, mid) + _dot(tril_bf, lo)


def _hgrn_prompt_body(x_ref, g_ref, w_ref, lb_ref, gn_ref, o_ref, st_ref, q_scr, k_scr, b_scr, v_scr, s_scr, *, tb):
    c, w = HG_CHUNK, HG_W

    @pl.when(pl.program_id(1) == 0)
    def _():
        s_scr[...] = jnp.zeros_like(s_scr)

    row = lax.broadcasted_iota(jnp.int32, (c, c), 0)
    col = lax.broadcasted_iota(jnp.int32, (c, c), 1)
    tril = (row >= col).astype(BF16)
    xn = _rms(x_ref[...], g_ref[...]).astype(BF16)
    log_f, kf = _hgrn_forget(_dot(xn, w_ref[:, w:2 * w]), lb_ref[...])
    k_scr[...] = kf
    for ci in range(tb // c):
        b_scr[ci * c:(ci + 1) * c, :] = _cumsum_rows(tril, log_f[ci * c:(ci + 1) * c, :])
    q_scr[...] = _silu(_dot(xn, w_ref[:, 0:w]))
    v_scr[...] = _dot(xn, w_ref[:, 2 * w:3 * w]).astype(BF16)
    o_ref[...] = _silu(_dot(xn, w_ref[:, 3 * w:4 * w]))

    masks = [(row // HG_BASE == col // HG_BASE) & (col <= row)]
    strides = []
    s = 2 * HG_BASE
    while s <= c:
        strides.append(s)
        masks.append(row // s == col // s)
        s *= 2
    masks_all = [None if st == c else jnp.concatenate([mk] * HG_HEADS, axis=0)
                 for st, mk in zip([HG_BASE] + strides, masks)]
    cat = lambda parts: jnp.concatenate(parts, axis=0)

    def chunk(ci, carry):
        rows = pl.ds(pl.multiple_of(ci * c, c), c)
        q, k, vb = q_scr[rows, :], k_scr[rows, :], v_scr[rows, :]
        b = b_scr[rows, :]
        b_last = b[c - 1:c, :]
        q0, k0, q_in, k_st = [], [], [], []
        for m in range(c // HG_BASE):
            blk = slice(m * HG_BASE, (m + 1) * HG_BASE)
            mid = b[m * HG_BASE + HG_BASE // 2:m * HG_BASE + HG_BASE // 2 + 1, :]
            d = b[blk, :] - mid
            q0.append(q[blk, :] * jnp.exp(d))
            k0.append(k[blk, :] * jnp.exp(-d))
            q_in.append(q0[-1] * jnp.exp(mid))
            k_st.append(k0[-1] * jnp.exp(b_last - mid))
        qs, ks = [cat(q0).astype(BF16)], [cat(k0).astype(BF16)]
        q_in, k_st = cat(q_in).astype(BF16), cat(k_st).astype(BF16)
        for st in strides:
            ql, kl = [], []
            zeros = jnp.zeros((st // 2, w), BF16)
            for m in range(c // st):
                lower = slice(m * st, m * st + st // 2)
                upper = slice(m * st + st // 2, (m + 1) * st)
                mid = b[m * st + st // 2:m * st + st // 2 + 1, :]
                ql += [zeros, (q[upper, :] * jnp.exp(b[upper, :] - mid)).astype(BF16)]
                kl += [(k[lower, :] * jnp.exp(mid - b[lower, :])).astype(BF16), zeros]
            qs.append(cat(ql))
            ks.append(cat(kl))
        e_last = jnp.exp(b_last)
        heads = [slice(h * HG_DK, (h + 1) * HG_DK) for h in range(HG_HEADS)]
        a = None
        for qh, kh, mk in zip(qs, ks, masks_all):
            t = cat([_dot_nt(qh[:, hs], kh[:, hs]) for hs in heads])
            if mk is not None:
                t = jnp.where(mk, t, 0.0)
            a = t if a is None else a + t
        a = a.astype(BF16)
        states = [s_scr[h] for h in range(HG_HEADS)]
        o = jnp.concatenate([_dot(a[h * c:(h + 1) * c, :], vb[:, hs]) + _dot_nt(q_in[:, hs], states[h].astype(BF16))
                             for h, hs in enumerate(heads)], axis=0)
        gate = jnp.concatenate([o_ref[rows, hs] for hs in heads], axis=0)
        out = _group_norm_gate(o, gn_ref[...], gate)
        for h, hs in enumerate(heads):
            s_scr[h] = states[h] * e_last[:, hs] + _dot_tn(vb[:, hs], k_st[:, hs])
            o_ref[rows, hs] = out[h * c:(h + 1) * c, :]
        return carry

    lax.fori_loop(0, tb // c, chunk, 0, unroll=True)
    st_ref[...] = s_scr[...]


def _hgrn_prompt(x, g, w_bf, lb, gn, tb):
    b, l, _ = x.shape
    row = pl.BlockSpec((None, tb, D_MODEL), lambda i, j: (i, j, 0))
    return pl.pallas_call(
        functools.partial(_hgrn_prompt_body, tb=tb),
        grid=(b, l // tb),
        in_specs=[row, _layer_resident(*g), _layer_resident(*w_bf), _layer_resident(*lb), _layer_resident(*gn)],
        out_specs=[row, pl.BlockSpec((None, HG_HEADS, HG_DV, HG_DK), lambda i, j: (i, 0, 0, 0))],
        out_shape=[jax.ShapeDtypeStruct((b, l, HG_W), F32),
                   jax.ShapeDtypeStruct((b, HG_HEADS, HG_DV, HG_DK), F32)],
        scratch_shapes=[pltpu.VMEM((tb, HG_W), F32)] * 3 + [pltpu.VMEM((tb, HG_W), BF16),
                                                             pltpu.VMEM((HG_HEADS, HG_DV, HG_DK), F32)],
        compiler_params=_cparams("parallel", "arbitrary"),
        name="hgrn_prompt",
    )(x, g[0], w_bf[0], lb[0], gn[0])


def _hgrn_sample_body(q_ref, fz_ref, iv_ref, gz_ref, s_ref, lb_ref, gn_ref, *rest, steps, bb, layer, creates):
    o_ref, so_all = rest[-2:]
    so_ref = so_all.at[layer] if creates else so_all
    if creates:
        for other in range(so_all.shape[0]):
            if other != layer:
                so_all[other] = jnp.zeros(so_all.shape[1:], F32)
    n = bb * steps
    qf = _silu(q_ref[...])
    log_f, kf = _hgrn_forget(fz_ref[...], lb_ref[...])
    v = iv_ref[...]
    gate = _silu(gz_ref[...])
    row = lax.broadcasted_iota(jnp.int32, (n, n), 0)
    col = lax.broadcasted_iota(jnp.int32, (n, n), 1)
    same = row // steps == col // steps
    causal = same & (col <= row)
    b = _dot_exact(causal.astype(F32), log_f)
    total = _dot_exact(same.astype(F32), log_f)
    first = (same & (col % steps == 0)).astype(F32)
    e0 = b - _dot_exact(first, log_f)
    q0 = (qf * jnp.exp(e0)).astype(BF16)
    k0 = (kf * jnp.exp(-e0)).astype(BF16)
    q_in = (qf * jnp.exp(b)).astype(BF16)
    k_st = (kf * jnp.exp(total - b)).astype(BF16)
    vb = v.astype(BF16)
    pick = (lax.broadcasted_iota(jnp.int32, (bb, n), 1) == steps * lax.broadcasted_iota(jnp.int32, (bb, n), 0))
    e_seq = jnp.exp(_dot_exact(pick.astype(F32), total))
    e_rows = jnp.concatenate([e_seq[:, h * HG_DK:(h + 1) * HG_DK] for h in range(HG_HEADS)], axis=0)
    pad = HG_DK - e_rows.shape[0]
    if pad:
        e_rows = jnp.concatenate([e_rows, jnp.zeros((pad, HG_DK), F32)], axis=0)
    e_cols = e_rows.T
    for h in range(HG_HEADS):
        hs = slice(h * HG_DK, (h + 1) * HG_DK)
        a = jnp.where(causal, _dot_nt(q0[:, hs], k0[:, hs]), 0.0)
        o_intra = _dot(a.astype(BF16), vb[:, hs])
        outs = []
        for i in range(bb):
            rs = slice(i * steps, (i + 1) * steps)
            s_old = s_ref[i, h]
            outs.append(_dot(q_in[rs, hs], s_old.astype(BF16)))
            so_ref[i, h] = (s_old * e_cols[:, h * bb + i:h * bb + i + 1]
                            + _dot_tn(k_st[rs, hs], vb[rs, hs]))
        o = o_intra + jnp.concatenate(outs, axis=0)
        o_ref[:, hs] = _group_norm_gate(o, gn_ref[...], gate[:, hs])


def _hgrn_sample(proj4, states, layer, out_states, lb, gn, steps):
    n_layers, b = states.shape[0], states.shape[1]
    bb = SAMPLE_SEQ_BLOCK
    assert HG_HEADS * bb <= HG_DK
    creates = out_states is None
    rows = pl.BlockSpec((bb * steps, HG_W), lambda i: (i, 0))
    sshape = (bb, HG_HEADS, HG_DK, HG_DV)
    s_in = pl.BlockSpec((None,) + sshape, lambda i: (layer, i, 0, 0, 0))
    s_out = pl.BlockSpec((n_layers,) + sshape, lambda i: (0, i, 0, 0, 0)) if creates else s_in
    in_specs = [rows, rows, rows, rows, s_in, _layer_resident(*lb), _layer_resident(*gn)]
    args = [*proj4, states, lb[0], gn[0]]
    aliases = {}
    if not creates:
        in_specs.append(pl.BlockSpec(memory_space=pl.ANY))
        args.append(out_states)
        aliases = {len(args) - 1: 1}
    return pl.pallas_call(
        functools.partial(_hgrn_sample_body, steps=steps, bb=bb, layer=layer, creates=creates),
        grid=(b // bb,),
        in_specs=in_specs,
        out_specs=[rows, s_out],
        out_shape=[jax.ShapeDtypeStruct((b * steps, HG_W), F32), jax.ShapeDtypeStruct(states.shape, F32)],
        input_output_aliases=aliases,
        compiler_params=_cparams("parallel"),
        name="hgrn_sample",
    )(*args)


def _trunk(x, cache_k, cache_v, st_re, st_im, st_h, p):
    prompt = cache_k is None
    bt, l, _ = x.shape
    tm = min(TOKEN_BLOCK, bt * l)
    depth = p["norm_mix_pre"].shape[0]
    kvw = N_KV * HEAD_DIM
    x2 = x.reshape(bt * l, D_MODEL)
    k_out, v_out, re_out, im_out, h_out = [], [], [], [], []
    h_all = kv_all = None
    if not prompt:
        cache_k = cache_k.reshape(cache_k.shape[:3] + (kvw,))
        cache_v = cache_v.reshape(cache_v.shape[:3] + (kvw,))
    for layer in range(depth):
        idx = layer // 2
        g_pre = (p["norm_mix_pre"], layer)
        if layer % 2 == 0:
            u, q, k, v = _norm_proj(x2, g_pre, (p["w_in_even"], idx), (S5_WIDTH, SWA_WIDTH, kvw, kvw),
                                    min(PROJ_TOKEN_BLOCK, bt * l))
            prep = p["s5_prep"]
            d_skip, wglu = p["s5_d"], p["s5_w_glu"]
            q3, k3, v3 = q.reshape(bt, l, SWA_WIDTH), k.reshape(bt, l, kvw), v.reshape(bt, l, kvw)
            if prompt:
                a_out, h_fin = _s5_prompt(u.reshape(bt, l, S5_WIDTH), prep, idx, d_skip, wglu, min(TOKEN_BLOCK, l))
                h_fin = h_fin.reshape(bt, -1)
                b_out = _swa_prompt(q3, k3, v3, p["swa_sinks"][idx], p["t5_band"])
                kw, vw = k3[:, l - WINDOW:], v3[:, l - WINDOW:]
            else:
                u_t = u.reshape(bt, l, S5_WIDTH).transpose(1, 0, 2)
                a_t, h_fin = _s5_sample(u_t, _s5_state_in(st_re[idx], st_im[idx]), prep, idx, d_skip, wglu)
                a_out = a_t.transpose(1, 0, 2)
                b_out, kv_all = _swa_sample(q3, k3, v3, cache_k, cache_v, idx, kv_all, p["swa_sinks"][idx],
                                            p["t5_band"])
            h_re, h_im = _s5_state_out(h_fin)
            if prompt:
                k_out.append(kw.reshape(bt, WINDOW, N_KV, HEAD_DIM))
                v_out.append(vw.reshape(bt, WINDOW, N_KV, HEAD_DIM))
            re_out.append(h_re)
            im_out.append(h_im)
            mixer_outs = (a_out.reshape(bt * l, S5_WIDTH), b_out.reshape(bt * l, SWA_WIDTH))
            w_out = (p["w_out_even"], idx)
        else:
            lb, gn = (p["hgrn_lb"], idx), (p["hgrn_gnorm"], idx)
            if prompt:
                c_out, s_t = _hgrn_prompt(x2.reshape(bt, l, D_MODEL), g_pre, (p["w_in_odd"], idx), lb, gn,
                                          min(TOKEN_BLOCK, l))
                h_out.append(s_t.transpose(0, 1, 3, 2))
            else:
                proj4 = _norm_proj(x2, g_pre, (p["w_in_odd"], idx), (HG_W,) * 4, tm)
                c_out, h_all = _hgrn_sample(proj4, st_h, idx, h_all, lb, gn, l)
            mixer_outs = (c_out.reshape(bt * l, HG_W),)
            w_out = (p["w_out_odd"], idx)
        x2 = _post(x2, mixer_outs, w_out, (p["norm_mix_post"], layer), (p["norm_mlp_pre"], layer),
                   (p["norm_mlp_post"], layer), (p["w_up"], layer), (p["w_down"], layer), tm)
    if prompt:
        k_new, v_new, h_new = jnp.stack(k_out), jnp.stack(v_out), jnp.stack(h_out)
    else:
        k_new, v_new = (a.reshape(a.shape[:3] + (N_KV, HEAD_DIM)) for a in kv_all)
        h_new = h_all
    return x2.reshape(bt, l, D_MODEL), k_new, v_new, jnp.stack(re_out), jnp.stack(im_out), h_new


def kernel(x_prompt, x_sample, cache_swa_k, cache_swa_v, state_s5_re, state_s5_im, state_hgrn, t5_bias_table,
           norm_mix_pre, norm_mix_post, norm_mlp_pre, norm_mlp_post, w_in_even, w_out_even, s5_lambda_re,
           s5_lambda_im, s5_log_step, s5_b_re, s5_b_im, s5_c_re, s5_c_im, s5_d, s5_w_glu, swa_sinks, w_in_odd,
           w_out_odd, hgrn_lb_param, hgrn_gnorm, w_up, w_down):
    bf = lambda a: a.astype(BF16)
    q0, q1 = S5_WIDTH, S5_WIDTH + SWA_WIDTH
    w_in_even = jnp.concatenate([w_in_even[..., :q0], _pair_heads(w_in_even[..., q0:q1], 2), w_in_even[..., q1:]],
                                axis=-1)
    w_out_even = jnp.concatenate([w_out_even[:, :q0], _pair_heads(w_out_even[:, q0:], 1)], axis=1)
    rows = lambda a: a[:, None, :]
    params = dict(
        norm_mix_pre=rows(norm_mix_pre), norm_mix_post=rows(norm_mix_post), norm_mlp_pre=rows(norm_mlp_pre),
        norm_mlp_post=rows(norm_mlp_post), w_in_even=bf(w_in_even), w_out_even=bf(w_out_even), s5_d=rows(s5_d),
        s5_w_glu=bf(s5_w_glu), swa_sinks=swa_sinks, w_in_odd=bf(w_in_odd), w_out_odd=bf(w_out_odd),
        hgrn_gnorm=rows(hgrn_gnorm), w_up=bf(w_up), w_down=bf(w_down),
        t5_band=_t5_band(t5_bias_table),
        hgrn_lb=rows(_hgrn_lower_bounds(hgrn_lb_param)),
        s5_prep=_s5_prep(s5_lambda_re, s5_lambda_im, s5_log_step, s5_b_re, s5_b_im, s5_c_re, s5_c_im),
    )
    y_prompt, k_p, v_p, re_p, im_p, hg_p = _trunk(x_prompt, None, None, None, None, None, params)
    y_sample, k_s, v_s, re_s, im_s, hg_s = _trunk(x_sample, cache_swa_k, cache_swa_v, state_s5_re, state_s5_im,
                                                  state_hgrn, params)
    return (y_prompt, y_sample, k_p, v_p, k_s, v_s, re_p, im_p, re_s, im_s, hg_p, hg_s)
```

```python
import functools
import math

import numpy as np
import jax
import jax.numpy as jnp
from jax import lax
from jax.experimental import pallas as pl
from jax.experimental.pallas import tpu as pltpu

F32 = jnp.float32
BF16 = jnp.bfloat16

D_MODEL = 1024
S5_WIDTH = 512
S5_GROUPS = 32
S5_GROUP = 16
S5_STATE = 64
S5_HALF_STATES = 1024
S5_HALF_CH = 256
SWA_WIDTH = 512
HEAD_DIM = 64
N_HEADS = 8
N_KV = 2
KV_GROUP = N_HEADS // N_KV
WINDOW = 128
T5_BUCKETS = 32
T5_MAX_DIST = 128
HG_HEADS = 8
HG_DK = 128
HG_DV = 128
HG_W = HG_HEADS * HG_DK
D_FF = 4096
RMS_EPS = 1e-6
NEG_INF = -1e30
LOG_FLOOR = 1e-30

S5_CHUNK = 64
HG_CHUNK = 128
HG_BASE = 32
TOKEN_BLOCK = 512
PROJ_TOKEN_BLOCK = 1024
SWA_BLOCKS_PER_STEP = 8
MLP_FF_BLOCK = 1024
SAMPLE_SEQ_BLOCK = 8
SWA_SAMPLE_SEQ_BLOCK = 32
VMEM_LIMIT = 56 * 1024 * 1024
MXU_WIDTH = 256


def _cparams(*sem):
    return pltpu.CompilerParams(dimension_semantics=sem, vmem_limit_bytes=VMEM_LIMIT)


def _resident(shape):
    nd = len(shape)
    return pl.BlockSpec(shape, lambda *_: (0,) * nd, pipeline_mode=pl.Buffered(1))


def _layer_resident(stack, layer):
    nd = stack.ndim
    return pl.BlockSpec((None,) + stack.shape[1:], lambda *_: (layer,) + (0,) * (nd - 1),
                        pipeline_mode=pl.Buffered(1))


def _rms(x, w):
    return x * lax.rsqrt(jnp.mean(x * x, axis=-1, keepdims=True) + RMS_EPS) * w


def _sigmoid(x):
    return 1.0 / (1.0 + jnp.exp(-x))


def _silu(x):
    return x * _sigmoid(x)


def _gelu_tanh(x):
    return 0.5 * x * (1.0 + jnp.tanh(math.sqrt(2.0 / math.pi) * (x + 0.044715 * (x * x * x))))


def _dot(a, b):
    return jnp.dot(a, b, preferred_element_type=F32)


def _dot_nt(a, b):
    return lax.dot_general(a, b, (((1,), (1,)), ((), ())), preferred_element_type=F32)


def _dot_tn(a, b):
    return lax.dot_general(a, b, (((0,), (0,)), ((), ())), preferred_element_type=F32)


def _dot_exact(a, b):
    return jnp.dot(a, b, preferred_element_type=F32, precision=lax.Precision.HIGHEST)


def _norm_proj_body(x_ref, g_ref, w_ref, *o_refs, splits):
    xn = _rms(x_ref[...], g_ref[...]).astype(BF16)
    off = 0
    i = 0
    while i < len(splits):
        j = i + 1
        while sum(splits[i:j]) < MXU_WIDTH and j < len(splits):
            j += 1
        y = _dot(xn, w_ref[:, off:off + sum(splits[i:j])])
        sub = 0
        for o_ref, wd in zip(o_refs[i:j], splits[i:j]):
            o_ref[...] = y[:, sub:sub + wd]
            sub += wd
        off += sub
        i = j


def _norm_proj(x2d, g, w_bf, splits, tm):
    t, n = x2d.shape[0], w_bf[0].shape[-1]
    return pl.pallas_call(
        functools.partial(_norm_proj_body, splits=splits),
        grid=(t // tm,),
        in_specs=[pl.BlockSpec((tm, D_MODEL), lambda i: (i, 0)),
                  _layer_resident(*g), _layer_resident(*w_bf)],
        out_specs=[pl.BlockSpec((tm, wd), lambda i: (i, 0)) for wd in splits],
        out_shape=[jax.ShapeDtypeStruct((t, wd), F32) for wd in splits],
        compiler_params=_cparams("parallel"),
        name="norm_proj",
    )(x2d, g[0], w_bf[0])


def _post_body(*refs, widths):
    x_ref = refs[0]
    in_refs = refs[1:1 + len(widths)]
    wout_ref, gpost_ref, gpre_ref, gmpost_ref, wup_ref, wdn_ref, o_ref = refs[1 + len(widths):]
    mix = None
    off = 0
    for r, wd in zip(in_refs, widths):
        t = _dot(r[...].astype(BF16), wout_ref[off:off + wd, :])
        mix = t if mix is None else mix + t
        off += wd
    x1 = x_ref[...] + _rms(mix, gpost_ref[...])
    hn = _rms(x1, gpre_ref[...]).astype(BF16)
    acc = None
    for c in range(D_FF // MLP_FF_BLOCK):
        cs = slice(c * MLP_FF_BLOCK, (c + 1) * MLP_FF_BLOCK)
        hk = jnp.square(jnp.maximum(_dot(hn, wup_ref[:, cs]), 0.0)).astype(BF16)
        t = _dot(hk, wdn_ref[cs, :])
        acc = t if acc is None else acc + t
    o_ref[...] = x1 + _rms(acc, gmpost_ref[...])


def _post_stream_body(*refs, widths):
    x_ref = refs[0]
    in_refs = refs[1:1 + len(widths)]
    wout_ref, gpost_ref, gpre_ref, gmpost_ref, wup_ref, wdn_ref, o_ref, x1_scr, hn_scr, acc_scr = refs[1 + len(widths):]
    c = pl.program_id(0)

    @pl.when(c == 0)
    def _():
        mix = None
        off = 0
        for r, wd in zip(in_refs, widths):
            t = _dot(r[...].astype(BF16), wout_ref[off:off + wd, :])
            mix = t if mix is None else mix + t
            off += wd
        x1 = x_ref[...] + _rms(mix, gpost_ref[...])
        x1_scr[...] = x1
        hn_scr[...] = _rms(x1, gpre_ref[...]).astype(BF16)
        acc_scr[...] = jnp.zeros_like(acc_scr)

    hk = jnp.square(jnp.maximum(_dot(hn_scr[...], wup_ref[...]), 0.0)).astype(BF16)
    acc_scr[...] += _dot(hk, wdn_ref[...])

    @pl.when(c == pl.num_programs(0) - 1)
    def _():
        o_ref[...] = x1_scr[...] + _rms(acc_scr[...], gmpost_ref[...])


def _post(x2d, mixer_outs, wout_bf, gpost, gpre, gmpost, wup_bf, wdn_bf, tm):
    t = x2d.shape[0]
    widths = tuple(a.shape[1] for a in mixer_outs)
    params = [wout_bf, gpost, gpre, gmpost, wup_bf, wdn_bf]
    stacks = [a for a, _ in params]
    if t == tm:
        whole = lambda wd: pl.BlockSpec((t, wd), lambda c: (0, 0))
        slab = lambda pr, blk, idx: pl.BlockSpec((None,) + blk, lambda c: (pr[1],) + idx(c))
        vec = lambda pr: slab(pr, (1, D_MODEL), lambda c: (0, 0))
        return pl.pallas_call(
            functools.partial(_post_stream_body, widths=widths),
            grid=(D_FF // MLP_FF_BLOCK,),
            in_specs=[whole(D_MODEL)] + [whole(wd) for wd in widths] + [
                slab(wout_bf, (D_MODEL, D_MODEL), lambda c: (0, 0)), vec(gpost), vec(gpre), vec(gmpost),
                slab(wup_bf, (D_MODEL, MLP_FF_BLOCK), lambda c: (0, c)),
                slab(wdn_bf, (MLP_FF_BLOCK, D_MODEL), lambda c: (c, 0))],
            out_specs=whole(D_MODEL),
            out_shape=jax.ShapeDtypeStruct((t, D_MODEL), F32),
            scratch_shapes=[pltpu.VMEM((t, D_MODEL), F32), pltpu.VMEM((t, D_MODEL), BF16),
                            pltpu.VMEM((t, D_MODEL), F32)],
            compiler_params=_cparams("arbitrary"),
            name="post_mlp_stream",
        )(x2d, *mixer_outs, *stacks)
    row = lambda wd: pl.BlockSpec((tm, wd), lambda i: (i, 0))
    return pl.pallas_call(
        functools.partial(_post_body, widths=widths),
        grid=(t // tm,),
        in_specs=[row(D_MODEL)] + [row(wd) for wd in widths] + [_layer_resident(*pr) for pr in params],
        out_specs=row(D_MODEL),
        out_shape=jax.ShapeDtypeStruct((t, D_MODEL), F32),
        compiler_params=_cparams("parallel"),
        name="post_mlp",
    )(x2d, *mixer_outs, *stacks)


def _bucket_map():
    t = np.arange(WINDOW)[:, None]
    m = np.arange(2 * WINDOW)[None, :]
    n = np.maximum(WINDOW + t - m, 0)
    max_exact = T5_BUCKETS // 2
    large = max_exact + np.floor(np.log(np.maximum(n, max_exact) / max_exact)
                                 / math.log(T5_MAX_DIST / max_exact) * (T5_BUCKETS - max_exact)).astype(np.int64)
    large = np.minimum(large, T5_BUCKETS - 1)
    return np.where(n < max_exact, n, large).astype(np.int32)


def _bias_body(tab_ref, bm_ref, o_ref):
    bm = bm_ref[...]
    row = lax.broadcasted_iota(jnp.int32, bm.shape, 0)
    col = lax.broadcasted_iota(jnp.int32, bm.shape, 1)
    dist = WINDOW + row - col
    in_win = (dist >= 0) & (dist < WINDOW)
    for h in range(N_HEADS):
        acc = jnp.zeros(bm.shape, F32)
        for b in range(T5_BUCKETS):
            acc = jnp.where(bm == b, tab_ref[b, h], acc)
        o_ref[0, h] = jnp.where(in_win & (col >= WINDOW), acc, NEG_INF)
        o_ref[1, h] = jnp.where(in_win, acc, NEG_INF)


def _t5_band(table):
    return pl.pallas_call(
        _bias_body,
        in_specs=[pl.BlockSpec(memory_space=pltpu.SMEM),
                  pl.BlockSpec((WINDOW, 2 * WINDOW), lambda: (0, 0))],
        out_specs=pl.BlockSpec((2, N_HEADS, WINDOW, 2 * WINDOW), lambda: (0, 0, 0, 0)),
        out_shape=jax.ShapeDtypeStruct((2, N_HEADS, WINDOW, 2 * WINDOW), F32),
        name="t5_band",
    )(table, jnp.asarray(_bucket_map()))


def _pair_heads(a, axis):
    shp = a.shape
    a = a.reshape(shp[:axis] + (N_KV, KV_GROUP, HEAD_DIM) + shp[axis + 1:])
    a = jnp.swapaxes(a, axis, axis + 1)
    return a.reshape(shp)


def _swa_prompt_body(sink_ref, q_ref, kc_ref, kp_ref, vc_ref, vp_ref, band_ref, o_ref, *, nsub):
    tile = N_KV * HEAD_DIM
    lo = lax.broadcasted_iota(jnp.int32, (WINDOW, tile), 1) < HEAD_DIM
    first_band = jnp.minimum(pl.program_id(1), 1)
    for sb in range(nsub):
        rows = slice(sb * WINDOW, (sb + 1) * WINDOW)
        before = slice((sb - 1) * WINDOW, sb * WINDOW)
        k_prev = kp_ref[...] if sb == 0 else kc_ref[before, :]
        v_prev = vp_ref[...] if sb == 0 else vc_ref[before, :]
        kb = jnp.concatenate([k_prev, kc_ref[rows, :]], axis=0).astype(BF16)
        vb = jnp.concatenate([v_prev, vc_ref[rows, :]], axis=0).astype(BF16)
        v_ext = jnp.concatenate([vb, jnp.ones_like(vb)], axis=1)
        for t in range(KV_GROUP):
            qt = q_ref[rows, t * tile:(t + 1) * tile] * (HEAD_DIM ** -0.5)
            halves = []
            for kv in range(N_KV):
                h = kv * KV_GROUP + t
                qh = jnp.where(lo if kv == 0 else jnp.logical_not(lo), qt, 0.0).astype(BF16)
                s = _dot_nt(qh, kb) + (band_ref[first_band, h] if sb == 0 else band_ref[1, h])
                sink = sink_ref[h]
                m = jnp.maximum(jnp.max(s, axis=-1, keepdims=True), sink)
                oe = _dot(jnp.exp(s - m).astype(BF16), v_ext)
                halves.append(oe[:, 0:tile] / (oe[:, tile:tile + 1] + jnp.exp(sink - m)))
            o_ref[rows, t * tile:(t + 1) * tile] = jnp.where(lo, halves[0], halves[1])


def _swa_prompt(q, k, v, sinks, band):
    b, l, _ = q.shape
    kvw = N_KV * HEAD_DIM
    nsub = min(SWA_BLOCKS_PER_STEP, l // WINDOW)
    rows = nsub * WINDOW
    cur = lambda w: pl.BlockSpec((None, rows, w), lambda i, n: (i, n, 0))
    prev = lambda w: pl.BlockSpec((None, WINDOW, w), lambda i, n: (i, jnp.maximum(n * nsub - 1, 0), 0))
    return pl.pallas_call(
        functools.partial(_swa_prompt_body, nsub=nsub),
        grid=(b, l // rows),
        in_specs=[pl.BlockSpec(memory_space=pltpu.SMEM), cur(SWA_WIDTH), cur(kvw), prev(kvw), cur(kvw), prev(kvw),
                  _resident(band.shape)],
        out_specs=cur(SWA_WIDTH),
        out_shape=jax.ShapeDtypeStruct((b, l, SWA_WIDTH), F32),
        compiler_params=_cparams("parallel", "arbitrary"),
        name="swa_prompt",
    )(sinks, q, k, k, v, v, band)


def _swa_sample_body(q_ref, kn_ref, vn_ref, ck_ref, cv_ref, bias_ref, sink_ref, *rest, steps, layer, creates):
    o_ref, ko_all, vo_all = rest[-3:]
    ko_ref = ko_all.at[layer] if creates else ko_all
    vo_ref = vo_all.at[layer] if creates else vo_all
    if creates:
        for other in range(ko_all.shape[0]):
            if other != layer:
                ko_all[other] = jnp.zeros(ko_all.shape[1:], F32)
                vo_all[other] = jnp.zeros(vo_all.shape[1:], F32)
    tile = N_KV * HEAD_DIM
    ck, cv, kn, vn = ck_ref[...], cv_ref[...], kn_ref[...], vn_ref[...]
    ko_ref[:, 0:WINDOW - steps, :] = ck[:, steps:, :]
    ko_ref[:, WINDOW - steps:, :] = kn
    vo_ref[:, 0:WINDOW - steps, :] = cv[:, steps:, :]
    vo_ref[:, WINDOW - steps:, :] = vn
    lo = lax.broadcasted_iota(jnp.int32, (1, 1, tile), 2) < HEAD_DIM
    q = q_ref[...] * (HEAD_DIM ** -0.5)
    qs = []
    for kv in range(N_KV):
        for t in range(KV_GROUP):
            qs.append(jnp.where(lo if kv == 0 else jnp.logical_not(lo), q[:, :, t * tile:(t + 1) * tile], 0.0))
    qall = jnp.concatenate(qs, axis=1).astype(BF16)
    bias = bias_ref[...]
    sink = sink_ref[...][None]
    bdot = lambda a, b_, spec: jnp.einsum(spec, a, b_, preferred_element_type=F32)
    sc = bdot(qall, ck.astype(BF16), "bqd,bkd->bqk") + bias[None, :, 0:WINDOW]
    sn = bdot(qall, kn.astype(BF16), "bqd,bkd->bqk") + bias[None, :, WINDOW:WINDOW + steps]
    m = jnp.maximum(jnp.maximum(jnp.max(sc, axis=-1, keepdims=True), jnp.max(sn, axis=-1, keepdims=True)), sink)
    ec, en = jnp.exp(sc - m), jnp.exp(sn - m)
    den = jnp.sum(ec, axis=-1, keepdims=True) + jnp.sum(en, axis=-1, keepdims=True) + jnp.exp(sink - m)
    o = (bdot(ec.astype(BF16), cv.astype(BF16), "bqk,bkd->bqd")
         + bdot(en.astype(BF16), vn.astype(BF16), "bqk,bkd->bqd")) / den
    for t in range(KV_GROUP):
        o_ref[:, :, t * tile:(t + 1) * tile] = jnp.where(
            lo, o[:, t * steps:(t + 1) * steps, :], o[:, (KV_GROUP + t) * steps:(KV_GROUP + t + 1) * steps, :])


def _swa_sample(q, kn, vn, cache_k, cache_v, idx, new_caches, sinks, band):
    b, steps, _ = q.shape
    kvw = N_KV * HEAD_DIM
    bb = min(SWA_SAMPLE_SEQ_BLOCK, b)
    nrow = N_HEADS * steps
    creates = new_caches is None
    bias = band[1, :, 0:steps, :].reshape(nrow, 2 * WINDOW)
    sink_rows = jnp.repeat(sinks, steps).reshape(nrow, 1)
    blk = lambda r, w: pl.BlockSpec((bb, r, w), lambda i: (i, 0, 0))
    cache = pl.BlockSpec((None, bb, WINDOW, kvw), lambda i: (idx, i, 0, 0))
    cache_out = (pl.BlockSpec((cache_k.shape[0], bb, WINDOW, kvw), lambda i: (0, i, 0, 0)) if creates else cache)
    in_specs = [blk(steps, SWA_WIDTH), blk(steps, kvw), blk(steps, kvw), cache, cache,
                _resident((nrow, 2 * WINDOW)), _resident((nrow, 1))]
    args = [q, kn, vn, cache_k, cache_v, bias, sink_rows]
    aliases = {}
    if not creates:
        in_specs += [pl.BlockSpec(memory_space=pl.ANY)] * 2
        args += list(new_caches)
        aliases = {len(args) - 2: 1, len(args) - 1: 2}
    o, k_new, v_new = pl.pallas_call(
        functools.partial(_swa_sample_body, steps=steps, layer=idx, creates=creates),
        grid=(b // bb,),
        in_specs=in_specs,
        out_specs=[blk(steps, SWA_WIDTH), cache_out, cache_out],
        out_shape=[jax.ShapeDtypeStruct((b, steps, SWA_WIDTH), F32),
                   jax.ShapeDtypeStruct(cache_k.shape, F32), jax.ShapeDtypeStruct(cache_v.shape, F32)],
        input_output_aliases=aliases,
        compiler_params=_cparams("parallel"),
        name="swa_sample",
    )(*args)
    return o, (k_new, v_new)


def _s5_prep_body(lr_ref, li_ref, ls_ref, br_ref, bi_ref, cr_ref, ci_ref, bm_ref, cm_ref, pw_ref, pinv_ref):
    hs = S5_HALF_STATES
    lr, li = lr_ref[...], li_ref[...]
    dt = jnp.exp(ls_ref[...])
    mag = jnp.exp(lr * dt)
    ang = li * dt
    ar = mag * jnp.cos(ang)
    ai = mag * jnp.sin(ang)
    den = lr * lr + li * li
    nr = ar - 1.0
    kr = (nr * lr + ai * li) / den
    ki = (ai * lr - nr * li) / den
    hc = S5_HALF_CH
    own = (lax.broadcasted_iota(jnp.int32, (hc, hs), 0) // S5_GROUP
           == lax.broadcasted_iota(jnp.int32, (hc, hs), 1) // S5_STATE)
    blockdiag = lambda a: jnp.where(own, jnp.concatenate([a] * (hc // S5_GROUP), axis=0), 0.0)
    br, bi = br_ref[...], bi_ref[...]
    bm_ref[:, 0:hs] = blockdiag(kr * br - ki * bi).astype(BF16)
    bm_ref[:, hs:2 * hs] = blockdiag(kr * bi + ki * br).astype(BF16)
    cm_ref[0:hs, :] = blockdiag(cr_ref[...]).T.astype(BF16)
    cm_ref[hs:2 * hs, :] = blockdiag(-ci_ref[...]).T.astype(BF16)
    n2 = ar * ar + ai * ai
    ir, ii = ar / n2, -ai / n2
    pr, pi = jnp.ones_like(ar), jnp.zeros_like(ar)
    qr, qi = pr, pi
    for j in range(S5_CHUNK):
        pw_ref[0, j:j + 1, :] = pr
        pw_ref[1, j:j + 1, :] = pi
        pinv_ref[0, j:j + 1, :] = qr
        pinv_ref[1, j:j + 1, :] = qi
        pr, pi = pr * ar - pi * ai, pr * ai + pi * ar
        qr, qi = qr * ir - qi * ii, qr * ii + qi * ir


def _s5_prep(lam_re, lam_im, log_step, b_re, b_im, c_re, c_im):
    hs, hc = S5_HALF_STATES, S5_HALF_CH
    gh = S5_GROUPS // 2
    nh = 2 * lam_re.shape[0]
    vec = lambda a: a.reshape(nh, 1, hs)
    ls = jnp.broadcast_to(log_step[..., None], log_step.shape + (S5_STATE,))
    b_rows = lambda b: b.reshape(nh, gh, S5_STATE, S5_GROUP).transpose(0, 3, 1, 2).reshape(nh, S5_GROUP, hs)
    c_rows = lambda c: c.reshape(nh, gh, S5_GROUP, S5_STATE).transpose(0, 2, 1, 3).reshape(nh, S5_GROUP, hs)

    vspec = pl.BlockSpec((None, 1, hs), lambda h: (h, 0, 0))
    rspec = pl.BlockSpec((None, S5_GROUP, hs), lambda h: (h, 0, 0))
    pspec = pl.BlockSpec((None, 2, S5_CHUNK, hs), lambda h: (h, 0, 0, 0))
    return pl.pallas_call(
        _s5_prep_body,
        grid=(nh,),
        in_specs=[vspec, vspec, vspec, rspec, rspec, rspec, rspec],
        out_specs=[pl.BlockSpec((None, hc, 2 * hs), lambda h: (h, 0, 0)),
                   pl.BlockSpec((None, 2 * hs, hc), lambda h: (h, 0, 0)), pspec, pspec],
        out_shape=[jax.ShapeDtypeStruct((nh, hc, 2 * hs), BF16), jax.ShapeDtypeStruct((nh, 2 * hs, hc), BF16),
                   jax.ShapeDtypeStruct((nh, 2, S5_CHUNK, hs), F32), jax.ShapeDtypeStruct((nh, 2, S5_CHUNK, hs), F32)],
        compiler_params=_cparams("parallel"),
        name="s5_prep",
    )(vec(lam_re), vec(lam_im), vec(ls), b_rows(b_re), b_rows(b_im), c_rows(c_re), c_rows(c_im))


def _s5_glu(y, u, d_ref, wg_ref):
    g = _gelu_tanh(y + d_ref[...] * u)
    return g * _sigmoid(_dot(g.astype(BF16), wg_ref[...]))


def _s5_prompt_body(u_ref, bm_ref, cm_ref, pw_ref, pinv_ref, d_ref, wg_ref, o_ref, hfin_ref, bu_scr, hb_scr,
                    h_scr, *, tb):
    hs, hc, c0 = S5_HALF_STATES, S5_HALF_CH, S5_CHUNK

    @pl.when(pl.program_id(1) == 0)
    def _():
        h_scr[...] = jnp.zeros_like(h_scr)

    u = u_ref[...]
    ub = u.astype(BF16)
    tril = (lax.broadcasted_iota(jnp.int32, (c0, c0), 0) >= lax.broadcasted_iota(jnp.int32, (c0, c0), 1)).astype(BF16)
    ys = []
    nchunk = tb // c0
    last = c0 - 1
    for hf in range(2):
        re = slice(2 * hf * hs, (2 * hf + 1) * hs)
        im = slice((2 * hf + 1) * hs, (2 * hf + 2) * hs)
        both = slice(2 * hf * hs, (2 * hf + 2) * hs)
        bu_scr[:, both] = _dot(ub[:, hf * hc:(hf + 1) * hc], bm_ref[hf])
        for ci in range(nchunk):
            rows = slice(ci * c0, (ci + 1) * c0)
            bur, bui = bu_scr[rows, re], bu_scr[rows, im]
            qr, qi = pinv_ref[hf, 0], pinv_ref[hf, 1]
            bu_scr[rows, re] = _dot(tril, (qr * bur - qi * bui).astype(BF16))
            bu_scr[rows, im] = _dot(tril, (qr * bui + qi * bur).astype(BF16))
        ar, ai = pw_ref[hf, 0, 1:2, :], pw_ref[hf, 1, 1:2, :]
        er, ei = pw_ref[hf, 0, last:c0, :], pw_ref[hf, 1, last:c0, :]
        hr0, hi0 = h_scr[:, re], h_scr[:, im]
        carries = []
        for ci in range(nchunk):
            gr, gi = ar * hr0 - ai * hi0, ar * hi0 + ai * hr0
            carries.append((gr, gi))
            sr = bu_scr[ci * c0 + last:(ci + 1) * c0, re] + gr
            si = bu_scr[ci * c0 + last:(ci + 1) * c0, im] + gi
            hr0, hi0 = er * sr - ei * si, er * si + ei * sr
        h_scr[:, re] = hr0
        h_scr[:, im] = hi0
        for ci in range(nchunk):
            rows = slice(ci * c0, (ci + 1) * c0)
            sr = bu_scr[rows, re] + carries[ci][0]
            si = bu_scr[rows, im] + carries[ci][1]
            pr, pi = pw_ref[hf, 0], pw_ref[hf, 1]
            hb_scr[rows, re] = (pr * sr - pi * si).astype(BF16)
            hb_scr[rows, im] = (pr * si + pi * sr).astype(BF16)
        ys.append(_dot(hb_scr[:, both], cm_ref[hf]))
    o_ref[...] = _s5_glu(jnp.concatenate(ys, axis=1), u, d_ref, wg_ref)
    hfin_ref[...] = h_scr[...]


def _layer_halves(a, layer):
    return pl.BlockSpec((2,) + a.shape[1:], lambda *_: (layer,) + (0,) * (a.ndim - 1), pipeline_mode=pl.Buffered(1))


def _s5_prompt(u, prep, layer, d_skip, wglu_bf, tb):
    b, l, _ = u.shape
    bm, cm, pw, pinv = prep
    ns = 4 * S5_HALF_STATES
    return pl.pallas_call(
        functools.partial(_s5_prompt_body, tb=tb),
        grid=(b, l // tb),
        in_specs=[pl.BlockSpec((None, tb, S5_WIDTH), lambda i, j: (i, j, 0)),
                  *[_layer_halves(a, layer) for a in (bm, cm, pw, pinv)],
                  _layer_resident(d_skip, layer), _layer_resident(wglu_bf, layer)],
        out_specs=[pl.BlockSpec((None, tb, S5_WIDTH), lambda i, j: (i, j, 0)),
                   pl.BlockSpec((None, 1, ns), lambda i, j: (i, 0, 0))],
        out_shape=[jax.ShapeDtypeStruct((b, l, S5_WIDTH), F32), jax.ShapeDtypeStruct((b, 1, ns), F32)],
        scratch_shapes=[pltpu.VMEM((tb, ns), F32), pltpu.VMEM((tb, ns), BF16), pltpu.VMEM((1, ns), F32)],
        compiler_params=_cparams("parallel", "arbitrary"),
        name="s5_prompt",
    )(u, bm, cm, pw, pinv, d_skip, wglu_bf)


def _s5_sample_body(u_ref, re_ref, im_ref, bm_ref, cm_ref, pw_ref, d_ref, wg_ref, o_ref, re_out_ref, im_out_ref, *,
                    steps):
    hs, hc = S5_HALF_STATES, S5_HALF_CH
    h = [re_ref[:, 0:hs], im_ref[:, 0:hs], re_ref[:, hs:2 * hs], im_ref[:, hs:2 * hs]]
    for t in range(steps):
        u = u_ref[:, t, :]
        ub = u.astype(BF16)
        ys = []
        for hf in range(2):
            bu = _dot(ub[:, hf * hc:(hf + 1) * hc], bm_ref[hf])
            ar, ai = pw_ref[hf, 0, 1:2, :], pw_ref[hf, 1, 1:2, :]
            hr, hi = h[2 * hf], h[2 * hf + 1]
            h[2 * hf] = ar * hr - ai * hi + bu[:, 0:hs]
            h[2 * hf + 1] = ar * hi + ai * hr + bu[:, hs:2 * hs]
            ys.append(_dot(jnp.concatenate([h[2 * hf], h[2 * hf + 1]], axis=1).astype(BF16), cm_ref[hf]))
        o_ref[:, t, :] = _s5_glu(jnp.concatenate(ys, axis=1), u, d_ref, wg_ref)
    for hf in range(2):
        re_out_ref[:, hf * hs:(hf + 1) * hs] = h[2 * hf]
        im_out_ref[:, hf * hs:(hf + 1) * hs] = h[2 * hf + 1]


def _s5_sample(u, st_re, st_im, prep, layer, d_skip, wglu_bf):
    b, steps, _ = u.shape
    bm, cm, pw, _ = prep
    ns = 2 * S5_HALF_STATES
    full = lambda shape: pl.BlockSpec(shape, lambda i: (0,) * len(shape))
    halves = lambda a: pl.BlockSpec((2,) + a.shape[1:], lambda i: (layer,) + (0,) * (a.ndim - 1))
    state = pl.BlockSpec((None, b, ns), lambda i: (layer, 0, 0))
    return pl.pallas_call(
        functools.partial(_s5_sample_body, steps=steps),
        grid=(1,),
        in_specs=[full(u.shape), state, state, halves(bm), halves(cm), halves(pw),
                  _layer_resident(d_skip, layer), _layer_resident(wglu_bf, layer)],
        out_specs=[full(u.shape), full((b, ns)), full((b, ns))],
        out_shape=[jax.ShapeDtypeStruct(u.shape, F32), jax.ShapeDtypeStruct((b, ns), F32),
                   jax.ShapeDtypeStruct((b, ns), F32)],
        compiler_params=_cparams("arbitrary"),
        name="s5_sample",
    )(u, st_re, st_im, bm, cm, pw, d_skip, wglu_bf)


def _s5_state_out(h):
    b = h.shape[0]
    h4 = h.reshape(b, 2, 2, S5_HALF_STATES)
    return (h4[:, :, 0].reshape(b, S5_GROUPS, S5_STATE), h4[:, :, 1].reshape(b, S5_GROUPS, S5_STATE))


def _lb_body(p_ref, o_ref):
    p = p_ref[...]
    e = jnp.exp(p - jnp.max(p, axis=0, keepdims=True))
    sm = e / jnp.sum(e, axis=0, keepdims=True)
    acc = jnp.zeros_like(sm[0:1])
    for i in range(p.shape[0]):
        acc = acc + sm[i:i + 1]
        o_ref[i:i + 1, :] = acc - sm[0:1]


def _hgrn_lower_bounds(lb_param):
    n = lb_param.shape[0]
    return pl.pallas_call(
        _lb_body,
        in_specs=[pl.BlockSpec((n, HG_W), lambda: (0, 0))],
        out_specs=pl.BlockSpec((n, HG_W), lambda: (0, 0)),
        out_shape=jax.ShapeDtypeStruct((n, HG_W), F32),
        name="hgrn_lower_bounds",
    )(lb_param)


def _hgrn_forget(fz, lb):
    log_f = jnp.log(jnp.maximum(lb, LOG_FLOOR) + (1.0 - lb) * _sigmoid(fz))
    return log_f, (1.0 - lb) * _sigmoid(-fz)


def _group_norm_gate(o, gn, gate):
    return o * lax.rsqrt(jnp.mean(o * o, axis=-1, keepdims=True) + RMS_EPS) * gn * gate


def _block_rows(b, stride, offset):
    return [b[m * stride + offset:m * stride + offset + 1, :] for m in range(b.shape[0] // stride)]


def _spread_rows(rows, rowi, stride):
    out = rows[-1]
    for m in range(len(rows) - 2, -1, -1):
        out = jnp.where(rowi < (m + 1) * stride, rows[m], out)
    return out


def _cumsum_rows(tril_bf, x):
    hi = x.astype(BF16)
    r1 = x - hi.astype(F32)
    mid = r1.astype(BF16)
    lo = (r1 - mid.astype(F32)).astype(BF16)
    return _dot(tril_bf, hi) + _dot(tril_bf, mid) + _dot(tril_bf, lo)


def _hgrn_prompt_body(x_ref, g_ref, w_ref, lb_ref, gn_ref, o_ref, st_ref, q_scr, k_scr, b_scr, v_scr, s_scr, *, tb):
    c, w = HG_CHUNK, HG_W

    @pl.when(pl.program_id(1) == 0)
    def _():
        s_scr[...] = jnp.zeros_like(s_scr)

    row = lax.broadcasted_iota(jnp.int32, (c, c), 0)
    col = lax.broadcasted_iota(jnp.int32, (c, c), 1)
    tril = (row >= col).astype(BF16)
    xn = _rms(x_ref[...], g_ref[...]).astype(BF16)
    log_f, kf = _hgrn_forget(_dot(xn, w_ref[:, w:2 * w]), lb_ref[...])
    k_scr[...] = kf
    for ci in range(tb // c):
        b_scr[ci * c:(ci + 1) * c, :] = _cumsum_rows(tril, log_f[ci * c:(ci + 1) * c, :])
    q_scr[...] = _silu(_dot(xn, w_ref[:, 0:w]))
    v_scr[...] = _dot(xn, w_ref[:, 2 * w:3 * w]).astype(BF16)
    o_ref[...] = _silu(_dot(xn, w_ref[:, 3 * w:4 * w]))

    masks = [(row // HG_BASE == col // HG_BASE) & (col <= row)]
    strides = []
    s = 2 * HG_BASE
    while s <= c:
        strides.append(s)
        masks.append(row // s == col // s)
        s *= 2
    masks_all = [None if st == c else jnp.concatenate([mk] * HG_HEADS, axis=0)
                 for st, mk in zip([HG_BASE] + strides, masks)]
    cat = lambda parts: jnp.concatenate(parts, axis=0)

    def chunk(ci, carry):
        rows = pl.ds(pl.multiple_of(ci * c, c), c)
        q, k, vb = q_scr[rows, :], k_scr[rows, :], v_scr[rows, :]
        b = b_scr[rows, :]
        b_last = b[c - 1:c, :]
        q0, k0, q_in, k_st = [], [], [], []
        for m in range(c // HG_BASE):
            blk = slice(m * HG_BASE, (m + 1) * HG_BASE)
            mid = b[m * HG_BASE + HG_BASE // 2:m * HG_BASE + HG_BASE // 2 + 1, :]
            d = b[blk, :] - mid
            q0.append(q[blk, :] * jnp.exp(d))
            k0.append(k[blk, :] * jnp.exp(-d))
            q_in.append(q0[-1] * jnp.exp(mid))
            k_st.append(k0[-1] * jnp.exp(b_last - mid))
        qs, ks = [cat(q0).astype(BF16)], [cat(k0).astype(BF16)]
        q_in, k_st = cat(q_in).astype(BF16), cat(k_st).astype(BF16)
        for st in strides:
            ql, kl = [], []
            zeros = jnp.zeros((st // 2, w), BF16)
            for m in range(c // st):
                lower = slice(m * st, m * st + st // 2)
                upper = slice(m * st + st // 2, (m + 1) * st)
                mid = b[m * st + st // 2:m * st + st // 2 + 1, :]
                ql += [zeros, (q[upper, :] * jnp.exp(b[upper, :] - mid)).astype(BF16)]
                kl += [(k[lower, :] * jnp.exp(mid - b[lower, :])).astype(BF16), zeros]
            qs.append(cat(ql))
            ks.append(cat(kl))
        e_last = jnp.exp(b_last)
        heads = [slice(h * HG_DK, (h + 1) * HG_DK) for h in range(HG_HEADS)]
        a = None
        for qh, kh, mk in zip(qs, ks, masks_all):
            t = cat([_dot_nt(qh[:, hs], kh[:, hs]) for hs in heads])
            if mk is not None:
                t = jnp.where(mk, t, 0.0)
            a = t if a is None else a + t
        a = a.astype(BF16)
        states = [s_scr[h] for h in range(HG_HEADS)]
        o = jnp.concatenate([_dot(a[h * c:(h + 1) * c, :], vb[:, hs]) + _dot_nt(q_in[:, hs], states[h].astype(BF16))
                             for h, hs in enumerate(heads)], axis=0)
        gate = jnp.concatenate([o_ref[rows, hs] for hs in heads], axis=0)
        out = _group_norm_gate(o, gn_ref[...], gate)
        for h, hs in enumerate(heads):
            s_scr[h] = states[h] * e_last[:, hs] + _dot_tn(vb[:, hs], k_st[:, hs])
            o_ref[rows, hs] = out[h * c:(h + 1) * c, :]
        return carry

    lax.fori_loop(0, tb // c, chunk, 0, unroll=True)
    st_ref[...] = s_scr[...]


def _hgrn_prompt(x, g, w_bf, lb, gn, tb):
    b, l, _ = x.shape
    row = pl.BlockSpec((None, tb, D_MODEL), lambda i, j: (i, j, 0))
    return pl.pallas_call(
        functools.partial(_hgrn_prompt_body, tb=tb),
        grid=(b, l // tb),
        in_specs=[row, _layer_resident(*g), _layer_resident(*w_bf), _layer_resident(*lb), _layer_resident(*gn)],
        out_specs=[row, pl.BlockSpec((None, HG_HEADS, HG_DV, HG_DK), lambda i, j: (i, 0, 0, 0))],
        out_shape=[jax.ShapeDtypeStruct((b, l, HG_W), F32),
                   jax.ShapeDtypeStruct((b, HG_HEADS, HG_DV, HG_DK), F32)],
        scratch_shapes=[pltpu.VMEM((tb, HG_W), F32)] * 3 + [pltpu.VMEM((tb, HG_W), BF16),
                                                             pltpu.VMEM((HG_HEADS, HG_DV, HG_DK), F32)],
        compiler_params=_cparams("parallel", "arbitrary"),
        name="hgrn_prompt",
    )(x, g[0], w_bf[0], lb[0], gn[0])


def _hgrn_sample_body(q_ref, fz_ref, iv_ref, gz_ref, s_ref, lb_ref, gn_ref, *rest, steps, bb, layer, creates):
    o_ref, so_all = rest[-2:]
    so_ref = so_all.at[layer] if creates else so_all
    if creates:
        for other in range(so_all.shape[0]):
            if other != layer:
                so_all[other] = jnp.zeros(so_all.shape[1:], F32)
    n = bb * steps
    qf = _silu(q_ref[...])
    log_f, kf = _hgrn_forget(fz_ref[...], lb_ref[...])
    v = iv_ref[...]
    gate = _silu(gz_ref[...])
    row = lax.broadcasted_iota(jnp.int32, (n, n), 0)
    col = lax.broadcasted_iota(jnp.int32, (n, n), 1)
    same = row // steps == col // steps
    causal = same & (col <= row)
    b = _dot_exact(causal.astype(F32), log_f)
    total = _dot_exact(same.astype(F32), log_f)
    first = (same & (col % steps == 0)).astype(F32)
    e0 = b - _dot_exact(first, log_f)
    q0 = (qf * jnp.exp(e0)).astype(BF16)
    k0 = (kf * jnp.exp(-e0)).astype(BF16)
    q_in = (qf * jnp.exp(b)).astype(BF16)
    k_st = (kf * jnp.exp(total - b)).astype(BF16)
    vb = v.astype(BF16)
    pick = (lax.broadcasted_iota(jnp.int32, (bb, n), 1) == steps * lax.broadcasted_iota(jnp.int32, (bb, n), 0))
    e_seq = jnp.exp(_dot_exact(pick.astype(F32), total))
    e_rows = jnp.concatenate([e_seq[:, h * HG_DK:(h + 1) * HG_DK] for h in range(HG_HEADS)], axis=0)
    pad = HG_DK - e_rows.shape[0]
    if pad:
        e_rows = jnp.concatenate([e_rows, jnp.zeros((pad, HG_DK), F32)], axis=0)
    e_cols = e_rows.T
    for h in range(HG_HEADS):
        hs = slice(h * HG_DK, (h + 1) * HG_DK)
        a = jnp.where(causal, _dot_nt(q0[:, hs], k0[:, hs]), 0.0)
        o_intra = _dot(a.astype(BF16), vb[:, hs])
        outs = []
        for i in range(bb):
            rs = slice(i * steps, (i + 1) * steps)
            s_old = s_ref[i, h]
            outs.append(_dot(q_in[rs, hs], s_old.astype(BF16)))
            so_ref[i, h] = (s_old * e_cols[:, h * bb + i:h * bb + i + 1]
                            + _dot_tn(k_st[rs, hs], vb[rs, hs]))
        o = o_intra + jnp.concatenate(outs, axis=0)
        o_ref[:, hs] = _group_norm_gate(o, gn_ref[...], gate[:, hs])


def _hgrn_sample(proj4, states, layer, out_states, lb, gn, steps):
    n_layers, b = states.shape[0], states.shape[1]
    bb = SAMPLE_SEQ_BLOCK
    assert HG_HEADS * bb <= HG_DK
    creates = out_states is None
    rows = pl.BlockSpec((bb * steps, HG_W), lambda i: (i, 0))
    sshape = (bb, HG_HEADS, HG_DK, HG_DV)
    s_in = pl.BlockSpec((None,) + sshape, lambda i: (layer, i, 0, 0, 0))
    s_out = pl.BlockSpec((n_layers,) + sshape, lambda i: (0, i, 0, 0, 0)) if creates else s_in
    in_specs = [rows, rows, rows, rows, s_in, _layer_resident(*lb), _layer_resident(*gn)]
    args = [*proj4, states, lb[0], gn[0]]
    aliases = {}
    if not creates:
        in_specs.append(pl.BlockSpec(memory_space=pl.ANY))
        args.append(out_states)
        aliases = {len(args) - 1: 1}
    return pl.pallas_call(
        functools.partial(_hgrn_sample_body, steps=steps, bb=bb, layer=layer, creates=creates),
        grid=(b // bb,),
        in_specs=in_specs,
        out_specs=[rows, s_out],
        out_shape=[jax.ShapeDtypeStruct((b * steps, HG_W), F32), jax.ShapeDtypeStruct(states.shape, F32)],
        input_output_aliases=aliases,
        compiler_params=_cparams("parallel"),
        name="hgrn_sample",
    )(*args)


def _trunk(x, cache_k, cache_v, st_re, st_im, st_h, p):
    prompt = cache_k is None
    bt, l, _ = x.shape
    tm = min(TOKEN_BLOCK, bt * l)
    depth = p["norm_mix_pre"].shape[0]
    kvw = N_KV * HEAD_DIM
    x2 = x.reshape(bt * l, D_MODEL)
    k_out, v_out, re_out, im_out, h_out = [], [], [], [], []
    h_all = kv_all = None
    if not prompt:
        cache_k = cache_k.reshape(cache_k.shape[:3] + (kvw,))
        cache_v = cache_v.reshape(cache_v.shape[:3] + (kvw,))
        st_re, st_im = (a.reshape(a.shape[:2] + (S5_GROUPS * S5_STATE,)) for a in (st_re, st_im))
    for layer in range(depth):
        idx = layer // 2
        g_pre = (p["norm_mix_pre"], layer)
        if layer % 2 == 0:
            u, q, k, v = _norm_proj(x2, g_pre, (p["w_in_even"], idx), (S5_WIDTH, SWA_WIDTH, kvw, kvw),
                                    min(PROJ_TOKEN_BLOCK, bt * l))
            prep = p["s5_prep"]
            d_skip, wglu = p["s5_d"], p["s5_w_glu"]
            q3, k3, v3 = q.reshape(bt, l, SWA_WIDTH), k.reshape(bt, l, kvw), v.reshape(bt, l, kvw)
            if prompt:
                a_out, h_fin = _s5_prompt(u.reshape(bt, l, S5_WIDTH), prep, idx, d_skip, wglu, min(TOKEN_BLOCK, l))
                h_fin = h_fin.reshape(bt, -1)
                b_out = _swa_prompt(q3, k3, v3, p["swa_sinks"][idx], p["t5_band"])
                kw, vw = k3[:, l - WINDOW:], v3[:, l - WINDOW:]
            else:
                a_out, h_re, h_im = _s5_sample(u.reshape(bt, l, S5_WIDTH), st_re, st_im, prep, idx, d_skip, wglu)
                h_re, h_im = (a.reshape(bt, S5_GROUPS, S5_STATE) for a in (h_re, h_im))
                b_out, kv_all = _swa_sample(q3, k3, v3, cache_k, cache_v, idx, kv_all, p["swa_sinks"][idx],
                                            p["t5_band"])
            if prompt:
                h_re, h_im = _s5_state_out(h_fin)
                k_out.append(kw.reshape(bt, WINDOW, N_KV, HEAD_DIM))
                v_out.append(vw.reshape(bt, WINDOW, N_KV, HEAD_DIM))
            re_out.append(h_re)
            im_out.append(h_im)
            mixer_outs = (a_out.reshape(bt * l, S5_WIDTH), b_out.reshape(bt * l, SWA_WIDTH))
            w_out = (p["w_out_even"], idx)
        else:
            lb, gn = (p["hgrn_lb"], idx), (p["hgrn_gnorm"], idx)
            if prompt:
                c_out, s_t = _hgrn_prompt(x2.reshape(bt, l, D_MODEL), g_pre, (p["w_in_odd"], idx), lb, gn,
                                          min(TOKEN_BLOCK, l))
                h_out.append(s_t.transpose(0, 1, 3, 2))
            else:
                proj4 = _norm_proj(x2, g_pre, (p["w_in_odd"], idx), (HG_W,) * 4, tm)
                c_out, h_all = _hgrn_sample(proj4, st_h, idx, h_all, lb, gn, l)
            mixer_outs = (c_out.reshape(bt * l, HG_W),)
            w_out = (p["w_out_odd"], idx)
        x2 = _post(x2, mixer_outs, w_out, (p["norm_mix_post"], layer), (p["norm_mlp_pre"], layer),
                   (p["norm_mlp_post"], layer), (p["w_up"], layer), (p["w_down"], layer), tm)
    if prompt:
        k_new, v_new, h_new = jnp.stack(k_out), jnp.stack(v_out), jnp.stack(h_out)
    else:
        k_new, v_new = (a.reshape(a.shape[:3] + (N_KV, HEAD_DIM)) for a in kv_all)
        h_new = h_all
    return x2.reshape(bt, l, D_MODEL), k_new, v_new, jnp.stack(re_out), jnp.stack(im_out), h_new


def kernel(x_prompt, x_sample, cache_swa_k, cache_swa_v, state_s5_re, state_s5_im, state_hgrn, t5_bias_table,
           norm_mix_pre, norm_mix_post, norm_mlp_pre, norm_mlp_post, w_in_even, w_out_even, s5_lambda_re,
           s5_lambda_im, s5_log_step, s5_b_re, s5_b_im, s5_c_re, s5_c_im, s5_d, s5_w_glu, swa_sinks, w_in_odd,
           w_out_odd, hgrn_lb_param, hgrn_gnorm, w_up, w_down):
    bf = lambda a: a.astype(BF16)
    q0, q1 = S5_WIDTH, S5_WIDTH + SWA_WIDTH
    w_in_even = jnp.concatenate([w_in_even[..., :q0], _pair_heads(w_in_even[..., q0:q1], 2), w_in_even[..., q1:]],
                                axis=-1)
    w_out_even = jnp.concatenate([w_out_even[:, :q0], _pair_heads(w_out_even[:, q0:], 1)], axis=1)
    rows = lambda a: a[:, None, :]
    params = dict(
        norm_mix_pre=rows(norm_mix_pre), norm_mix_post=rows(norm_mix_post), norm_mlp_pre=rows(norm_mlp_pre),
        norm_mlp_post=rows(norm_mlp_post), w_in_even=bf(w_in_even), w_out_even=bf(w_out_even), s5_d=rows(s5_d),
        s5_w_glu=bf(s5_w_glu), swa_sinks=swa_sinks, w_in_odd=bf(w_in_odd), w_out_odd=bf(w_out_odd),
        hgrn_gnorm=rows(hgrn_gnorm), w_up=bf(w_up), w_down=bf(w_down),
        t5_band=_t5_band(t5_bias_table),
        hgrn_lb=rows(_hgrn_lower_bounds(hgrn_lb_param)),
        s5_prep=_s5_prep(s5_lambda_re, s5_lambda_im, s5_log_step, s5_b_re, s5_b_im, s5_c_re, s5_c_im),
    )
    y_prompt, k_p, v_p, re_p, im_p, hg_p = _trunk(x_prompt, None, None, None, None, None, params)
    y_sample, k_s, v_s, re_s, im_s, hg_s = _trunk(x_sample, cache_swa_k, cache_swa_v, state_s5_re, state_s5_im,
                                                  state_hgrn, params)
    return (y_prompt, y_sample, k_p, v_p, k_s, v_s, re_p, im_p, re_s, im_s, hg_p, hg_s)
```

```python
import functools
import math

import numpy as np
import jax
import jax.numpy as jnp
from jax import lax
from jax.experimental import pallas as pl
from jax.experimental.pallas import tpu as pltpu

F32 = jnp.float32
BF16 = jnp.bfloat16

D_MODEL = 1024
S5_WIDTH = 512
S5_GROUPS = 32
S5_GROUP = 16
S5_STATE = 64
S5_HALF_STATES = 1024
S5_HALF_CH = 256
SWA_WIDTH = 512
HEAD_DIM = 64
N_HEADS = 8
N_KV = 2
KV_GROUP = N_HEADS // N_KV
WINDOW = 128
T5_BUCKETS = 32
T5_MAX_DIST = 128
HG_HEADS = 8
HG_DK = 128
HG_DV = 128
HG_W = HG_HEADS * HG_DK
D_FF = 4096
RMS_EPS = 1e-6
NEG_INF = -1e30
LOG_FLOOR = 1e-30

S5_CHUNK = 64
HG_CHUNK = 128
HG_BASE = 32
TOKEN_BLOCK = 512
PROJ_TOKEN_BLOCK = 1024
SWA_BLOCKS_PER_STEP = 8
MLP_FF_BLOCK = 1024
SAMPLE_SEQ_BLOCK = 8
SWA_SAMPLE_SEQ_BLOCK = 32
VMEM_LIMIT = 56 * 1024 * 1024
MXU_WIDTH = 256


def _cparams(*sem):
    return pltpu.CompilerParams(dimension_semantics=sem, vmem_limit_bytes=VMEM_LIMIT)


def _resident(shape):
    nd = len(shape)
    return pl.BlockSpec(shape, lambda *_: (0,) * nd, pipeline_mode=pl.Buffered(1))


def _layer_resident(stack, layer):
    nd = stack.ndim
    return pl.BlockSpec((None,) + stack.shape[1:], lambda *_: (layer,) + (0,) * (nd - 1),
                        pipeline_mode=pl.Buffered(1))


def _rms(x, w):
    return x * lax.rsqrt(jnp.mean(x * x, axis=-1, keepdims=True) + RMS_EPS) * w


def _sigmoid(x):
    return 1.0 / (1.0 + jnp.exp(-x))


def _silu(x):
    return x * _sigmoid(x)


def _gelu_tanh(x):
    return 0.5 * x * (1.0 + jnp.tanh(math.sqrt(2.0 / math.pi) * (x + 0.044715 * (x * x * x))))


def _dot(a, b):
    return jnp.dot(a, b, preferred_element_type=F32)


def _dot_nt(a, b):
    return lax.dot_general(a, b, (((1,), (1,)), ((), ())), preferred_element_type=F32)


def _dot_tn(a, b):
    return lax.dot_general(a, b, (((0,), (0,)), ((), ())), preferred_element_type=F32)


def _dot_exact(a, b):
    return jnp.dot(a, b, preferred_element_type=F32, precision=lax.Precision.HIGHEST)


def _norm_proj_body(x_ref, g_ref, w_ref, *o_refs, splits):
    xn = _rms(x_ref[...], g_ref[...]).astype(BF16)
    off = 0
    i = 0
    while i < len(splits):
        j = i + 1
        while sum(splits[i:j]) < MXU_WIDTH and j < len(splits):
            j += 1
        y = _dot(xn, w_ref[:, off:off + sum(splits[i:j])])
        sub = 0
        for o_ref, wd in zip(o_refs[i:j], splits[i:j]):
            o_ref[...] = y[:, sub:sub + wd]
            sub += wd
        off += sub
        i = j


def _norm_proj(x2d, g, w_bf, splits, tm):
    t, n = x2d.shape[0], w_bf[0].shape[-1]
    return pl.pallas_call(
        functools.partial(_norm_proj_body, splits=splits),
        grid=(t // tm,),
        in_specs=[pl.BlockSpec((tm, D_MODEL), lambda i: (i, 0)),
                  _layer_resident(*g), _layer_resident(*w_bf)],
        out_specs=[pl.BlockSpec((tm, wd), lambda i: (i, 0)) for wd in splits],
        out_shape=[jax.ShapeDtypeStruct((t, wd), F32) for wd in splits],
        compiler_params=_cparams("parallel"),
        name="norm_proj",
    )(x2d, g[0], w_bf[0])


def _post_body(*refs, widths):
    x_ref = refs[0]
    in_refs = refs[1:1 + len(widths)]
    wout_ref, gpost_ref, gpre_ref, gmpost_ref, wup_ref, wdn_ref, o_ref = refs[1 + len(widths):]
    mix = None
    off = 0
    for r, wd in zip(in_refs, widths):
        t = _dot(r[...].astype(BF16), wout_ref[off:off + wd, :])
        mix = t if mix is None else mix + t
        off += wd
    x1 = x_ref[...] + _rms(mix, gpost_ref[...])
    hn = _rms(x1, gpre_ref[...]).astype(BF16)
    acc = None
    for c in range(D_FF // MLP_FF_BLOCK):
        cs = slice(c * MLP_FF_BLOCK, (c + 1) * MLP_FF_BLOCK)
        hk = jnp.square(jnp.maximum(_dot(hn, wup_ref[:, cs]), 0.0)).astype(BF16)
        t = _dot(hk, wdn_ref[cs, :])
        acc = t if acc is None else acc + t
    o_ref[...] = x1 + _rms(acc, gmpost_ref[...])


def _post_stream_body(*refs, widths):
    x_ref = refs[0]
    in_refs = refs[1:1 + len(widths)]
    wout_ref, gpost_ref, gpre_ref, gmpost_ref, wup_ref, wdn_ref, o_ref, x1_scr, hn_scr, acc_scr = refs[1 + len(widths):]
    c = pl.program_id(0)

    @pl.when(c == 0)
    def _():
        mix = None
        off = 0
        for r, wd in zip(in_refs, widths):
            t = _dot(r[...].astype(BF16), wout_ref[off:off + wd, :])
            mix = t if mix is None else mix + t
            off += wd
        x1 = x_ref[...] + _rms(mix, gpost_ref[...])
        x1_scr[...] = x1
        hn_scr[...] = _rms(x1, gpre_ref[...]).astype(BF16)
        acc_scr[...] = jnp.zeros_like(acc_scr)

    hk = jnp.square(jnp.maximum(_dot(hn_scr[...], wup_ref[...]), 0.0)).astype(BF16)
    acc_scr[...] += _dot(hk, wdn_ref[...])

    @pl.when(c == pl.num_programs(0) - 1)
    def _():
        o_ref[...] = x1_scr[...] + _rms(acc_scr[...], gmpost_ref[...])


def _post(x2d, mixer_outs, wout_bf, gpost, gpre, gmpost, wup_bf, wdn_bf, tm):
    t = x2d.shape[0]
    widths = tuple(a.shape[1] for a in mixer_outs)
    params = [wout_bf, gpost, gpre, gmpost, wup_bf, wdn_bf]
    stacks = [a for a, _ in params]
    if t == tm:
        whole = lambda wd: pl.BlockSpec((t, wd), lambda c: (0, 0))
        slab = lambda pr, blk, idx: pl.BlockSpec((None,) + blk, lambda c: (pr[1],) + idx(c))
        vec = lambda pr: slab(pr, (1, D_MODEL), lambda c: (0, 0))
        return pl.pallas_call(
            functools.partial(_post_stream_body, widths=widths),
            grid=(D_FF // MLP_FF_BLOCK,),
            in_specs=[whole(D_MODEL)] + [whole(wd) for wd in widths] + [
                slab(wout_bf, (D_MODEL, D_MODEL), lambda c: (0, 0)), vec(gpost), vec(gpre), vec(gmpost),
                slab(wup_bf, (D_MODEL, MLP_FF_BLOCK), lambda c: (0, c)),
                slab(wdn_bf, (MLP_FF_BLOCK, D_MODEL), lambda c: (c, 0))],
            out_specs=whole(D_MODEL),
            out_shape=jax.ShapeDtypeStruct((t, D_MODEL), F32),
            scratch_shapes=[pltpu.VMEM((t, D_MODEL), F32), pltpu.VMEM((t, D_MODEL), BF16),
                            pltpu.VMEM((t, D_MODEL), F32)],
            compiler_params=_cparams("arbitrary"),
            name="post_mlp_stream",
        )(x2d, *mixer_outs, *stacks)
    row = lambda wd: pl.BlockSpec((tm, wd), lambda i: (i, 0))
    return pl.pallas_call(
        functools.partial(_post_body, widths=widths),
        grid=(t // tm,),
        in_specs=[row(D_MODEL)] + [row(wd) for wd in widths] + [_layer_resident(*pr) for pr in params],
        out_specs=row(D_MODEL),
        out_shape=jax.ShapeDtypeStruct((t, D_MODEL), F32),
        compiler_params=_cparams("parallel"),
        name="post_mlp",
    )(x2d, *mixer_outs, *stacks)


def _bucket_map():
    t = np.arange(WINDOW)[:, None]
    m = np.arange(2 * WINDOW)[None, :]
    n = np.maximum(WINDOW + t - m, 0)
    max_exact = T5_BUCKETS // 2
    large = max_exact + np.floor(np.log(np.maximum(n, max_exact) / max_exact)
                                 / math.log(T5_MAX_DIST / max_exact) * (T5_BUCKETS - max_exact)).astype(np.int64)
    large = np.minimum(large, T5_BUCKETS - 1)
    return np.where(n < max_exact, n, large).astype(np.int32)


def _bias_body(tab_ref, bm_ref, o_ref):
    bm = bm_ref[...]
    row = lax.broadcasted_iota(jnp.int32, bm.shape, 0)
    col = lax.broadcasted_iota(jnp.int32, bm.shape, 1)
    dist = WINDOW + row - col
    in_win = (dist >= 0) & (dist < WINDOW)
    for h in range(N_HEADS):
        acc = jnp.zeros(bm.shape, F32)
        for b in range(T5_BUCKETS):
            acc = jnp.where(bm == b, tab_ref[b, h], acc)
        o_ref[0, h] = jnp.where(in_win & (col >= WINDOW), acc, NEG_INF)
        o_ref[1, h] = jnp.where(in_win, acc, NEG_INF)


def _t5_band(table):
    return pl.pallas_call(
        _bias_body,
        in_specs=[pl.BlockSpec(memory_space=pltpu.SMEM),
                  pl.BlockSpec((WINDOW, 2 * WINDOW), lambda: (0, 0))],
        out_specs=pl.BlockSpec((2, N_HEADS, WINDOW, 2 * WINDOW), lambda: (0, 0, 0, 0)),
        out_shape=jax.ShapeDtypeStruct((2, N_HEADS, WINDOW, 2 * WINDOW), F32),
        name="t5_band",
    )(table, jnp.asarray(_bucket_map()))


def _pair_heads(a, axis):
    shp = a.shape
    a = a.reshape(shp[:axis] + (N_KV, KV_GROUP, HEAD_DIM) + shp[axis + 1:])
    a = jnp.swapaxes(a, axis, axis + 1)
    return a.reshape(shp)


def _swa_prompt_body(sink_ref, q_ref, kc_ref, kp_ref, vc_ref, vp_ref, band_ref, o_ref, *, nsub):
    tile = N_KV * HEAD_DIM
    lo = lax.broadcasted_iota(jnp.int32, (WINDOW, tile), 1) < HEAD_DIM
    first_band = jnp.minimum(pl.program_id(1), 1)
    for sb in range(nsub):
        rows = slice(sb * WINDOW, (sb + 1) * WINDOW)
        before = slice((sb - 1) * WINDOW, sb * WINDOW)
        k_prev = kp_ref[...] if sb == 0 else kc_ref[before, :]
        v_prev = vp_ref[...] if sb == 0 else vc_ref[before, :]
        kb = jnp.concatenate([k_prev, kc_ref[rows, :]], axis=0).astype(BF16)
        vb = jnp.concatenate([v_prev, vc_ref[rows, :]], axis=0).astype(BF16)
        v_ext = jnp.concatenate([vb, jnp.ones_like(vb)], axis=1)
        for t in range(KV_GROUP):
            qt = q_ref[rows, t * tile:(t + 1) * tile] * (HEAD_DIM ** -0.5)
            halves = []
            for kv in range(N_KV):
                h = kv * KV_GROUP + t
                qh = jnp.where(lo if kv == 0 else jnp.logical_not(lo), qt, 0.0).astype(BF16)
                s = _dot_nt(qh, kb) + (band_ref[first_band, h] if sb == 0 else band_ref[1, h])
                sink = sink_ref[h]
                m = jnp.maximum(jnp.max(s, axis=-1, keepdims=True), sink)
                oe = _dot(jnp.exp(s - m).astype(BF16), v_ext)
                halves.append(oe[:, 0:tile] / (oe[:, tile:tile + 1] + jnp.exp(sink - m)))
            o_ref[rows, t * tile:(t + 1) * tile] = jnp.where(lo, halves[0], halves[1])


def _swa_prompt(q, k, v, sinks, band):
    b, l, _ = q.shape
    kvw = N_KV * HEAD_DIM
    nsub = min(SWA_BLOCKS_PER_STEP, l // WINDOW)
    rows = nsub * WINDOW
    cur = lambda w: pl.BlockSpec((None, rows, w), lambda i, n: (i, n, 0))
    prev = lambda w: pl.BlockSpec((None, WINDOW, w), lambda i, n: (i, jnp.maximum(n * nsub - 1, 0), 0))
    return pl.pallas_call(
        functools.partial(_swa_prompt_body, nsub=nsub),
        grid=(b, l // rows),
        in_specs=[pl.BlockSpec(memory_space=pltpu.SMEM), cur(SWA_WIDTH), cur(kvw), prev(kvw), cur(kvw), prev(kvw),
                  _resident(band.shape)],
        out_specs=cur(SWA_WIDTH),
        out_shape=jax.ShapeDtypeStruct((b, l, SWA_WIDTH), F32),
        compiler_params=_cparams("parallel", "arbitrary"),
        name="swa_prompt",
    )(sinks, q, k, k, v, v, band)


def _swa_sample_body(q_ref, kn_ref, vn_ref, ck_ref, cv_ref, bias_ref, sink_ref, *rest, steps, layer, creates):
    o_ref, ko_all, vo_all = rest[-3:]
    ko_ref = ko_all.at[layer] if creates else ko_all
    vo_ref = vo_all.at[layer] if creates else vo_all
    if creates:
        for other in range(ko_all.shape[0]):
            if other != layer:
                ko_all[other] = jnp.zeros(ko_all.shape[1:], F32)
                vo_all[other] = jnp.zeros(vo_all.shape[1:], F32)
    tile = N_KV * HEAD_DIM
    ck, cv, kn, vn = ck_ref[...], cv_ref[...], kn_ref[...], vn_ref[...]
    ko_ref[:, 0:WINDOW - steps, :] = ck[:, steps:, :]
    ko_ref[:, WINDOW - steps:, :] = kn
    vo_ref[:, 0:WINDOW - steps, :] = cv[:, steps:, :]
    vo_ref[:, WINDOW - steps:, :] = vn
    lo = lax.broadcasted_iota(jnp.int32, (1, 1, tile), 2) < HEAD_DIM
    q = q_ref[...] * (HEAD_DIM ** -0.5)
    qs = []
    for kv in range(N_KV):
        for t in range(KV_GROUP):
            qs.append(jnp.where(lo if kv == 0 else jnp.logical_not(lo), q[:, :, t * tile:(t + 1) * tile], 0.0))
    qall = jnp.concatenate(qs, axis=1).astype(BF16)
    bias = bias_ref[...]
    sink = sink_ref[...][None]
    bdot = lambda a, b_, spec: jnp.einsum(spec, a, b_, preferred_element_type=F32)
    sc = bdot(qall, ck.astype(BF16), "bqd,bkd->bqk") + bias[None, :, 0:WINDOW]
    sn = bdot(qall, kn.astype(BF16), "bqd,bkd->bqk") + bias[None, :, WINDOW:WINDOW + steps]
    m = jnp.maximum(jnp.maximum(jnp.max(sc, axis=-1, keepdims=True), jnp.max(sn, axis=-1, keepdims=True)), sink)
    ec, en = jnp.exp(sc - m), jnp.exp(sn - m)
    den = jnp.sum(ec, axis=-1, keepdims=True) + jnp.sum(en, axis=-1, keepdims=True) + jnp.exp(sink - m)
    o = (bdot(ec.astype(BF16), cv.astype(BF16), "bqk,bkd->bqd")
         + bdot(en.astype(BF16), vn.astype(BF16), "bqk,bkd->bqd")) / den
    for t in range(KV_GROUP):
        o_ref[:, :, t * tile:(t + 1) * tile] = jnp.where(
            lo, o[:, t * steps:(t + 1) * steps, :], o[:, (KV_GROUP + t) * steps:(KV_GROUP + t + 1) * steps, :])


def _swa_sample(q, kn, vn, cache_k, cache_v, idx, new_caches, sinks, band):
    b, steps, _ = q.shape
    kvw = N_KV * HEAD_DIM
    bb = min(SWA_SAMPLE_SEQ_BLOCK, b)
    nrow = N_HEADS * steps
    creates = new_caches is None
    bias = band[1, :, 0:steps, :].reshape(nrow, 2 * WINDOW)
    sink_rows = jnp.repeat(sinks, steps).reshape(nrow, 1)
    blk = lambda r, w: pl.BlockSpec((bb, r, w), lambda i: (i, 0, 0))
    cache = pl.BlockSpec((None, bb, WINDOW, kvw), lambda i: (idx, i, 0, 0))
    cache_out = (pl.BlockSpec((cache_k.shape[0], bb, WINDOW, kvw), lambda i: (0, i, 0, 0)) if creates else cache)
    in_specs = [blk(steps, SWA_WIDTH), blk(steps, kvw), blk(steps, kvw), cache, cache,
                _resident((nrow, 2 * WINDOW)), _resident((nrow, 1))]
    args = [q, kn, vn, cache_k, cache_v, bias, sink_rows]
    aliases = {}
    if not creates:
        in_specs += [pl.BlockSpec(memory_space=pl.ANY)] * 2
        args += list(new_caches)
        aliases = {len(args) - 2: 1, len(args) - 1: 2}
    o, k_new, v_new = pl.pallas_call(
        functools.partial(_swa_sample_body, steps=steps, layer=idx, creates=creates),
        grid=(b // bb,),
        in_specs=in_specs,
        out_specs=[blk(steps, SWA_WIDTH), cache_out, cache_out],
        out_shape=[jax.ShapeDtypeStruct((b, steps, SWA_WIDTH), F32),
                   jax.ShapeDtypeStruct(cache_k.shape, F32), jax.ShapeDtypeStruct(cache_v.shape, F32)],
        input_output_aliases=aliases,
        compiler_params=_cparams("parallel"),
        name="swa_sample",
    )(*args)
    return o, (k_new, v_new)


def _s5_prep_body(lr_ref, li_ref, ls_ref, br_ref, bi_ref, cr_ref, ci_ref, bm_ref, cm_ref, pw_ref, pinv_ref):
    hs = S5_HALF_STATES
    lr, li = lr_ref[...], li_ref[...]
    dt = jnp.exp(ls_ref[...])
    mag = jnp.exp(lr * dt)
    ang = li * dt
    ar = mag * jnp.cos(ang)
    ai = mag * jnp.sin(ang)
    den = lr * lr + li * li
    nr = ar - 1.0
    kr = (nr * lr + ai * li) / den
    ki = (ai * lr - nr * li) / den
    hc = S5_HALF_CH
    own = (lax.broadcasted_iota(jnp.int32, (hc, hs), 0) // S5_GROUP
           == lax.broadcasted_iota(jnp.int32, (hc, hs), 1) // S5_STATE)
    blockdiag = lambda a: jnp.where(own, jnp.concatenate([a] * (hc // S5_GROUP), axis=0), 0.0)
    br, bi = br_ref[...], bi_ref[...]
    bm_ref[:, 0:hs] = blockdiag(kr * br - ki * bi).astype(BF16)
    bm_ref[:, hs:2 * hs] = blockdiag(kr * bi + ki * br).astype(BF16)
    cm_ref[0:hs, :] = blockdiag(cr_ref[...]).T.astype(BF16)
    cm_ref[hs:2 * hs, :] = blockdiag(-ci_ref[...]).T.astype(BF16)
    n2 = ar * ar + ai * ai
    ir, ii = ar / n2, -ai / n2
    pr, pi = jnp.ones_like(ar), jnp.zeros_like(ar)
    qr, qi = pr, pi
    for j in range(S5_CHUNK):
        pw_ref[0, j:j + 1, :] = pr
        pw_ref[1, j:j + 1, :] = pi
        pinv_ref[0, j:j + 1, :] = qr
        pinv_ref[1, j:j + 1, :] = qi
        pr, pi = pr * ar - pi * ai, pr * ai + pi * ar
        qr, qi = qr * ir - qi * ii, qr * ii + qi * ir


def _s5_prep(lam_re, lam_im, log_step, b_re, b_im, c_re, c_im):
    hs, hc = S5_HALF_STATES, S5_HALF_CH
    gh = S5_GROUPS // 2
    nh = 2 * lam_re.shape[0]
    vec = lambda a: a.reshape(nh, 1, hs)
    ls = jnp.broadcast_to(log_step[..., None], log_step.shape + (S5_STATE,))
    b_rows = lambda b: b.reshape(nh, gh, S5_STATE, S5_GROUP).transpose(0, 3, 1, 2).reshape(nh, S5_GROUP, hs)
    c_rows = lambda c: c.reshape(nh, gh, S5_GROUP, S5_STATE).transpose(0, 2, 1, 3).reshape(nh, S5_GROUP, hs)

    vspec = pl.BlockSpec((None, 1, hs), lambda h: (h, 0, 0))
    rspec = pl.BlockSpec((None, S5_GROUP, hs), lambda h: (h, 0, 0))
    pspec = pl.BlockSpec((None, 2, S5_CHUNK, hs), lambda h: (h, 0, 0, 0))
    return pl.pallas_call(
        _s5_prep_body,
        grid=(nh,),
        in_specs=[vspec, vspec, vspec, rspec, rspec, rspec, rspec],
        out_specs=[pl.BlockSpec((None, hc, 2 * hs), lambda h: (h, 0, 0)),
                   pl.BlockSpec((None, 2 * hs, hc), lambda h: (h, 0, 0)), pspec, pspec],
        out_shape=[jax.ShapeDtypeStruct((nh, hc, 2 * hs), BF16), jax.ShapeDtypeStruct((nh, 2 * hs, hc), BF16),
                   jax.ShapeDtypeStruct((nh, 2, S5_CHUNK, hs), F32), jax.ShapeDtypeStruct((nh, 2, S5_CHUNK, hs), F32)],
        compiler_params=_cparams("parallel"),
        name="s5_prep",
    )(vec(lam_re), vec(lam_im), vec(ls), b_rows(b_re), b_rows(b_im), c_rows(c_re), c_rows(c_im))


def _s5_glu(y, u, d_ref, wg_ref):
    g = _gelu_tanh(y + d_ref[...] * u)
    return g * _sigmoid(_dot(g.astype(BF16), wg_ref[...]))


def _s5_prompt_body(u_ref, bm_ref, cm_ref, pw_ref, pinv_ref, d_ref, wg_ref, o_ref, hfin_ref, bu_scr, hb_scr,
                    h_scr, *, tb):
    hs, hc, c0 = S5_HALF_STATES, S5_HALF_CH, S5_CHUNK

    @pl.when(pl.program_id(1) == 0)
    def _():
        h_scr[...] = jnp.zeros_like(h_scr)

    u = u_ref[...]
    ub = u.astype(BF16)
    tril = (lax.broadcasted_iota(jnp.int32, (c0, c0), 0) >= lax.broadcasted_iota(jnp.int32, (c0, c0), 1)).astype(BF16)
    ys = []
    nchunk = tb // c0
    last = c0 - 1
    for hf in range(2):
        re = slice(2 * hf * hs, (2 * hf + 1) * hs)
        im = slice((2 * hf + 1) * hs, (2 * hf + 2) * hs)
        both = slice(2 * hf * hs, (2 * hf + 2) * hs)
        bu_scr[:, both] = _dot(ub[:, hf * hc:(hf + 1) * hc], bm_ref[hf])
        for ci in range(nchunk):
            rows = slice(ci * c0, (ci + 1) * c0)
            bur, bui = bu_scr[rows, re], bu_scr[rows, im]
            qr, qi = pinv_ref[hf, 0], pinv_ref[hf, 1]
            bu_scr[rows, re] = _dot(tril, (qr * bur - qi * bui).astype(BF16))
            bu_scr[rows, im] = _dot(tril, (qr * bui + qi * bur).astype(BF16))
        ar, ai = pw_ref[hf, 0, 1:2, :], pw_ref[hf, 1, 1:2, :]
        er, ei = pw_ref[hf, 0, last:c0, :], pw_ref[hf, 1, last:c0, :]
        hr0, hi0 = h_scr[:, re], h_scr[:, im]
        carries = []
        for ci in range(nchunk):
            gr, gi = ar * hr0 - ai * hi0, ar * hi0 + ai * hr0
            carries.append((gr, gi))
            sr = bu_scr[ci * c0 + last:(ci + 1) * c0, re] + gr
            si = bu_scr[ci * c0 + last:(ci + 1) * c0, im] + gi
            hr0, hi0 = er * sr - ei * si, er * si + ei * sr
        h_scr[:, re] = hr0
        h_scr[:, im] = hi0
        for ci in range(nchunk):
            rows = slice(ci * c0, (ci + 1) * c0)
            sr = bu_scr[rows, re] + carries[ci][0]
            si = bu_scr[rows, im] + carries[ci][1]
            pr, pi = pw_ref[hf, 0], pw_ref[hf, 1]
            hb_scr[rows, re] = (pr * sr - pi * si).astype(BF16)
            hb_scr[rows, im] = (pr * si + pi * sr).astype(BF16)
        ys.append(_dot(hb_scr[:, both], cm_ref[hf]))
    o_ref[...] = _s5_glu(jnp.concatenate(ys, axis=1), u, d_ref, wg_ref)
    hfin_ref[...] = h_scr[...]


def _layer_halves(a, layer):
    return pl.BlockSpec((2,) + a.shape[1:], lambda *_: (layer,) + (0,) * (a.ndim - 1), pipeline_mode=pl.Buffered(1))


def _s5_prompt(u, prep, layer, d_skip, wglu_bf, tb):
    b, l, _ = u.shape
    bm, cm, pw, pinv = prep
    ns = 4 * S5_HALF_STATES
    return pl.pallas_call(
        functools.partial(_s5_prompt_body, tb=tb),
        grid=(b, l // tb),
        in_specs=[pl.BlockSpec((None, tb, S5_WIDTH), lambda i, j: (i, j, 0)),
                  *[_layer_halves(a, layer) for a in (bm, cm, pw, pinv)],
                  _layer_resident(d_skip, layer), _layer_resident(wglu_bf, layer)],
        out_specs=[pl.BlockSpec((None, tb, S5_WIDTH), lambda i, j: (i, j, 0)),
                   pl.BlockSpec((None, 1, ns), lambda i, j: (i, 0, 0))],
        out_shape=[jax.ShapeDtypeStruct((b, l, S5_WIDTH), F32), jax.ShapeDtypeStruct((b, 1, ns), F32)],
        scratch_shapes=[pltpu.VMEM((tb, ns), F32), pltpu.VMEM((tb, ns), BF16), pltpu.VMEM((1, ns), F32)],
        compiler_params=_cparams("parallel", "arbitrary"),
        name="s5_prompt",
    )(u, bm, cm, pw, pinv, d_skip, wglu_bf)


def _s5_sample_body(u_ref, re_ref, im_ref, bm_ref, cm_ref, pw_ref, d_ref, wg_ref, o_ref, re_out_ref, im_out_ref, *,
                    steps):
    hs, hc = S5_HALF_STATES, S5_HALF_CH
    h = [re_ref[:, 0:hs], im_ref[:, 0:hs], re_ref[:, hs:2 * hs], im_ref[:, hs:2 * hs]]
    for t in range(steps):
        u = u_ref[:, t, :]
        ub = u.astype(BF16)
        ys = []
        for hf in range(2):
            bu = _dot(ub[:, hf * hc:(hf + 1) * hc], bm_ref[hf])
            ar, ai = pw_ref[hf, 0, 1:2, :], pw_ref[hf, 1, 1:2, :]
            hr, hi = h[2 * hf], h[2 * hf + 1]
            h[2 * hf] = ar * hr - ai * hi + bu[:, 0:hs]
            h[2 * hf + 1] = ar * hi + ai * hr + bu[:, hs:2 * hs]
            ys.append(_dot(jnp.concatenate([h[2 * hf], h[2 * hf + 1]], axis=1).astype(BF16), cm_ref[hf]))
        o_ref[:, t, :] = _s5_glu(jnp.concatenate(ys, axis=1), u, d_ref, wg_ref)
    for hf in range(2):
        re_out_ref[:, hf * hs:(hf + 1) * hs] = h[2 * hf]
        im_out_ref[:, hf * hs:(hf + 1) * hs] = h[2 * hf + 1]


def _s5_sample(u, st_re, st_im, prep, layer, d_skip, wglu_bf):
    b, steps, _ = u.shape
    bm, cm, pw, _ = prep
    ns = 2 * S5_HALF_STATES
    full = lambda shape: pl.BlockSpec(shape, lambda i: (0,) * len(shape))
    halves = lambda a: pl.BlockSpec((2,) + a.shape[1:], lambda i: (layer,) + (0,) * (a.ndim - 1))
    state = pl.BlockSpec((None, b, ns), lambda i: (layer, 0, 0))
    return pl.pallas_call(
        functools.partial(_s5_sample_body, steps=steps),
        grid=(1,),
        in_specs=[full(u.shape), state, state, halves(bm), halves(cm), halves(pw),
                  _layer_resident(d_skip, layer), _layer_resident(wglu_bf, layer)],
        out_specs=[full(u.shape), full((b, ns)), full((b, ns))],
        out_shape=[jax.ShapeDtypeStruct(u.shape, F32), jax.ShapeDtypeStruct((b, ns), F32),
                   jax.ShapeDtypeStruct((b, ns), F32)],
        compiler_params=_cparams("arbitrary"),
        name="s5_sample",
    )(u, st_re, st_im, bm, cm, pw, d_skip, wglu_bf)


def _s5_state_out(h):
    b = h.shape[0]
    h4 = h.reshape(b, 2, 2, S5_HALF_STATES)
    return (h4[:, :, 0].reshape(b, S5_GROUPS, S5_STATE), h4[:, :, 1].reshape(b, S5_GROUPS, S5_STATE))


def _lb_body(p_ref, o_ref):
    p = p_ref[...]
    e = jnp.exp(p - jnp.max(p, axis=0, keepdims=True))
    sm = e / jnp.sum(e, axis=0, keepdims=True)
    acc = jnp.zeros_like(sm[0:1])
    for i in range(p.shape[0]):
        acc = acc + sm[i:i + 1]
        o_ref[i:i + 1, :] = acc - sm[0:1]


def _hgrn_lower_bounds(lb_param):
    n = lb_param.shape[0]
    return pl.pallas_call(
        _lb_body,
        in_specs=[pl.BlockSpec((n, HG_W), lambda: (0, 0))],
        out_specs=pl.BlockSpec((n, HG_W), lambda: (0, 0)),
        out_shape=jax.ShapeDtypeStruct((n, HG_W), F32),
        name="hgrn_lower_bounds",
    )(lb_param)


def _hgrn_forget(fz, lb):
    log_f = jnp.log(jnp.maximum(lb, LOG_FLOOR) + (1.0 - lb) * _sigmoid(fz))
    return log_f, (1.0 - lb) * _sigmoid(-fz)


def _group_norm_gate(o, gn, gate):
    return o * lax.rsqrt(jnp.mean(o * o, axis=-1, keepdims=True) + RMS_EPS) * gn * gate


def _cumsum_rows(tril_bf, x):
    hi = x.astype(BF16)
    r1 = x - hi.astype(F32)
    mid = r1.astype(BF16)
    lo = (r1 - mid.astype(F32)).astype(BF16)
    return _dot(tril_bf, hi) + _dot(tril_bf, mid) + _dot(tril_bf, lo)


def _hgrn_prompt_body(x_ref, g_ref, w_ref, lb_ref, gn_ref, o_ref, st_ref, q_scr, k_scr, b_scr, v_scr, s_scr, *, tb):
    c, w = HG_CHUNK, HG_W

    @pl.when(pl.program_id(1) == 0)
    def _():
        s_scr[...] = jnp.zeros_like(s_scr)

    row = lax.broadcasted_iota(jnp.int32, (c, c), 0)
    col = lax.broadcasted_iota(jnp.int32, (c, c), 1)
    tril = (row >= col).astype(BF16)
    xn = _rms(x_ref[...], g_ref[...]).astype(BF16)
    log_f, kf = _hgrn_forget(_dot(xn, w_ref[:, w:2 * w]), lb_ref[...])
    k_scr[...] = kf
    for ci in range(tb // c):
        b_scr[ci * c:(ci + 1) * c, :] = _cumsum_rows(tril, log_f[ci * c:(ci + 1) * c, :])
    q_scr[...] = _silu(_dot(xn, w_ref[:, 0:w]))
    v_scr[...] = _dot(xn, w_ref[:, 2 * w:3 * w]).astype(BF16)
    o_ref[...] = _silu(_dot(xn, w_ref[:, 3 * w:4 * w]))

    masks = [(row // HG_BASE == col // HG_BASE) & (col <= row)]
    strides = []
    s = 2 * HG_BASE
    while s <= c:
        strides.append(s)
        masks.append(row // s == col // s)
        s *= 2
    masks_all = [None if st == c else jnp.concatenate([mk] * HG_HEADS, axis=0)
                 for st, mk in zip([HG_BASE] + strides, masks)]
    cat = lambda parts: jnp.concatenate(parts, axis=0)

    def chunk(ci, carry):
        rows = pl.ds(pl.multiple_of(ci * c, c), c)
        q, k, vb = q_scr[rows, :], k_scr[rows, :], v_scr[rows, :]
        b = b_scr[rows, :]
        b_last = b[c - 1:c, :]
        q0, k0, q_in, k_st = [], [], [], []
        for m in range(c // HG_BASE):
            blk = slice(m * HG_BASE, (m + 1) * HG_BASE)
            mid = b[m * HG_BASE + HG_BASE // 2:m * HG_BASE + HG_BASE // 2 + 1, :]
            d = b[blk, :] - mid
            q0.append(q[blk, :] * jnp.exp(d))
            k0.append(k[blk, :] * jnp.exp(-d))
            q_in.append(q0[-1] * jnp.exp(mid))
            k_st.append(k0[-1] * jnp.exp(b_last - mid))
        qs, ks = [cat(q0).astype(BF16)], [cat(k0).astype(BF16)]
        q_in, k_st = cat(q_in).astype(BF16), cat(k_st).astype(BF16)
        for st in strides:
            ql, kl = [], []
            zeros = jnp.zeros((st // 2, w), BF16)
            for m in range(c // st):
                lower = slice(m * st, m * st + st // 2)
                upper = slice(m * st + st // 2, (m + 1) * st)
                mid = b[m * st + st // 2:m * st + st // 2 + 1, :]
                ql += [zeros, (q[upper, :] * jnp.exp(b[upper, :] - mid)).astype(BF16)]
                kl += [(k[lower, :] * jnp.exp(mid - b[lower, :])).astype(BF16), zeros]
            qs.append(cat(ql))
            ks.append(cat(kl))
        e_last = jnp.exp(b_last)
        heads = [slice(h * HG_DK, (h + 1) * HG_DK) for h in range(HG_HEADS)]
        a = None
        for qh, kh, mk in zip(qs, ks, masks_all):
            t = cat([_dot_nt(qh[:, hs], kh[:, hs]) for hs in heads])
            if mk is not None:
                t = jnp.where(mk, t, 0.0)
            a = t if a is None else a + t
        a = a.astype(BF16)
        states = [s_scr[h] for h in range(HG_HEADS)]
        o = jnp.concatenate([_dot(a[h * c:(h + 1) * c, :], vb[:, hs]) + _dot_nt(q_in[:, hs], states[h].astype(BF16))
                             for h, hs in enumerate(heads)], axis=0)
        gate = jnp.concatenate([o_ref[rows, hs] for hs in heads], axis=0)
        out = _group_norm_gate(o, gn_ref[...], gate)
        for h, hs in enumerate(heads):
            s_scr[h] = states[h] * e_last[:, hs] + _dot_tn(vb[:, hs], k_st[:, hs])
            o_ref[rows, hs] = out[h * c:(h + 1) * c, :]
        return carry

    lax.fori_loop(0, tb // c, chunk, 0, unroll=True)
    st_ref[...] = s_scr[...]


def _hgrn_prompt(x, g, w_bf, lb, gn, tb):
    b, l, _ = x.shape
    row = pl.BlockSpec((None, tb, D_MODEL), lambda i, j: (i, j, 0))
    return pl.pallas_call(
        functools.partial(_hgrn_prompt_body, tb=tb),
        grid=(b, l // tb),
        in_specs=[row, _layer_resident(*g), _layer_resident(*w_bf), _layer_resident(*lb), _layer_resident(*gn)],
        out_specs=[row, pl.BlockSpec((None, HG_HEADS, HG_DV, HG_DK), lambda i, j: (i, 0, 0, 0))],
        out_shape=[jax.ShapeDtypeStruct((b, l, HG_W), F32),
                   jax.ShapeDtypeStruct((b, HG_HEADS, HG_DV, HG_DK), F32)],
        scratch_shapes=[pltpu.VMEM((tb, HG_W), F32)] * 3 + [pltpu.VMEM((tb, HG_W), BF16),
                                                             pltpu.VMEM((HG_HEADS, HG_DV, HG_DK), F32)],
        compiler_params=_cparams("parallel", "arbitrary"),
        name="hgrn_prompt",
    )(x, g[0], w_bf[0], lb[0], gn[0])


def _hgrn_sample_body(q_ref, fz_ref, iv_ref, gz_ref, s_ref, lb_ref, gn_ref, *rest, steps, bb, layer, creates):
    o_ref, so_all = rest[-2:]
    so_ref = so_all.at[layer] if creates else so_all
    if creates:
        for other in range(so_all.shape[0]):
            if other != layer:
                so_all[other] = jnp.zeros(so_all.shape[1:], F32)
    n = bb * steps
    qf = _silu(q_ref[...])
    log_f, kf = _hgrn_forget(fz_ref[...], lb_ref[...])
    v = iv_ref[...]
    gate = _silu(gz_ref[...])
    row = lax.broadcasted_iota(jnp.int32, (n, n), 0)
    col = lax.broadcasted_iota(jnp.int32, (n, n), 1)
    same = row // steps == col // steps
    causal = same & (col <= row)
    b = _dot_exact(causal.astype(F32), log_f)
    total = _dot_exact(same.astype(F32), log_f)
    first = (same & (col % steps == 0)).astype(F32)
    e0 = b - _dot_exact(first, log_f)
    q0 = (qf * jnp.exp(e0)).astype(BF16)
    k0 = (kf * jnp.exp(-e0)).astype(BF16)
    q_in = (qf * jnp.exp(b)).astype(BF16)
    k_st = (kf * jnp.exp(total - b)).astype(BF16)
    vb = v.astype(BF16)
    pick = (lax.broadcasted_iota(jnp.int32, (bb, n), 1) == steps * lax.broadcasted_iota(jnp.int32, (bb, n), 0))
    e_seq = jnp.exp(_dot_exact(pick.astype(F32), total))
    e_rows = jnp.concatenate([e_seq[:, h * HG_DK:(h + 1) * HG_DK] for h in range(HG_HEADS)], axis=0)
    pad = HG_DK - e_rows.shape[0]
    if pad:
        e_rows = jnp.concatenate([e_rows, jnp.zeros((pad, HG_DK), F32)], axis=0)
    e_cols = e_rows.T
    for h in range(HG_HEADS):
        hs = slice(h * HG_DK, (h + 1) * HG_DK)
        a = jnp.where(causal, _dot_nt(q0[:, hs], k0[:, hs]), 0.0)
        o_intra = _dot(a.astype(BF16), vb[:, hs])
        outs = []
        for i in range(bb):
            rs = slice(i * steps, (i + 1) * steps)
            s_old = s_ref[i, h]
            outs.append(_dot(q_in[rs, hs], s_old.astype(BF16)))
            so_ref[i, h] = (s_old * e_cols[:, h * bb + i:h * bb + i + 1]
                            + _dot_tn(k_st[rs, hs], vb[rs, hs]))
        o = o_intra + jnp.concatenate(outs, axis=0)
        o_ref[:, hs] = _group_norm_gate(o, gn_ref[...], gate[:, hs])


def _hgrn_sample(proj4, states, layer, out_states, lb, gn, steps):
    n_layers, b = states.shape[0], states.shape[1]
    bb = SAMPLE_SEQ_BLOCK
    assert HG_HEADS * bb <= HG_DK
    creates = out_states is None
    rows = pl.BlockSpec((bb * steps, HG_W), lambda i: (i, 0))
    sshape = (bb, HG_HEADS, HG_DK, HG_DV)
    s_in = pl.BlockSpec((None,) + sshape, lambda i: (layer, i, 0, 0, 0))
    s_out = pl.BlockSpec((n_layers,) + sshape, lambda i: (0, i, 0, 0, 0)) if creates else s_in
    in_specs = [rows, rows, rows, rows, s_in, _layer_resident(*lb), _layer_resident(*gn)]
    args = [*proj4, states, lb[0], gn[0]]
    aliases = {}
    if not creates:
        in_specs.append(pl.BlockSpec(memory_space=pl.ANY))
        args.append(out_states)
        aliases = {len(args) - 1: 1}
    return pl.pallas_call(
        functools.partial(_hgrn_sample_body, steps=steps, bb=bb, layer=layer, creates=creates),
        grid=(b // bb,),
        in_specs=in_specs,
        out_specs=[rows, s_out],
        out_shape=[jax.ShapeDtypeStruct((b * steps, HG_W), F32), jax.ShapeDtypeStruct(states.shape, F32)],
        input_output_aliases=aliases,
        compiler_params=_cparams("parallel"),
        name="hgrn_sample",
    )(*args)


def _trunk(x, cache_k, cache_v, st_re, st_im, st_h, p):
    prompt = cache_k is None
    bt, l, _ = x.shape
    tm = min(TOKEN_BLOCK, bt * l)
    depth = p["norm_mix_pre"].shape[0]
    kvw = N_KV * HEAD_DIM
    x2 = x.reshape(bt * l, D_MODEL)
    k_out, v_out, re_out, im_out, h_out = [], [], [], [], []
    h_all = kv_all = None
    if not prompt:
        cache_k = cache_k.reshape(cache_k.shape[:3] + (kvw,))
        cache_v = cache_v.reshape(cache_v.shape[:3] + (kvw,))
        st_re, st_im = (a.reshape(a.shape[:2] + (S5_GROUPS * S5_STATE,)) for a in (st_re, st_im))
    for layer in range(depth):
        idx = layer // 2
        g_pre = (p["norm_mix_pre"], layer)
        if layer % 2 == 0:
            u, q, k, v = _norm_proj(x2, g_pre, (p["w_in_even"], idx), (S5_WIDTH, SWA_WIDTH, kvw, kvw),
                                    min(PROJ_TOKEN_BLOCK, bt * l))
            prep = p["s5_prep"]
            d_skip, wglu = p["s5_d"], p["s5_w_glu"]
            q3, k3, v3 = q.reshape(bt, l, SWA_WIDTH), k.reshape(bt, l, kvw), v.reshape(bt, l, kvw)
            if prompt:
                a_out, h_fin = _s5_prompt(u.reshape(bt, l, S5_WIDTH), prep, idx, d_skip, wglu, min(TOKEN_BLOCK, l))
                h_fin = h_fin.reshape(bt, -1)
                b_out = _swa_prompt(q3, k3, v3, p["swa_sinks"][idx], p["t5_band"])
                kw, vw = k3[:, l - WINDOW:], v3[:, l - WINDOW:]
            else:
                a_out, h_re, h_im = _s5_sample(u.reshape(bt, l, S5_WIDTH), st_re, st_im, prep, idx, d_skip, wglu)
                h_re, h_im = (a.reshape(bt, S5_GROUPS, S5_STATE) for a in (h_re, h_im))
                b_out, kv_all = _swa_sample(q3, k3, v3, cache_k, cache_v, idx, kv_all, p["swa_sinks"][idx],
                                            p["t5_band"])
            if prompt:
                h_re, h_im = _s5_state_out(h_fin)
                k_out.append(kw.reshape(bt, WINDOW, N_KV, HEAD_DIM))
                v_out.append(vw.reshape(bt, WINDOW, N_KV, HEAD_DIM))
            re_out.append(h_re)
            im_out.append(h_im)
            mixer_outs = (a_out.reshape(bt * l, S5_WIDTH), b_out.reshape(bt * l, SWA_WIDTH))
            w_out = (p["w_out_even"], idx)
        else:
            lb, gn = (p["hgrn_lb"], idx), (p["hgrn_gnorm"], idx)
            if prompt:
                c_out, s_t = _hgrn_prompt(x2.reshape(bt, l, D_MODEL), g_pre, (p["w_in_odd"], idx), lb, gn,
                                          min(TOKEN_BLOCK, l))
                h_out.append(s_t.transpose(0, 1, 3, 2))
            else:
                proj4 = _norm_proj(x2, g_pre, (p["w_in_odd"], idx), (HG_W,) * 4, tm)
                c_out, h_all = _hgrn_sample(proj4, st_h, idx, h_all, lb, gn, l)
            mixer_outs = (c_out.reshape(bt * l, HG_W),)
            w_out = (p["w_out_odd"], idx)
        x2 = _post(x2, mixer_outs, w_out, (p["norm_mix_post"], layer), (p["norm_mlp_pre"], layer),
                   (p["norm_mlp_post"], layer), (p["w_up"], layer), (p["w_down"], layer), tm)
    if prompt:
        k_new, v_new, h_new = jnp.stack(k_out), jnp.stack(v_out), jnp.stack(h_out)
    else:
        k_new, v_new = (a.reshape(a.shape[:3] + (N_KV, HEAD_DIM)) for a in kv_all)
        h_new = h_all
    return x2.reshape(bt, l, D_MODEL), k_new, v_new, jnp.stack(re_out), jnp.stack(im_out), h_new


def kernel(x_prompt, x_sample, cache_swa_k, cache_swa_v, state_s5_re, state_s5_im, state_hgrn, t5_bias_table,
           norm_mix_pre, norm_mix_post, norm_mlp_pre, norm_mlp_post, w_in_even, w_out_even, s5_lambda_re,
           s5_lambda_im, s5_log_step, s5_b_re, s5_b_im, s5_c_re, s5_c_im, s5_d, s5_w_glu, swa_sinks, w_in_odd,
           w_out_odd, hgrn_lb_param, hgrn_gnorm, w_up, w_down):
    bf = lambda a: a.astype(BF16)
    q0, q1 = S5_WIDTH, S5_WIDTH + SWA_WIDTH
    w_in_even = jnp.concatenate([w_in_even[..., :q0], _pair_heads(w_in_even[..., q0:q1], 2), w_in_even[..., q1:]],
                                axis=-1)
    w_out_even = jnp.concatenate([w_out_even[:, :q0], _pair_heads(w_out_even[:, q0:], 1)], axis=1)
    rows = lambda a: a[:, None, :]
    params = dict(
        norm_mix_pre=rows(norm_mix_pre), norm_mix_post=rows(norm_mix_post), norm_mlp_pre=rows(norm_mlp_pre),
        norm_mlp_post=rows(norm_mlp_post), w_in_even=bf(w_in_even), w_out_even=bf(w_out_even), s5_d=rows(s5_d),
        s5_w_glu=bf(s5_w_glu), swa_sinks=swa_sinks, w_in_odd=bf(w_in_odd), w_out_odd=bf(w_out_odd),
        hgrn_gnorm=rows(hgrn_gnorm), w_up=bf(w_up), w_down=bf(w_down),
        t5_band=_t5_band(t5_bias_table),
        hgrn_lb=rows(_hgrn_lower_bounds(hgrn_lb_param)),
        s5_prep=_s5_prep(s5_lambda_re, s5_lambda_im, s5_log_step, s5_b_re, s5_b_im, s5_c_re, s5_c_im),
    )
    y_prompt, k_p, v_p, re_p, im_p, hg_p = _trunk(x_prompt, None, None, None, None, None, params)
    y_sample, k_s, v_s, re_s, im_s, hg_s = _trunk(x_sample, cache_swa_k, cache_swa_v, state_s5_re, state_s5_im,
                                                  state_hgrn, params)
    return (y_prompt, y_sample, k_p, v_p, k_s, v_s, re_p, im_p, re_s, im_s, hg_p, hg_s)
```

```python
import functools
import math

import numpy as np
import jax
import jax.numpy as jnp
from jax import lax
from jax.experimental import pallas as pl
from jax.experimental.pallas import tpu as pltpu

F32 = jnp.float32
BF16 = jnp.bfloat16

D_MODEL = 1024
S5_WIDTH = 512
S5_GROUPS = 32
S5_GROUP = 16
S5_STATE = 64
S5_HALF_STATES = 1024
S5_HALF_CH = 256
SWA_WIDTH = 512
HEAD_DIM = 64
N_HEADS = 8
N_KV = 2
KV_GROUP = N_HEADS // N_KV
WINDOW = 128
T5_BUCKETS = 32
T5_MAX_DIST = 128
HG_HEADS = 8
HG_DK = 128
HG_DV = 128
HG_W = HG_HEADS * HG_DK
D_FF = 4096
RMS_EPS = 1e-6
NEG_INF = -1e30
LOG_FLOOR = 1e-30

S5_CHUNK = 64
HG_CHUNK = 128
HG_BASE = 32
TOKEN_BLOCK = 512
PROJ_TOKEN_BLOCK = 1024
HGRN_TOKEN_BLOCK = 1024
SWA_BLOCKS_PER_STEP = 8
MLP_FF_BLOCK = 1024
SAMPLE_SEQ_BLOCK = 8
SWA_SAMPLE_SEQ_BLOCK = 32
VMEM_LIMIT = 56 * 1024 * 1024
MXU_WIDTH = 256


def _cparams(*sem):
    return pltpu.CompilerParams(dimension_semantics=sem, vmem_limit_bytes=VMEM_LIMIT)


def _resident(shape):
    nd = len(shape)
    return pl.BlockSpec(shape, lambda *_: (0,) * nd, pipeline_mode=pl.Buffered(1))


def _layer_resident(stack, layer):
    nd = stack.ndim
    return pl.BlockSpec((None,) + stack.shape[1:], lambda *_: (layer,) + (0,) * (nd - 1),
                        pipeline_mode=pl.Buffered(1))


def _rms(x, w):
    return x * lax.rsqrt(jnp.mean(x * x, axis=-1, keepdims=True) + RMS_EPS) * w


def _sigmoid(x):
    return 1.0 / (1.0 + jnp.exp(-x))


def _silu(x):
    return x * _sigmoid(x)


def _gelu_tanh(x):
    return 0.5 * x * (1.0 + jnp.tanh(math.sqrt(2.0 / math.pi) * (x + 0.044715 * (x * x * x))))


def _dot(a, b):
    return jnp.dot(a, b, preferred_element_type=F32)


def _dot_nt(a, b):
    return lax.dot_general(a, b, (((1,), (1,)), ((), ())), preferred_element_type=F32)


def _dot_tn(a, b):
    return lax.dot_general(a, b, (((0,), (0,)), ((), ())), preferred_element_type=F32)


def _dot_exact(a, b):
    return jnp.dot(a, b, preferred_element_type=F32, precision=lax.Precision.HIGHEST)


def _norm_proj_body(x_ref, g_ref, w_ref, *o_refs, splits):
    xn = _rms(x_ref[...], g_ref[...]).astype(BF16)
    off = 0
    i = 0
    while i < len(splits):
        j = i + 1
        while sum(splits[i:j]) < MXU_WIDTH and j < len(splits):
            j += 1
        y = _dot(xn, w_ref[:, off:off + sum(splits[i:j])])
        sub = 0
        for o_ref, wd in zip(o_refs[i:j], splits[i:j]):
            o_ref[...] = y[:, sub:sub + wd]
            sub += wd
        off += sub
        i = j


def _norm_proj(x2d, g, w_bf, splits, tm):
    t, n = x2d.shape[0], w_bf[0].shape[-1]
    return pl.pallas_call(
        functools.partial(_norm_proj_body, splits=splits),
        grid=(t // tm,),
        in_specs=[pl.BlockSpec((tm, D_MODEL), lambda i: (i, 0)),
                  _layer_resident(*g), _layer_resident(*w_bf)],
        out_specs=[pl.BlockSpec((tm, wd), lambda i: (i, 0)) for wd in splits],
        out_shape=[jax.ShapeDtypeStruct((t, wd), F32) for wd in splits],
        compiler_params=_cparams("parallel"),
        name="norm_proj",
    )(x2d, g[0], w_bf[0])


def _post_body(*refs, widths):
    x_ref = refs[0]
    in_refs = refs[1:1 + len(widths)]
    wout_ref, gpost_ref, gpre_ref, gmpost_ref, wup_ref, wdn_ref, o_ref = refs[1 + len(widths):]
    mix = None
    off = 0
    for r, wd in zip(in_refs, widths):
        t = _dot(r[...].astype(BF16), wout_ref[off:off + wd, :])
        mix = t if mix is None else mix + t
        off += wd
    x1 = x_ref[...] + _rms(mix, gpost_ref[...])
    hn = _rms(x1, gpre_ref[...]).astype(BF16)
    acc = None
    for c in range(D_FF // MLP_FF_BLOCK):
        cs = slice(c * MLP_FF_BLOCK, (c + 1) * MLP_FF_BLOCK)
        hk = jnp.square(jnp.maximum(_dot(hn, wup_ref[:, cs]), 0.0)).astype(BF16)
        t = _dot(hk, wdn_ref[cs, :])
        acc = t if acc is None else acc + t
    o_ref[...] = x1 + _rms(acc, gmpost_ref[...])


def _post_stream_body(*refs, widths):
    x_ref = refs[0]
    in_refs = refs[1:1 + len(widths)]
    wout_ref, gpost_ref, gpre_ref, gmpost_ref, wup_ref, wdn_ref, o_ref, x1_scr, hn_scr, acc_scr = refs[1 + len(widths):]
    c = pl.program_id(0)

    @pl.when(c == 0)
    def _():
        mix = None
        off = 0
        for r, wd in zip(in_refs, widths):
            t = _dot(r[...].astype(BF16), wout_ref[off:off + wd, :])
            mix = t if mix is None else mix + t
            off += wd
        x1 = x_ref[...] + _rms(mix, gpost_ref[...])
        x1_scr[...] = x1
        hn_scr[...] = _rms(x1, gpre_ref[...]).astype(BF16)
        acc_scr[...] = jnp.zeros_like(acc_scr)

    hk = jnp.square(jnp.maximum(_dot(hn_scr[...], wup_ref[...]), 0.0)).astype(BF16)
    acc_scr[...] += _dot(hk, wdn_ref[...])

    @pl.when(c == pl.num_programs(0) - 1)
    def _():
        o_ref[...] = x1_scr[...] + _rms(acc_scr[...], gmpost_ref[...])


def _post(x2d, mixer_outs, wout_bf, gpost, gpre, gmpost, wup_bf, wdn_bf, tm):
    t = x2d.shape[0]
    widths = tuple(a.shape[1] for a in mixer_outs)
    params = [wout_bf, gpost, gpre, gmpost, wup_bf, wdn_bf]
    stacks = [a for a, _ in params]
    if t == tm:
        whole = lambda wd: pl.BlockSpec((t, wd), lambda c: (0, 0))
        slab = lambda pr, blk, idx: pl.BlockSpec((None,) + blk, lambda c: (pr[1],) + idx(c))
        vec = lambda pr: slab(pr, (1, D_MODEL), lambda c: (0, 0))
        return pl.pallas_call(
            functools.partial(_post_stream_body, widths=widths),
            grid=(D_FF // MLP_FF_BLOCK,),
            in_specs=[whole(D_MODEL)] + [whole(wd) for wd in widths] + [
                slab(wout_bf, (D_MODEL, D_MODEL), lambda c: (0, 0)), vec(gpost), vec(gpre), vec(gmpost),
                slab(wup_bf, (D_MODEL, MLP_FF_BLOCK), lambda c: (0, c)),
                slab(wdn_bf, (MLP_FF_BLOCK, D_MODEL), lambda c: (c, 0))],
            out_specs=whole(D_MODEL),
            out_shape=jax.ShapeDtypeStruct((t, D_MODEL), F32),
            scratch_shapes=[pltpu.VMEM((t, D_MODEL), F32), pltpu.VMEM((t, D_MODEL), BF16),
                            pltpu.VMEM((t, D_MODEL), F32)],
            compiler_params=_cparams("arbitrary"),
            name="post_mlp_stream",
        )(x2d, *mixer_outs, *stacks)
    row = lambda wd: pl.BlockSpec((tm, wd), lambda i: (i, 0))
    return pl.pallas_call(
        functools.partial(_post_body, widths=widths),
        grid=(t // tm,),
        in_specs=[row(D_MODEL)] + [row(wd) for wd in widths] + [_layer_resident(*pr) for pr in params],
        out_specs=row(D_MODEL),
        out_shape=jax.ShapeDtypeStruct((t, D_MODEL), F32),
        compiler_params=_cparams("parallel"),
        name="post_mlp",
    )(x2d, *mixer_outs, *stacks)


def _bucket_map():
    t = np.arange(WINDOW)[:, None]
    m = np.arange(2 * WINDOW)[None, :]
    n = np.maximum(WINDOW + t - m, 0)
    max_exact = T5_BUCKETS // 2
    large = max_exact + np.floor(np.log(np.maximum(n, max_exact) / max_exact)
                                 / math.log(T5_MAX_DIST / max_exact) * (T5_BUCKETS - max_exact)).astype(np.int64)
    large = np.minimum(large, T5_BUCKETS - 1)
    return np.where(n < max_exact, n, large).astype(np.int32)


def _bias_body(tab_ref, bm_ref, o_ref):
    bm = bm_ref[...]
    row = lax.broadcasted_iota(jnp.int32, bm.shape, 0)
    col = lax.broadcasted_iota(jnp.int32, bm.shape, 1)
    dist = WINDOW + row - col
    in_win = (dist >= 0) & (dist < WINDOW)
    for h in range(N_HEADS):
        acc = jnp.zeros(bm.shape, F32)
        for b in range(T5_BUCKETS):
            acc = jnp.where(bm == b, tab_ref[b, h], acc)
        o_ref[0, h] = jnp.where(in_win & (col >= WINDOW), acc, NEG_INF)
        o_ref[1, h] = jnp.where(in_win, acc, NEG_INF)


def _t5_band(table):
    return pl.pallas_call(
        _bias_body,
        in_specs=[pl.BlockSpec(memory_space=pltpu.SMEM),
                  pl.BlockSpec((WINDOW, 2 * WINDOW), lambda: (0, 0))],
        out_specs=pl.BlockSpec((2, N_HEADS, WINDOW, 2 * WINDOW), lambda: (0, 0, 0, 0)),
        out_shape=jax.ShapeDtypeStruct((2, N_HEADS, WINDOW, 2 * WINDOW), F32),
        name="t5_band",
    )(table, jnp.asarray(_bucket_map()))


def _pair_heads(a, axis):
    shp = a.shape
    a = a.reshape(shp[:axis] + (N_KV, KV_GROUP, HEAD_DIM) + shp[axis + 1:])
    a = jnp.swapaxes(a, axis, axis + 1)
    return a.reshape(shp)


def _swa_prompt_body(sink_ref, q_ref, kc_ref, kp_ref, vc_ref, vp_ref, band_ref, o_ref, *, nsub):
    tile = N_KV * HEAD_DIM
    lo = lax.broadcasted_iota(jnp.int32, (WINDOW, tile), 1) < HEAD_DIM
    first_band = jnp.minimum(pl.program_id(1), 1)
    for sb in range(nsub):
        rows = slice(sb * WINDOW, (sb + 1) * WINDOW)
        before = slice((sb - 1) * WINDOW, sb * WINDOW)
        k_prev = kp_ref[...] if sb == 0 else kc_ref[before, :]
        v_prev = vp_ref[...] if sb == 0 else vc_ref[before, :]
        kb = jnp.concatenate([k_prev, kc_ref[rows, :]], axis=0).astype(BF16)
        vb = jnp.concatenate([v_prev, vc_ref[rows, :]], axis=0).astype(BF16)
        v_ext = jnp.concatenate([vb, jnp.ones_like(vb)], axis=1)
        for t in range(KV_GROUP):
            qt = q_ref[rows, t * tile:(t + 1) * tile] * (HEAD_DIM ** -0.5)
            halves = []
            for kv in range(N_KV):
                h = kv * KV_GROUP + t
                qh = jnp.where(lo if kv == 0 else jnp.logical_not(lo), qt, 0.0).astype(BF16)
                s = _dot_nt(qh, kb) + (band_ref[first_band, h] if sb == 0 else band_ref[1, h])
                sink = sink_ref[h]
                m = jnp.maximum(jnp.max(s, axis=-1, keepdims=True), sink)
                oe = _dot(jnp.exp(s - m).astype(BF16), v_ext)
                halves.append(oe[:, 0:tile] / (oe[:, tile:tile + 1] + jnp.exp(sink - m)))
            o_ref[rows, t * tile:(t + 1) * tile] = jnp.where(lo, halves[0], halves[1])


def _swa_prompt(q, k, v, sinks, band):
    b, l, _ = q.shape
    kvw = N_KV * HEAD_DIM
    nsub = min(SWA_BLOCKS_PER_STEP, l // WINDOW)
    rows = nsub * WINDOW
    cur = lambda w: pl.BlockSpec((None, rows, w), lambda i, n: (i, n, 0))
    prev = lambda w: pl.BlockSpec((None, WINDOW, w), lambda i, n: (i, jnp.maximum(n * nsub - 1, 0), 0))
    return pl.pallas_call(
        functools.partial(_swa_prompt_body, nsub=nsub),
        grid=(b, l // rows),
        in_specs=[pl.BlockSpec(memory_space=pltpu.SMEM), cur(SWA_WIDTH), cur(kvw), prev(kvw), cur(kvw), prev(kvw),
                  _resident(band.shape)],
        out_specs=cur(SWA_WIDTH),
        out_shape=jax.ShapeDtypeStruct((b, l, SWA_WIDTH), F32),
        compiler_params=_cparams("parallel", "arbitrary"),
        name="swa_prompt",
    )(sinks, q, k, k, v, v, band)


def _swa_sample_body(q_ref, kn_ref, vn_ref, ck_ref, cv_ref, bias_ref, sink_ref, *rest, steps, layer, creates):
    o_ref, ko_all, vo_all = rest[-3:]
    ko_ref = ko_all.at[layer] if creates else ko_all
    vo_ref = vo_all.at[layer] if creates else vo_all
    if creates:
        for other in range(ko_all.shape[0]):
            if other != layer:
                ko_all[other] = jnp.zeros(ko_all.shape[1:], F32)
                vo_all[other] = jnp.zeros(vo_all.shape[1:], F32)
    tile = N_KV * HEAD_DIM
    ck, cv, kn, vn = ck_ref[...], cv_ref[...], kn_ref[...], vn_ref[...]
    ko_ref[:, 0:WINDOW - steps, :] = ck[:, steps:, :]
    ko_ref[:, WINDOW - steps:, :] = kn
    vo_ref[:, 0:WINDOW - steps, :] = cv[:, steps:, :]
    vo_ref[:, WINDOW - steps:, :] = vn
    lo = lax.broadcasted_iota(jnp.int32, (1, 1, tile), 2) < HEAD_DIM
    q = q_ref[...] * (HEAD_DIM ** -0.5)
    qs = []
    for kv in range(N_KV):
        for t in range(KV_GROUP):
            qs.append(jnp.where(lo if kv == 0 else jnp.logical_not(lo), q[:, :, t * tile:(t + 1) * tile], 0.0))
    qall = jnp.concatenate(qs, axis=1).astype(BF16)
    bias = bias_ref[...]
    sink = sink_ref[...][None]
    bdot = lambda a, b_, spec: jnp.einsum(spec, a, b_, preferred_element_type=F32)
    sc = bdot(qall, ck.astype(BF16), "bqd,bkd->bqk") + bias[None, :, 0:WINDOW]
    sn = bdot(qall, kn.astype(BF16), "bqd,bkd->bqk") + bias[None, :, WINDOW:WINDOW + steps]
    m = jnp.maximum(jnp.maximum(jnp.max(sc, axis=-1, keepdims=True), jnp.max(sn, axis=-1, keepdims=True)), sink)
    ec, en = jnp.exp(sc - m), jnp.exp(sn - m)
    den = jnp.sum(ec, axis=-1, keepdims=True) + jnp.sum(en, axis=-1, keepdims=True) + jnp.exp(sink - m)
    o = (bdot(ec.astype(BF16), cv.astype(BF16), "bqk,bkd->bqd")
         + bdot(en.astype(BF16), vn.astype(BF16), "bqk,bkd->bqd")) / den
    for t in range(KV_GROUP):
        o_ref[:, :, t * tile:(t + 1) * tile] = jnp.where(
            lo, o[:, t * steps:(t + 1) * steps, :], o[:, (KV_GROUP + t) * steps:(KV_GROUP + t + 1) * steps, :])


def _swa_sample(q, kn, vn, cache_k, cache_v, idx, new_caches, sinks, band):
    b, steps, _ = q.shape
    kvw = N_KV * HEAD_DIM
    bb = min(SWA_SAMPLE_SEQ_BLOCK, b)
    nrow = N_HEADS * steps
    creates = new_caches is None
    bias = band[1, :, 0:steps, :].reshape(nrow, 2 * WINDOW)
    sink_rows = jnp.repeat(sinks, steps).reshape(nrow, 1)
    blk = lambda r, w: pl.BlockSpec((bb, r, w), lambda i: (i, 0, 0))
    cache = pl.BlockSpec((None, bb, WINDOW, kvw), lambda i: (idx, i, 0, 0))
    cache_out = (pl.BlockSpec((cache_k.shape[0], bb, WINDOW, kvw), lambda i: (0, i, 0, 0)) if creates else cache)
    in_specs = [blk(steps, SWA_WIDTH), blk(steps, kvw), blk(steps, kvw), cache, cache,
                _resident((nrow, 2 * WINDOW)), _resident((nrow, 1))]
    args = [q, kn, vn, cache_k, cache_v, bias, sink_rows]
    aliases = {}
    if not creates:
        in_specs += [pl.BlockSpec(memory_space=pl.ANY)] * 2
        args += list(new_caches)
        aliases = {len(args) - 2: 1, len(args) - 1: 2}
    o, k_new, v_new = pl.pallas_call(
        functools.partial(_swa_sample_body, steps=steps, layer=idx, creates=creates),
        grid=(b // bb,),
        in_specs=in_specs,
        out_specs=[blk(steps, SWA_WIDTH), cache_out, cache_out],
        out_shape=[jax.ShapeDtypeStruct((b, steps, SWA_WIDTH), F32),
                   jax.ShapeDtypeStruct(cache_k.shape, F32), jax.ShapeDtypeStruct(cache_v.shape, F32)],
        input_output_aliases=aliases,
        compiler_params=_cparams("parallel"),
        name="swa_sample",
    )(*args)
    return o, (k_new, v_new)


def _s5_prep_body(lr_ref, li_ref, ls_ref, br_ref, bi_ref, cr_ref, ci_ref, bm_ref, cm_ref, pw_ref, pinv_ref):
    hs = S5_HALF_STATES
    lr, li = lr_ref[...], li_ref[...]
    dt = jnp.exp(ls_ref[...])
    mag = jnp.exp(lr * dt)
    ang = li * dt
    ar = mag * jnp.cos(ang)
    ai = mag * jnp.sin(ang)
    den = lr * lr + li * li
    nr = ar - 1.0
    kr = (nr * lr + ai * li) / den
    ki = (ai * lr - nr * li) / den
    hc = S5_HALF_CH
    own = (lax.broadcasted_iota(jnp.int32, (hc, hs), 0) // S5_GROUP
           == lax.broadcasted_iota(jnp.int32, (hc, hs), 1) // S5_STATE)
    blockdiag = lambda a: jnp.where(own, jnp.concatenate([a] * (hc // S5_GROUP), axis=0), 0.0)
    br, bi = br_ref[...], bi_ref[...]
    bm_ref[:, 0:hs] = blockdiag(kr * br - ki * bi).astype(BF16)
    bm_ref[:, hs:2 * hs] = blockdiag(kr * bi + ki * br).astype(BF16)
    cm_ref[0:hs, :] = blockdiag(cr_ref[...]).T.astype(BF16)
    cm_ref[hs:2 * hs, :] = blockdiag(-ci_ref[...]).T.astype(BF16)
    n2 = ar * ar + ai * ai
    ir, ii = ar / n2, -ai / n2
    pr, pi = jnp.ones_like(ar), jnp.zeros_like(ar)
    qr, qi = pr, pi
    for j in range(S5_CHUNK):
        pw_ref[0, j:j + 1, :] = pr
        pw_ref[1, j:j + 1, :] = pi
        pinv_ref[0, j:j + 1, :] = qr
        pinv_ref[1, j:j + 1, :] = qi
        pr, pi = pr * ar - pi * ai, pr * ai + pi * ar
        qr, qi = qr * ir - qi * ii, qr * ii + qi * ir


def _s5_prep(lam_re, lam_im, log_step, b_re, b_im, c_re, c_im):
    hs, hc = S5_HALF_STATES, S5_HALF_CH
    gh = S5_GROUPS // 2
    nh = 2 * lam_re.shape[0]
    vec = lambda a: a.reshape(nh, 1, hs)
    ls = jnp.broadcast_to(log_step[..., None], log_step.shape + (S5_STATE,))
    b_rows = lambda b: b.reshape(nh, gh, S5_STATE, S5_GROUP).transpose(0, 3, 1, 2).reshape(nh, S5_GROUP, hs)
    c_rows = lambda c: c.reshape(nh, gh, S5_GROUP, S5_STATE).transpose(0, 2, 1, 3).reshape(nh, S5_GROUP, hs)

    vspec = pl.BlockSpec((None, 1, hs), lambda h: (h, 0, 0))
    rspec = pl.BlockSpec((None, S5_GROUP, hs), lambda h: (h, 0, 0))
    pspec = pl.BlockSpec((None, 2, S5_CHUNK, hs), lambda h: (h, 0, 0, 0))
    return pl.pallas_call(
        _s5_prep_body,
        grid=(nh,),
        in_specs=[vspec, vspec, vspec, rspec, rspec, rspec, rspec],
        out_specs=[pl.BlockSpec((None, hc, 2 * hs), lambda h: (h, 0, 0)),
                   pl.BlockSpec((None, 2 * hs, hc), lambda h: (h, 0, 0)), pspec, pspec],
        out_shape=[jax.ShapeDtypeStruct((nh, hc, 2 * hs), BF16), jax.ShapeDtypeStruct((nh, 2 * hs, hc), BF16),
                   jax.ShapeDtypeStruct((nh, 2, S5_CHUNK, hs), F32), jax.ShapeDtypeStruct((nh, 2, S5_CHUNK, hs), F32)],
        compiler_params=_cparams("parallel"),
        name="s5_prep",
    )(vec(lam_re), vec(lam_im), vec(ls), b_rows(b_re), b_rows(b_im), c_rows(c_re), c_rows(c_im))


def _s5_glu(y, u, d_ref, wg_ref):
    g = _gelu_tanh(y + d_ref[...] * u)
    return g * _sigmoid(_dot(g.astype(BF16), wg_ref[...]))


def _s5_prompt_body(u_ref, bm_ref, cm_ref, pw_ref, pinv_ref, d_ref, wg_ref, o_ref, hfin_ref, bu_scr, hb_scr,
                    h_scr, *, tb):
    hs, hc, c0 = S5_HALF_STATES, S5_HALF_CH, S5_CHUNK

    @pl.when(pl.program_id(1) == 0)
    def _():
        h_scr[...] = jnp.zeros_like(h_scr)

    u = u_ref[...]
    ub = u.astype(BF16)
    tril = (lax.broadcasted_iota(jnp.int32, (c0, c0), 0) >= lax.broadcasted_iota(jnp.int32, (c0, c0), 1)).astype(BF16)
    ys = []
    nchunk = tb // c0
    last = c0 - 1
    for hf in range(2):
        re = slice(2 * hf * hs, (2 * hf + 1) * hs)
        im = slice((2 * hf + 1) * hs, (2 * hf + 2) * hs)
        both = slice(2 * hf * hs, (2 * hf + 2) * hs)
        bu_scr[:, both] = _dot(ub[:, hf * hc:(hf + 1) * hc], bm_ref[hf])
        for ci in range(nchunk):
            rows = slice(ci * c0, (ci + 1) * c0)
            bur, bui = bu_scr[rows, re], bu_scr[rows, im]
            qr, qi = pinv_ref[hf, 0], pinv_ref[hf, 1]
            bu_scr[rows, re] = _dot(tril, (qr * bur - qi * bui).astype(BF16))
            bu_scr[rows, im] = _dot(tril, (qr * bui + qi * bur).astype(BF16))
        ar, ai = pw_ref[hf, 0, 1:2, :], pw_ref[hf, 1, 1:2, :]
        er, ei = pw_ref[hf, 0, last:c0, :], pw_ref[hf, 1, last:c0, :]
        hr0, hi0 = h_scr[:, re], h_scr[:, im]
        carries = []
        for ci in range(nchunk):
            gr, gi = ar * hr0 - ai * hi0, ar * hi0 + ai * hr0
            carries.append((gr, gi))
            sr = bu_scr[ci * c0 + last:(ci + 1) * c0, re] + gr
            si = bu_scr[ci * c0 + last:(ci + 1) * c0, im] + gi
            hr0, hi0 = er * sr - ei * si, er * si + ei * sr
        h_scr[:, re] = hr0
        h_scr[:, im] = hi0
        for ci in range(nchunk):
            rows = slice(ci * c0, (ci + 1) * c0)
            sr = bu_scr[rows, re] + carries[ci][0]
            si = bu_scr[rows, im] + carries[ci][1]
            pr, pi = pw_ref[hf, 0], pw_ref[hf, 1]
            hb_scr[rows, re] = (pr * sr - pi * si).astype(BF16)
            hb_scr[rows, im] = (pr * si + pi * sr).astype(BF16)
        ys.append(_dot(hb_scr[:, both], cm_ref[hf]))
    o_ref[...] = _s5_glu(jnp.concatenate(ys, axis=1), u, d_ref, wg_ref)
    hfin_ref[...] = h_scr[...]


def _layer_halves(a, layer):
    return pl.BlockSpec((2,) + a.shape[1:], lambda *_: (layer,) + (0,) * (a.ndim - 1), pipeline_mode=pl.Buffered(1))


def _s5_prompt(u, prep, layer, d_skip, wglu_bf, tb):
    b, l, _ = u.shape
    bm, cm, pw, pinv = prep
    ns = 4 * S5_HALF_STATES
    return pl.pallas_call(
        functools.partial(_s5_prompt_body, tb=tb),
        grid=(b, l // tb),
        in_specs=[pl.BlockSpec((None, tb, S5_WIDTH), lambda i, j: (i, j, 0)),
                  *[_layer_halves(a, layer) for a in (bm, cm, pw, pinv)],
                  _layer_resident(d_skip, layer), _layer_resident(wglu_bf, layer)],
        out_specs=[pl.BlockSpec((None, tb, S5_WIDTH), lambda i, j: (i, j, 0)),
                   pl.BlockSpec((None, 1, ns), lambda i, j: (i, 0, 0))],
        out_shape=[jax.ShapeDtypeStruct((b, l, S5_WIDTH), F32), jax.ShapeDtypeStruct((b, 1, ns), F32)],
        scratch_shapes=[pltpu.VMEM((tb, ns), F32), pltpu.VMEM((tb, ns), BF16), pltpu.VMEM((1, ns), F32)],
        compiler_params=_cparams("parallel", "arbitrary"),
        name="s5_prompt",
    )(u, bm, cm, pw, pinv, d_skip, wglu_bf)


def _s5_sample_body(u_ref, re_ref, im_ref, bm_ref, cm_ref, pw_ref, d_ref, wg_ref, o_ref, re_out_ref, im_out_ref, *,
                    steps):
    hs, hc = S5_HALF_STATES, S5_HALF_CH
    h = [re_ref[:, 0:hs], im_ref[:, 0:hs], re_ref[:, hs:2 * hs], im_ref[:, hs:2 * hs]]
    for t in range(steps):
        u = u_ref[:, t, :]
        ub = u.astype(BF16)
        ys = []
        for hf in range(2):
            bu = _dot(ub[:, hf * hc:(hf + 1) * hc], bm_ref[hf])
            ar, ai = pw_ref[hf, 0, 1:2, :], pw_ref[hf, 1, 1:2, :]
            hr, hi = h[2 * hf], h[2 * hf + 1]
            h[2 * hf] = ar * hr - ai * hi + bu[:, 0:hs]
            h[2 * hf + 1] = ar * hi + ai * hr + bu[:, hs:2 * hs]
            ys.append(_dot(jnp.concatenate([h[2 * hf], h[2 * hf + 1]], axis=1).astype(BF16), cm_ref[hf]))
        o_ref[:, t, :] = _s5_glu(jnp.concatenate(ys, axis=1), u, d_ref, wg_ref)
    for hf in range(2):
        re_out_ref[:, hf * hs:(hf + 1) * hs] = h[2 * hf]
        im_out_ref[:, hf * hs:(hf + 1) * hs] = h[2 * hf + 1]


def _s5_sample(u, st_re, st_im, prep, layer, d_skip, wglu_bf):
    b, steps, _ = u.shape
    bm, cm, pw, _ = prep
    ns = 2 * S5_HALF_STATES
    full = lambda shape: pl.BlockSpec(shape, lambda i: (0,) * len(shape))
    halves = lambda a: pl.BlockSpec((2,) + a.shape[1:], lambda i: (layer,) + (0,) * (a.ndim - 1))
    state = pl.BlockSpec((None, b, ns), lambda i: (layer, 0, 0))
    return pl.pallas_call(
        functools.partial(_s5_sample_body, steps=steps),
        grid=(1,),
        in_specs=[full(u.shape), state, state, halves(bm), halves(cm), halves(pw),
                  _layer_resident(d_skip, layer), _layer_resident(wglu_bf, layer)],
        out_specs=[full(u.shape), full((b, ns)), full((b, ns))],
        out_shape=[jax.ShapeDtypeStruct(u.shape, F32), jax.ShapeDtypeStruct((b, ns), F32),
                   jax.ShapeDtypeStruct((b, ns), F32)],
        compiler_params=_cparams("arbitrary"),
        name="s5_sample",
    )(u, st_re, st_im, bm, cm, pw, d_skip, wglu_bf)


def _s5_state_out(h):
    b = h.shape[0]
    h4 = h.reshape(b, 2, 2, S5_HALF_STATES)
    return (h4[:, :, 0].reshape(b, S5_GROUPS, S5_STATE), h4[:, :, 1].reshape(b, S5_GROUPS, S5_STATE))


def _lb_body(p_ref, o_ref):
    p = p_ref[...]
    e = jnp.exp(p - jnp.max(p, axis=0, keepdims=True))
    sm = e / jnp.sum(e, axis=0, keepdims=True)
    acc = jnp.zeros_like(sm[0:1])
    for i in range(p.shape[0]):
        acc = acc + sm[i:i + 1]
        o_ref[i:i + 1, :] = acc - sm[0:1]


def _hgrn_lower_bounds(lb_param):
    n = lb_param.shape[0]
    return pl.pallas_call(
        _lb_body,
        in_specs=[pl.BlockSpec((n, HG_W), lambda: (0, 0))],
        out_specs=pl.BlockSpec((n, HG_W), lambda: (0, 0)),
        out_shape=jax.ShapeDtypeStruct((n, HG_W), F32),
        name="hgrn_lower_bounds",
    )(lb_param)


def _hgrn_forget(fz, lb):
    log_f = jnp.log(jnp.maximum(lb, LOG_FLOOR) + (1.0 - lb) * _sigmoid(fz))
    return log_f, (1.0 - lb) * _sigmoid(-fz)


def _group_norm_gate(o, gn, gate):
    return o * lax.rsqrt(jnp.mean(o * o, axis=-1, keepdims=True) + RMS_EPS) * gn * gate


def _cumsum_rows(tril_bf, x):
    hi = x.astype(BF16)
    r1 = x - hi.astype(F32)
    mid = r1.astype(BF16)
    lo = (r1 - mid.astype(F32)).astype(BF16)
    return _dot(tril_bf, hi) + _dot(tril_bf, mid) + _dot(tril_bf, lo)


def _hgrn_prompt_body(x_ref, g_ref, w_ref, lb_ref, gn_ref, o_ref, st_ref, q_scr, k_scr, b_scr, v_scr, s_scr, *, tb):
    c, w = HG_CHUNK, HG_W

    @pl.when(pl.program_id(1) == 0)
    def _():
        s_scr[...] = jnp.zeros_like(s_scr)

    row = lax.broadcasted_iota(jnp.int32, (c, c), 0)
    col = lax.broadcasted_iota(jnp.int32, (c, c), 1)
    tril = (row >= col).astype(BF16)
    xn = _rms(x_ref[...], g_ref[...]).astype(BF16)
    log_f, kf = _hgrn_forget(_dot(xn, w_ref[:, w:2 * w]), lb_ref[...])
    k_scr[...] = kf
    for ci in range(tb // c):
        b_scr[ci * c:(ci + 1) * c, :] = _cumsum_rows(tril, log_f[ci * c:(ci + 1) * c, :])
    q_scr[...] = _silu(_dot(xn, w_ref[:, 0:w]))
    v_scr[...] = _dot(xn, w_ref[:, 2 * w:3 * w]).astype(BF16)
    o_ref[...] = _silu(_dot(xn, w_ref[:, 3 * w:4 * w]))

    masks = [(row // HG_BASE == col // HG_BASE) & (col <= row)]
    strides = []
    s = 2 * HG_BASE
    while s <= c:
        strides.append(s)
        masks.append(row // s == col // s)
        s *= 2
    masks_all = [None if st == c else jnp.concatenate([mk] * HG_HEADS, axis=0)
                 for st, mk in zip([HG_BASE] + strides, masks)]
    cat = lambda parts: jnp.concatenate(parts, axis=0)

    def chunk(ci, carry):
        rows = pl.ds(pl.multiple_of(ci * c, c), c)
        q, k, vb = q_scr[rows, :], k_scr[rows, :], v_scr[rows, :]
        b = b_scr[rows, :]
        b_last = b[c - 1:c, :]
        q0, k0, q_in, k_st = [], [], [], []
        for m in range(c // HG_BASE):
            blk = slice(m * HG_BASE, (m + 1) * HG_BASE)
            mid = b[m * HG_BASE + HG_BASE // 2:m * HG_BASE + HG_BASE // 2 + 1, :]
            d = b[blk, :] - mid
            q0.append(q[blk, :] * jnp.exp(d))
            k0.append(k[blk, :] * jnp.exp(-d))
            q_in.append(q0[-1] * jnp.exp(mid))
            k_st.append(k0[-1] * jnp.exp(b_last - mid))
        qs, ks = [cat(q0).astype(BF16)], [cat(k0).astype(BF16)]
        q_in, k_st = cat(q_in).astype(BF16), cat(k_st).astype(BF16)
        for st in strides:
            ql, kl = [], []
            zeros = jnp.zeros((st // 2, w), BF16)
            for m in range(c // st):
                lower = slice(m * st, m * st + st // 2)
                upper = slice(m * st + st // 2, (m + 1) * st)
                mid = b[m * st + st // 2:m * st + st // 2 + 1, :]
                ql += [zeros, (q[upper, :] * jnp.exp(b[upper, :] - mid)).astype(BF16)]
                kl += [(k[lower, :] * jnp.exp(mid - b[lower, :])).astype(BF16), zeros]
            qs.append(cat(ql))
            ks.append(cat(kl))
        e_last = jnp.exp(b_last)
        heads = [slice(h * HG_DK, (h + 1) * HG_DK) for h in range(HG_HEADS)]
        a = None
        for qh, kh, mk in zip(qs, ks, masks_all):
            t = cat([_dot_nt(qh[:, hs], kh[:, hs]) for hs in heads])
            if mk is not None:
                t = jnp.where(mk, t, 0.0)
            a = t if a is None else a + t
        a = a.astype(BF16)
        states = [s_scr[h] for h in range(HG_HEADS)]
        o = jnp.concatenate([_dot(a[h * c:(h + 1) * c, :], vb[:, hs]) + _dot_nt(q_in[:, hs], states[h].astype(BF16))
                             for h, hs in enumerate(heads)], axis=0)
        gate = jnp.concatenate([o_ref[rows, hs] for hs in heads], axis=0)
        out = _group_norm_gate(o, gn_ref[...], gate)
        for h, hs in enumerate(heads):
            s_scr[h] = states[h] * e_last[:, hs] + _dot_tn(vb[:, hs], k_st[:, hs])
            o_ref[rows, hs] = out[h * c:(h + 1) * c, :]
        return carry

    lax.fori_loop(0, tb // c, chunk, 0, unroll=True)
    st_ref[...] = s_scr[...]


def _hgrn_prompt(x, g, w_bf, lb, gn, tb):
    b, l, _ = x.shape
    row = pl.BlockSpec((None, tb, D_MODEL), lambda i, j: (i, j, 0))
    return pl.pallas_call(
        functools.partial(_hgrn_prompt_body, tb=tb),
        grid=(b, l // tb),
        in_specs=[row, _layer_resident(*g), _layer_resident(*w_bf), _layer_resident(*lb), _layer_resident(*gn)],
        out_specs=[row, pl.BlockSpec((None, HG_HEADS, HG_DV, HG_DK), lambda i, j: (i, 0, 0, 0))],
        out_shape=[jax.ShapeDtypeStruct((b, l, HG_W), F32),
                   jax.ShapeDtypeStruct((b, HG_HEADS, HG_DV, HG_DK), F32)],
        scratch_shapes=[pltpu.VMEM((tb, HG_W), F32)] * 3 + [pltpu.VMEM((tb, HG_W), BF16),
                                                             pltpu.VMEM((HG_HEADS, HG_DV, HG_DK), F32)],
        compiler_params=_cparams("parallel", "arbitrary"),
        name="hgrn_prompt",
    )(x, g[0], w_bf[0], lb[0], gn[0])


def _hgrn_sample_body(q_ref, fz_ref, iv_ref, gz_ref, s_ref, lb_ref, gn_ref, *rest, steps, bb, layer, creates):
    o_ref, so_all = rest[-2:]
    so_ref = so_all.at[layer] if creates else so_all
    if creates:
        for other in range(so_all.shape[0]):
            if other != layer:
                so_all[other] = jnp.zeros(so_all.shape[1:], F32)
    n = bb * steps
    qf = _silu(q_ref[...])
    log_f, kf = _hgrn_forget(fz_ref[...], lb_ref[...])
    v = iv_ref[...]
    gate = _silu(gz_ref[...])
    row = lax.broadcasted_iota(jnp.int32, (n, n), 0)
    col = lax.broadcasted_iota(jnp.int32, (n, n), 1)
    same = row // steps == col // steps
    causal = same & (col <= row)
    b = _dot_exact(causal.astype(F32), log_f)
    total = _dot_exact(same.astype(F32), log_f)
    first = (same & (col % steps == 0)).astype(F32)
    e0 = b - _dot_exact(first, log_f)
    q0 = (qf * jnp.exp(e0)).astype(BF16)
    k0 = (kf * jnp.exp(-e0)).astype(BF16)
    q_in = (qf * jnp.exp(b)).astype(BF16)
    k_st = (kf * jnp.exp(total - b)).astype(BF16)
    vb = v.astype(BF16)
    pick = (lax.broadcasted_iota(jnp.int32, (bb, n), 1) == steps * lax.broadcasted_iota(jnp.int32, (bb, n), 0))
    e_seq = jnp.exp(_dot_exact(pick.astype(F32), total))
    e_rows = jnp.concatenate([e_seq[:, h * HG_DK:(h + 1) * HG_DK] for h in range(HG_HEADS)], axis=0)
    pad = HG_DK - e_rows.shape[0]
    if pad:
        e_rows = jnp.concatenate([e_rows, jnp.zeros((pad, HG_DK), F32)], axis=0)
    e_cols = e_rows.T
    for h in range(HG_HEADS):
        hs = slice(h * HG_DK, (h + 1) * HG_DK)
        a = jnp.where(causal, _dot_nt(q0[:, hs], k0[:, hs]), 0.0)
        o_intra = _dot(a.astype(BF16), vb[:, hs])
        outs = []
        for i in range(bb):
            rs = slice(i * steps, (i + 1) * steps)
            s_old = s_ref[i, h]
            outs.append(_dot(q_in[rs, hs], s_old.astype(BF16)))
            so_ref[i, h] = (s_old * e_cols[:, h * bb + i:h * bb + i + 1]
                            + _dot_tn(k_st[rs, hs], vb[rs, hs]))
        o = o_intra + jnp.concatenate(outs, axis=0)
        o_ref[:, hs] = _group_norm_gate(o, gn_ref[...], gate[:, hs])


def _hgrn_sample(proj4, states, layer, out_states, lb, gn, steps):
    n_layers, b = states.shape[0], states.shape[1]
    bb = SAMPLE_SEQ_BLOCK
    assert HG_HEADS * bb <= HG_DK
    creates = out_states is None
    rows = pl.BlockSpec((bb * steps, HG_W), lambda i: (i, 0))
    sshape = (bb, HG_HEADS, HG_DK, HG_DV)
    s_in = pl.BlockSpec((None,) + sshape, lambda i: (layer, i, 0, 0, 0))
    s_out = pl.BlockSpec((n_layers,) + sshape, lambda i: (0, i, 0, 0, 0)) if creates else s_in
    in_specs = [rows, rows, rows, rows, s_in, _layer_resident(*lb), _layer_resident(*gn)]
    args = [*proj4, states, lb[0], gn[0]]
    aliases = {}
    if not creates:
        in_specs.append(pl.BlockSpec(memory_space=pl.ANY))
        args.append(out_states)
        aliases = {len(args) - 1: 1}
    return pl.pallas_call(
        functools.partial(_hgrn_sample_body, steps=steps, bb=bb, layer=layer, creates=creates),
        grid=(b // bb,),
        in_specs=in_specs,
        out_specs=[rows, s_out],
        out_shape=[jax.ShapeDtypeStruct((b * steps, HG_W), F32), jax.ShapeDtypeStruct(states.shape, F32)],
        input_output_aliases=aliases,
        compiler_params=_cparams("parallel"),
        name="hgrn_sample",
    )(*args)


def _trunk(x, cache_k, cache_v, st_re, st_im, st_h, p):
    prompt = cache_k is None
    bt, l, _ = x.shape
    tm = min(TOKEN_BLOCK, bt * l)
    depth = p["norm_mix_pre"].shape[0]
    kvw = N_KV * HEAD_DIM
    x2 = x.reshape(bt * l, D_MODEL)
    k_out, v_out, re_out, im_out, h_out = [], [], [], [], []
    h_all = kv_all = None
    if not prompt:
        cache_k = cache_k.reshape(cache_k.shape[:3] + (kvw,))
        cache_v = cache_v.reshape(cache_v.shape[:3] + (kvw,))
        st_re, st_im = (a.reshape(a.shape[:2] + (S5_GROUPS * S5_STATE,)) for a in (st_re, st_im))
    for layer in range(depth):
        idx = layer // 2
        g_pre = (p["norm_mix_pre"], layer)
        if layer % 2 == 0:
            u, q, k, v = _norm_proj(x2, g_pre, (p["w_in_even"], idx), (S5_WIDTH, SWA_WIDTH, kvw, kvw),
                                    min(PROJ_TOKEN_BLOCK, bt * l))
            prep = p["s5_prep"]
            d_skip, wglu = p["s5_d"], p["s5_w_glu"]
            q3, k3, v3 = q.reshape(bt, l, SWA_WIDTH), k.reshape(bt, l, kvw), v.reshape(bt, l, kvw)
            if prompt:
                a_out, h_fin = _s5_prompt(u.reshape(bt, l, S5_WIDTH), prep, idx, d_skip, wglu, min(TOKEN_BLOCK, l))
                h_fin = h_fin.reshape(bt, -1)
                b_out = _swa_prompt(q3, k3, v3, p["swa_sinks"][idx], p["t5_band"])
                kw, vw = k3[:, l - WINDOW:], v3[:, l - WINDOW:]
            else:
                a_out, h_re, h_im = _s5_sample(u.reshape(bt, l, S5_WIDTH), st_re, st_im, prep, idx, d_skip, wglu)
                h_re, h_im = (a.reshape(bt, S5_GROUPS, S5_STATE) for a in (h_re, h_im))
                b_out, kv_all = _swa_sample(q3, k3, v3, cache_k, cache_v, idx, kv_all, p["swa_sinks"][idx],
                                            p["t5_band"])
            if prompt:
                h_re, h_im = _s5_state_out(h_fin)
                k_out.append(kw.reshape(bt, WINDOW, N_KV, HEAD_DIM))
                v_out.append(vw.reshape(bt, WINDOW, N_KV, HEAD_DIM))
            re_out.append(h_re)
            im_out.append(h_im)
            mixer_outs = (a_out.reshape(bt * l, S5_WIDTH), b_out.reshape(bt * l, SWA_WIDTH))
            w_out = (p["w_out_even"], idx)
        else:
            lb, gn = (p["hgrn_lb"], idx), (p["hgrn_gnorm"], idx)
            if prompt:
                c_out, s_t = _hgrn_prompt(x2.reshape(bt, l, D_MODEL), g_pre, (p["w_in_odd"], idx), lb, gn,
                                          min(HGRN_TOKEN_BLOCK, l))
                h_out.append(s_t.transpose(0, 1, 3, 2))
            else:
                proj4 = _norm_proj(x2, g_pre, (p["w_in_odd"], idx), (HG_W,) * 4, tm)
                c_out, h_all = _hgrn_sample(proj4, st_h, idx, h_all, lb, gn, l)
            mixer_outs = (c_out.reshape(bt * l, HG_W),)
            w_out = (p["w_out_odd"], idx)
        x2 = _post(x2, mixer_outs, w_out, (p["norm_mix_post"], layer), (p["norm_mlp_pre"], layer),
                   (p["norm_mlp_post"], layer), (p["w_up"], layer), (p["w_down"], layer), tm)
    if prompt:
        k_new, v_new, h_new = jnp.stack(k_out), jnp.stack(v_out), jnp.stack(h_out)
    else:
        k_new, v_new = (a.reshape(a.shape[:3] + (N_KV, HEAD_DIM)) for a in kv_all)
        h_new = h_all
    return x2.reshape(bt, l, D_MODEL), k_new, v_new, jnp.stack(re_out), jnp.stack(im_out), h_new


def kernel(x_prompt, x_sample, cache_swa_k, cache_swa_v, state_s5_re, state_s5_im, state_hgrn, t5_bias_table,
           norm_mix_pre, norm_mix_post, norm_mlp_pre, norm_mlp_post, w_in_even, w_out_even, s5_lambda_re,
           s5_lambda_im, s5_log_step, s5_b_re, s5_b_im, s5_c_re, s5_c_im, s5_d, s5_w_glu, swa_sinks, w_in_odd,
           w_out_odd, hgrn_lb_param, hgrn_gnorm, w_up, w_down):
    bf = lambda a: a.astype(BF16)
    q0, q1 = S5_WIDTH, S5_WIDTH + SWA_WIDTH
    w_in_even = jnp.concatenate([w_in_even[..., :q0], _pair_heads(w_in_even[..., q0:q1], 2), w_in_even[..., q1:]],
                                axis=-1)
    w_out_even = jnp.concatenate([w_out_even[:, :q0], _pair_heads(w_out_even[:, q0:], 1)], axis=1)
    rows = lambda a: a[:, None, :]
    params = dict(
        norm_mix_pre=rows(norm_mix_pre), norm_mix_post=rows(norm_mix_post), norm_mlp_pre=rows(norm_mlp_pre),
        norm_mlp_post=rows(norm_mlp_post), w_in_even=bf(w_in_even), w_out_even=bf(w_out_even), s5_d=rows(s5_d),
        s5_w_glu=bf(s5_w_glu), swa_sinks=swa_sinks, w_in_odd=bf(w_in_odd), w_out_odd=bf(w_out_odd),
        hgrn_gnorm=rows(hgrn_gnorm), w_up=bf(w_up), w_down=bf(w_down),
        t5_band=_t5_band(t5_bias_table),
        hgrn_lb=rows(_hgrn_lower_bounds(hgrn_lb_param)),
        s5_prep=_s5_prep(s5_lambda_re, s5_lambda_im, s5_log_step, s5_b_re, s5_b_im, s5_c_re, s5_c_im),
    )
    y_prompt, k_p, v_p, re_p, im_p, hg_p = _trunk(x_prompt, None, None, None, None, None, params)
    y_sample, k_s, v_s, re_s, im_s, hg_s = _trunk(x_sample, cache_swa_k, cache_swa_v, state_s5_re, state_s5_im,
                                                  state_hgrn, params)
    return (y_prompt, y_sample, k_p, v_p, k_s, v_s, re_p, im_p, re_s, im_s, hg_p, hg_s)
```

```python
import functools
import math

import numpy as np
import jax
import jax.numpy as jnp
from jax import lax
from jax.experimental import pallas as pl
from jax.experimental.pallas import tpu as pltpu

F32 = jnp.float32
BF16 = jnp.bfloat16

D_MODEL = 1024
S5_WIDTH = 512
S5_GROUPS = 32
S5_GROUP = 16
S5_STATE = 64
S5_HALF_STATES = 1024
S5_HALF_CH = 256
SWA_WIDTH = 512
HEAD_DIM = 64
N_HEADS = 8
N_KV = 2
KV_GROUP = N_HEADS // N_KV
WINDOW = 128
T5_BUCKETS = 32
T5_MAX_DIST = 128
HG_HEADS = 8
HG_DK = 128
HG_DV = 128
HG_W = HG_HEADS * HG_DK
D_FF = 4096
RMS_EPS = 1e-6
NEG_INF = -1e30
LOG_FLOOR = 1e-30

S5_CHUNK = 64
HG_CHUNK = 128
HG_BASE = 32
TOKEN_BLOCK = 512
PROJ_TOKEN_BLOCK = 1024
SWA_BLOCKS_PER_STEP = 8
MLP_FF_BLOCK = 1024
SAMPLE_SEQ_BLOCK = 8
SWA_SAMPLE_SEQ_BLOCK = 32
VMEM_LIMIT = 56 * 1024 * 1024
MXU_WIDTH = 256


def _cparams(*sem):
    return pltpu.CompilerParams(dimension_semantics=sem, vmem_limit_bytes=VMEM_LIMIT)


def _resident(shape):
    nd = len(shape)
    return pl.BlockSpec(shape, lambda *_: (0,) * nd, pipeline_mode=pl.Buffered(1))


def _layer_resident(stack, layer):
    nd = stack.ndim
    return pl.BlockSpec((None,) + stack.shape[1:], lambda *_: (layer,) + (0,) * (nd - 1),
                        pipeline_mode=pl.Buffered(1))


def _rms(x, w):
    return x * lax.rsqrt(jnp.mean(x * x, axis=-1, keepdims=True) + RMS_EPS) * w


def _sigmoid(x):
    return 1.0 / (1.0 + jnp.exp(-x))


def _silu(x):
    return x * _sigmoid(x)


def _gelu_tanh(x):
    return 0.5 * x * (1.0 + jnp.tanh(math.sqrt(2.0 / math.pi) * (x + 0.044715 * (x * x * x))))


def _dot(a, b):
    return jnp.dot(a, b, preferred_element_type=F32)


def _dot_nt(a, b):
    return lax.dot_general(a, b, (((1,), (1,)), ((), ())), preferred_element_type=F32)


def _dot_tn(a, b):
    return lax.dot_general(a, b, (((0,), (0,)), ((), ())), preferred_element_type=F32)


def _norm_proj_body(x_ref, g_ref, w_ref, *o_refs, splits):
    xn = _rms(x_ref[...], g_ref[...]).astype(BF16)
    off = 0
    i = 0
    while i < len(splits):
        j = i + 1
        while sum(splits[i:j]) < MXU_WIDTH and j < len(splits):
            j += 1
        y = _dot(xn, w_ref[:, off:off + sum(splits[i:j])])
        sub = 0
        for o_ref, wd in zip(o_refs[i:j], splits[i:j]):
            o_ref[...] = y[:, sub:sub + wd]
            sub += wd
        off += sub
        i = j


def _norm_proj(x2d, g, w_bf, splits, tm):
    t, n = x2d.shape[0], w_bf[0].shape[-1]
    return pl.pallas_call(
        functools.partial(_norm_proj_body, splits=splits),
        grid=(t // tm,),
        in_specs=[pl.BlockSpec((tm, D_MODEL), lambda i: (i, 0)),
                  _layer_resident(*g), _layer_resident(*w_bf)],
        out_specs=[pl.BlockSpec((tm, wd), lambda i: (i, 0)) for wd in splits],
        out_shape=[jax.ShapeDtypeStruct((t, wd), F32) for wd in splits],
        compiler_params=_cparams("parallel"),
        name="norm_proj",
    )(x2d, g[0], w_bf[0])


def _post_body(*refs, widths):
    x_ref = refs[0]
    in_refs = refs[1:1 + len(widths)]
    wout_ref, gpost_ref, gpre_ref, gmpost_ref, wup_ref, wdn_ref, o_ref = refs[1 + len(widths):]
    mix = None
    off = 0
    for r, wd in zip(in_refs, widths):
        t = _dot(r[...].astype(BF16), wout_ref[off:off + wd, :])
        mix = t if mix is None else mix + t
        off += wd
    x1 = x_ref[...] + _rms(mix, gpost_ref[...])
    hn = _rms(x1, gpre_ref[...]).astype(BF16)
    acc = None
    for c in range(D_FF // MLP_FF_BLOCK):
        cs = slice(c * MLP_FF_BLOCK, (c + 1) * MLP_FF_BLOCK)
        hk = jnp.square(jnp.maximum(_dot(hn, wup_ref[:, cs]), 0.0)).astype(BF16)
        t = _dot(hk, wdn_ref[cs, :])
        acc = t if acc is None else acc + t
    o_ref[...] = x1 + _rms(acc, gmpost_ref[...])


def _post_stream_body(*refs, widths):
    x_ref = refs[0]
    in_refs = refs[1:1 + len(widths)]
    wout_ref, gpost_ref, gpre_ref, gmpost_ref, wup_ref, wdn_ref, o_ref, x1_scr, hn_scr, acc_scr = refs[1 + len(widths):]
    c = pl.program_id(0)

    @pl.when(c == 0)
    def _():
        mix = None
        off = 0
        for r, wd in zip(in_refs, widths):
            t = _dot(r[...].astype(BF16), wout_ref[off:off + wd, :])
            mix = t if mix is None else mix + t
            off += wd
        x1 = x_ref[...] + _rms(mix, gpost_ref[...])
        x1_scr[...] = x1
        hn_scr[...] = _rms(x1, gpre_ref[...]).astype(BF16)
        acc_scr[...] = jnp.zeros_like(acc_scr)

    hk = jnp.square(jnp.maximum(_dot(hn_scr[...], wup_ref[...]), 0.0)).astype(BF16)
    acc_scr[...] += _dot(hk, wdn_ref[...])

    @pl.when(c == pl.num_programs(0) - 1)
    def _():
        o_ref[...] = x1_scr[...] + _rms(acc_scr[...], gmpost_ref[...])


def _post(x2d, mixer_outs, wout_bf, gpost, gpre, gmpost, wup_bf, wdn_bf, tm):
    t = x2d.shape[0]
    widths = tuple(a.shape[1] for a in mixer_outs)
    params = [wout_bf, gpost, gpre, gmpost, wup_bf, wdn_bf]
    stacks = [a for a, _ in params]
    if t == tm:
        whole = lambda wd: pl.BlockSpec((t, wd), lambda c: (0, 0))
        slab = lambda pr, blk, idx: pl.BlockSpec((None,) + blk, lambda c: (pr[1],) + idx(c))
        vec = lambda pr: slab(pr, (1, D_MODEL), lambda c: (0, 0))
        return pl.pallas_call(
            functools.partial(_post_stream_body, widths=widths),
            grid=(D_FF // MLP_FF_BLOCK,),
            in_specs=[whole(D_MODEL)] + [whole(wd) for wd in widths] + [
                slab(wout_bf, (D_MODEL, D_MODEL), lambda c: (0, 0)), vec(gpost), vec(gpre), vec(gmpost),
                slab(wup_bf, (D_MODEL, MLP_FF_BLOCK), lambda c: (0, c)),
                slab(wdn_bf, (MLP_FF_BLOCK, D_MODEL), lambda c: (c, 0))],
            out_specs=whole(D_MODEL),
            out_shape=jax.ShapeDtypeStruct((t, D_MODEL), F32),
            scratch_shapes=[pltpu.VMEM((t, D_MODEL), F32), pltpu.VMEM((t, D_MODEL), BF16),
                            pltpu.VMEM((t, D_MODEL), F32)],
            compiler_params=_cparams("arbitrary"),
            name="post_mlp_stream",
        )(x2d, *mixer_outs, *stacks)
    row = lambda wd: pl.BlockSpec((tm, wd), lambda i: (i, 0))
    return pl.pallas_call(
        functools.partial(_post_body, widths=widths),
        grid=(t // tm,),
        in_specs=[row(D_MODEL)] + [row(wd) for wd in widths] + [_layer_resident(*pr) for pr in params],
        out_specs=row(D_MODEL),
        out_shape=jax.ShapeDtypeStruct((t, D_MODEL), F32),
        compiler_params=_cparams("parallel"),
        name="post_mlp",
    )(x2d, *mixer_outs, *stacks)


def _bucket_map():
    t = np.arange(WINDOW)[:, None]
    m = np.arange(2 * WINDOW)[None, :]
    n = np.maximum(WINDOW + t - m, 0)
    max_exact = T5_BUCKETS // 2
    large = max_exact + np.floor(np.log(np.maximum(n, max_exact) / max_exact)
                                 / math.log(T5_MAX_DIST / max_exact) * (T5_BUCKETS - max_exact)).astype(np.int64)
    large = np.minimum(large, T5_BUCKETS - 1)
    return np.where(n < max_exact, n, large).astype(np.int32)


def _bias_body(tab_ref, bm_ref, o_ref):
    bm = bm_ref[...]
    row = lax.broadcasted_iota(jnp.int32, bm.shape, 0)
    col = lax.broadcasted_iota(jnp.int32, bm.shape, 1)
    dist = WINDOW + row - col
    in_win = (dist >= 0) & (dist < WINDOW)
    for h in range(N_HEADS):
        acc = jnp.zeros(bm.shape, F32)
        for b in range(T5_BUCKETS):
            acc = jnp.where(bm == b, tab_ref[b, h], acc)
        o_ref[0, h] = jnp.where(in_win & (col >= WINDOW), acc, NEG_INF)
        o_ref[1, h] = jnp.where(in_win, acc, NEG_INF)


def _t5_band(table):
    return pl.pallas_call(
        _bias_body,
        in_specs=[pl.BlockSpec(memory_space=pltpu.SMEM),
                  pl.BlockSpec((WINDOW, 2 * WINDOW), lambda: (0, 0))],
        out_specs=pl.BlockSpec((2, N_HEADS, WINDOW, 2 * WINDOW), lambda: (0, 0, 0, 0)),
        out_shape=jax.ShapeDtypeStruct((2, N_HEADS, WINDOW, 2 * WINDOW), F32),
        name="t5_band",
    )(table, jnp.asarray(_bucket_map()))


def _pair_heads(a, axis):
    shp = a.shape
    a = a.reshape(shp[:axis] + (N_KV, KV_GROUP, HEAD_DIM) + shp[axis + 1:])
    a = jnp.swapaxes(a, axis, axis + 1)
    return a.reshape(shp)


def _swa_prompt_body(sink_ref, q_ref, kc_ref, kp_ref, vc_ref, vp_ref, band_ref, o_ref, *, nsub):
    tile = N_KV * HEAD_DIM
    lo = lax.broadcasted_iota(jnp.int32, (WINDOW, tile), 1) < HEAD_DIM
    first_band = jnp.minimum(pl.program_id(1), 1)
    for sb in range(nsub):
        rows = slice(sb * WINDOW, (sb + 1) * WINDOW)
        before = slice((sb - 1) * WINDOW, sb * WINDOW)
        k_prev = kp_ref[...] if sb == 0 else kc_ref[before, :]
        v_prev = vp_ref[...] if sb == 0 else vc_ref[before, :]
        kb = jnp.concatenate([k_prev, kc_ref[rows, :]], axis=0).astype(BF16)
        vb = jnp.concatenate([v_prev, vc_ref[rows, :]], axis=0).astype(BF16)
        v_ext = jnp.concatenate([vb, jnp.ones_like(vb)], axis=1)
        for t in range(KV_GROUP):
            qt = q_ref[rows, t * tile:(t + 1) * tile] * (HEAD_DIM ** -0.5)
            halves = []
            for kv in range(N_KV):
                h = kv * KV_GROUP + t
                qh = jnp.where(lo if kv == 0 else jnp.logical_not(lo), qt, 0.0).astype(BF16)
                s = _dot_nt(qh, kb) + (band_ref[first_band, h] if sb == 0 else band_ref[1, h])
                sink = sink_ref[h]
                m = jnp.maximum(jnp.max(s, axis=-1, keepdims=True), sink)
                oe = _dot(jnp.exp(s - m).astype(BF16), v_ext)
                halves.append(oe[:, 0:tile] / (oe[:, tile:tile + 1] + jnp.exp(sink - m)))
            o_ref[rows, t * tile:(t + 1) * tile] = jnp.where(lo, halves[0], halves[1])


def _swa_prompt(q, k, v, sinks, band):
    b, l, _ = q.shape
    kvw = N_KV * HEAD_DIM
    nsub = min(SWA_BLOCKS_PER_STEP, l // WINDOW)
    rows = nsub * WINDOW
    cur = lambda w: pl.BlockSpec((None, rows, w), lambda i, n: (i, n, 0))
    prev = lambda w: pl.BlockSpec((None, WINDOW, w), lambda i, n: (i, jnp.maximum(n * nsub - 1, 0), 0))
    return pl.pallas_call(
        functools.partial(_swa_prompt_body, nsub=nsub),
        grid=(b, l // rows),
        in_specs=[pl.BlockSpec(memory_space=pltpu.SMEM), cur(SWA_WIDTH), cur(kvw), prev(kvw), cur(kvw), prev(kvw),
                  _resident(band.shape)],
        out_specs=cur(SWA_WIDTH),
        out_shape=jax.ShapeDtypeStruct((b, l, SWA_WIDTH), F32),
        compiler_params=_cparams("parallel", "arbitrary"),
        name="swa_prompt",
    )(sinks, q, k, k, v, v, band)


def _swa_sample_body(q_ref, kn_ref, vn_ref, ck_ref, cv_ref, bias_ref, sink_ref, *rest, steps, layer, creates):
    o_ref, ko_all, vo_all = rest[-3:]
    ko_ref = ko_all.at[layer] if creates else ko_all
    vo_ref = vo_all.at[layer] if creates else vo_all
    if creates:
        for other in range(ko_all.shape[0]):
            if other != layer:
                ko_all[other] = jnp.zeros(ko_all.shape[1:], F32)
                vo_all[other] = jnp.zeros(vo_all.shape[1:], F32)
    tile = N_KV * HEAD_DIM
    ck, cv, kn, vn = ck_ref[...], cv_ref[...], kn_ref[...], vn_ref[...]
    ko_ref[:, 0:WINDOW - steps, :] = ck[:, steps:, :]
    ko_ref[:, WINDOW - steps:, :] = kn
    vo_ref[:, 0:WINDOW - steps, :] = cv[:, steps:, :]
    vo_ref[:, WINDOW - steps:, :] = vn
    lo = lax.broadcasted_iota(jnp.int32, (1, 1, tile), 2) < HEAD_DIM
    q = q_ref[...] * (HEAD_DIM ** -0.5)
    qs = []
    for kv in range(N_KV):
        for t in range(KV_GROUP):
            qs.append(jnp.where(lo if kv == 0 else jnp.logical_not(lo), q[:, :, t * tile:(t + 1) * tile], 0.0))
    qall = jnp.concatenate(qs, axis=1).astype(BF16)
    bias = bias_ref[...]
    sink = sink_ref[...][None]
    bdot = lambda a, b_, spec: jnp.einsum(spec, a, b_, preferred_element_type=F32)
    sc = bdot(qall, ck.astype(BF16), "bqd,bkd->bqk") + bias[None, :, 0:WINDOW]
    sn = bdot(qall, kn.astype(BF16), "bqd,bkd->bqk") + bias[None, :, WINDOW:WINDOW + steps]
    m = jnp.maximum(jnp.maximum(jnp.max(sc, axis=-1, keepdims=True), jnp.max(sn, axis=-1, keepdims=True)), sink)
    ec, en = jnp.exp(sc - m), jnp.exp(sn - m)
    den = jnp.sum(ec, axis=-1, keepdims=True) + jnp.sum(en, axis=-1, keepdims=True) + jnp.exp(sink - m)
    o = (bdot(ec.astype(BF16), cv.astype(BF16), "bqk,bkd->bqd")
         + bdot(en.astype(BF16), vn.astype(BF16), "bqk,bkd->bqd")) / den
    for t in range(KV_GROUP):
        o_ref[:, :, t * tile:(t + 1) * tile] = jnp.where(
            lo, o[:, t * steps:(t + 1) * steps, :], o[:, (KV_GROUP + t) * steps:(KV_GROUP + t + 1) * steps, :])


def _swa_sample(q, kn, vn, cache_k, cache_v, idx, new_caches, sinks, band):
    b, steps, _ = q.shape
    kvw = N_KV * HEAD_DIM
    bb = min(SWA_SAMPLE_SEQ_BLOCK, b)
    nrow = N_HEADS * steps
    creates = new_caches is None
    bias = band[1, :, 0:steps, :].reshape(nrow, 2 * WINDOW)
    sink_rows = jnp.repeat(sinks, steps).reshape(nrow, 1)
    blk = lambda r, w: pl.BlockSpec((bb, r, w), lambda i: (i, 0, 0))
    cache = pl.BlockSpec((None, bb, WINDOW, kvw), lambda i: (idx, i, 0, 0))
    cache_out = (pl.BlockSpec((cache_k.shape[0], bb, WINDOW, kvw), lambda i: (0, i, 0, 0)) if creates else cache)
    in_specs = [blk(steps, SWA_WIDTH), blk(steps, kvw), blk(steps, kvw), cache, cache,
                _resident((nrow, 2 * WINDOW)), _resident((nrow, 1))]
    args = [q, kn, vn, cache_k, cache_v, bias, sink_rows]
    aliases = {}
    if not creates:
        in_specs += [pl.BlockSpec(memory_space=pl.ANY)] * 2
        args += list(new_caches)
        aliases = {len(args) - 2: 1, len(args) - 1: 2}
    o, k_new, v_new = pl.pallas_call(
        functools.partial(_swa_sample_body, steps=steps, layer=idx, creates=creates),
        grid=(b // bb,),
        in_specs=in_specs,
        out_specs=[blk(steps, SWA_WIDTH), cache_out, cache_out],
        out_shape=[jax.ShapeDtypeStruct((b, steps, SWA_WIDTH), F32),
                   jax.ShapeDtypeStruct(cache_k.shape, F32), jax.ShapeDtypeStruct(cache_v.shape, F32)],
        input_output_aliases=aliases,
        compiler_params=_cparams("parallel"),
        name="swa_sample",
    )(*args)
    return o, (k_new, v_new)


def _s5_prep_body(lr_ref, li_ref, ls_ref, br_ref, bi_ref, cr_ref, ci_ref, bm_ref, cm_ref, pw_ref, pinv_ref):
    hs = S5_HALF_STATES
    lr, li = lr_ref[...], li_ref[...]
    dt = jnp.exp(ls_ref[...])
    mag = jnp.exp(lr * dt)
    ang = li * dt
    ar = mag * jnp.cos(ang)
    ai = mag * jnp.sin(ang)
    den = lr * lr + li * li
    nr = ar - 1.0
    kr = (nr * lr + ai * li) / den
    ki = (ai * lr - nr * li) / den
    hc = S5_HALF_CH
    own = (lax.broadcasted_iota(jnp.int32, (hc, hs), 0) // S5_GROUP
           == lax.broadcasted_iota(jnp.int32, (hc, hs), 1) // S5_STATE)
    blockdiag = lambda a: jnp.where(own, jnp.concatenate([a] * (hc // S5_GROUP), axis=0), 0.0)
    br, bi = br_ref[...], bi_ref[...]
    bm_ref[:, 0:hs] = blockdiag(kr * br - ki * bi).astype(BF16)
    bm_ref[:, hs:2 * hs] = blockdiag(kr * bi + ki * br).astype(BF16)
    cm_ref[0:hs, :] = blockdiag(cr_ref[...]).T.astype(BF16)
    cm_ref[hs:2 * hs, :] = blockdiag(-ci_ref[...]).T.astype(BF16)
    n2 = ar * ar + ai * ai
    ir, ii = ar / n2, -ai / n2
    pr, pi = jnp.ones_like(ar), jnp.zeros_like(ar)
    qr, qi = pr, pi
    for j in range(S5_CHUNK):
        pw_ref[0, j:j + 1, :] = pr
        pw_ref[1, j:j + 1, :] = pi
        pinv_ref[0, j:j + 1, :] = qr
        pinv_ref[1, j:j + 1, :] = qi
        pr, pi = pr * ar - pi * ai, pr * ai + pi * ar
        qr, qi = qr * ir - qi * ii, qr * ii + qi * ir


def _s5_prep(lam_re, lam_im, log_step, b_re, b_im, c_re, c_im):
    hs, hc = S5_HALF_STATES, S5_HALF_CH
    gh = S5_GROUPS // 2
    nh = 2 * lam_re.shape[0]
    vec = lambda a: a.reshape(nh, 1, hs)
    ls = jnp.broadcast_to(log_step[..., None], log_step.shape + (S5_STATE,))
    b_rows = lambda b: b.reshape(nh, gh, S5_STATE, S5_GROUP).transpose(0, 3, 1, 2).reshape(nh, S5_GROUP, hs)
    c_rows = lambda c: c.reshape(nh, gh, S5_GROUP, S5_STATE).transpose(0, 2, 1, 3).reshape(nh, S5_GROUP, hs)

    vspec = pl.BlockSpec((None, 1, hs), lambda h: (h, 0, 0))
    rspec = pl.BlockSpec((None, S5_GROUP, hs), lambda h: (h, 0, 0))
    pspec = pl.BlockSpec((None, 2, S5_CHUNK, hs), lambda h: (h, 0, 0, 0))
    return pl.pallas_call(
        _s5_prep_body,
        grid=(nh,),
        in_specs=[vspec, vspec, vspec, rspec, rspec, rspec, rspec],
        out_specs=[pl.BlockSpec((None, hc, 2 * hs), lambda h: (h, 0, 0)),
                   pl.BlockSpec((None, 2 * hs, hc), lambda h: (h, 0, 0)), pspec, pspec],
        out_shape=[jax.ShapeDtypeStruct((nh, hc, 2 * hs), BF16), jax.ShapeDtypeStruct((nh, 2 * hs, hc), BF16),
                   jax.ShapeDtypeStruct((nh, 2, S5_CHUNK, hs), F32), jax.ShapeDtypeStruct((nh, 2, S5_CHUNK, hs), F32)],
        compiler_params=_cparams("parallel"),
        name="s5_prep",
    )(vec(lam_re), vec(lam_im), vec(ls), b_rows(b_re), b_rows(b_im), c_rows(c_re), c_rows(c_im))


def _s5_glu(y, u, d_ref, wg_ref):
    g = _gelu_tanh(y + d_ref[...] * u)
    return g * _sigmoid(_dot(g.astype(BF16), wg_ref[...]))


def _s5_prompt_body(u_ref, bm_ref, cm_ref, pw_ref, pinv_ref, d_ref, wg_ref, o_ref, hfin_ref, bu_scr, hb_scr,
                    h_scr, *, tb):
    hs, hc, c0 = S5_HALF_STATES, S5_HALF_CH, S5_CHUNK

    @pl.when(pl.program_id(1) == 0)
    def _():
        h_scr[...] = jnp.zeros_like(h_scr)

    u = u_ref[...]
    ub = u.astype(BF16)
    tril = (lax.broadcasted_iota(jnp.int32, (c0, c0), 0) >= lax.broadcasted_iota(jnp.int32, (c0, c0), 1)).astype(BF16)
    ys = []
    nchunk = tb // c0
    last = c0 - 1
    for hf in range(2):
        re = slice(2 * hf * hs, (2 * hf + 1) * hs)
        im = slice((2 * hf + 1) * hs, (2 * hf + 2) * hs)
        both = slice(2 * hf * hs, (2 * hf + 2) * hs)
        bu_scr[:, both] = _dot(ub[:, hf * hc:(hf + 1) * hc], bm_ref[hf])
        for ci in range(nchunk):
            rows = slice(ci * c0, (ci + 1) * c0)
            bur, bui = bu_scr[rows, re], bu_scr[rows, im]
            qr, qi = pinv_ref[hf, 0], pinv_ref[hf, 1]
            bu_scr[rows, re] = _dot(tril, (qr * bur - qi * bui).astype(BF16))
            bu_scr[rows, im] = _dot(tril, (qr * bui + qi * bur).astype(BF16))
        ar, ai = pw_ref[hf, 0, 1:2, :], pw_ref[hf, 1, 1:2, :]
        er, ei = pw_ref[hf, 0, last:c0, :], pw_ref[hf, 1, last:c0, :]
        hr0, hi0 = h_scr[:, re], h_scr[:, im]
        carries = []
        for ci in range(nchunk):
            gr, gi = ar * hr0 - ai * hi0, ar * hi0 + ai * hr0
            carries.append((gr, gi))
            sr = bu_scr[ci * c0 + last:(ci + 1) * c0, re] + gr
            si = bu_scr[ci * c0 + last:(ci + 1) * c0, im] + gi
            hr0, hi0 = er * sr - ei * si, er * si + ei * sr
        h_scr[:, re] = hr0
        h_scr[:, im] = hi0
        for ci in range(nchunk):
            rows = slice(ci * c0, (ci + 1) * c0)
            sr = bu_scr[rows, re] + carries[ci][0]
            si = bu_scr[rows, im] + carries[ci][1]
            pr, pi = pw_ref[hf, 0], pw_ref[hf, 1]
            hb_scr[rows, re] = (pr * sr - pi * si).astype(BF16)
            hb_scr[rows, im] = (pr * si + pi * sr).astype(BF16)
        ys.append(_dot(hb_scr[:, both], cm_ref[hf]))
    o_ref[...] = _s5_glu(jnp.concatenate(ys, axis=1), u, d_ref, wg_ref)
    hfin_ref[...] = h_scr[...]


def _layer_halves(a, layer):
    return pl.BlockSpec((2,) + a.shape[1:], lambda *_: (layer,) + (0,) * (a.ndim - 1), pipeline_mode=pl.Buffered(1))


def _s5_prompt(u, prep, layer, d_skip, wglu_bf, tb):
    b, l, _ = u.shape
    bm, cm, pw, pinv = prep
    ns = 4 * S5_HALF_STATES
    return pl.pallas_call(
        functools.partial(_s5_prompt_body, tb=tb),
        grid=(b, l // tb),
        in_specs=[pl.BlockSpec((None, tb, S5_WIDTH), lambda i, j: (i, j, 0)),
                  *[_layer_halves(a, layer) for a in (bm, cm, pw, pinv)],
                  _layer_resident(d_skip, layer), _layer_resident(wglu_bf, layer)],
        out_specs=[pl.BlockSpec((None, tb, S5_WIDTH), lambda i, j: (i, j, 0)),
                   pl.BlockSpec((None, 1, ns), lambda i, j: (i, 0, 0))],
        out_shape=[jax.ShapeDtypeStruct((b, l, S5_WIDTH), F32), jax.ShapeDtypeStruct((b, 1, ns), F32)],
        scratch_shapes=[pltpu.VMEM((tb, ns), F32), pltpu.VMEM((tb, ns), BF16), pltpu.VMEM((1, ns), F32)],
        compiler_params=_cparams("parallel", "arbitrary"),
        name="s5_prompt",
    )(u, bm, cm, pw, pinv, d_skip, wglu_bf)


def _s5_sample_body(u_ref, re_ref, im_ref, bm_ref, cm_ref, pw_ref, d_ref, wg_ref, o_ref, re_out_ref, im_out_ref, *,
                    steps):
    hs, hc = S5_HALF_STATES, S5_HALF_CH
    h = [re_ref[:, 0:hs], im_ref[:, 0:hs], re_ref[:, hs:2 * hs], im_ref[:, hs:2 * hs]]
    for t in range(steps):
        u = u_ref[:, t, :]
        ub = u.astype(BF16)
        ys = []
        for hf in range(2):
            bu = _dot(ub[:, hf * hc:(hf + 1) * hc], bm_ref[hf])
            ar, ai = pw_ref[hf, 0, 1:2, :], pw_ref[hf, 1, 1:2, :]
            hr, hi = h[2 * hf], h[2 * hf + 1]
            h[2 * hf] = ar * hr - ai * hi + bu[:, 0:hs]
            h[2 * hf + 1] = ar * hi + ai * hr + bu[:, hs:2 * hs]
            ys.append(_dot(jnp.concatenate([h[2 * hf], h[2 * hf + 1]], axis=1).astype(BF16), cm_ref[hf]))
        o_ref[:, t, :] = _s5_glu(jnp.concatenate(ys, axis=1), u, d_ref, wg_ref)
    for hf in range(2):
        re_out_ref[:, hf * hs:(hf + 1) * hs] = h[2 * hf]
        im_out_ref[:, hf * hs:(hf + 1) * hs] = h[2 * hf + 1]


def _s5_sample(u, st_re, st_im, prep, layer, d_skip, wglu_bf):
    b, steps, _ = u.shape
    bm, cm, pw, _ = prep
    ns = 2 * S5_HALF_STATES
    full = lambda shape: pl.BlockSpec(shape, lambda i: (0,) * len(shape))
    halves = lambda a: pl.BlockSpec((2,) + a.shape[1:], lambda i: (layer,) + (0,) * (a.ndim - 1))
    state = pl.BlockSpec((None, b, ns), lambda i: (layer, 0, 0))
    return pl.pallas_call(
        functools.partial(_s5_sample_body, steps=steps),
        grid=(1,),
        in_specs=[full(u.shape), state, state, halves(bm), halves(cm), halves(pw),
                  _layer_resident(d_skip, layer), _layer_resident(wglu_bf, layer)],
        out_specs=[full(u.shape), full((b, ns)), full((b, ns))],
        out_shape=[jax.ShapeDtypeStruct(u.shape, F32), jax.ShapeDtypeStruct((b, ns), F32),
                   jax.ShapeDtypeStruct((b, ns), F32)],
        compiler_params=_cparams("arbitrary"),
        name="s5_sample",
    )(u, st_re, st_im, bm, cm, pw, d_skip, wglu_bf)


def _s5_state_out(h):
    b = h.shape[0]
    h4 = h.reshape(b, 2, 2, S5_HALF_STATES)
    return (h4[:, :, 0].reshape(b, S5_GROUPS, S5_STATE), h4[:, :, 1].reshape(b, S5_GROUPS, S5_STATE))


def _lb_body(p_ref, o_ref):
    p = p_ref[...]
    e = jnp.exp(p - jnp.max(p, axis=0, keepdims=True))
    sm = e / jnp.sum(e, axis=0, keepdims=True)
    acc = jnp.zeros_like(sm[0:1])
    for i in range(p.shape[0]):
        acc = acc + sm[i:i + 1]
        o_ref[i:i + 1, :] = acc - sm[0:1]


def _hgrn_lower_bounds(lb_param):
    n = lb_param.shape[0]
    return pl.pallas_call(
        _lb_body,
        in_specs=[pl.BlockSpec((n, HG_W), lambda: (0, 0))],
        out_specs=pl.BlockSpec((n, HG_W), lambda: (0, 0)),
        out_shape=jax.ShapeDtypeStruct((n, HG_W), F32),
        name="hgrn_lower_bounds",
    )(lb_param)


def _hgrn_forget(fz, lb):
    log_f = jnp.log(jnp.maximum(lb, LOG_FLOOR) + (1.0 - lb) * _sigmoid(fz))
    return log_f, (1.0 - lb) * _sigmoid(-fz)


def _group_norm_gate(o, gn, gate):
    return o * lax.rsqrt(jnp.mean(o * o, axis=-1, keepdims=True) + RMS_EPS) * gn * gate


def _cumsum_rows(tril_bf, x):
    hi = x.astype(BF16)
    r1 = x - hi.astype(F32)
    mid = r1.astype(BF16)
    lo = (r1 - mid.astype(F32)).astype(BF16)
    return _dot(tril_bf, hi) + _dot(tril_bf, mid) + _dot(tril_bf, lo)


def _hgrn_prompt_body(x_ref, g_ref, w_ref, lb_ref, gn_ref, o_ref, st_ref, q_scr, k_scr, b_scr, v_scr, s_scr, *, tb):
    c, w = HG_CHUNK, HG_W

    @pl.when(pl.program_id(1) == 0)
    def _():
        s_scr[...] = jnp.zeros_like(s_scr)

    row = lax.broadcasted_iota(jnp.int32, (c, c), 0)
    col = lax.broadcasted_iota(jnp.int32, (c, c), 1)
    tril = (row >= col).astype(BF16)
    xn = _rms(x_ref[...], g_ref[...]).astype(BF16)
    log_f, kf = _hgrn_forget(_dot(xn, w_ref[:, w:2 * w]), lb_ref[...])
    k_scr[...] = kf
    for ci in range(tb // c):
        b_scr[ci * c:(ci + 1) * c, :] = _cumsum_rows(tril, log_f[ci * c:(ci + 1) * c, :])
    q_scr[...] = _silu(_dot(xn, w_ref[:, 0:w]))
    v_scr[...] = _dot(xn, w_ref[:, 2 * w:3 * w]).astype(BF16)
    o_ref[...] = _silu(_dot(xn, w_ref[:, 3 * w:4 * w]))

    masks = [(row // HG_BASE == col // HG_BASE) & (col <= row)]
    strides = []
    s = 2 * HG_BASE
    while s <= c:
        strides.append(s)
        masks.append(row // s == col // s)
        s *= 2
    masks_all = [None if st == c else jnp.concatenate([mk] * HG_HEADS, axis=0)
                 for st, mk in zip([HG_BASE] + strides, masks)]
    cat = lambda parts: jnp.concatenate(parts, axis=0)

    def chunk(ci, carry):
        rows = pl.ds(pl.multiple_of(ci * c, c), c)
        q, k, vb = q_scr[rows, :], k_scr[rows, :], v_scr[rows, :]
        b = b_scr[rows, :]
        b_last = b[c - 1:c, :]
        q0, k0, q_in, k_st = [], [], [], []
        for m in range(c // HG_BASE):
            blk = slice(m * HG_BASE, (m + 1) * HG_BASE)
            mid = b[m * HG_BASE + HG_BASE // 2:m * HG_BASE + HG_BASE // 2 + 1, :]
            d = b[blk, :] - mid
            q0.append(q[blk, :] * jnp.exp(d))
            k0.append(k[blk, :] * jnp.exp(-d))
            q_in.append(q0[-1] * jnp.exp(mid))
            k_st.append(k0[-1] * jnp.exp(b_last - mid))
        qs, ks = [cat(q0).astype(BF16)], [cat(k0).astype(BF16)]
        q_in, k_st = cat(q_in).astype(BF16), cat(k_st).astype(BF16)
        for st in strides:
            ql, kl = [], []
            zeros = jnp.zeros((st // 2, w), BF16)
            for m in range(c // st):
                lower = slice(m * st, m * st + st // 2)
                upper = slice(m * st + st // 2, (m + 1) * st)
                mid = b[m * st + st // 2:m * st + st // 2 + 1, :]
                ql += [zeros, (q[upper, :] * jnp.exp(b[upper, :] - mid)).astype(BF16)]
                kl += [(k[lower, :] * jnp.exp(mid - b[lower, :])).astype(BF16), zeros]
            qs.append(cat(ql))
            ks.append(cat(kl))
        e_last = jnp.exp(b_last)
        heads = [slice(h * HG_DK, (h + 1) * HG_DK) for h in range(HG_HEADS)]
        a = None
        for qh, kh, mk in zip(qs, ks, masks_all):
            t = cat([_dot_nt(qh[:, hs], kh[:, hs]) for hs in heads])
            if mk is not None:
                t = jnp.where(mk, t, 0.0)
            a = t if a is None else a + t
        a = a.astype(BF16)
        states = [s_scr[h] for h in range(HG_HEADS)]
        o = jnp.concatenate([_dot(a[h * c:(h + 1) * c, :], vb[:, hs]) + _dot_nt(q_in[:, hs], states[h].astype(BF16))
                             for h, hs in enumerate(heads)], axis=0)
        gate = jnp.concatenate([o_ref[rows, hs] for hs in heads], axis=0)
        out = _group_norm_gate(o, gn_ref[...], gate)
        for h, hs in enumerate(heads):
            s_scr[h] = states[h] * e_last[:, hs] + _dot_tn(vb[:, hs], k_st[:, hs])
            o_ref[rows, hs] = out[h * c:(h + 1) * c, :]
        return carry

    lax.fori_loop(0, tb // c, chunk, 0, unroll=True)
    st_ref[...] = s_scr[...]


def _hgrn_prompt(x, g, w_bf, lb, gn, tb):
    b, l, _ = x.shape
    row = pl.BlockSpec((None, tb, D_MODEL), lambda i, j: (i, j, 0))
    return pl.pallas_call(
        functools.partial(_hgrn_prompt_body, tb=tb),
        grid=(b, l // tb),
        in_specs=[row, _layer_resident(*g), _layer_resident(*w_bf), _layer_resident(*lb), _layer_resident(*gn)],
        out_specs=[row, pl.BlockSpec((None, HG_HEADS, HG_DV, HG_DK), lambda i, j: (i, 0, 0, 0))],
        out_shape=[jax.ShapeDtypeStruct((b, l, HG_W), F32),
                   jax.ShapeDtypeStruct((b, HG_HEADS, HG_DV, HG_DK), F32)],
        scratch_shapes=[pltpu.VMEM((tb, HG_W), F32)] * 3 + [pltpu.VMEM((tb, HG_W), BF16),
                                                             pltpu.VMEM((HG_HEADS, HG_DV, HG_DK), F32)],
        compiler_params=_cparams("parallel", "arbitrary"),
        name="hgrn_prompt",
    )(x, g[0], w_bf[0], lb[0], gn[0])


def _hgrn_sample_body(q_ref, fz_ref, iv_ref, gz_ref, s_ref, lb_ref, gn_ref, *rest, steps, bb, layer, creates):
    o_ref, so_all = rest[-2:]
    so_ref = so_all.at[layer] if creates else so_all
    if creates:
        for other in range(so_all.shape[0]):
            if other != layer:
                so_all[other] = jnp.zeros(so_all.shape[1:], F32)
    n = bb * steps
    qf = _silu(q_ref[...])
    log_f, kf = _hgrn_forget(fz_ref[...], lb_ref[...])
    v = iv_ref[...]
    gate = _silu(gz_ref[...])
    row = lax.broadcasted_iota(jnp.int32, (n, n), 0)
    col = lax.broadcasted_iota(jnp.int32, (n, n), 1)
    same = row // steps == col // steps
    causal = same & (col <= row)
    first = same & (col % steps == 0)
    per_seq = lax.broadcasted_iota(jnp.int32, (bb, n), 1) // steps == lax.broadcasted_iota(jnp.int32, (bb, n), 0)
    sums = _cumsum_rows(jnp.concatenate([causal, same, first, per_seq], axis=0).astype(BF16), log_f)
    b, total, b_first, total_seq = sums[0:n], sums[n:2 * n], sums[2 * n:3 * n], sums[3 * n:3 * n + bb]
    e0 = b - b_first
    q0 = (qf * jnp.exp(e0)).astype(BF16)
    k0 = (kf * jnp.exp(-e0)).astype(BF16)
    q_in = (qf * jnp.exp(b)).astype(BF16)
    k_st = (kf * jnp.exp(total - b)).astype(BF16)
    vb = v.astype(BF16)
    e_seq = jnp.exp(total_seq)
    e_rows = jnp.concatenate([e_seq[:, h * HG_DK:(h + 1) * HG_DK] for h in range(HG_HEADS)], axis=0)
    pad = HG_DK - e_rows.shape[0]
    if pad:
        e_rows = jnp.concatenate([e_rows, jnp.zeros((pad, HG_DK), F32)], axis=0)
    e_cols = e_rows.T
    for h in range(HG_HEADS):
        hs = slice(h * HG_DK, (h + 1) * HG_DK)
        a = jnp.where(causal, _dot_nt(q0[:, hs], k0[:, hs]), 0.0)
        o_intra = _dot(a.astype(BF16), vb[:, hs])
        outs = []
        for i in range(bb):
            rs = slice(i * steps, (i + 1) * steps)
            s_old = s_ref[i, h]
            outs.append(_dot(q_in[rs, hs], s_old.astype(BF16)))
            so_ref[i, h] = (s_old * e_cols[:, h * bb + i:h * bb + i + 1]
                            + _dot_tn(k_st[rs, hs], vb[rs, hs]))
        o = o_intra + jnp.concatenate(outs, axis=0)
        o_ref[:, hs] = _group_norm_gate(o, gn_ref[...], gate[:, hs])


def _hgrn_sample(proj4, states, layer, out_states, lb, gn, steps):
    n_layers, b = states.shape[0], states.shape[1]
    bb = SAMPLE_SEQ_BLOCK
    assert HG_HEADS * bb <= HG_DK
    creates = out_states is None
    rows = pl.BlockSpec((bb * steps, HG_W), lambda i: (i, 0))
    sshape = (bb, HG_HEADS, HG_DK, HG_DV)
    s_in = pl.BlockSpec((None,) + sshape, lambda i: (layer, i, 0, 0, 0))
    s_out = pl.BlockSpec((n_layers,) + sshape, lambda i: (0, i, 0, 0, 0)) if creates else s_in
    in_specs = [rows, rows, rows, rows, s_in, _layer_resident(*lb), _layer_resident(*gn)]
    args = [*proj4, states, lb[0], gn[0]]
    aliases = {}
    if not creates:
        in_specs.append(pl.BlockSpec(memory_space=pl.ANY))
        args.append(out_states)
        aliases = {len(args) - 1: 1}
    return pl.pallas_call(
        functools.partial(_hgrn_sample_body, steps=steps, bb=bb, layer=layer, creates=creates),
        grid=(b // bb,),
        in_specs=in_specs,
        out_specs=[rows, s_out],
        out_shape=[jax.ShapeDtypeStruct((b * steps, HG_W), F32), jax.ShapeDtypeStruct(states.shape, F32)],
        input_output_aliases=aliases,
        compiler_params=_cparams("parallel"),
        name="hgrn_sample",
    )(*args)


def _trunk(x, cache_k, cache_v, st_re, st_im, st_h, p):
    prompt = cache_k is None
    bt, l, _ = x.shape
    tm = min(TOKEN_BLOCK, bt * l)
    depth = p["norm_mix_pre"].shape[0]
    kvw = N_KV * HEAD_DIM
    x2 = x.reshape(bt * l, D_MODEL)
    k_out, v_out, re_out, im_out, h_out = [], [], [], [], []
    h_all = kv_all = None
    if not prompt:
        cache_k = cache_k.reshape(cache_k.shape[:3] + (kvw,))
        cache_v = cache_v.reshape(cache_v.shape[:3] + (kvw,))
        st_re, st_im = (a.reshape(a.shape[:2] + (S5_GROUPS * S5_STATE,)) for a in (st_re, st_im))
    for layer in range(depth):
        idx = layer // 2
        g_pre = (p["norm_mix_pre"], layer)
        if layer % 2 == 0:
            u, q, k, v = _norm_proj(x2, g_pre, (p["w_in_even"], idx), (S5_WIDTH, SWA_WIDTH, kvw, kvw),
                                    min(PROJ_TOKEN_BLOCK, bt * l))
            prep = p["s5_prep"]
            d_skip, wglu = p["s5_d"], p["s5_w_glu"]
            q3, k3, v3 = q.reshape(bt, l, SWA_WIDTH), k.reshape(bt, l, kvw), v.reshape(bt, l, kvw)
            if prompt:
                a_out, h_fin = _s5_prompt(u.reshape(bt, l, S5_WIDTH), prep, idx, d_skip, wglu, min(TOKEN_BLOCK, l))
                h_fin = h_fin.reshape(bt, -1)
                b_out = _swa_prompt(q3, k3, v3, p["swa_sinks"][idx], p["t5_band"])
                kw, vw = k3[:, l - WINDOW:], v3[:, l - WINDOW:]
            else:
                a_out, h_re, h_im = _s5_sample(u.reshape(bt, l, S5_WIDTH), st_re, st_im, prep, idx, d_skip, wglu)
                h_re, h_im = (a.reshape(bt, S5_GROUPS, S5_STATE) for a in (h_re, h_im))
                b_out, kv_all = _swa_sample(q3, k3, v3, cache_k, cache_v, idx, kv_all, p["swa_sinks"][idx],
                                            p["t5_band"])
            if prompt:
                h_re, h_im = _s5_state_out(h_fin)
                k_out.append(kw.reshape(bt, WINDOW, N_KV, HEAD_DIM))
                v_out.append(vw.reshape(bt, WINDOW, N_KV, HEAD_DIM))
            re_out.append(h_re)
            im_out.append(h_im)
            mixer_outs = (a_out.reshape(bt * l, S5_WIDTH), b_out.reshape(bt * l, SWA_WIDTH))
            w_out = (p["w_out_even"], idx)
        else:
            lb, gn = (p["hgrn_lb"], idx), (p["hgrn_gnorm"], idx)
            if prompt:
                c_out, s_t = _hgrn_prompt(x2.reshape(bt, l, D_MODEL), g_pre, (p["w_in_odd"], idx), lb, gn,
                                          min(TOKEN_BLOCK, l))
                h_out.append(s_t.transpose(0, 1, 3, 2))
            else:
                proj4 = _norm_proj(x2, g_pre, (p["w_in_odd"], idx), (HG_W,) * 4, tm)
                c_out, h_all = _hgrn_sample(proj4, st_h, idx, h_all, lb, gn, l)
            mixer_outs = (c_out.reshape(bt * l, HG_W),)
            w_out = (p["w_out_odd"], idx)
        x2 = _post(x2, mixer_outs, w_out, (p["norm_mix_post"], layer), (p["norm_mlp_pre"], layer),
                   (p["norm_mlp_post"], layer), (p["w_up"], layer), (p["w_down"], layer), tm)
    if prompt:
        k_new, v_new, h_new = jnp.stack(k_out), jnp.stack(v_out), jnp.stack(h_out)
    else:
        k_new, v_new = (a.reshape(a.shape[:3] + (N_KV, HEAD_DIM)) for a in kv_all)
        h_new = h_all
    return x2.reshape(bt, l, D_MODEL), k_new, v_new, jnp.stack(re_out), jnp.stack(im_out), h_new


def kernel(x_prompt, x_sample, cache_swa_k, cache_swa_v, state_s5_re, state_s5_im, state_hgrn, t5_bias_table,
           norm_mix_pre, norm_mix_post, norm_mlp_pre, norm_mlp_post, w_in_even, w_out_even, s5_lambda_re,
           s5_lambda_im, s5_log_step, s5_b_re, s5_b_im, s5_c_re, s5_c_im, s5_d, s5_w_glu, swa_sinks, w_in_odd,
           w_out_odd, hgrn_lb_param, hgrn_gnorm, w_up, w_down):
    bf = lambda a: a.astype(BF16)
    q0, q1 = S5_WIDTH, S5_WIDTH + SWA_WIDTH
    w_in_even = jnp.concatenate([w_in_even[..., :q0], _pair_heads(w_in_even[..., q0:q1], 2), w_in_even[..., q1:]],
                                axis=-1)
    w_out_even = jnp.concatenate([w_out_even[:, :q0], _pair_heads(w_out_even[:, q0:], 1)], axis=1)
    rows = lambda a: a[:, None, :]
    params = dict(
        norm_mix_pre=rows(norm_mix_pre), norm_mix_post=rows(norm_mix_post), norm_mlp_pre=rows(norm_mlp_pre),
        norm_mlp_post=rows(norm_mlp_post), w_in_even=bf(w_in_even), w_out_even=bf(w_out_even), s5_d=rows(s5_d),
        s5_w_glu=bf(s5_w_glu), swa_sinks=swa_sinks, w_in_odd=bf(w_in_odd), w_out_odd=bf(w_out_odd),
        hgrn_gnorm=rows(hgrn_gnorm), w_up=bf(w_up), w_down=bf(w_down),
        t5_band=_t5_band(t5_bias_table),
        hgrn_lb=rows(_hgrn_lower_bounds(hgrn_lb_param)),
        s5_prep=_s5_prep(s5_lambda_re, s5_lambda_im, s5_log_step, s5_b_re, s5_b_im, s5_c_re, s5_c_im),
    )
    y_prompt, k_p, v_p, re_p, im_p, hg_p = _trunk(x_prompt, None, None, None, None, None, params)
    y_sample, k_s, v_s, re_s, im_s, hg_s = _trunk(x_sample, cache_swa_k, cache_swa_v, state_s5_re, state_s5_im,
                                                  state_hgrn, params)
    return (y_prompt, y_sample, k_p, v_p, k_s, v_s, re_p, im_p, re_s, im_s, hg_p, hg_s)
```

```python
import functools
import math

import numpy as np
import jax
import jax.numpy as jnp
from jax import lax
from jax.experimental import pallas as pl
from jax.experimental.pallas import tpu as pltpu

F32 = jnp.float32
BF16 = jnp.bfloat16

D_MODEL = 1024
S5_WIDTH = 512
S5_GROUPS = 32
S5_GROUP = 16
S5_STATE = 64
S5_HALF_STATES = 1024
S5_HALF_CH = 256
SWA_WIDTH = 512
HEAD_DIM = 64
N_HEADS = 8
N_KV = 2
KV_GROUP = N_HEADS // N_KV
WINDOW = 128
T5_BUCKETS = 32
T5_MAX_DIST = 128
HG_HEADS = 8
HG_DK = 128
HG_DV = 128
HG_W = HG_HEADS * HG_DK
D_FF = 4096
RMS_EPS = 1e-6
NEG_INF = -1e30
LOG_FLOOR = 1e-30

S5_CHUNK = 64
HG_CHUNK = 128
HG_BASE = 32
TOKEN_BLOCK = 512
PROJ_TOKEN_BLOCK = 1024
SWA_BLOCKS_PER_STEP = 8
MLP_FF_BLOCK = 1024
SAMPLE_SEQ_BLOCK = 8
SWA_SAMPLE_SEQ_BLOCK = 32
VMEM_LIMIT = 56 * 1024 * 1024
MXU_WIDTH = 256


def _cparams(*sem):
    return pltpu.CompilerParams(dimension_semantics=sem, vmem_limit_bytes=VMEM_LIMIT)


def _resident(shape):
    nd = len(shape)
    return pl.BlockSpec(shape, lambda *_: (0,) * nd, pipeline_mode=pl.Buffered(1))


def _layer_resident(stack, layer):
    nd = stack.ndim
    return pl.BlockSpec((None,) + stack.shape[1:], lambda *_: (layer,) + (0,) * (nd - 1),
                        pipeline_mode=pl.Buffered(1))


def _rms(x, w):
    return x * lax.rsqrt(jnp.mean(x * x, axis=-1, keepdims=True) + RMS_EPS) * w


def _sigmoid(x):
    return 1.0 / (1.0 + jnp.exp(-x))


def _silu(x):
    return x * _sigmoid(x)


def _gelu_tanh(x):
    return 0.5 * x * (1.0 + jnp.tanh(math.sqrt(2.0 / math.pi) * (x + 0.044715 * (x * x * x))))


def _dot(a, b):
    return jnp.dot(a, b, preferred_element_type=F32)


def _dot_nt(a, b):
    return lax.dot_general(a, b, (((1,), (1,)), ((), ())), preferred_element_type=F32)


def _dot_tn(a, b):
    return lax.dot_general(a, b, (((0,), (0,)), ((), ())), preferred_element_type=F32)


def _dot_exact(a, b):
    return jnp.dot(a, b, preferred_element_type=F32, precision=lax.Precision.HIGHEST)


def _norm_proj_body(x_ref, g_ref, w_ref, *o_refs, splits):
    xn = _rms(x_ref[...], g_ref[...]).astype(BF16)
    off = 0
    i = 0
    while i < len(splits):
        j = i + 1
        while sum(splits[i:j]) < MXU_WIDTH and j < len(splits):
            j += 1
        y = _dot(xn, w_ref[:, off:off + sum(splits[i:j])])
        sub = 0
        for o_ref, wd in zip(o_refs[i:j], splits[i:j]):
            o_ref[...] = y[:, sub:sub + wd]
            sub += wd
        off += sub
        i = j


def _norm_proj(x2d, g, w_bf, splits, tm):
    t, n = x2d.shape[0], w_bf[0].shape[-1]
    return pl.pallas_call(
        functools.partial(_norm_proj_body, splits=splits),
        grid=(t // tm,),
        in_specs=[pl.BlockSpec((tm, D_MODEL), lambda i: (i, 0)),
                  _layer_resident(*g), _layer_resident(*w_bf)],
        out_specs=[pl.BlockSpec((tm, wd), lambda i: (i, 0)) for wd in splits],
        out_shape=[jax.ShapeDtypeStruct((t, wd), F32) for wd in splits],
        compiler_params=_cparams("parallel"),
        name="norm_proj",
    )(x2d, g[0], w_bf[0])


def _post_body(*refs, widths):
    x_ref = refs[0]
    in_refs = refs[1:1 + len(widths)]
    wout_ref, gpost_ref, gpre_ref, gmpost_ref, wup_ref, wdn_ref, o_ref = refs[1 + len(widths):]
    mix = None
    off = 0
    for r, wd in zip(in_refs, widths):
        t = _dot(r[...].astype(BF16), wout_ref[off:off + wd, :])
        mix = t if mix is None else mix + t
        off += wd
    x1 = x_ref[...] + _rms(mix, gpost_ref[...])
    o_ref[...] = x1
    hn = _rms(x1, gpre_ref[...]).astype(BF16)
    acc = None
    for c in range(D_FF // MLP_FF_BLOCK):
        cs = slice(c * MLP_FF_BLOCK, (c + 1) * MLP_FF_BLOCK)
        hk = jnp.square(jnp.maximum(_dot(hn, wup_ref[:, cs]), 0.0)).astype(BF16)
        t = _dot(hk, wdn_ref[cs, :])
        acc = t if acc is None else acc + t
    o_ref[...] += _rms(acc, gmpost_ref[...])


def _post_stream_body(*refs, widths):
    x_ref = refs[0]
    in_refs = refs[1:1 + len(widths)]
    wout_ref, gpost_ref, gpre_ref, gmpost_ref, wup_ref, wdn_ref, o_ref, x1_scr, hn_scr, acc_scr = refs[1 + len(widths):]
    c = pl.program_id(0)

    @pl.when(c == 0)
    def _():
        mix = None
        off = 0
        for r, wd in zip(in_refs, widths):
            t = _dot(r[...].astype(BF16), wout_ref[off:off + wd, :])
            mix = t if mix is None else mix + t
            off += wd
        x1 = x_ref[...] + _rms(mix, gpost_ref[...])
        x1_scr[...] = x1
        hn_scr[...] = _rms(x1, gpre_ref[...]).astype(BF16)
        acc_scr[...] = jnp.zeros_like(acc_scr)

    hk = jnp.square(jnp.maximum(_dot(hn_scr[...], wup_ref[...]), 0.0)).astype(BF16)
    acc_scr[...] += _dot(hk, wdn_ref[...])

    @pl.when(c == pl.num_programs(0) - 1)
    def _():
        o_ref[...] = x1_scr[...] + _rms(acc_scr[...], gmpost_ref[...])


def _post(x2d, mixer_outs, wout_bf, gpost, gpre, gmpost, wup_bf, wdn_bf, tm):
    t = x2d.shape[0]
    widths = tuple(a.shape[1] for a in mixer_outs)
    params = [wout_bf, gpost, gpre, gmpost, wup_bf, wdn_bf]
    stacks = [a for a, _ in params]
    if t == tm:
        whole = lambda wd: pl.BlockSpec((t, wd), lambda c: (0, 0))
        slab = lambda pr, blk, idx: pl.BlockSpec((None,) + blk, lambda c: (pr[1],) + idx(c))
        vec = lambda pr: slab(pr, (1, D_MODEL), lambda c: (0, 0))
        return pl.pallas_call(
            functools.partial(_post_stream_body, widths=widths),
            grid=(D_FF // MLP_FF_BLOCK,),
            in_specs=[whole(D_MODEL)] + [whole(wd) for wd in widths] + [
                slab(wout_bf, (D_MODEL, D_MODEL), lambda c: (0, 0)), vec(gpost), vec(gpre), vec(gmpost),
                slab(wup_bf, (D_MODEL, MLP_FF_BLOCK), lambda c: (0, c)),
                slab(wdn_bf, (MLP_FF_BLOCK, D_MODEL), lambda c: (c, 0))],
            out_specs=whole(D_MODEL),
            out_shape=jax.ShapeDtypeStruct((t, D_MODEL), F32),
            scratch_shapes=[pltpu.VMEM((t, D_MODEL), F32), pltpu.VMEM((t, D_MODEL), BF16),
                            pltpu.VMEM((t, D_MODEL), F32)],
            compiler_params=_cparams("arbitrary"),
            name="post_mlp_stream",
        )(x2d, *mixer_outs, *stacks)
    row = lambda wd: pl.BlockSpec((tm, wd), lambda i: (i, 0))
    return pl.pallas_call(
        functools.partial(_post_body, widths=widths),
        grid=(t // tm,),
        in_specs=[row(D_MODEL)] + [row(wd) for wd in widths] + [_layer_resident(*pr) for pr in params],
        out_specs=row(D_MODEL),
        out_shape=jax.ShapeDtypeStruct((t, D_MODEL), F32),
        compiler_params=_cparams("parallel"),
        name="post_mlp",
    )(x2d, *mixer_outs, *stacks)


def _bucket_map():
    t = np.arange(WINDOW)[:, None]
    m = np.arange(2 * WINDOW)[None, :]
    n = np.maximum(WINDOW + t - m, 0)
    max_exact = T5_BUCKETS // 2
    large = max_exact + np.floor(np.log(np.maximum(n, max_exact) / max_exact)
                                 / math.log(T5_MAX_DIST / max_exact) * (T5_BUCKETS - max_exact)).astype(np.int64)
    large = np.minimum(large, T5_BUCKETS - 1)
    return np.where(n < max_exact, n, large).astype(np.int32)


def _bias_body(tab_ref, bm_ref, o_ref):
    bm = bm_ref[...]
    row = lax.broadcasted_iota(jnp.int32, bm.shape, 0)
    col = lax.broadcasted_iota(jnp.int32, bm.shape, 1)
    dist = WINDOW + row - col
    in_win = (dist >= 0) & (dist < WINDOW)
    for h in range(N_HEADS):
        acc = jnp.zeros(bm.shape, F32)
        for b in range(T5_BUCKETS):
            acc = jnp.where(bm == b, tab_ref[b, h], acc)
        o_ref[0, h] = jnp.where(in_win & (col >= WINDOW), acc, NEG_INF)
        o_ref[1, h] = jnp.where(in_win, acc, NEG_INF)


def _t5_band(table):
    return pl.pallas_call(
        _bias_body,
        in_specs=[pl.BlockSpec(memory_space=pltpu.SMEM),
                  pl.BlockSpec((WINDOW, 2 * WINDOW), lambda: (0, 0))],
        out_specs=pl.BlockSpec((2, N_HEADS, WINDOW, 2 * WINDOW), lambda: (0, 0, 0, 0)),
        out_shape=jax.ShapeDtypeStruct((2, N_HEADS, WINDOW, 2 * WINDOW), F32),
        name="t5_band",
    )(table, jnp.asarray(_bucket_map()))


def _pair_heads(a, axis):
    shp = a.shape
    a = a.reshape(shp[:axis] + (N_KV, KV_GROUP, HEAD_DIM) + shp[axis + 1:])
    a = jnp.swapaxes(a, axis, axis + 1)
    return a.reshape(shp)


def _swa_prompt_body(sink_ref, q_ref, kc_ref, kp_ref, vc_ref, vp_ref, band_ref, o_ref, *, nsub):
    tile = N_KV * HEAD_DIM
    lo = lax.broadcasted_iota(jnp.int32, (WINDOW, tile), 1) < HEAD_DIM
    first_band = jnp.minimum(pl.program_id(1), 1)
    for sb in range(nsub):
        rows = slice(sb * WINDOW, (sb + 1) * WINDOW)
        before = slice((sb - 1) * WINDOW, sb * WINDOW)
        k_prev = kp_ref[...] if sb == 0 else kc_ref[before, :]
        v_prev = vp_ref[...] if sb == 0 else vc_ref[before, :]
        kb = jnp.concatenate([k_prev, kc_ref[rows, :]], axis=0).astype(BF16)
        vb = jnp.concatenate([v_prev, vc_ref[rows, :]], axis=0).astype(BF16)
        v_ext = jnp.concatenate([vb, jnp.ones_like(vb)], axis=1)
        for t in range(KV_GROUP):
            qt = q_ref[rows, t * tile:(t + 1) * tile] * (HEAD_DIM ** -0.5)
            halves = []
            for kv in range(N_KV):
                h = kv * KV_GROUP + t
                qh = jnp.where(lo if kv == 0 else jnp.logical_not(lo), qt, 0.0).astype(BF16)
                s = _dot_nt(qh, kb) + (band_ref[first_band, h] if sb == 0 else band_ref[1, h])
                sink = sink_ref[h]
                m = jnp.maximum(jnp.max(s, axis=-1, keepdims=True), sink)
                oe = _dot(jnp.exp(s - m).astype(BF16), v_ext)
                halves.append(oe[:, 0:tile] / (oe[:, tile:tile + 1] + jnp.exp(sink - m)))
            o_ref[rows, t * tile:(t + 1) * tile] = jnp.where(lo, halves[0], halves[1])


def _swa_prompt(q, k, v, sinks, band):
    b, l, _ = q.shape
    kvw = N_KV * HEAD_DIM
    nsub = min(SWA_BLOCKS_PER_STEP, l // WINDOW)
    rows = nsub * WINDOW
    cur = lambda w: pl.BlockSpec((None, rows, w), lambda i, n: (i, n, 0))
    prev = lambda w: pl.BlockSpec((None, WINDOW, w), lambda i, n: (i, jnp.maximum(n * nsub - 1, 0), 0))
    return pl.pallas_call(
        functools.partial(_swa_prompt_body, nsub=nsub),
        grid=(b, l // rows),
        in_specs=[pl.BlockSpec(memory_space=pltpu.SMEM), cur(SWA_WIDTH), cur(kvw), prev(kvw), cur(kvw), prev(kvw),
                  _resident(band.shape)],
        out_specs=cur(SWA_WIDTH),
        out_shape=jax.ShapeDtypeStruct((b, l, SWA_WIDTH), F32),
        compiler_params=_cparams("parallel", "arbitrary"),
        name="swa_prompt",
    )(sinks, q, k, k, v, v, band)


def _swa_sample_body(q_ref, kn_ref, vn_ref, ck_ref, cv_ref, bias_ref, sink_ref, *rest, steps, layer, creates):
    o_ref, ko_all, vo_all = rest[-3:]
    ko_ref = ko_all.at[layer] if creates else ko_all
    vo_ref = vo_all.at[layer] if creates else vo_all
    if creates:
        for other in range(ko_all.shape[0]):
            if other != layer:
                ko_all[other] = jnp.zeros(ko_all.shape[1:], F32)
                vo_all[other] = jnp.zeros(vo_all.shape[1:], F32)
    tile = N_KV * HEAD_DIM
    ck, cv, kn, vn = ck_ref[...], cv_ref[...], kn_ref[...], vn_ref[...]
    ko_ref[:, 0:WINDOW - steps, :] = ck[:, steps:, :]
    ko_ref[:, WINDOW - steps:, :] = kn
    vo_ref[:, 0:WINDOW - steps, :] = cv[:, steps:, :]
    vo_ref[:, WINDOW - steps:, :] = vn
    lo = lax.broadcasted_iota(jnp.int32, (1, 1, tile), 2) < HEAD_DIM
    q = q_ref[...] * (HEAD_DIM ** -0.5)
    qs = []
    for kv in range(N_KV):
        for t in range(KV_GROUP):
            qs.append(jnp.where(lo if kv == 0 else jnp.logical_not(lo), q[:, :, t * tile:(t + 1) * tile], 0.0))
    qall = jnp.concatenate(qs, axis=1).astype(BF16)
    bias = bias_ref[...]
    sink = sink_ref[...][None]
    bdot = lambda a, b_, spec: jnp.einsum(spec, a, b_, preferred_element_type=F32)
    sc = bdot(qall, ck.astype(BF16), "bqd,bkd->bqk") + bias[None, :, 0:WINDOW]
    sn = bdot(qall, kn.astype(BF16), "bqd,bkd->bqk") + bias[None, :, WINDOW:WINDOW + steps]
    m = jnp.maximum(jnp.maximum(jnp.max(sc, axis=-1, keepdims=True), jnp.max(sn, axis=-1, keepdims=True)), sink)
    ec, en = jnp.exp(sc - m), jnp.exp(sn - m)
    den = jnp.sum(ec, axis=-1, keepdims=True) + jnp.sum(en, axis=-1, keepdims=True) + jnp.exp(sink - m)
    o = (bdot(ec.astype(BF16), cv.astype(BF16), "bqk,bkd->bqd")
         + bdot(en.astype(BF16), vn.astype(BF16), "bqk,bkd->bqd")) / den
    for t in range(KV_GROUP):
        o_ref[:, :, t * tile:(t + 1) * tile] = jnp.where(
            lo, o[:, t * steps:(t + 1) * steps, :], o[:, (KV_GROUP + t) * steps:(KV_GROUP + t + 1) * steps, :])


def _swa_sample(q, kn, vn, cache_k, cache_v, idx, new_caches, sinks, band):
    b, steps, _ = q.shape
    kvw = N_KV * HEAD_DIM
    bb = min(SWA_SAMPLE_SEQ_BLOCK, b)
    nrow = N_HEADS * steps
    creates = new_caches is None
    bias = band[1, :, 0:steps, :].reshape(nrow, 2 * WINDOW)
    sink_rows = jnp.repeat(sinks, steps).reshape(nrow, 1)
    blk = lambda r, w: pl.BlockSpec((bb, r, w), lambda i: (i, 0, 0))
    cache = pl.BlockSpec((None, bb, WINDOW, kvw), lambda i: (idx, i, 0, 0))
    cache_out = (pl.BlockSpec((cache_k.shape[0], bb, WINDOW, kvw), lambda i: (0, i, 0, 0)) if creates else cache)
    in_specs = [blk(steps, SWA_WIDTH), blk(steps, kvw), blk(steps, kvw), cache, cache,
                _resident((nrow, 2 * WINDOW)), _resident((nrow, 1))]
    args = [q, kn, vn, cache_k, cache_v, bias, sink_rows]
    aliases = {}
    if not creates:
        in_specs += [pl.BlockSpec(memory_space=pl.ANY)] * 2
        args += list(new_caches)
        aliases = {len(args) - 2: 1, len(args) - 1: 2}
    o, k_new, v_new = pl.pallas_call(
        functools.partial(_swa_sample_body, steps=steps, layer=idx, creates=creates),
        grid=(b // bb,),
        in_specs=in_specs,
        out_specs=[blk(steps, SWA_WIDTH), cache_out, cache_out],
        out_shape=[jax.ShapeDtypeStruct((b, steps, SWA_WIDTH), F32),
                   jax.ShapeDtypeStruct(cache_k.shape, F32), jax.ShapeDtypeStruct(cache_v.shape, F32)],
        input_output_aliases=aliases,
        compiler_params=_cparams("parallel"),
        name="swa_sample",
    )(*args)
    return o, (k_new, v_new)


def _s5_prep_body(lr_ref, li_ref, ls_ref, br_ref, bi_ref, cr_ref, ci_ref, bm_ref, cm_ref, pw_ref, pinv_ref):
    hs = S5_HALF_STATES
    lr, li = lr_ref[...], li_ref[...]
    dt = jnp.exp(ls_ref[...])
    mag = jnp.exp(lr * dt)
    ang = li * dt
    ar = mag * jnp.cos(ang)
    ai = mag * jnp.sin(ang)
    den = lr * lr + li * li
    nr = ar - 1.0
    kr = (nr * lr + ai * li) / den
    ki = (ai * lr - nr * li) / den
    hc = S5_HALF_CH
    own = (lax.broadcasted_iota(jnp.int32, (hc, hs), 0) // S5_GROUP
           == lax.broadcasted_iota(jnp.int32, (hc, hs), 1) // S5_STATE)
    blockdiag = lambda a: jnp.where(own, jnp.concatenate([a] * (hc // S5_GROUP), axis=0), 0.0)
    br, bi = br_ref[...], bi_ref[...]
    bm_ref[:, 0:hs] = blockdiag(kr * br - ki * bi).astype(BF16)
    bm_ref[:, hs:2 * hs] = blockdiag(kr * bi + ki * br).astype(BF16)
    cm_ref[0:hs, :] = blockdiag(cr_ref[...]).T.astype(BF16)
    cm_ref[hs:2 * hs, :] = blockdiag(-ci_ref[...]).T.astype(BF16)
    n2 = ar * ar + ai * ai
    ir, ii = ar / n2, -ai / n2
    pr, pi = jnp.ones_like(ar), jnp.zeros_like(ar)
    qr, qi = pr, pi
    for j in range(S5_CHUNK):
        pw_ref[0, j:j + 1, :] = pr
        pw_ref[1, j:j + 1, :] = pi
        pinv_ref[0, j:j + 1, :] = qr
        pinv_ref[1, j:j + 1, :] = qi
        pr, pi = pr * ar - pi * ai, pr * ai + pi * ar
        qr, qi = qr * ir - qi * ii, qr * ii + qi * ir


def _s5_prep(lam_re, lam_im, log_step, b_re, b_im, c_re, c_im):
    hs, hc = S5_HALF_STATES, S5_HALF_CH
    gh = S5_GROUPS // 2
    nh = 2 * lam_re.shape[0]
    vec = lambda a: a.reshape(nh, 1, hs)
    ls = jnp.broadcast_to(log_step[..., None], log_step.shape + (S5_STATE,))
    b_rows = lambda b: b.reshape(nh, gh, S5_STATE, S5_GROUP).transpose(0, 3, 1, 2).reshape(nh, S5_GROUP, hs)
    c_rows = lambda c: c.reshape(nh, gh, S5_GROUP, S5_STATE).transpose(0, 2, 1, 3).reshape(nh, S5_GROUP, hs)

    vspec = pl.BlockSpec((None, 1, hs), lambda h: (h, 0, 0))
    rspec = pl.BlockSpec((None, S5_GROUP, hs), lambda h: (h, 0, 0))
    pspec = pl.BlockSpec((None, 2, S5_CHUNK, hs), lambda h: (h, 0, 0, 0))
    return pl.pallas_call(
        _s5_prep_body,
        grid=(nh,),
        in_specs=[vspec, vspec, vspec, rspec, rspec, rspec, rspec],
        out_specs=[pl.BlockSpec((None, hc, 2 * hs), lambda h: (h, 0, 0)),
                   pl.BlockSpec((None, 2 * hs, hc), lambda h: (h, 0, 0)), pspec, pspec],
        out_shape=[jax.ShapeDtypeStruct((nh, hc, 2 * hs), BF16), jax.ShapeDtypeStruct((nh, 2 * hs, hc), BF16),
                   jax.ShapeDtypeStruct((nh, 2, S5_CHUNK, hs), F32), jax.ShapeDtypeStruct((nh, 2, S5_CHUNK, hs), F32)],
        compiler_params=_cparams("parallel"),
        name="s5_prep",
    )(vec(lam_re), vec(lam_im), vec(ls), b_rows(b_re), b_rows(b_im), c_rows(c_re), c_rows(c_im))


def _s5_glu(y, u, d_ref, wg_ref):
    g = _gelu_tanh(y + d_ref[...] * u)
    return g * _sigmoid(_dot(g.astype(BF16), wg_ref[...]))


def _s5_prompt_body(u_ref, bm_ref, cm_ref, pw_ref, pinv_ref, d_ref, wg_ref, o_ref, hfin_ref, bu_scr, hb_scr,
                    h_scr, *, tb):
    hs, hc, c0 = S5_HALF_STATES, S5_HALF_CH, S5_CHUNK

    @pl.when(pl.program_id(1) == 0)
    def _():
        h_scr[...] = jnp.zeros_like(h_scr)

    u = u_ref[...]
    ub = u.astype(BF16)
    tril = (lax.broadcasted_iota(jnp.int32, (c0, c0), 0) >= lax.broadcasted_iota(jnp.int32, (c0, c0), 1)).astype(BF16)
    ys = []
    nchunk = tb // c0
    last = c0 - 1
    for hf in range(2):
        re = slice(2 * hf * hs, (2 * hf + 1) * hs)
        im = slice((2 * hf + 1) * hs, (2 * hf + 2) * hs)
        both = slice(2 * hf * hs, (2 * hf + 2) * hs)
        bu_scr[:, both] = _dot(ub[:, hf * hc:(hf + 1) * hc], bm_ref[hf])
        for ci in range(nchunk):
            rows = slice(ci * c0, (ci + 1) * c0)
            bur, bui = bu_scr[rows, re], bu_scr[rows, im]
            qr, qi = pinv_ref[hf, 0], pinv_ref[hf, 1]
            bu_scr[rows, re] = _dot(tril, (qr * bur - qi * bui).astype(BF16))
            bu_scr[rows, im] = _dot(tril, (qr * bui + qi * bur).astype(BF16))
        ar, ai = pw_ref[hf, 0, 1:2, :], pw_ref[hf, 1, 1:2, :]
        er, ei = pw_ref[hf, 0, last:c0, :], pw_ref[hf, 1, last:c0, :]
        hr0, hi0 = h_scr[:, re], h_scr[:, im]
        carries = []
        for ci in range(nchunk):
            gr, gi = ar * hr0 - ai * hi0, ar * hi0 + ai * hr0
            carries.append((gr, gi))
            sr = bu_scr[ci * c0 + last:(ci + 1) * c0, re] + gr
            si = bu_scr[ci * c0 + last:(ci + 1) * c0, im] + gi
            hr0, hi0 = er * sr - ei * si, er * si + ei * sr
        h_scr[:, re] = hr0
        h_scr[:, im] = hi0
        for ci in range(nchunk):
            rows = slice(ci * c0, (ci + 1) * c0)
            sr = bu_scr[rows, re] + carries[ci][0]
            si = bu_scr[rows, im] + carries[ci][1]
            pr, pi = pw_ref[hf, 0], pw_ref[hf, 1]
            hb_scr[rows, re] = (pr * sr - pi * si).astype(BF16)
            hb_scr[rows, im] = (pr * si + pi * sr).astype(BF16)
        ys.append(_dot(hb_scr[:, both], cm_ref[hf]))
    o_ref[...] = _s5_glu(jnp.concatenate(ys, axis=1), u, d_ref, wg_ref)
    hfin_ref[...] = h_scr[...]


def _layer_halves(a, layer):
    return pl.BlockSpec((2,) + a.shape[1:], lambda *_: (layer,) + (0,) * (a.ndim - 1), pipeline_mode=pl.Buffered(1))


def _s5_prompt(u, prep, layer, d_skip, wglu_bf, tb):
    b, l, _ = u.shape
    bm, cm, pw, pinv = prep
    ns = 4 * S5_HALF_STATES
    return pl.pallas_call(
        functools.partial(_s5_prompt_body, tb=tb),
        grid=(b, l // tb),
        in_specs=[pl.BlockSpec((None, tb, S5_WIDTH), lambda i, j: (i, j, 0)),
                  *[_layer_halves(a, layer) for a in (bm, cm, pw, pinv)],
                  _layer_resident(d_skip, layer), _layer_resident(wglu_bf, layer)],
        out_specs=[pl.BlockSpec((None, tb, S5_WIDTH), lambda i, j: (i, j, 0)),
                   pl.BlockSpec((None, 1, ns), lambda i, j: (i, 0, 0))],
        out_shape=[jax.ShapeDtypeStruct((b, l, S5_WIDTH), F32), jax.ShapeDtypeStruct((b, 1, ns), F32)],
        scratch_shapes=[pltpu.VMEM((tb, ns), F32), pltpu.VMEM((tb, ns), BF16), pltpu.VMEM((1, ns), F32)],
        compiler_params=_cparams("parallel", "arbitrary"),
        name="s5_prompt",
    )(u, bm, cm, pw, pinv, d_skip, wglu_bf)


def _s5_sample_body(u_ref, re_ref, im_ref, bm_ref, cm_ref, pw_ref, d_ref, wg_ref, o_ref, re_out_ref, im_out_ref, *,
                    steps):
    hs, hc = S5_HALF_STATES, S5_HALF_CH
    h = [re_ref[:, 0:hs], im_ref[:, 0:hs], re_ref[:, hs:2 * hs], im_ref[:, hs:2 * hs]]
    for t in range(steps):
        u = u_ref[:, t, :]
        ub = u.astype(BF16)
        ys = []
        for hf in range(2):
            bu = _dot(ub[:, hf * hc:(hf + 1) * hc], bm_ref[hf])
            ar, ai = pw_ref[hf, 0, 1:2, :], pw_ref[hf, 1, 1:2, :]
            hr, hi = h[2 * hf], h[2 * hf + 1]
            h[2 * hf] = ar * hr - ai * hi + bu[:, 0:hs]
            h[2 * hf + 1] = ar * hi + ai * hr + bu[:, hs:2 * hs]
            ys.append(_dot(jnp.concatenate([h[2 * hf], h[2 * hf + 1]], axis=1).astype(BF16), cm_ref[hf]))
        o_ref[:, t, :] = _s5_glu(jnp.concatenate(ys, axis=1), u, d_ref, wg_ref)
    for hf in range(2):
        re_out_ref[:, hf * hs:(hf + 1) * hs] = h[2 * hf]
        im_out_ref[:, hf * hs:(hf + 1) * hs] = h[2 * hf + 1]


def _s5_sample(u, st_re, st_im, prep, layer, d_skip, wglu_bf):
    b, steps, _ = u.shape
    bm, cm, pw, _ = prep
    ns = 2 * S5_HALF_STATES
    full = lambda shape: pl.BlockSpec(shape, lambda i: (0,) * len(shape))
    halves = lambda a: pl.BlockSpec((2,) + a.shape[1:], lambda i: (layer,) + (0,) * (a.ndim - 1))
    state = pl.BlockSpec((None, b, ns), lambda i: (layer, 0, 0))
    return pl.pallas_call(
        functools.partial(_s5_sample_body, steps=steps),
        grid=(1,),
        in_specs=[full(u.shape), state, state, halves(bm), halves(cm), halves(pw),
                  _layer_resident(d_skip, layer), _layer_resident(wglu_bf, layer)],
        out_specs=[full(u.shape), full((b, ns)), full((b, ns))],
        out_shape=[jax.ShapeDtypeStruct(u.shape, F32), jax.ShapeDtypeStruct((b, ns), F32),
                   jax.ShapeDtypeStruct((b, ns), F32)],
        compiler_params=_cparams("arbitrary"),
        name="s5_sample",
    )(u, st_re, st_im, bm, cm, pw, d_skip, wglu_bf)


def _s5_state_out(h):
    b = h.shape[0]
    h4 = h.reshape(b, 2, 2, S5_HALF_STATES)
    return (h4[:, :, 0].reshape(b, S5_GROUPS, S5_STATE), h4[:, :, 1].reshape(b, S5_GROUPS, S5_STATE))


def _lb_body(p_ref, o_ref):
    p = p_ref[...]
    e = jnp.exp(p - jnp.max(p, axis=0, keepdims=True))
    sm = e / jnp.sum(e, axis=0, keepdims=True)
    acc = jnp.zeros_like(sm[0:1])
    for i in range(p.shape[0]):
        acc = acc + sm[i:i + 1]
        o_ref[i:i + 1, :] = acc - sm[0:1]


def _hgrn_lower_bounds(lb_param):
    n = lb_param.shape[0]
    return pl.pallas_call(
        _lb_body,
        in_specs=[pl.BlockSpec((n, HG_W), lambda: (0, 0))],
        out_specs=pl.BlockSpec((n, HG_W), lambda: (0, 0)),
        out_shape=jax.ShapeDtypeStruct((n, HG_W), F32),
        name="hgrn_lower_bounds",
    )(lb_param)


def _hgrn_forget(fz, lb):
    log_f = jnp.log(jnp.maximum(lb, LOG_FLOOR) + (1.0 - lb) * _sigmoid(fz))
    return log_f, (1.0 - lb) * _sigmoid(-fz)


def _group_norm_gate(o, gn, gate):
    return o * lax.rsqrt(jnp.mean(o * o, axis=-1, keepdims=True) + RMS_EPS) * gn * gate


def _cumsum_rows(tril_bf, x):
    hi = x.astype(BF16)
    r1 = x - hi.astype(F32)
    mid = r1.astype(BF16)
    lo = (r1 - mid.astype(F32)).astype(BF16)
    return _dot(tril_bf, hi) + _dot(tril_bf, mid) + _dot(tril_bf, lo)


def _hgrn_prompt_body(x_ref, g_ref, w_ref, lb_ref, gn_ref, o_ref, st_ref, q_scr, k_scr, b_scr, v_scr, s_scr, *, tb):
    c, w = HG_CHUNK, HG_W

    @pl.when(pl.program_id(1) == 0)
    def _():
        s_scr[...] = jnp.zeros_like(s_scr)

    row = lax.broadcasted_iota(jnp.int32, (c, c), 0)
    col = lax.broadcasted_iota(jnp.int32, (c, c), 1)
    tril = (row >= col).astype(BF16)
    xn = _rms(x_ref[...], g_ref[...]).astype(BF16)
    log_f, kf = _hgrn_forget(_dot(xn, w_ref[:, w:2 * w]), lb_ref[...])
    k_scr[...] = kf
    for ci in range(tb // c):
        b_scr[ci * c:(ci + 1) * c, :] = _cumsum_rows(tril, log_f[ci * c:(ci + 1) * c, :])
    q_scr[...] = _silu(_dot(xn, w_ref[:, 0:w]))
    v_scr[...] = _dot(xn, w_ref[:, 2 * w:3 * w]).astype(BF16)
    o_ref[...] = _silu(_dot(xn, w_ref[:, 3 * w:4 * w]))

    masks = [(row // HG_BASE == col // HG_BASE) & (col <= row)]
    strides = []
    s = 2 * HG_BASE
    while s <= c:
        strides.append(s)
        masks.append(row // s == col // s)
        s *= 2
    masks_all = [None if st == c else jnp.concatenate([mk] * HG_HEADS, axis=0)
                 for st, mk in zip([HG_BASE] + strides, masks)]
    cat = lambda parts: jnp.concatenate(parts, axis=0)

    def chunk(ci, carry):
        rows = pl.ds(pl.multiple_of(ci * c, c), c)
        q, k, vb = q_scr[rows, :], k_scr[rows, :], v_scr[rows, :]
        b = b_scr[rows, :]
        b_last = b[c - 1:c, :]
        q0, k0, q_in, k_st = [], [], [], []
        for m in range(c // HG_BASE):
            blk = slice(m * HG_BASE, (m + 1) * HG_BASE)
            mid = b[m * HG_BASE + HG_BASE // 2:m * HG_BASE + HG_BASE // 2 + 1, :]
            d = b[blk, :] - mid
            q0.append(q[blk, :] * jnp.exp(d))
            k0.append(k[blk, :] * jnp.exp(-d))
            q_in.append(q0[-1] * jnp.exp(mid))
            k_st.append(k0[-1] * jnp.exp(b_last - mid))
        qs, ks = [cat(q0).astype(BF16)], [cat(k0).astype(BF16)]
        q_in, k_st = cat(q_in).astype(BF16), cat(k_st).astype(BF16)
        for st in strides:
            ql, kl = [], []
            zeros = jnp.zeros((st // 2, w), BF16)
            for m in range(c // st):
                lower = slice(m * st, m * st + st // 2)
                upper = slice(m * st + st // 2, (m + 1) * st)
                mid = b[m * st + st // 2:m * st + st // 2 + 1, :]
                ql += [zeros, (q[upper, :] * jnp.exp(b[upper, :] - mid)).astype(BF16)]
                kl += [(k[lower, :] * jnp.exp(mid - b[lower, :])).astype(BF16), zeros]
            qs.append(cat(ql))
            ks.append(cat(kl))
        e_last = jnp.exp(b_last)
        heads = [slice(h * HG_DK, (h + 1) * HG_DK) for h in range(HG_HEADS)]
        a = None
        for qh, kh, mk in zip(qs, ks, masks_all):
            t = cat([_dot_nt(qh[:, hs], kh[:, hs]) for hs in heads])
            if mk is not None:
                t = jnp.where(mk, t, 0.0)
            a = t if a is None else a + t
        a = a.astype(BF16)
        states = [s_scr[h] for h in range(HG_HEADS)]
        o = jnp.concatenate([_dot(a[h * c:(h + 1) * c, :], vb[:, hs]) + _dot_nt(q_in[:, hs], states[h].astype(BF16))
                             for h, hs in enumerate(heads)], axis=0)
        gate = jnp.concatenate([o_ref[rows, hs] for hs in heads], axis=0)
        out = _group_norm_gate(o, gn_ref[...], gate)
        for h, hs in enumerate(heads):
            s_scr[h] = states[h] * e_last[:, hs] + _dot_tn(vb[:, hs], k_st[:, hs])
            o_ref[rows, hs] = out[h * c:(h + 1) * c, :]
        return carry

    lax.fori_loop(0, tb // c, chunk, 0, unroll=True)
    st_ref[...] = s_scr[...]


def _hgrn_prompt(x, g, w_bf, lb, gn, tb):
    b, l, _ = x.shape
    row = pl.BlockSpec((None, tb, D_MODEL), lambda i, j: (i, j, 0))
    return pl.pallas_call(
        functools.partial(_hgrn_prompt_body, tb=tb),
        grid=(b, l // tb),
        in_specs=[row, _layer_resident(*g), _layer_resident(*w_bf), _layer_resident(*lb), _layer_resident(*gn)],
        out_specs=[row, pl.BlockSpec((None, HG_HEADS, HG_DV, HG_DK), lambda i, j: (i, 0, 0, 0))],
        out_shape=[jax.ShapeDtypeStruct((b, l, HG_W), F32),
                   jax.ShapeDtypeStruct((b, HG_HEADS, HG_DV, HG_DK), F32)],
        scratch_shapes=[pltpu.VMEM((tb, HG_W), F32)] * 3 + [pltpu.VMEM((tb, HG_W), BF16),
                                                             pltpu.VMEM((HG_HEADS, HG_DV, HG_DK), F32)],
        compiler_params=_cparams("parallel", "arbitrary"),
        name="hgrn_prompt",
    )(x, g[0], w_bf[0], lb[0], gn[0])


def _hgrn_sample_body(q_ref, fz_ref, iv_ref, gz_ref, s_ref, lb_ref, gn_ref, *rest, steps, bb, layer, creates):
    o_ref, so_all = rest[-2:]
    so_ref = so_all.at[layer] if creates else so_all
    if creates:
        for other in range(so_all.shape[0]):
            if other != layer:
                so_all[other] = jnp.zeros(so_all.shape[1:], F32)
    n = bb * steps
    qf = _silu(q_ref[...])
    log_f, kf = _hgrn_forget(fz_ref[...], lb_ref[...])
    v = iv_ref[...]
    gate = _silu(gz_ref[...])
    row = lax.broadcasted_iota(jnp.int32, (n, n), 0)
    col = lax.broadcasted_iota(jnp.int32, (n, n), 1)
    same = row // steps == col // steps
    causal = same & (col <= row)
    b = _dot_exact(causal.astype(F32), log_f)
    total = _dot_exact(same.astype(F32), log_f)
    first = (same & (col % steps == 0)).astype(F32)
    e0 = b - _dot_exact(first, log_f)
    q0 = (qf * jnp.exp(e0)).astype(BF16)
    k0 = (kf * jnp.exp(-e0)).astype(BF16)
    q_in = (qf * jnp.exp(b)).astype(BF16)
    k_st = (kf * jnp.exp(total - b)).astype(BF16)
    vb = v.astype(BF16)
    pick = (lax.broadcasted_iota(jnp.int32, (bb, n), 1) == steps * lax.broadcasted_iota(jnp.int32, (bb, n), 0))
    e_seq = jnp.exp(_dot_exact(pick.astype(F32), total))
    e_rows = jnp.concatenate([e_seq[:, h * HG_DK:(h + 1) * HG_DK] for h in range(HG_HEADS)], axis=0)
    pad = HG_DK - e_rows.shape[0]
    if pad:
        e_rows = jnp.concatenate([e_rows, jnp.zeros((pad, HG_DK), F32)], axis=0)
    e_cols = e_rows.T
    for h in range(HG_HEADS):
        hs = slice(h * HG_DK, (h + 1) * HG_DK)
        a = jnp.where(causal, _dot_nt(q0[:, hs], k0[:, hs]), 0.0)
        o_intra = _dot(a.astype(BF16), vb[:, hs])
        outs = []
        for i in range(bb):
            rs = slice(i * steps, (i + 1) * steps)
            s_old = s_ref[i, h]
            outs.append(_dot(q_in[rs, hs], s_old.astype(BF16)))
            so_ref[i, h] = (s_old * e_cols[:, h * bb + i:h * bb + i + 1]
                            + _dot_tn(k_st[rs, hs], vb[rs, hs]))
        o = o_intra + jnp.concatenate(outs, axis=0)
        o_ref[:, hs] = _group_norm_gate(o, gn_ref[...], gate[:, hs])


def _hgrn_sample(proj4, states, layer, out_states, lb, gn, steps):
    n_layers, b = states.shape[0], states.shape[1]
    bb = SAMPLE_SEQ_BLOCK
    assert HG_HEADS * bb <= HG_DK
    creates = out_states is None
    rows = pl.BlockSpec((bb * steps, HG_W), lambda i: (i, 0))
    sshape = (bb, HG_HEADS, HG_DK, HG_DV)
    s_in = pl.BlockSpec((None,) + sshape, lambda i: (layer, i, 0, 0, 0))
    s_out = pl.BlockSpec((n_layers,) + sshape, lambda i: (0, i, 0, 0, 0)) if creates else s_in
    in_specs = [rows, rows, rows, rows, s_in, _layer_resident(*lb), _layer_resident(*gn)]
    args = [*proj4, states, lb[0], gn[0]]
    aliases = {}
    if not creates:
        in_specs.append(pl.BlockSpec(memory_space=pl.ANY))
        args.append(out_states)
        aliases = {len(args) - 1: 1}
    return pl.pallas_call(
        functools.partial(_hgrn_sample_body, steps=steps, bb=bb, layer=layer, creates=creates),
        grid=(b // bb,),
        in_specs=in_specs,
        out_specs=[rows, s_out],
        out_shape=[jax.ShapeDtypeStruct((b * steps, HG_W), F32), jax.ShapeDtypeStruct(states.shape, F32)],
        input_output_aliases=aliases,
        compiler_params=_cparams("parallel"),
        name="hgrn_sample",
    )(*args)


def _trunk(x, cache_k, cache_v, st_re, st_im, st_h, p):
    prompt = cache_k is None
    bt, l, _ = x.shape
    tm = min(TOKEN_BLOCK, bt * l)
    depth = p["norm_mix_pre"].shape[0]
    kvw = N_KV * HEAD_DIM
    x2 = x.reshape(bt * l, D_MODEL)
    k_out, v_out, re_out, im_out, h_out = [], [], [], [], []
    h_all = kv_all = None
    if not prompt:
        cache_k = cache_k.reshape(cache_k.shape[:3] + (kvw,))
        cache_v = cache_v.reshape(cache_v.shape[:3] + (kvw,))
        st_re, st_im = (a.reshape(a.shape[:2] + (S5_GROUPS * S5_STATE,)) for a in (st_re, st_im))
    for layer in range(depth):
        idx = layer // 2
        g_pre = (p["norm_mix_pre"], layer)
        if layer % 2 == 0:
            u, q, k, v = _norm_proj(x2, g_pre, (p["w_in_even"], idx), (S5_WIDTH, SWA_WIDTH, kvw, kvw),
                                    min(PROJ_TOKEN_BLOCK, bt * l))
            prep = p["s5_prep"]
            d_skip, wglu = p["s5_d"], p["s5_w_glu"]
            q3, k3, v3 = q.reshape(bt, l, SWA_WIDTH), k.reshape(bt, l, kvw), v.reshape(bt, l, kvw)
            if prompt:
                a_out, h_fin = _s5_prompt(u.reshape(bt, l, S5_WIDTH), prep, idx, d_skip, wglu, min(TOKEN_BLOCK, l))
                h_fin = h_fin.reshape(bt, -1)
                b_out = _swa_prompt(q3, k3, v3, p["swa_sinks"][idx], p["t5_band"])
                kw, vw = k3[:, l - WINDOW:], v3[:, l - WINDOW:]
            else:
                a_out, h_re, h_im = _s5_sample(u.reshape(bt, l, S5_WIDTH), st_re, st_im, prep, idx, d_skip, wglu)
                h_re, h_im = (a.reshape(bt, S5_GROUPS, S5_STATE) for a in (h_re, h_im))
                b_out, kv_all = _swa_sample(q3, k3, v3, cache_k, cache_v, idx, kv_all, p["swa_sinks"][idx],
                                            p["t5_band"])
            if prompt:
                h_re, h_im = _s5_state_out(h_fin)
                k_out.append(kw.reshape(bt, WINDOW, N_KV, HEAD_DIM))
                v_out.append(vw.reshape(bt, WINDOW, N_KV, HEAD_DIM))
            re_out.append(h_re)
            im_out.append(h_im)
            mixer_outs = (a_out.reshape(bt * l, S5_WIDTH), b_out.reshape(bt * l, SWA_WIDTH))
            w_out = (p["w_out_even"], idx)
        else:
            lb, gn = (p["hgrn_lb"], idx), (p["hgrn_gnorm"], idx)
            if prompt:
                c_out, s_t = _hgrn_prompt(x2.reshape(bt, l, D_MODEL), g_pre, (p["w_in_odd"], idx), lb, gn,
                                          min(TOKEN_BLOCK, l))
                h_out.append(s_t.transpose(0, 1, 3, 2))
            else:
                proj4 = _norm_proj(x2, g_pre, (p["w_in_odd"], idx), (HG_W,) * 4, tm)
                c_out, h_all = _hgrn_sample(proj4, st_h, idx, h_all, lb, gn, l)
            mixer_outs = (c_out.reshape(bt * l, HG_W),)
            w_out = (p["w_out_odd"], idx)
        x2 = _post(x2, mixer_outs, w_out, (p["norm_mix_post"], layer), (p["norm_mlp_pre"], layer),
                   (p["norm_mlp_post"], layer), (p["w_up"], layer), (p["w_down"], layer), tm)
    if prompt:
        k_new, v_new, h_new = jnp.stack(k_out), jnp.stack(v_out), jnp.stack(h_out)
    else:
        k_new, v_new = (a.reshape(a.shape[:3] + (N_KV, HEAD_DIM)) for a in kv_all)
        h_new = h_all
    return x2.reshape(bt, l, D_MODEL), k_new, v_new, jnp.stack(re_out), jnp.stack(im_out), h_new


def kernel(x_prompt, x_sample, cache_swa_k, cache_swa_v, state_s5_re, state_s5_im, state_hgrn, t5_bias_table,
           norm_mix_pre, norm_mix_post, norm_mlp_pre, norm_mlp_post, w_in_even, w_out_even, s5_lambda_re,
           s5_lambda_im, s5_log_step, s5_b_re, s5_b_im, s5_c_re, s5_c_im, s5_d, s5_w_glu, swa_sinks, w_in_odd,
           w_out_odd, hgrn_lb_param, hgrn_gnorm, w_up, w_down):
    bf = lambda a: a.astype(BF16)
    q0, q1 = S5_WIDTH, S5_WIDTH + SWA_WIDTH
    w_in_even = jnp.concatenate([w_in_even[..., :q0], _pair_heads(w_in_even[..., q0:q1], 2), w_in_even[..., q1:]],
                                axis=-1)
    w_out_even = jnp.concatenate([w_out_even[:, :q0], _pair_heads(w_out_even[:, q0:], 1)], axis=1)
    rows = lambda a: a[:, None, :]
    params = dict(
        norm_mix_pre=rows(norm_mix_pre), norm_mix_post=rows(norm_mix_post), norm_mlp_pre=rows(norm_mlp_pre),
        norm_mlp_post=rows(norm_mlp_post), w_in_even=bf(w_in_even), w_out_even=bf(w_out_even), s5_d=rows(s5_d),
        s5_w_glu=bf(s5_w_glu), swa_sinks=swa_sinks, w_in_odd=bf(w_in_odd), w_out_odd=bf(w_out_odd),
        hgrn_gnorm=rows(hgrn_gnorm), w_up=bf(w_up), w_down=bf(w_down),
        t5_band=_t5_band(t5_bias_table),
        hgrn_lb=rows(_hgrn_lower_bounds(hgrn_lb_param)),
        s5_prep=_s5_prep(s5_lambda_re, s5_lambda_im, s5_log_step, s5_b_re, s5_b_im, s5_c_re, s5_c_im),
    )
    y_prompt, k_p, v_p, re_p, im_p, hg_p = _trunk(x_prompt, None, None, None, None, None, params)
    y_sample, k_s, v_s, re_s, im_s, hg_s = _trunk(x_sample, cache_swa_k, cache_swa_v, state_s5_re, state_s5_im,
                                                  state_hgrn, params)
    return (y_prompt, y_sample, k_p, v_p, k_s, v_s, re_p, im_p, re_s, im_s, hg_p, hg_s)
```
